```python
import jax, jax.numpy as jnp
from jax import lax
import numpy as np

D_MODEL = 2048
BATCH = 1
SEQ = 8192
DEPTH = 1
DEC_BATCH = 8
DEC_SEQ = 16
PAST_LEN = 2048

CHUNK = 64
D_RNN = 2048
N_LRU_HEADS = 16
LRU_HEAD_DIM = D_RNN // N_LRU_HEADS
CONV_A_W = 4
RG_C = 8.0
D_CONV = 1024
CONV_B_W = 3
D_FF = 5632
CONV_F_W = 3
N_MEM = 256
N_XHEADS = 4
XHEAD_DIM = D_MODEL // N_XHEADS
EPS = 1e-6

kernel_name = 'hawk_shortconv_convffn_memxattn_step'


def rmsnorm(x, g):
    xf = x.astype(jnp.float32)
    y = xf * lax.rsqrt(jnp.mean(xf * xf, axis=-1, keepdims=True) + EPS)
    return (y * g.astype(jnp.float32)).astype(x.dtype)


def causal_dwconv(x, state, w, b=None):
    width = w.shape[0]
    t = x.shape[1]
    xp = jnp.concatenate([state.astype(x.dtype), x], axis=1)
    y = xp[:, 0:t] * w[0]
    for k in range(1, width):
        y = y + xp[:, k:k + t] * w[k]
    if b is not None:
        y = y + b
    return y, xp[:, t:]


def blockdiag(x, w, b):
    bsz, t, _ = x.shape
    xh = x.reshape(bsz, t, N_LRU_HEADS, LRU_HEAD_DIM)
    return (jnp.einsum('bthi,hij->bthj', xh, w) + b).reshape(bsz, t, D_RNN)


def rglru_scan(a, b, h0):
    b = b.at[:, 0].add(a[:, 0] * h0)
    def comb(l, r):
        return (l[0] * r[0], r[0] * l[1] + r[1])
    _, h = lax.associative_scan(comb, (a, b), axis=1)
    return h


def memory_kv(mem, g_mem, w_k, w_v):
    bsz = mem.shape[0]
    mn = rmsnorm(mem, g_mem)
    k = (mn @ w_k).reshape(bsz, N_MEM, N_XHEADS, XHEAD_DIM)
    v = (mn @ w_v).reshape(bsz, N_MEM, N_XHEADS, XHEAD_DIM)
    return k, v


def layer(x, pos0, mem_k, mem_v, s_a, h0, s_b, s_f, p):
    bsz, t, _ = x.shape
    u = rmsnorm(x, p['g_mix_pre'])
    z = u @ p['w_in']
    cuts = [D_RNN, 2 * D_RNN, 2 * D_RNN + D_CONV, 2 * D_RNN + 2 * D_CONV,
            2 * D_RNN + 3 * D_CONV, 2 * D_RNN + 3 * D_CONV + D_MODEL]
    xa, ga, gb_out, gc_in, hb, gate_a, gate_b = jnp.split(z, cuts, axis=-1)
    xa, new_s_a = causal_dwconv(xa, s_a, p['conv_a_w'], p['conv_a_b'])
    r = jax.nn.sigmoid(blockdiag(xa, p['w_r'], p['b_r']).astype(jnp.float32))
    i = jax.nn.sigmoid(blockdiag(xa, p['w_i'], p['b_i']).astype(jnp.float32))
    log_a = -RG_C * r * jax.nn.softplus(-p['lru_lambda'].astype(jnp.float32))
    a = jnp.exp(log_a)
    pos = pos0 + jnp.arange(t)
    reset = (pos == 0)[None, :, None]
    mult = jnp.where(reset, 1.0, jnp.sqrt(-jnp.expm1(2.0 * log_a)))
    h = rglru_scan(a, mult * i * xa.astype(jnp.float32), h0.astype(jnp.float32))
    new_h = h[:, -1]
    y_a = (h.astype(x.dtype) * jax.nn.gelu(ga, approximate=True)) @ p['p_a']
    c_in = gc_in * hb
    c_out, new_s_b = causal_dwconv(c_in, s_b, p['conv_b_w'])
    y_b = (gb_out * c_out) @ p['p_b']
    mix = jax.nn.sigmoid(gate_a) * y_a + jax.nn.sigmoid(gate_b) * y_b
    x = x + rmsnorm(mix @ p['w_o'], p['g_mix_post'])
    u = rmsnorm(x, p['g_x_pre'])
    q = (u @ p['w_q']).reshape(bsz, t, N_XHEADS, XHEAD_DIM)
    s = jnp.einsum('bthd,bmhd->bhtm', q, mem_k.astype(q.dtype)).astype(jnp.float32) * (XHEAD_DIM ** -0.5)
    att = jax.nn.softmax(s, axis=-1).astype(x.dtype)
    o = jnp.einsum('bhtm,bmhd->bthd', att, mem_v.astype(x.dtype)).reshape(bsz, t, D_MODEL)
    x = x + rmsnorm(o @ p['w_xo'], p['g_x_post'])
    u = rmsnorm(x, p['g_ffn_pre'])
    up = u @ p['w_up']
    up, new_s_f = causal_dwconv(up, s_f, p['ffn_conv_w'], p['ffn_conv_b'])
    g, v = jnp.split(up, 2, axis=-1)
    y = (jax.nn.gelu(g, approximate=True) * v) @ p['w_down']
    x = x + rmsnorm(y, p['g_ffn_post'])
    return x, new_s_a, new_h, new_s_b, new_s_f


def setup_inputs(seed: int = 0) -> dict:
    key = jax.random.key(seed)
    ks = jax.random.split(key, 40)
    f32 = jnp.float32
    def nrm(k, shape, scale):
        return jax.random.normal(k, shape, f32) * scale
    def gain(k, n):
        return 1.0 + 0.02 * jax.random.normal(k, (DEPTH, n), f32)
    n_in = 2 * D_RNN + 3 * D_CONV + 2 * D_MODEL
    u = jax.random.uniform(ks[12], (DEPTH, D_RNN), f32, minval=0.9, maxval=0.999)
    sa = u ** (1.0 / RG_C)
    lam = jnp.log(sa) - jnp.log1p(-sa)
    return {
        'x_prompt': nrm(ks[0], (BATCH, SEQ, D_MODEL), 1.0),
        'x_sample': nrm(ks[1], (DEC_BATCH, DEC_SEQ, D_MODEL), 1.0),
        'mem_prompt': nrm(ks[2], (BATCH, N_MEM, D_MODEL), 1.0),
        'state_conv_a': nrm(ks[3], (DEPTH, DEC_BATCH, CONV_A_W - 1, D_RNN), 1.0),
        'state_rglru': nrm(ks[4], (DEPTH, DEC_BATCH, D_RNN), 0.5),
        'state_conv_b': nrm(ks[5], (DEPTH, DEC_BATCH, CONV_B_W - 1, D_CONV), 1.0),
        'state_ffn_conv': nrm(ks[6], (DEPTH, DEC_BATCH, CONV_F_W - 1, 2 * D_FF), 1.0),
        'cache_mem_k': nrm(ks[7], (DEPTH, DEC_BATCH, N_MEM, N_XHEADS, XHEAD_DIM), 1.0),
        'cache_mem_v': nrm(ks[8], (DEPTH, DEC_BATCH, N_MEM, N_XHEADS, XHEAD_DIM), 1.0),
        'g_mix_pre': gain(ks[9], D_MODEL),
        'g_mix_post': gain(ks[10], D_MODEL),
        'w_in': nrm(ks[11], (DEPTH, D_MODEL, n_in), D_MODEL ** -0.5),
        'conv_a_w': nrm(ks[13], (DEPTH, CONV_A_W, D_RNN), CONV_A_W ** -0.5),
        'conv_a_b': nrm(ks[14], (DEPTH, D_RNN), 0.01),
        'w_r': nrm(ks[15], (DEPTH, N_LRU_HEADS, LRU_HEAD_DIM, LRU_HEAD_DIM), LRU_HEAD_DIM ** -0.5),
        'b_r': nrm(ks[16], (DEPTH, N_LRU_HEADS, LRU_HEAD_DIM), 0.01),
        'w_i': nrm(ks[17], (DEPTH, N_LRU_HEADS, LRU_HEAD_DIM, LRU_HEAD_DIM), LRU_HEAD_DIM ** -0.5),
        'b_i': nrm(ks[18], (DEPTH, N_LRU_HEADS, LRU_HEAD_DIM), 0.01),
        'lru_lambda': lam,
        'conv_b_w': nrm(ks[19], (DEPTH, CONV_B_W, D_CONV), CONV_B_W ** -0.5),
        'p_a': nrm(ks[20], (DEPTH, D_RNN, D_MODEL), D_RNN ** -0.5),
        'p_b': nrm(ks[21], (DEPTH, D_CONV, D_MODEL), D_CONV ** -0.5),
        'w_o': nrm(ks[22], (DEPTH, D_MODEL, D_MODEL), D_MODEL ** -0.5),
        'g_x_pre': gain(ks[23], D_MODEL),
        'g_x_post': gain(ks[24], D_MODEL),
        'g_mem': gain(ks[25], D_MODEL),
        'w_q': nrm(ks[26], (DEPTH, D_MODEL, D_MODEL), D_MODEL ** -0.5),
        'w_k': nrm(ks[27], (DEPTH, D_MODEL, D_MODEL), D_MODEL ** -0.5),
        'w_v': nrm(ks[28], (DEPTH, D_MODEL, D_MODEL), D_MODEL ** -0.5),
        'w_xo': nrm(ks[29], (DEPTH, D_MODEL, D_MODEL), D_MODEL ** -0.5),
        'g_ffn_pre': gain(ks[30], D_MODEL),
        'g_ffn_post': gain(ks[31], D_MODEL),
        'w_up': nrm(ks[32], (DEPTH, D_MODEL, 2 * D_FF), D_MODEL ** -0.5),
        'ffn_conv_w': nrm(ks[33], (DEPTH, CONV_F_W, 2 * D_FF), CONV_F_W ** -0.5),
        'ffn_conv_b': nrm(ks[34], (DEPTH, 2 * D_FF), 0.01),
        'w_down': nrm(ks[35], (DEPTH, D_FF, D_MODEL), D_FF ** -0.5),
    }


def reference(x_prompt, x_sample, mem_prompt, state_conv_a, state_rglru, state_conv_b, state_ffn_conv,
              cache_mem_k, cache_mem_v, g_mix_pre, g_mix_post, w_in, conv_a_w, conv_a_b, w_r, b_r, w_i, b_i,
              lru_lambda, conv_b_w, p_a, p_b, w_o, g_x_pre, g_x_post, g_mem, w_q, w_k, w_v, w_xo,
              g_ffn_pre, g_ffn_post, w_up, ffn_conv_w, ffn_conv_b, w_down):
    yp, ys = x_prompt, x_sample
    pca, prh, pcb, pcf, pmk, pmv = [], [], [], [], [], []
    sca, srh, scb, scf = [], [], [], []
    for l in range(DEPTH):
        p = {'g_mix_pre': g_mix_pre[l], 'g_mix_post': g_mix_post[l], 'w_in': w_in[l],
             'conv_a_w': conv_a_w[l], 'conv_a_b': conv_a_b[l], 'w_r': w_r[l], 'b_r': b_r[l],
             'w_i': w_i[l], 'b_i': b_i[l], 'lru_lambda': lru_lambda[l], 'conv_b_w': conv_b_w[l],
             'p_a': p_a[l], 'p_b': p_b[l], 'w_o': w_o[l], 'g_x_pre': g_x_pre[l], 'g_x_post': g_x_post[l],
             'w_q': w_q[l], 'w_xo': w_xo[l], 'g_ffn_pre': g_ffn_pre[l], 'g_ffn_post': g_ffn_post[l],
             'w_up': w_up[l], 'ffn_conv_w': ffn_conv_w[l], 'ffn_conv_b': ffn_conv_b[l], 'w_down': w_down[l]}
        mk, mv = memory_kv(mem_prompt, g_mem[l], w_k[l], w_v[l])
        dt = yp.dtype
        z_a = jnp.zeros((BATCH, CONV_A_W - 1, D_RNN), dt)
        z_h = jnp.zeros((BATCH, D_RNN), jnp.float32)
        z_b = jnp.zeros((BATCH, CONV_B_W - 1, D_CONV), dt)
        z_f = jnp.zeros((BATCH, CONV_F_W - 1, 2 * D_FF), dt)
        yp, na, nh, nb, nf = layer(yp, 0, mk, mv, z_a, z_h, z_b, z_f, p)
        pca.append(na); prh.append(nh); pcb.append(nb); pcf.append(nf); pmk.append(mk); pmv.append(mv)
        ys, na, nh, nb, nf = layer(ys, PAST_LEN, cache_mem_k[l], cache_mem_v[l], state_conv_a[l],
                                   state_rglru[l], state_conv_b[l], state_ffn_conv[l], p)
        sca.append(na); srh.append(nh); scb.append(nb); scf.append(nf)
    return (yp, ys, jnp.stack(pca), jnp.stack(prh), jnp.stack(pcb), jnp.stack(pcf), jnp.stack(pmk), jnp.stack(pmv),
            jnp.stack(sca), jnp.stack(srh), jnp.stack(scb), jnp.stack(scf))
```

```python
import functools

import jax
import jax.numpy as jnp
from jax import lax
from jax.experimental import pallas as pl
from jax.experimental.pallas import tpu as pltpu

F32 = jnp.float32
BF16 = jnp.bfloat16

EPS = 1e-6
RG_C = 8.0
PAST_LEN = 2048

LANES = 128
SUBLANES = 8
BF16_ROWS = 16
VMEM_LIMIT_BYTES = 56 * 1024 * 1024

ROW_CHUNK = 128
SCAN_LANES = 512


def _params(*semantics):
    return pltpu.CompilerParams(dimension_semantics=semantics, vmem_limit_bytes=VMEM_LIMIT_BYTES)


def _resident(shape):
    nd = len(shape)
    return pl.BlockSpec(shape, lambda *_: (0,) * nd, pipeline_mode=pl.Buffered(1))


def _rmsnorm(x, g):
    y = x * lax.rsqrt(jnp.mean(x * x, axis=-1, keepdims=True) + EPS)
    return y * g


def _softplus(x):
    return jnp.maximum(x, 0.0) + jnp.log1p(jnp.exp(-jnp.abs(x)))


def _norm_matmul_kernel(x_ref, g_ref, w_ref, o_ref, u_ref):
    tm = x_ref.shape[0]
    chunk = min(ROW_CHUNK, tm)

    @pl.when(pl.program_id(1) == 0)
    def _():
        g = g_ref[...]

        def body(c, _):
            rows = pl.ds(pl.multiple_of(c * chunk, chunk), chunk)
            u_ref[rows, :] = _rmsnorm(x_ref[rows, :], g).astype(BF16)
            return None

        lax.fori_loop(0, tm // chunk, body, None)

    o_ref[...] = jnp.dot(u_ref[...], w_ref[...], preferred_element_type=F32).astype(o_ref.dtype)


def _norm_matmul(x, g, w, out_dtype, tm, tn):
    m, d = x.shape
    n = w.shape[1]
    tm, tn = min(tm, m), min(tn, n)
    assert m % tm == 0 and n % tn == 0 and tm % min(ROW_CHUNK, tm) == 0
    return pl.pallas_call(
        _norm_matmul_kernel,
        grid=(m // tm, n // tn),
        in_specs=[pl.BlockSpec((tm, d), lambda i, j: (i, 0)),
                  pl.BlockSpec((1, d), lambda i, j: (0, 0)),
                  pl.BlockSpec((d, tn), lambda i, j: (0, j))],
        out_specs=pl.BlockSpec((tm, tn), lambda i, j: (i, j)),
        out_shape=jax.ShapeDtypeStruct((m, n), out_dtype),
        scratch_shapes=[pltpu.VMEM((tm, d), BF16)],
        compiler_params=_params("parallel", "arbitrary"),
        name="norm_matmul",
    )(x, g, w)


def _mixer_pre_kernel(z_ref, sa_ref, h0_ref, sb_ref, caw_ref, cab_ref, wri_ref, br_ref, bi_ref, lam_ref, cbw_ref,
                      ya_ref, yb_ref, nsa_ref, nh_ref, nsb_ref,
                      xa_scr, xc_scr, xcb_scr, ri_scr, cb_scr, h_scr, *, d_rnn, d_conv, pos0):
    t = pl.program_id(1)
    nb, seq, _ = ya_ref.shape
    wa = caw_ref.shape[0]
    wb = cbw_ref.shape[0]
    n_heads, head_dim, _ = wri_ref.shape
    top = SUBLANES

    @pl.when(t == 0)
    def _():
        xa_scr[:, top - (wa - 1):top, :] = sa_ref[...]
        cb_scr[:, top - (wb - 1):top, :] = sb_ref[...]
        h_scr[...] = jnp.broadcast_to(h0_ref[...], h_scr.shape)

    xa_scr[:, top:top + seq, :] = z_ref[:, :, 0:d_rnn].astype(F32)
    base = top - (wa - 1)
    xc = xa_scr[:, base:base + seq, :] * caw_ref[0:1, :]
    for k in range(1, wa):
        xc = xc + xa_scr[:, base + k:base + k + seq, :] * caw_ref[k:k + 1, :]
    xc = xc + cab_ref[...]
    xc_scr[...] = xc
    xcb_scr[...] = xc.reshape(nb * seq, d_rnn).astype(BF16)
    nsa_ref[...] = xa_scr[:, top + seq - (wa - 1):top + seq, :]
    xa_scr[:, 0:top, :] = xa_scr[:, seq:seq + top, :]

    for h in range(n_heads):
        cols = slice(h * head_dim, (h + 1) * head_dim)
        ri = jnp.dot(xcb_scr[:, cols], wri_ref[h], preferred_element_type=F32)
        ri_scr[:, :, cols] = ri[:, :head_dim].reshape(nb, seq, head_dim)
        ri_scr[:, :, d_rnn + h * head_dim:d_rnn + (h + 1) * head_dim] = ri[:, head_dim:].reshape(nb, seq, head_dim)

    strip = max(LANES, SCAN_LANES // nb)
    rows_per_iter = BF16_ROWS
    row_in_group = lax.broadcasted_iota(jnp.int32, (1, SUBLANES, strip), 1)
    for c in range(d_rnn // strip):
        cs = slice(c * strip, (c + 1) * strip)
        cs_i = slice(d_rnn + c * strip, d_rnn + (c + 1) * strip)
        cs_g = slice(d_rnn + c * strip, d_rnn + (c + 1) * strip)
        sp = _softplus(-lam_ref[:, cs])
        b_r = br_ref[:, cs]
        b_i = bi_ref[:, cs]

        def body(it, carry, cs=cs, cs_i=cs_i, cs_g=cs_g, sp=sp, b_r=b_r, b_i=b_i):
            r0 = pl.multiple_of(it * rows_per_iter, rows_per_iter)
            rows = pl.ds(r0, rows_per_iter)
            r = jax.nn.sigmoid(ri_scr[:, rows, cs] + b_r)
            i = jax.nn.sigmoid(ri_scr[:, rows, cs_i] + b_i)
            log_a = -RG_C * r * sp
            a = jnp.exp(log_a)
            mult = jnp.sqrt(-jnp.tanh(log_a) * (a * a + 1.0))
            if pos0 == 0:
                pos = t * seq + r0 + lax.broadcasted_iota(jnp.int32, (1, rows_per_iter, strip), 1)
                mult = jnp.where(pos == 0, 1.0, mult)
            b = mult * i * xc_scr[:, rows, cs]
            hs = []
            for sub in range(rows_per_iter // SUBLANES):
                a8 = a[:, sub * SUBLANES:(sub + 1) * SUBLANES, :]
                b8 = b[:, sub * SUBLANES:(sub + 1) * SUBLANES, :]
                for s in (1, 2, 4):
                    keep = row_in_group >= s
                    a_prev = pltpu.roll(a8, s, 1)
                    b_prev = pltpu.roll(b8, s, 1)
                    b8 = jnp.where(keep, a8 * b_prev + b8, b8)
                    a8 = jnp.where(keep, a8 * a_prev, a8)
                h8 = a8 * carry + b8
                carry = jnp.broadcast_to(h8[:, SUBLANES - 1:SUBLANES, :], h8.shape)
                hs.append(h8)
            h = jnp.concatenate(hs, axis=1)
            gate = jax.nn.gelu(z_ref[:, rows, cs_g].astype(F32), approximate=True)
            ya_ref[:, rows, cs] = (h * gate).astype(BF16)
            return carry

        last = lax.fori_loop(0, seq // rows_per_iter, body, h_scr[:, :, cs])
        h_scr[:, :, cs] = last
        nh_ref[:, :, cs] = last[:, 0:1, :]

    o_gb, o_gc, o_hb = 2 * d_rnn, 2 * d_rnn + d_conv, 2 * d_rnn + 2 * d_conv
    cb_scr[:, top:top + seq, :] = (z_ref[:, :, o_gc:o_gc + d_conv].astype(F32)
                                   * z_ref[:, :, o_hb:o_hb + d_conv].astype(F32))
    base = top - (wb - 1)
    co = cb_scr[:, base:base + seq, :] * cbw_ref[0:1, :]
    for k in range(1, wb):
        co = co + cb_scr[:, base + k:base + k + seq, :] * cbw_ref[k:k + 1, :]
    yb_ref[...] = (z_ref[:, :, o_gb:o_gb + d_conv].astype(F32) * co).astype(BF16)
    nsb_ref[...] = cb_scr[:, top + seq - (wb - 1):top + seq, :]
    cb_scr[:, 0:top, :] = cb_scr[:, seq:seq + top, :]


def _mixer_pre(z, state_a, h0, state_b, conv_a_w, conv_a_b, w_ri, b_r, b_i, lam, conv_b_w, *, seq_tile, pos0):
    bsz, t, _ = z.shape
    d_rnn = conv_a_w.shape[1]
    d_conv = conv_b_w.shape[1]
    n_pre = 2 * d_rnn + 3 * d_conv
    nb = bsz if t <= seq_tile else 1
    seq = min(seq_tile, t)
    assert t % seq == 0 and seq % BF16_ROWS == 0 and bsz % nb == 0
    wa, wb = conv_a_w.shape[0], conv_b_w.shape[0]
    kern = functools.partial(_mixer_pre_kernel, d_rnn=d_rnn, d_conv=d_conv, pos0=pos0)
    row = lambda b, i: (b, i, 0)
    per_b = lambda b, i: (b, 0, 0)
    const2 = lambda b, i: (0, 0)
    return pl.pallas_call(
        kern,
        grid=(bsz // nb, t // seq),
        in_specs=[pl.BlockSpec((nb, seq, n_pre), row),
                  pl.BlockSpec((nb, wa - 1, d_rnn), per_b),
                  pl.BlockSpec((nb, 1, d_rnn), per_b),
                  pl.BlockSpec((nb, wb - 1, d_conv), per_b),
                  pl.BlockSpec(conv_a_w.shape, const2),
                  pl.BlockSpec((1, d_rnn), const2),
                  pl.BlockSpec(w_ri.shape, lambda b, i: (0, 0, 0)),
                  pl.BlockSpec((1, d_rnn), const2),
                  pl.BlockSpec((1, d_rnn), const2),
                  pl.BlockSpec((1, d_rnn), const2),
                  pl.BlockSpec(conv_b_w.shape, const2)],
        out_specs=[pl.BlockSpec((nb, seq, d_rnn), row),
                   pl.BlockSpec((nb, seq, d_conv), row),
                   pl.BlockSpec((nb, wa - 1, d_rnn), per_b),
                   pl.BlockSpec((nb, 1, d_rnn), per_b),
                   pl.BlockSpec((nb, wb - 1, d_conv), per_b)],
        out_shape=[jax.ShapeDtypeStruct((bsz, t, d_rnn), BF16),
                   jax.ShapeDtypeStruct((bsz, t, d_conv), BF16),
                   jax.ShapeDtypeStruct((bsz, wa - 1, d_rnn), F32),
                   jax.ShapeDtypeStruct((bsz, 1, d_rnn), F32),
                   jax.ShapeDtypeStruct((bsz, wb - 1, d_conv), F32)],
        scratch_shapes=[pltpu.VMEM((nb, SUBLANES + seq, d_rnn), F32),
                        pltpu.VMEM((nb, seq, d_rnn), F32),
                        pltpu.VMEM((nb * seq, d_rnn), BF16),
                        pltpu.VMEM((nb, seq, 2 * d_rnn), F32),
                        pltpu.VMEM((nb, SUBLANES + seq, d_conv), F32),
                        pltpu.VMEM((nb, SUBLANES, d_rnn), F32)],
        compiler_params=_params("arbitrary", "arbitrary"),
        name="mixer_pre",
    )(z, state_a, h0, state_b, conv_a_w, conv_a_b, w_ri, b_r, b_i, lam, conv_b_w)


def _mixer_post_kernel(ya_ref, yb_ref, ga0_ref, ga1_ref, gb0_ref, gb1_ref, x_ref, pa_ref, pb_ref, wo_ref, g_ref,
                       o_ref, mix_scr):
    half = ga0_ref.shape[1]
    y_a = jnp.dot(ya_ref[...], pa_ref[...], preferred_element_type=F32)
    y_b = jnp.dot(yb_ref[...], pb_ref[...], preferred_element_type=F32)
    for c, (ga_ref, gb_ref) in enumerate(((ga0_ref, gb0_ref), (ga1_ref, gb1_ref))):
        cols = slice(c * half, (c + 1) * half)
        mix = (jax.nn.sigmoid(ga_ref[...].astype(F32)) * y_a[:, cols]
               + jax.nn.sigmoid(gb_ref[...].astype(F32)) * y_b[:, cols])
        mix_scr[:, cols] = mix.astype(BF16)
    out = jnp.dot(mix_scr[...], wo_ref[...], preferred_element_type=F32)
    o_ref[...] = x_ref[...] + _rmsnorm(out, g_ref[...])


def _mixer_post(ya, yb, z, x, p_a, p_b, w_o, g_post, *, tm):
    m, d = x.shape
    d_rnn, d_conv = ya.shape[1], yb.shape[1]
    tm = min(tm, m)
    half = d // 2
    gate0 = (2 * d_rnn + 3 * d_conv) // half
    assert m % tm == 0 and (2 * d_rnn + 3 * d_conv) % half == 0
    gate_spec = lambda k: pl.BlockSpec((tm, half), lambda i: (i, gate0 + k))
    row = lambda i: (i, 0)
    return pl.pallas_call(
        _mixer_post_kernel,
        grid=(m // tm,),
        in_specs=[pl.BlockSpec((tm, d_rnn), row), pl.BlockSpec((tm, d_conv), row),
                  gate_spec(0), gate_spec(1), gate_spec(2), gate_spec(3),
                  pl.BlockSpec((tm, d), row),
                  _resident(p_a.shape), _resident(p_b.shape), _resident(w_o.shape),
                  pl.BlockSpec((1, d), lambda i: (0, 0))],
        out_specs=pl.BlockSpec((tm, d), row),
        out_shape=jax.ShapeDtypeStruct((m, d), F32),
        scratch_shapes=[pltpu.VMEM((tm, d), BF16)],
        compiler_params=_params("parallel"),
        name="mixer_post",
    )(ya, yb, z, z, z, z, x, p_a, p_b, w_o, g_post)


def _xattn_kernel(x_ref, k_ref, v_ref, wq_ref, wxo_ref, gpre_ref, gpost_ref, o_ref, q_scr, o_scr, *, n_heads):
    nb, seq, d = x_ref.shape
    hd = d // n_heads
    scale = hd ** -0.5
    x = x_ref[...].reshape(nb * seq, d)
    u = _rmsnorm(x, gpre_ref[...]).astype(BF16)
    q_scr[...] = jnp.dot(u, wq_ref[...], preferred_element_type=F32).astype(BF16)
    for b in range(nb):
        rows = slice(b * seq, (b + 1) * seq)
        for h in range(n_heads):
            cols = slice(h * hd, (h + 1) * hd)
            s = lax.dot_general(q_scr[rows, cols], k_ref[b, :, cols], (((1,), (1,)), ((), ())),
                                preferred_element_type=F32) * scale
            p = jnp.exp(s - jnp.max(s, axis=-1, keepdims=True))
            att = (p / jnp.sum(p, axis=-1, keepdims=True)).astype(BF16)
            o_scr[rows, cols] = jnp.dot(att, v_ref[b, :, cols], preferred_element_type=F32).astype(BF16)
    out = jnp.dot(o_scr[...], wxo_ref[...], preferred_element_type=F32)
    o_ref[...] = (x + _rmsnorm(out, gpost_ref[...])).reshape(nb, seq, d)


def _xattn(x, mem_k, mem_v, w_q, w_xo, g_pre, g_post, *, n_heads, seq_tile):
    bsz, t, d = x.shape
    n_mem = mem_k.shape[1]
    nb = bsz if t <= seq_tile else 1
    seq = min(seq_tile, t)
    assert t % seq == 0 and seq % BF16_ROWS == 0 and bsz % nb == 0
    single = bsz // nb == 1
    mem_spec = (_resident((nb, n_mem, d)) if single else pl.BlockSpec((nb, n_mem, d), lambda b, i: (b, 0, 0)))
    row = lambda b, i: (b, i, 0)
    return pl.pallas_call(
        functools.partial(_xattn_kernel, n_heads=n_heads),
        grid=(bsz // nb, t // seq),
        in_specs=[pl.BlockSpec((nb, seq, d), row), mem_spec, mem_spec,
                  _resident(w_q.shape), _resident(w_xo.shape),
                  pl.BlockSpec((1, d), lambda b, i: (0, 0)), pl.BlockSpec((1, d), lambda b, i: (0, 0))],
        out_specs=pl.BlockSpec((nb, seq, d), row),
        out_shape=jax.ShapeDtypeStruct((bsz, t, d), F32),
        scratch_shapes=[pltpu.VMEM((nb * seq, d), BF16), pltpu.VMEM((nb * seq, d), BF16)],
        compiler_params=_params("parallel", "parallel"),
        name="xattn",
    )(x, mem_k, mem_v, w_q, w_xo, g_pre, g_post)


def _ffn_post_kernel(upg_ref, upv_ref, sg_ref, sv_ref, x_ref, cwg_ref, cwv_ref, cbg_ref, cbv_ref, wd_ref, g_ref,
                     o_ref, nsg_ref, nsv_ref, g_scr, v_scr, carry_scr, acc_scr):
    t = pl.program_id(1)
    kk = pl.program_id(2)
    n_k = pl.num_programs(2)
    nb, seq, tk = upg_ref.shape
    d_ff = nsg_ref.shape[2]
    width = cwg_ref.shape[0]
    top = SUBLANES
    base = top - (width - 1)

    def conv(up_ref, st_ref, scr, cw_ref, cb_ref, ns_ref, slot):
        scr[:, top:top + seq, :] = up_ref[...].astype(F32)

        @pl.when(t == 0)
        def _():
            scr[:, base:top, :] = st_ref[...]

        @pl.when(t > 0)
        def _():
            scr[:, 0:top, :] = carry_scr[slot]

        y = scr[:, base:base + seq, :] * cw_ref[0:1, :]
        for k in range(1, width):
            y = y + scr[:, base + k:base + k + seq, :] * cw_ref[k:k + 1, :]
        y = y + cb_ref[...]
        carry_scr[slot] = scr[:, seq:seq + top, :]
        for k in range(d_ff // tk):
            @pl.when(kk == k)
            def _(k=k):
                ns_ref[:, :, k * tk:(k + 1) * tk] = scr[:, top + seq - (width - 1):top + seq, :]
        return y

    yg = conv(upg_ref, sg_ref, g_scr, cwg_ref, cbg_ref, nsg_ref, kk)
    yv = conv(upv_ref, sv_ref, v_scr, cwv_ref, cbv_ref, nsv_ref, n_k + kk)
    hid = (jax.nn.gelu(yg, approximate=True) * yv).reshape(nb * seq, tk).astype(BF16)
    part = jnp.dot(hid, wd_ref[pl.ds(pl.multiple_of(kk * tk, tk), tk), :], preferred_element_type=F32)

    @pl.when(kk == 0)
    def _():
        acc_scr[...] = part

    @pl.when(kk > 0)
    def _():
        acc_scr[...] += part

    @pl.when(kk == n_k - 1)
    def _():
        d = x_ref.shape[2]
        x = x_ref[...].reshape(nb * seq, d)
        o_ref[...] = (x + _rmsnorm(acc_scr[...], g_ref[...])).reshape(nb, seq, d)


def _ffn_post(up, state_f, x, conv_w, conv_b, w_down, g_post, *, seq_tile, tk):
    bsz, t, d = x.shape
    d_ff = w_down.shape[0]
    width = conv_w.shape[0]
    nb = bsz if t <= seq_tile else 1
    seq = min(seq_tile, t)
    n_k = d_ff // tk
    assert t % seq == 0 and seq % BF16_ROWS == 0 and bsz % nb == 0 and d_ff % tk == 0 and tk % LANES == 0
    blk_g = lambda b, i, k: (b, i, k)
    blk_v = lambda b, i, k: (b, i, n_k + k)
    st_g = lambda b, i, k: (b, 0, k)
    st_v = lambda b, i, k: (b, 0, n_k + k)
    par_g = lambda b, i, k: (0, k)
    par_v = lambda b, i, k: (0, n_k + k)
    row = lambda b, i, k: (b, i, 0)
    return pl.pallas_call(
        _ffn_post_kernel,
        grid=(bsz // nb, t // seq, n_k),
        in_specs=[pl.BlockSpec((nb, seq, tk), blk_g), pl.BlockSpec((nb, seq, tk), blk_v),
                  pl.BlockSpec((nb, width - 1, tk), st_g), pl.BlockSpec((nb, width - 1, tk), st_v),
                  pl.BlockSpec((nb, seq, d), row),
                  pl.BlockSpec((width, tk), par_g), pl.BlockSpec((width, tk), par_v),
                  pl.BlockSpec((1, tk), par_g), pl.BlockSpec((1, tk), par_v),
                  _resident(w_down.shape),
                  pl.BlockSpec((1, d), lambda b, i, k: (0, 0))],
        out_specs=[pl.BlockSpec((nb, seq, d), row),
                   pl.BlockSpec((nb, width - 1, d_ff), lambda b, i, k: (b, 0, 0)),
                   pl.BlockSpec((nb, width - 1, d_ff), lambda b, i, k: (b, 0, 0))],
        out_shape=[jax.ShapeDtypeStruct((bsz, t, d), F32),
                   jax.ShapeDtypeStruct((bsz, width - 1, d_ff), F32),
                   jax.ShapeDtypeStruct((bsz, width - 1, d_ff), F32)],
        scratch_shapes=[pltpu.VMEM((nb, SUBLANES + seq, tk), F32),
                        pltpu.VMEM((nb, SUBLANES + seq, tk), F32),
                        pltpu.VMEM((2 * n_k, nb, SUBLANES, tk), F32),
                        pltpu.VMEM((nb * seq, d), F32)],
        compiler_params=_params("arbitrary", "arbitrary", "arbitrary"),
        name="ffn_post",
    )(up, up, state_f, state_f, x, conv_w, conv_w, conv_b, conv_b, w_down, g_post)


MM_TM, MM_TN = 1024, 1024
SEQ_TILE = 256
FFN_TK = 1408
N_XHEADS = 4


def _layer(x, pos0, mem_k, mem_v, s_a, h0, s_b, s_f, p):
    bsz, t, d = x.shape
    m = bsz * t
    x2 = x.reshape(m, d)
    z = _norm_matmul(x2, p['g_mix_pre'], p['w_in'], BF16, MM_TM, MM_TN)
    ya, yb, ns_a, nh, ns_b = _mixer_pre(z.reshape(bsz, t, -1), s_a, h0[:, None, :], s_b,
                                        p['conv_a_w'], p['conv_a_b'], p['w_ri'], p['b_r'], p['b_i'],
                                        p['lru_lambda'], p['conv_b_w'], seq_tile=SEQ_TILE, pos0=pos0)
    x1 = _mixer_post(ya.reshape(m, -1), yb.reshape(m, -1), z, x2, p['p_a'], p['p_b'], p['w_o'],
                     p['g_mix_post'], tm=SEQ_TILE)
    x2a = _xattn(x1.reshape(bsz, t, d), mem_k, mem_v, p['w_q'], p['w_xo'], p['g_x_pre'], p['g_x_post'],
                 n_heads=N_XHEADS, seq_tile=SEQ_TILE)
    up = _norm_matmul(x2a.reshape(m, d), p['g_ffn_pre'], p['w_up'], BF16, MM_TM, MM_TN)
    x3, ns_fg, ns_fv = _ffn_post(up.reshape(bsz, t, -1), s_f, x2a, p['ffn_conv_w'], p['ffn_conv_b'],
                                 p['w_down'], p['g_ffn_post'], seq_tile=SEQ_TILE, tk=FFN_TK)
    return x3, ns_a, nh[:, 0, :], ns_b, jnp.concatenate([ns_fg, ns_fv], axis=-1)


def kernel(x_prompt, x_sample, mem_prompt, state_conv_a, state_rglru, state_conv_b, state_ffn_conv, cache_mem_k, cache_mem_v, g_mix_pre, g_mix_post, w_in, conv_a_w, conv_a_b, w_r, b_r, w_i, b_i, lru_lambda, conv_b_w, p_a, p_b, w_o, g_x_pre, g_x_post, g_mem, w_q, w_k, w_v, w_xo, g_ffn_pre, g_ffn_post, w_up, ffn_conv_w, ffn_conv_b, w_down):
    depth = w_in.shape[0]
    bsz, _, d = x_prompt.shape
    n_mem = mem_prompt.shape[1]
    yp, ys = x_prompt, x_sample
    outs = [[] for _ in range(10)]
    row = lambda v: v.reshape(1, -1).astype(F32)
    for l in range(depth):
        p = {'g_mix_pre': row(g_mix_pre[l]), 'g_mix_post': row(g_mix_post[l]), 'w_in': w_in[l].astype(BF16),
             'conv_a_w': conv_a_w[l], 'conv_a_b': row(conv_a_b[l]),
             'w_ri': jnp.concatenate([w_r[l], w_i[l]], axis=-1).astype(BF16),
             'b_r': row(b_r[l]), 'b_i': row(b_i[l]), 'lru_lambda': row(lru_lambda[l]),
             'conv_b_w': conv_b_w[l], 'p_a': p_a[l].astype(BF16), 'p_b': p_b[l].astype(BF16),
             'w_o': w_o[l].astype(BF16), 'g_x_pre': row(g_x_pre[l]), 'g_x_post': row(g_x_post[l]),
             'w_q': w_q[l].astype(BF16), 'w_xo': w_xo[l].astype(BF16),
             'g_ffn_pre': row(g_ffn_pre[l]), 'g_ffn_post': row(g_ffn_post[l]), 'w_up': w_up[l].astype(BF16),
             'ffn_conv_w': ffn_conv_w[l], 'ffn_conv_b': row(ffn_conv_b[l]), 'w_down': w_down[l].astype(BF16)}
        d_rnn, d_conv, d_up = conv_a_w.shape[2], conv_b_w.shape[2], ffn_conv_w.shape[2]
        mem2 = mem_prompt.reshape(bsz * n_mem, d)
        mk = _norm_matmul(mem2, row(g_mem[l]), w_k[l].astype(BF16), F32, MM_TM, MM_TN).reshape(bsz, n_mem, d)
        mv = _norm_matmul(mem2, row(g_mem[l]), w_v[l].astype(BF16), F32, MM_TM, MM_TN).reshape(bsz, n_mem, d)
        zeros = lambda *s: jnp.zeros(s, F32)
        yp, na, nh, nb_, nf = _layer(yp, 0, mk.astype(BF16), mv.astype(BF16),
                                     zeros(bsz, conv_a_w.shape[1] - 1, d_rnn), zeros(bsz, d_rnn),
                                     zeros(bsz, conv_b_w.shape[1] - 1, d_conv),
                                     zeros(bsz, ffn_conv_w.shape[1] - 1, d_up), p)
        for o, v in zip(outs[:6], (na, nh, nb_, nf, mk.reshape(bsz, n_mem, N_XHEADS, -1),
                                   mv.reshape(bsz, n_mem, N_XHEADS, -1))):
            o.append(v)
        dec_b = x_sample.shape[0]
        ck = cache_mem_k[l].reshape(dec_b, n_mem, d).astype(BF16)
        cv = cache_mem_v[l].reshape(dec_b, n_mem, d).astype(BF16)
        ys, na, nh, nb_, nf = _layer(ys, PAST_LEN, ck, cv, state_conv_a[l], state_rglru[l], state_conv_b[l],
                                     state_ffn_conv[l], p)
        for o, v in zip(outs[6:], (na, nh, nb_, nf)):
            o.append(v)
    return (yp, ys) + tuple(jnp.stack(o) for o in outs)
```

```python
import functools

import jax
import jax.numpy as jnp
from jax import lax
from jax.experimental import pallas as pl
from jax.experimental.pallas import tpu as pltpu

F32 = jnp.float32
BF16 = jnp.bfloat16

EPS = 1e-6
RG_C = 8.0
PAST_LEN = 2048

LANES = 128
SUBLANES = 8
BF16_ROWS = 16
MXU_COLS = 256
VMEM_LIMIT_BYTES = 56 * 1024 * 1024

ROW_CHUNK = 128
CAST_ROWS = 256
COL_CHUNK = MXU_COLS
SCAN_LANES = 512


def _params(*semantics):
    return pltpu.CompilerParams(dimension_semantics=semantics, vmem_limit_bytes=VMEM_LIMIT_BYTES)


def _resident(shape):
    nd = len(shape)
    return pl.BlockSpec(shape, lambda *_: (0,) * nd, pipeline_mode=pl.Buffered(1))


def _rmsnorm(x, g):
    y = x * lax.rsqrt(jnp.mean(x * x, axis=-1, keepdims=True) + EPS)
    return y * g


def _softplus(x):
    return jnp.maximum(x, 0.0) + jnp.log1p(jnp.exp(-jnp.abs(x)))


def _rmsnorm_kernel(x_ref, g_ref, u_ref):
    tm = x_ref.shape[0]
    chunk = min(ROW_CHUNK, tm)
    g = g_ref[...]

    def body(c, _):
        rows = pl.ds(pl.multiple_of(c * chunk, chunk), chunk)
        u_ref[rows, :] = _rmsnorm(x_ref[rows, :], g).astype(BF16)
        return None

    lax.fori_loop(0, tm // chunk, body, None)


def _rmsnorm_call(x, g, *, tm):
    m, d = x.shape
    tm = min(tm, m)
    assert m % tm == 0 and tm % min(ROW_CHUNK, tm) == 0
    return pl.pallas_call(
        _rmsnorm_kernel,
        grid=(m // tm,),
        in_specs=[pl.BlockSpec((tm, d), lambda i: (i, 0)), pl.BlockSpec((1, d), lambda i: (0, 0))],
        out_specs=pl.BlockSpec((tm, d), lambda i: (i, 0)),
        out_shape=jax.ShapeDtypeStruct((m, d), BF16),
        compiler_params=_params("parallel"),
        name="rmsnorm",
    )(x, g)


def _cast_rows(src_ref, dst_ref):
    rows_total = src_ref.shape[0]
    chunk = min(CAST_ROWS, rows_total)

    def body(c, _):
        rows = pl.ds(pl.multiple_of(c * chunk, chunk), chunk)
        dst_ref[rows, :] = src_ref[rows, :].astype(BF16)
        return None

    lax.fori_loop(0, rows_total // chunk, body, None)


def _proj_kernel(u_ref, w_ref, o_ref, w_scr):
    @pl.when(pl.program_id(1) == 0)
    def _():
        _cast_rows(w_ref, w_scr)

    o_ref[...] = jnp.dot(u_ref[...], w_scr[...], preferred_element_type=F32).astype(o_ref.dtype)


def _proj(u, w, out_dtype, *, tm, tn):
    m, d = u.shape
    n = w.shape[1]
    tm, tn = min(tm, m), min(tn, n)
    assert m % tm == 0 and n % tn == 0 and d % min(CAST_ROWS, d) == 0
    return pl.pallas_call(
        _proj_kernel,
        grid=(n // tn, m // tm),
        in_specs=[pl.BlockSpec((tm, d), lambda j, i: (i, 0)),
                  pl.BlockSpec((d, tn), lambda j, i: (0, j))],
        out_specs=pl.BlockSpec((tm, tn), lambda j, i: (i, j)),
        out_shape=jax.ShapeDtypeStruct((m, n), out_dtype),
        scratch_shapes=[pltpu.VMEM((d, tn), BF16)],
        compiler_params=_params("arbitrary", "arbitrary"),
        name="proj",
    )(u, w)


def _mixer_pre_kernel(z_ref, sa_ref, h0_ref, sb_ref, caw_ref, cab_ref, wri_ref, br_ref, bi_ref, lam_ref, cbw_ref,
                      ya_ref, yb_ref, nsa_ref, nh_ref, nsb_ref,
                      xa_scr, xc_scr, xcb_scr, ri_scr, cb_scr, h_scr, *, d_rnn, d_conv, pos0):
    t = pl.program_id(1)
    nb, seq, _ = ya_ref.shape
    wa = caw_ref.shape[0]
    wb = cbw_ref.shape[0]
    n_heads, head_dim, _ = wri_ref.shape
    top = SUBLANES

    @pl.when(t == 0)
    def _():
        xa_scr[:, top - (wa - 1):top, :] = sa_ref[...]
        cb_scr[:, top - (wb - 1):top, :] = sb_ref[...]
        h_scr[...] = jnp.broadcast_to(h0_ref[...], h_scr.shape)

    xa_scr[:, top:top + seq, :] = z_ref[:, :, 0:d_rnn].astype(F32)
    base = top - (wa - 1)
    xc = xa_scr[:, base:base + seq, :] * caw_ref[0:1, :]
    for k in range(1, wa):
        xc = xc + xa_scr[:, base + k:base + k + seq, :] * caw_ref[k:k + 1, :]
    xc = xc + cab_ref[...]
    xc_scr[...] = xc
    xcb_scr[...] = xc.reshape(nb * seq, d_rnn).astype(BF16)
    nsa_ref[...] = xa_scr[:, top + seq - (wa - 1):top + seq, :]
    xa_scr[:, 0:top, :] = xa_scr[:, seq:seq + top, :]

    for h in range(n_heads):
        cols = slice(h * head_dim, (h + 1) * head_dim)
        ri = jnp.dot(xcb_scr[:, cols], wri_ref[h], preferred_element_type=F32)
        ri_scr[:, :, cols] = ri[:, :head_dim].reshape(nb, seq, head_dim)
        ri_scr[:, :, d_rnn + h * head_dim:d_rnn + (h + 1) * head_dim] = ri[:, head_dim:].reshape(nb, seq, head_dim)

    strip = max(LANES, SCAN_LANES // nb)
    rows_per_iter = BF16_ROWS
    row_in_group = lax.broadcasted_iota(jnp.int32, (1, SUBLANES, strip), 1)
    for c in range(d_rnn // strip):
        cs = slice(c * strip, (c + 1) * strip)
        cs_i = slice(d_rnn + c * strip, d_rnn + (c + 1) * strip)
        cs_g = slice(d_rnn + c * strip, d_rnn + (c + 1) * strip)
        sp = _softplus(-lam_ref[:, cs])
        b_r = br_ref[:, cs]
        b_i = bi_ref[:, cs]

        def body(it, carry, cs=cs, cs_i=cs_i, cs_g=cs_g, sp=sp, b_r=b_r, b_i=b_i):
            r0 = pl.multiple_of(it * rows_per_iter, rows_per_iter)
            rows = pl.ds(r0, rows_per_iter)
            r = jax.nn.sigmoid(ri_scr[:, rows, cs] + b_r)
            i = jax.nn.sigmoid(ri_scr[:, rows, cs_i] + b_i)
            log_a = -RG_C * r * sp
            a = jnp.exp(log_a)
            mult = jnp.sqrt(-jnp.tanh(log_a) * (a * a + 1.0))
            if pos0 == 0:
                pos = t * seq + r0 + lax.broadcasted_iota(jnp.int32, (1, rows_per_iter, strip), 1)
                mult = jnp.where(pos == 0, 1.0, mult)
            b = mult * i * xc_scr[:, rows, cs]
            hs = []
            for sub in range(rows_per_iter // SUBLANES):
                a8 = a[:, sub * SUBLANES:(sub + 1) * SUBLANES, :]
                b8 = b[:, sub * SUBLANES:(sub + 1) * SUBLANES, :]
                for s in (1, 2, 4):
                    keep = row_in_group >= s
                    a_prev = pltpu.roll(a8, s, 1)
                    b_prev = pltpu.roll(b8, s, 1)
                    b8 = jnp.where(keep, a8 * b_prev + b8, b8)
                    a8 = jnp.where(keep, a8 * a_prev, a8)
                h8 = a8 * carry + b8
                carry = jnp.broadcast_to(h8[:, SUBLANES - 1:SUBLANES, :], h8.shape)
                hs.append(h8)
            h = jnp.concatenate(hs, axis=1)
            gate = jax.nn.gelu(z_ref[:, rows, cs_g].astype(F32), approximate=True)
            ya_ref[:, rows, cs] = (h * gate).astype(BF16)
            return carry

        last = lax.fori_loop(0, seq // rows_per_iter, body, h_scr[:, :, cs])
        h_scr[:, :, cs] = last
        nh_ref[:, :, cs] = last[:, 0:1, :]

    o_gb, o_gc, o_hb = 2 * d_rnn, 2 * d_rnn + d_conv, 2 * d_rnn + 2 * d_conv
    cb_scr[:, top:top + seq, :] = (z_ref[:, :, o_gc:o_gc + d_conv].astype(F32)
                                   * z_ref[:, :, o_hb:o_hb + d_conv].astype(F32))
    base = top - (wb - 1)
    co = cb_scr[:, base:base + seq, :] * cbw_ref[0:1, :]
    for k in range(1, wb):
        co = co + cb_scr[:, base + k:base + k + seq, :] * cbw_ref[k:k + 1, :]
    yb_ref[...] = (z_ref[:, :, o_gb:o_gb + d_conv].astype(F32) * co).astype(BF16)
    nsb_ref[...] = cb_scr[:, top + seq - (wb - 1):top + seq, :]
    cb_scr[:, 0:top, :] = cb_scr[:, seq:seq + top, :]


def _mixer_pre(z, state_a, h0, state_b, conv_a_w, conv_a_b, w_ri, b_r, b_i, lam, conv_b_w, *, seq_tile, pos0):
    bsz, t, _ = z.shape
    d_rnn = conv_a_w.shape[1]
    d_conv = conv_b_w.shape[1]
    n_pre = 2 * d_rnn + 3 * d_conv
    nb = bsz if t <= seq_tile else 1
    seq = min(seq_tile, t)
    assert t % seq == 0 and seq % BF16_ROWS == 0 and bsz % nb == 0
    wa, wb = conv_a_w.shape[0], conv_b_w.shape[0]
    kern = functools.partial(_mixer_pre_kernel, d_rnn=d_rnn, d_conv=d_conv, pos0=pos0)
    row = lambda b, i: (b, i, 0)
    per_b = lambda b, i: (b, 0, 0)
    const2 = lambda b, i: (0, 0)
    return pl.pallas_call(
        kern,
        grid=(bsz // nb, t // seq),
        in_specs=[pl.BlockSpec((nb, seq, n_pre), row),
                  pl.BlockSpec((nb, wa - 1, d_rnn), per_b),
                  pl.BlockSpec((nb, 1, d_rnn), per_b),
                  pl.BlockSpec((nb, wb - 1, d_conv), per_b),
                  pl.BlockSpec(conv_a_w.shape, const2),
                  pl.BlockSpec((1, d_rnn), const2),
                  pl.BlockSpec(w_ri.shape, lambda b, i: (0, 0, 0)),
                  pl.BlockSpec((1, d_rnn), const2),
                  pl.BlockSpec((1, d_rnn), const2),
                  pl.BlockSpec((1, d_rnn), const2),
                  pl.BlockSpec(conv_b_w.shape, const2)],
        out_specs=[pl.BlockSpec((nb, seq, d_rnn), row),
                   pl.BlockSpec((nb, seq, d_conv), row),
                   pl.BlockSpec((nb, wa - 1, d_rnn), per_b),
                   pl.BlockSpec((nb, 1, d_rnn), per_b),
                   pl.BlockSpec((nb, wb - 1, d_conv), per_b)],
        out_shape=[jax.ShapeDtypeStruct((bsz, t, d_rnn), BF16),
                   jax.ShapeDtypeStruct((bsz, t, d_conv), BF16),
                   jax.ShapeDtypeStruct((bsz, wa - 1, d_rnn), F32),
                   jax.ShapeDtypeStruct((bsz, 1, d_rnn), F32),
                   jax.ShapeDtypeStruct((bsz, wb - 1, d_conv), F32)],
        scratch_shapes=[pltpu.VMEM((nb, SUBLANES + seq, d_rnn), F32),
                        pltpu.VMEM((nb, seq, d_rnn), F32),
                        pltpu.VMEM((nb * seq, d_rnn), BF16),
                        pltpu.VMEM((nb, seq, 2 * d_rnn), F32),
                        pltpu.VMEM((nb, SUBLANES + seq, d_conv), F32),
                        pltpu.VMEM((nb, SUBLANES, d_rnn), F32)],
        compiler_params=_params("arbitrary", "arbitrary"),
        name="mixer_pre",
    )(z, state_a, h0, state_b, conv_a_w, conv_a_b, w_ri, b_r, b_i, lam, conv_b_w)


def _mixer_post_kernel(ya_ref, yb_ref, ga0_ref, ga1_ref, gb0_ref, gb1_ref, x_ref, pa_ref, pb_ref, wo_ref, g_ref,
                       o_ref, mix_scr):
    half = ga0_ref.shape[1]
    y_a = jnp.dot(ya_ref[...], pa_ref[...], preferred_element_type=F32)
    y_b = jnp.dot(yb_ref[...], pb_ref[...], preferred_element_type=F32)
    for c, (ga_ref, gb_ref) in enumerate(((ga0_ref, gb0_ref), (ga1_ref, gb1_ref))):
        cols = slice(c * half, (c + 1) * half)
        mix = (jax.nn.sigmoid(ga_ref[...].astype(F32)) * y_a[:, cols]
               + jax.nn.sigmoid(gb_ref[...].astype(F32)) * y_b[:, cols])
        mix_scr[:, cols] = mix.astype(BF16)
    out = jnp.dot(mix_scr[...], wo_ref[...], preferred_element_type=F32)
    o_ref[...] = x_ref[...] + _rmsnorm(out, g_ref[...])


def _mixer_post(ya, yb, z, x, p_a, p_b, w_o, g_post, *, tm):
    m, d = x.shape
    d_rnn, d_conv = ya.shape[1], yb.shape[1]
    tm = min(tm, m)
    half = d // 2
    gate0 = (2 * d_rnn + 3 * d_conv) // half
    assert m % tm == 0 and (2 * d_rnn + 3 * d_conv) % half == 0
    gate_spec = lambda k: pl.BlockSpec((tm, half), lambda i: (i, gate0 + k))
    row = lambda i: (i, 0)
    return pl.pallas_call(
        _mixer_post_kernel,
        grid=(m // tm,),
        in_specs=[pl.BlockSpec((tm, d_rnn), row), pl.BlockSpec((tm, d_conv), row),
                  gate_spec(0), gate_spec(1), gate_spec(2), gate_spec(3),
                  pl.BlockSpec((tm, d), row),
                  _resident(p_a.shape), _resident(p_b.shape), _resident(w_o.shape),
                  pl.BlockSpec((1, d), lambda i: (0, 0))],
        out_specs=pl.BlockSpec((tm, d), row),
        out_shape=jax.ShapeDtypeStruct((m, d), F32),
        scratch_shapes=[pltpu.VMEM((tm, d), BF16)],
        compiler_params=_params("parallel"),
        name="mixer_post",
    )(ya, yb, z, z, z, z, x, p_a, p_b, w_o, g_post)


def _xattn_kernel(x_ref, k_ref, v_ref, wq_ref, wxo_ref, gpre_ref, gpost_ref, gnext_ref, o_ref, u_ref, q_scr, o_scr,
                  *, n_heads):
    nb, seq, d = x_ref.shape
    hd = d // n_heads
    scale = hd ** -0.5
    x = x_ref[...].reshape(nb * seq, d)
    u = _rmsnorm(x, gpre_ref[...]).astype(BF16)
    q_scr[...] = jnp.dot(u, wq_ref[...], preferred_element_type=F32).astype(BF16)
    for b in range(nb):
        rows = slice(b * seq, (b + 1) * seq)
        for h in range(n_heads):
            cols = slice(h * hd, (h + 1) * hd)
            s = lax.dot_general(q_scr[rows, cols], k_ref[b, :, cols], (((1,), (1,)), ((), ())),
                                preferred_element_type=F32) * scale
            p = jnp.exp(s - jnp.max(s, axis=-1, keepdims=True))
            att = (p / jnp.sum(p, axis=-1, keepdims=True)).astype(BF16)
            o_scr[rows, cols] = jnp.dot(att, v_ref[b, :, cols], preferred_element_type=F32).astype(BF16)
    out = jnp.dot(o_scr[...], wxo_ref[...], preferred_element_type=F32)
    x_new = x + _rmsnorm(out, gpost_ref[...])
    o_ref[...] = x_new.reshape(nb, seq, d)
    u_ref[...] = _rmsnorm(x_new, gnext_ref[...]).astype(BF16).reshape(nb, seq, d)


def _xattn(x, mem_k, mem_v, w_q, w_xo, g_pre, g_post, g_next, *, n_heads, seq_tile):
    bsz, t, d = x.shape
    n_mem = mem_k.shape[1]
    nb = bsz if t <= seq_tile else 1
    seq = min(seq_tile, t)
    assert t % seq == 0 and seq % BF16_ROWS == 0 and bsz % nb == 0
    single = bsz // nb == 1
    mem_spec = (_resident((nb, n_mem, d)) if single else pl.BlockSpec((nb, n_mem, d), lambda b, i: (b, 0, 0)))
    row = lambda b, i: (b, i, 0)
    gain = pl.BlockSpec((1, d), lambda b, i: (0, 0))
    return pl.pallas_call(
        functools.partial(_xattn_kernel, n_heads=n_heads),
        grid=(bsz // nb, t // seq),
        in_specs=[pl.BlockSpec((nb, seq, d), row), mem_spec, mem_spec,
                  _resident(w_q.shape), _resident(w_xo.shape), gain, gain, gain],
        out_specs=[pl.BlockSpec((nb, seq, d), row), pl.BlockSpec((nb, seq, d), row)],
        out_shape=[jax.ShapeDtypeStruct((bsz, t, d), F32), jax.ShapeDtypeStruct((bsz, t, d), BF16)],
        scratch_shapes=[pltpu.VMEM((nb * seq, d), BF16), pltpu.VMEM((nb * seq, d), BF16)],
        compiler_params=_params("parallel", "parallel"),
        name="xattn",
    )(x, mem_k, mem_v, w_q, w_xo, g_pre, g_post, g_next)


def _up_geglu_kernel(u_ref, wg_ref, wv_ref, sg_ref, sv_ref, cwg_ref, cwv_ref, cbg_ref, cbv_ref,
                     hid_ref, nsg_ref, nsv_ref, wg_scr, wv_scr, pg_scr, pv_scr):
    nb, seq, k_dim = u_ref.shape
    tn = hid_ref.shape[2]
    width = cwg_ref.shape[0]
    top = SUBLANES
    halves = ((wg_ref, wg_scr, sg_ref, pg_scr, cwg_ref, cbg_ref, nsg_ref),
              (wv_ref, wv_scr, sv_ref, pv_scr, cwv_ref, cbv_ref, nsv_ref))

    @pl.when(pl.program_id(1) == 0)
    def _():
        for w_ref, w_scr, st_ref, p_scr, _, _, _ in halves:
            _cast_rows(w_ref, w_scr)
            p_scr[...] = jnp.zeros(p_scr.shape, F32)
            p_scr[:, top - (width - 1):top, :] = st_ref[...]

    u = u_ref[...].reshape(nb * seq, k_dim)
    chunk = min(COL_CHUNK, tn)
    for c in range(tn // chunk):
        cols = slice(c * chunk, (c + 1) * chunk)
        ys = []
        for _, w_scr, _, p_scr, cw_ref, cb_ref, ns_ref in halves:
            a = jnp.dot(u, w_scr[:, cols], preferred_element_type=F32).reshape(nb, seq, chunk)
            hist = jnp.concatenate([p_scr[:, :, cols], a], axis=1)
            y = None
            for k in range(width):
                lag = width - 1 - k
                tap = a if lag == 0 else pltpu.roll(hist, lag, 1)[:, top:, :]
                y = tap * cw_ref[k:k + 1, cols] if y is None else y + tap * cw_ref[k:k + 1, cols]
            ys.append(y + cb_ref[:, cols])
            p_scr[:, :, cols] = a[:, seq - top:, :]
            ns_ref[:, :, cols] = p_scr[:, top - (width - 1):top, cols]
        hid_ref[:, :, cols] = (jax.nn.gelu(ys[0], approximate=True) * ys[1]).astype(BF16)


def _up_geglu(u, w_up, state_f, conv_w, conv_b, *, tm, tn):
    bsz, t, d = u.shape
    d_ff = w_up.shape[1] // 2
    width = conv_w.shape[0]
    nb = bsz if t <= tm else 1
    seq = min(tm, t)
    n_j = d_ff // tn
    assert bsz == nb and t % seq == 0 and seq % BF16_ROWS == 0 and d_ff % tn == 0 and tn % min(COL_CHUNK, tn) == 0
    col_g = lambda j, i: (0, j)
    col_v = lambda j, i: (0, n_j + j)
    st_g = lambda j, i: (0, 0, j)
    st_v = lambda j, i: (0, 0, n_j + j)
    return pl.pallas_call(
        _up_geglu_kernel,
        grid=(n_j, t // seq),
        in_specs=[pl.BlockSpec((nb, seq, d), lambda j, i: (0, i, 0)),
                  pl.BlockSpec((d, tn), col_g), pl.BlockSpec((d, tn), col_v),
                  pl.BlockSpec((nb, width - 1, tn), st_g), pl.BlockSpec((nb, width - 1, tn), st_v),
                  pl.BlockSpec((width, tn), col_g), pl.BlockSpec((width, tn), col_v),
                  pl.BlockSpec((1, tn), col_g), pl.BlockSpec((1, tn), col_v)],
        out_specs=[pl.BlockSpec((nb, seq, tn), lambda j, i: (0, i, j)),
                   pl.BlockSpec((nb, width - 1, tn), st_g), pl.BlockSpec((nb, width - 1, tn), st_g)],
        out_shape=[jax.ShapeDtypeStruct((bsz, t, d_ff), BF16),
                   jax.ShapeDtypeStruct((bsz, width - 1, d_ff), F32),
                   jax.ShapeDtypeStruct((bsz, width - 1, d_ff), F32)],
        scratch_shapes=[pltpu.VMEM((d, tn), BF16), pltpu.VMEM((d, tn), BF16),
                        pltpu.VMEM((nb, SUBLANES, tn), F32), pltpu.VMEM((nb, SUBLANES, tn), F32)],
        compiler_params=_params("arbitrary", "arbitrary"),
        name="up_geglu",
    )(u, w_up, w_up, state_f, state_f, conv_w, conv_w, conv_b, conv_b)


def _ffn_down_kernel(hid_ref, x_ref, wd_ref, g_ref, o_ref):
    y = jnp.dot(hid_ref[...], wd_ref[...], preferred_element_type=F32)
    o_ref[...] = x_ref[...] + _rmsnorm(y, g_ref[...])


def _ffn_down(hid, x, w_down, g_post, *, tm):
    m, d = x.shape
    d_ff = hid.shape[1]
    tm = min(tm, m)
    assert m % tm == 0
    row = lambda i: (i, 0)
    return pl.pallas_call(
        _ffn_down_kernel,
        grid=(m // tm,),
        in_specs=[pl.BlockSpec((tm, d_ff), row), pl.BlockSpec((tm, d), row), _resident(w_down.shape),
                  pl.BlockSpec((1, d), lambda i: (0, 0))],
        out_specs=pl.BlockSpec((tm, d), row),
        out_shape=jax.ShapeDtypeStruct((m, d), F32),
        compiler_params=_params("parallel"),
        name="ffn_down",
    )(hid, x, w_down, g_post)


MM_TM, MM_TN = 1024, 1024
UP_TN = 512
SEQ_TILE = 256
NORM_TM = 512
N_XHEADS = 4


def _layer(x, pos0, mem_k, mem_v, s_a, h0, s_b, s_f, p):
    bsz, t, d = x.shape
    m = bsz * t
    x2 = x.reshape(m, d)
    u0 = _rmsnorm_call(x2, p['g_mix_pre'], tm=NORM_TM)
    z = _proj(u0, p['w_in'], BF16, tm=MM_TM, tn=MM_TN)
    ya, yb, ns_a, nh, ns_b = _mixer_pre(z.reshape(bsz, t, -1), s_a, h0[:, None, :], s_b,
                                        p['conv_a_w'], p['conv_a_b'], p['w_ri'], p['b_r'], p['b_i'],
                                        p['lru_lambda'], p['conv_b_w'], seq_tile=SEQ_TILE, pos0=pos0)
    x1 = _mixer_post(ya.reshape(m, -1), yb.reshape(m, -1), z, x2, p['p_a'], p['p_b'], p['w_o'],
                     p['g_mix_post'], tm=SEQ_TILE)
    x2a, u3 = _xattn(x1.reshape(bsz, t, d), mem_k, mem_v, p['w_q'], p['w_xo'], p['g_x_pre'], p['g_x_post'],
                     p['g_ffn_pre'], n_heads=N_XHEADS, seq_tile=SEQ_TILE)
    hid, ns_fg, ns_fv = _up_geglu(u3, p['w_up'], s_f, p['ffn_conv_w'], p['ffn_conv_b'], tm=MM_TM, tn=UP_TN)
    x3 = _ffn_down(hid.reshape(m, -1), x2a.reshape(m, d), p['w_down'], p['g_ffn_post'], tm=SEQ_TILE)
    return x3.reshape(bsz, t, d), ns_a, nh[:, 0, :], ns_b, jnp.concatenate([ns_fg, ns_fv], axis=-1)


def kernel(x_prompt, x_sample, mem_prompt, state_conv_a, state_rglru, state_conv_b, state_ffn_conv, cache_mem_k, cache_mem_v, g_mix_pre, g_mix_post, w_in, conv_a_w, conv_a_b, w_r, b_r, w_i, b_i, lru_lambda, conv_b_w, p_a, p_b, w_o, g_x_pre, g_x_post, g_mem, w_q, w_k, w_v, w_xo, g_ffn_pre, g_ffn_post, w_up, ffn_conv_w, ffn_conv_b, w_down):
    depth = w_in.shape[0]
    bsz, _, d = x_prompt.shape
    n_mem = mem_prompt.shape[1]
    yp, ys = x_prompt, x_sample
    outs = [[] for _ in range(10)]
    row = lambda v: v.reshape(1, -1).astype(F32)
    for l in range(depth):
        p = {'g_mix_pre': row(g_mix_pre[l]), 'g_mix_post': row(g_mix_post[l]), 'w_in': w_in[l],
             'conv_a_w': conv_a_w[l], 'conv_a_b': row(conv_a_b[l]),
             'w_ri': jnp.concatenate([w_r[l], w_i[l]], axis=-1).astype(BF16),
             'b_r': row(b_r[l]), 'b_i': row(b_i[l]), 'lru_lambda': row(lru_lambda[l]),
             'conv_b_w': conv_b_w[l], 'p_a': p_a[l].astype(BF16), 'p_b': p_b[l].astype(BF16),
             'w_o': w_o[l].astype(BF16), 'g_x_pre': row(g_x_pre[l]), 'g_x_post': row(g_x_post[l]),
             'w_q': w_q[l].astype(BF16), 'w_xo': w_xo[l].astype(BF16),
             'g_ffn_pre': row(g_ffn_pre[l]), 'g_ffn_post': row(g_ffn_post[l]), 'w_up': w_up[l],
             'ffn_conv_w': ffn_conv_w[l], 'ffn_conv_b': row(ffn_conv_b[l]), 'w_down': w_down[l].astype(BF16)}
        d_rnn, d_conv, d_up = conv_a_w.shape[2], conv_b_w.shape[2], ffn_conv_w.shape[2]
        mem_u = _rmsnorm_call(mem_prompt.reshape(bsz * n_mem, d), row(g_mem[l]), tm=NORM_TM)
        mk = _proj(mem_u, w_k[l], F32, tm=MM_TM, tn=MM_TN).reshape(bsz, n_mem, d)
        mv = _proj(mem_u, w_v[l], F32, tm=MM_TM, tn=MM_TN).reshape(bsz, n_mem, d)
        zeros = lambda *s: jnp.zeros(s, F32)
        yp, na, nh, nb_, nf = _layer(yp, 0, mk.astype(BF16), mv.astype(BF16),
                                     zeros(bsz, conv_a_w.shape[1] - 1, d_rnn), zeros(bsz, d_rnn),
                                     zeros(bsz, conv_b_w.shape[1] - 1, d_conv),
                                     zeros(bsz, ffn_conv_w.shape[1] - 1, d_up), p)
        for o, v in zip(outs[:6], (na, nh, nb_, nf, mk.reshape(bsz, n_mem, N_XHEADS, -1),
                                   mv.reshape(bsz, n_mem, N_XHEADS, -1))):
            o.append(v)
        dec_b = x_sample.shape[0]
        ck = cache_mem_k[l].reshape(dec_b, n_mem, d).astype(BF16)
        cv = cache_mem_v[l].reshape(dec_b, n_mem, d).astype(BF16)
        ys, na, nh, nb_, nf = _layer(ys, PAST_LEN, ck, cv, state_conv_a[l], state_rglru[l], state_conv_b[l],
                                     state_ffn_conv[l], p)
        for o, v in zip(outs[6:], (na, nh, nb_, nf)):
            o.append(v)
    return (yp, ys) + tuple(jnp.stack(o) for o in outs)
```

```python
import functools

import jax
import jax.numpy as jnp
from jax import lax
from jax.experimental import pallas as pl
from jax.experimental.pallas import tpu as pltpu

F32 = jnp.float32
BF16 = jnp.bfloat16

EPS = 1e-6
RG_C = 8.0
PAST_LEN = 2048

LANES = 128
SUBLANES = 8
BF16_ROWS = 16
MXU_COLS = 256
VMEM_LIMIT_BYTES = 56 * 1024 * 1024

ROW_CHUNK = 128
CAST_ROWS = 256
COL_CHUNK = MXU_COLS
SCAN_LANES = 512


def _params(*semantics):
    return pltpu.CompilerParams(dimension_semantics=semantics, vmem_limit_bytes=VMEM_LIMIT_BYTES)


def _resident(shape):
    nd = len(shape)
    return pl.BlockSpec(shape, lambda *_: (0,) * nd, pipeline_mode=pl.Buffered(1))


def _rmsnorm(x, g):
    y = x * lax.rsqrt(jnp.mean(x * x, axis=-1, keepdims=True) + EPS)
    return y * g


def _sigmoid(x):
    return 0.5 * jnp.tanh(0.5 * x) + 0.5


def _gelu_tanh(x):
    c1 = (2.0 / jnp.pi) ** 0.5
    return x * (0.5 * jnp.tanh(x * (c1 + (c1 * 0.044715) * (x * x))) + 0.5)


def _causal_conv(x, before, w, b):
    nb, t, c = x.shape
    width = w.shape[0]
    groups = t // SUBLANES
    full = jnp.concatenate([before, x], axis=1).reshape(nb * (groups + 1), SUBLANES, c)
    row = lax.broadcasted_iota(jnp.int32, (1, 1, SUBLANES, c), 2)
    y = None
    for k in range(width):
        lag = width - 1 - k
        if lag == 0:
            tap = x
        else:
            rot = pltpu.roll(full, lag, 1).reshape(nb, groups + 1, SUBLANES, c)
            tap = jnp.where(row >= lag, rot[:, 1:], rot[:, :-1]).reshape(nb, t, c)
        y = tap * w[k:k + 1, :] if y is None else y + tap * w[k:k + 1, :]
    return y if b is None else y + b


def _softplus(x):
    return jnp.maximum(x, 0.0) + jnp.log1p(jnp.exp(-jnp.abs(x)))


def _rmsnorm_kernel(x_ref, g_ref, u_ref):
    tm = x_ref.shape[0]
    chunk = min(ROW_CHUNK, tm)
    g = g_ref[...]

    def body(c, _):
        rows = pl.ds(pl.multiple_of(c * chunk, chunk), chunk)
        u_ref[rows, :] = _rmsnorm(x_ref[rows, :], g).astype(BF16)
        return None

    lax.fori_loop(0, tm // chunk, body, None)


def _rmsnorm_call(x, g, *, tm):
    m, d = x.shape
    tm = min(tm, m)
    assert m % tm == 0 and tm % min(ROW_CHUNK, tm) == 0
    return pl.pallas_call(
        _rmsnorm_kernel,
        grid=(m // tm,),
        in_specs=[pl.BlockSpec((tm, d), lambda i: (i, 0)), pl.BlockSpec((1, d), lambda i: (0, 0))],
        out_specs=pl.BlockSpec((tm, d), lambda i: (i, 0)),
        out_shape=jax.ShapeDtypeStruct((m, d), BF16),
        compiler_params=_params("parallel"),
        name="rmsnorm",
    )(x, g)


def _cast_rows(src_ref, dst_ref):
    rows_total = src_ref.shape[0]
    chunk = min(CAST_ROWS, rows_total)

    def body(c, _):
        rows = pl.ds(pl.multiple_of(c * chunk, chunk), chunk)
        dst_ref[rows, :] = src_ref[rows, :].astype(BF16)
        return None

    lax.fori_loop(0, rows_total // chunk, body, None)


def _proj_kernel(u_ref, w_ref, o_ref, w_scr):
    @pl.when(pl.program_id(1) == 0)
    def _():
        _cast_rows(w_ref, w_scr)

    o_ref[...] = jnp.dot(u_ref[...], w_scr[...], preferred_element_type=F32).astype(o_ref.dtype)


def _proj(u, w, out_dtype, *, tm, tn):
    m, d = u.shape
    n = w.shape[1]
    tm, tn = min(tm, m), min(tn, n)
    assert m % tm == 0 and n % tn == 0 and d % min(CAST_ROWS, d) == 0
    return pl.pallas_call(
        _proj_kernel,
        grid=(n // tn, m // tm),
        in_specs=[pl.BlockSpec((tm, d), lambda j, i: (i, 0)),
                  pl.BlockSpec((d, tn), lambda j, i: (0, j))],
        out_specs=pl.BlockSpec((tm, tn), lambda j, i: (i, j)),
        out_shape=jax.ShapeDtypeStruct((m, n), out_dtype),
        scratch_shapes=[pltpu.VMEM((d, tn), BF16)],
        compiler_params=_params("arbitrary", "arbitrary"),
        name="proj",
    )(u, w)


def _mixer_pre_kernel(z_ref, sa_ref, h0_ref, sb_ref, caw_ref, cab_ref, wri_ref, br_ref, bi_ref, lam_ref, cbw_ref,
                      ya_ref, yb_ref, nsa_ref, nh_ref, nsb_ref,
                      xa_scr, xc_scr, xcb_scr, ri_scr, cb_scr, h_scr, *, d_rnn, d_conv, pos0):
    t = pl.program_id(1)
    nb, seq, _ = ya_ref.shape
    wa = caw_ref.shape[0]
    wb = cbw_ref.shape[0]
    n_heads, head_dim, _ = wri_ref.shape
    top = SUBLANES

    @pl.when(t == 0)
    def _():
        xa_scr[:, top - (wa - 1):top, :] = sa_ref[...]
        cb_scr[:, top - (wb - 1):top, :] = sb_ref[...]
        h_scr[...] = jnp.broadcast_to(h0_ref[...], h_scr.shape)

    xa_scr[:, top:top + seq, :] = z_ref[:, :, 0:d_rnn].astype(F32)
    base = top - (wa - 1)
    xc = xa_scr[:, base:base + seq, :] * caw_ref[0:1, :]
    for k in range(1, wa):
        xc = xc + xa_scr[:, base + k:base + k + seq, :] * caw_ref[k:k + 1, :]
    xc = xc + cab_ref[...]
    xc_scr[...] = xc
    xcb_scr[...] = xc.reshape(nb * seq, d_rnn).astype(BF16)
    nsa_ref[...] = xa_scr[:, top + seq - (wa - 1):top + seq, :]
    xa_scr[:, 0:top, :] = xa_scr[:, seq:seq + top, :]

    for h in range(n_heads):
        cols = slice(h * head_dim, (h + 1) * head_dim)
        ri = jnp.dot(xcb_scr[:, cols], wri_ref[h], preferred_element_type=F32)
        ri_scr[:, :, cols] = ri[:, :head_dim].reshape(nb, seq, head_dim)
        ri_scr[:, :, d_rnn + h * head_dim:d_rnn + (h + 1) * head_dim] = ri[:, head_dim:].reshape(nb, seq, head_dim)

    strip = max(LANES, SCAN_LANES // nb)
    rows_per_iter = BF16_ROWS
    row_in_group = lax.broadcasted_iota(jnp.int32, (1, SUBLANES, strip), 1)
    for c in range(d_rnn // strip):
        cs = slice(c * strip, (c + 1) * strip)
        cs_i = slice(d_rnn + c * strip, d_rnn + (c + 1) * strip)
        cs_g = slice(d_rnn + c * strip, d_rnn + (c + 1) * strip)
        sp = _softplus(-lam_ref[:, cs])
        b_r = br_ref[:, cs]
        b_i = bi_ref[:, cs]

        def body(it, carry, cs=cs, cs_i=cs_i, cs_g=cs_g, sp=sp, b_r=b_r, b_i=b_i):
            r0 = pl.multiple_of(it * rows_per_iter, rows_per_iter)
            rows = pl.ds(r0, rows_per_iter)
            r = jax.nn.sigmoid(ri_scr[:, rows, cs] + b_r)
            i = jax.nn.sigmoid(ri_scr[:, rows, cs_i] + b_i)
            log_a = -RG_C * r * sp
            a = jnp.exp(log_a)
            mult = jnp.sqrt(-jnp.tanh(log_a) * (a * a + 1.0))
            if pos0 == 0:
                pos = t * seq + r0 + lax.broadcasted_iota(jnp.int32, (1, rows_per_iter, strip), 1)
                mult = jnp.where(pos == 0, 1.0, mult)
            b = mult * i * xc_scr[:, rows, cs]
            hs = []
            for sub in range(rows_per_iter // SUBLANES):
                a8 = a[:, sub * SUBLANES:(sub + 1) * SUBLANES, :]
                b8 = b[:, sub * SUBLANES:(sub + 1) * SUBLANES, :]
                for s in (1, 2, 4):
                    keep = row_in_group >= s
                    a_prev = pltpu.roll(a8, s, 1)
                    b_prev = pltpu.roll(b8, s, 1)
                    b8 = jnp.where(keep, a8 * b_prev + b8, b8)
                    a8 = jnp.where(keep, a8 * a_prev, a8)
                h8 = a8 * carry + b8
                carry = jnp.broadcast_to(h8[:, SUBLANES - 1:SUBLANES, :], h8.shape)
                hs.append(h8)
            h = jnp.concatenate(hs, axis=1)
            gate = jax.nn.gelu(z_ref[:, rows, cs_g].astype(F32), approximate=True)
            ya_ref[:, rows, cs] = (h * gate).astype(BF16)
            return carry

        last = lax.fori_loop(0, seq // rows_per_iter, body, h_scr[:, :, cs])
        h_scr[:, :, cs] = last
        nh_ref[:, :, cs] = last[:, 0:1, :]

    o_gb, o_gc, o_hb = 2 * d_rnn, 2 * d_rnn + d_conv, 2 * d_rnn + 2 * d_conv
    cb_scr[:, top:top + seq, :] = (z_ref[:, :, o_gc:o_gc + d_conv].astype(F32)
                                   * z_ref[:, :, o_hb:o_hb + d_conv].astype(F32))
    base = top - (wb - 1)
    co = cb_scr[:, base:base + seq, :] * cbw_ref[0:1, :]
    for k in range(1, wb):
        co = co + cb_scr[:, base + k:base + k + seq, :] * cbw_ref[k:k + 1, :]
    yb_ref[...] = (z_ref[:, :, o_gb:o_gb + d_conv].astype(F32) * co).astype(BF16)
    nsb_ref[...] = cb_scr[:, top + seq - (wb - 1):top + seq, :]
    cb_scr[:, 0:top, :] = cb_scr[:, seq:seq + top, :]


def _mixer_pre(z, state_a, h0, state_b, conv_a_w, conv_a_b, w_ri, b_r, b_i, lam, conv_b_w, *, seq_tile, pos0):
    bsz, t, _ = z.shape
    d_rnn = conv_a_w.shape[1]
    d_conv = conv_b_w.shape[1]
    n_pre = 2 * d_rnn + 3 * d_conv
    nb = bsz if t <= seq_tile else 1
    seq = min(seq_tile, t)
    assert t % seq == 0 and seq % BF16_ROWS == 0 and bsz % nb == 0
    wa, wb = conv_a_w.shape[0], conv_b_w.shape[0]
    kern = functools.partial(_mixer_pre_kernel, d_rnn=d_rnn, d_conv=d_conv, pos0=pos0)
    row = lambda b, i: (b, i, 0)
    per_b = lambda b, i: (b, 0, 0)
    const2 = lambda b, i: (0, 0)
    return pl.pallas_call(
        kern,
        grid=(bsz // nb, t // seq),
        in_specs=[pl.BlockSpec((nb, seq, n_pre), row),
                  pl.BlockSpec((nb, wa - 1, d_rnn), per_b),
                  pl.BlockSpec((nb, 1, d_rnn), per_b),
                  pl.BlockSpec((nb, wb - 1, d_conv), per_b),
                  pl.BlockSpec(conv_a_w.shape, const2),
                  pl.BlockSpec((1, d_rnn), const2),
                  pl.BlockSpec(w_ri.shape, lambda b, i: (0, 0, 0)),
                  pl.BlockSpec((1, d_rnn), const2),
                  pl.BlockSpec((1, d_rnn), const2),
                  pl.BlockSpec((1, d_rnn), const2),
                  pl.BlockSpec(conv_b_w.shape, const2)],
        out_specs=[pl.BlockSpec((nb, seq, d_rnn), row),
                   pl.BlockSpec((nb, seq, d_conv), row),
                   pl.BlockSpec((nb, wa - 1, d_rnn), per_b),
                   pl.BlockSpec((nb, 1, d_rnn), per_b),
                   pl.BlockSpec((nb, wb - 1, d_conv), per_b)],
        out_shape=[jax.ShapeDtypeStruct((bsz, t, d_rnn), BF16),
                   jax.ShapeDtypeStruct((bsz, t, d_conv), BF16),
                   jax.ShapeDtypeStruct((bsz, wa - 1, d_rnn), F32),
                   jax.ShapeDtypeStruct((bsz, 1, d_rnn), F32),
                   jax.ShapeDtypeStruct((bsz, wb - 1, d_conv), F32)],
        scratch_shapes=[pltpu.VMEM((nb, SUBLANES + seq, d_rnn), F32),
                        pltpu.VMEM((nb, seq, d_rnn), F32),
                        pltpu.VMEM((nb * seq, d_rnn), BF16),
                        pltpu.VMEM((nb, seq, 2 * d_rnn), F32),
                        pltpu.VMEM((nb, SUBLANES + seq, d_conv), F32),
                        pltpu.VMEM((nb, SUBLANES, d_rnn), F32)],
        compiler_params=_params("arbitrary", "arbitrary"),
        name="mixer_pre",
    )(z, state_a, h0, state_b, conv_a_w, conv_a_b, w_ri, b_r, b_i, lam, conv_b_w)


def _mixer_post_kernel(ya_ref, yb_ref, ga0_ref, ga1_ref, gb0_ref, gb1_ref, x_ref, pa_ref, pb_ref, wo_ref, g_ref,
                       o_ref, mix_scr):
    half = ga0_ref.shape[1]
    y_a = jnp.dot(ya_ref[...], pa_ref[...], preferred_element_type=F32)
    y_b = jnp.dot(yb_ref[...], pb_ref[...], preferred_element_type=F32)
    for c, (ga_ref, gb_ref) in enumerate(((ga0_ref, gb0_ref), (ga1_ref, gb1_ref))):
        cols = slice(c * half, (c + 1) * half)
        mix = (jax.nn.sigmoid(ga_ref[...].astype(F32)) * y_a[:, cols]
               + jax.nn.sigmoid(gb_ref[...].astype(F32)) * y_b[:, cols])
        mix_scr[:, cols] = mix.astype(BF16)
    out = jnp.dot(mix_scr[...], wo_ref[...], preferred_element_type=F32)
    o_ref[...] = x_ref[...] + _rmsnorm(out, g_ref[...])


def _mixer_post(ya, yb, z, x, p_a, p_b, w_o, g_post, *, tm):
    m, d = x.shape
    d_rnn, d_conv = ya.shape[1], yb.shape[1]
    tm = min(tm, m)
    half = d // 2
    gate0 = (2 * d_rnn + 3 * d_conv) // half
    assert m % tm == 0 and (2 * d_rnn + 3 * d_conv) % half == 0
    gate_spec = lambda k: pl.BlockSpec((tm, half), lambda i: (i, gate0 + k))
    row = lambda i: (i, 0)
    return pl.pallas_call(
        _mixer_post_kernel,
        grid=(m // tm,),
        in_specs=[pl.BlockSpec((tm, d_rnn), row), pl.BlockSpec((tm, d_conv), row),
                  gate_spec(0), gate_spec(1), gate_spec(2), gate_spec(3),
                  pl.BlockSpec((tm, d), row),
                  _resident(p_a.shape), _resident(p_b.shape), _resident(w_o.shape),
                  pl.BlockSpec((1, d), lambda i: (0, 0))],
        out_specs=pl.BlockSpec((tm, d), row),
        out_shape=jax.ShapeDtypeStruct((m, d), F32),
        scratch_shapes=[pltpu.VMEM((tm, d), BF16)],
        compiler_params=_params("parallel"),
        name="mixer_post",
    )(ya, yb, z, z, z, z, x, p_a, p_b, w_o, g_post)


def _xattn_kernel(x_ref, k_ref, v_ref, wq_ref, wxo_ref, gpre_ref, gpost_ref, gnext_ref, o_ref, u_ref, q_scr, o_scr,
                  *, n_heads):
    nb, seq, d = x_ref.shape
    hd = d // n_heads
    scale = hd ** -0.5
    x = x_ref[...].reshape(nb * seq, d)
    u = _rmsnorm(x, gpre_ref[...]).astype(BF16)
    q_scr[...] = jnp.dot(u, wq_ref[...], preferred_element_type=F32).astype(BF16)
    for b in range(nb):
        rows = slice(b * seq, (b + 1) * seq)
        for h in range(n_heads):
            cols = slice(h * hd, (h + 1) * hd)
            s = lax.dot_general(q_scr[rows, cols], k_ref[b, :, cols], (((1,), (1,)), ((), ())),
                                preferred_element_type=F32) * scale
            p = jnp.exp(s - jnp.max(s, axis=-1, keepdims=True))
            att = (p / jnp.sum(p, axis=-1, keepdims=True)).astype(BF16)
            o_scr[rows, cols] = jnp.dot(att, v_ref[b, :, cols], preferred_element_type=F32).astype(BF16)
    out = jnp.dot(o_scr[...], wxo_ref[...], preferred_element_type=F32)
    x_new = x + _rmsnorm(out, gpost_ref[...])
    o_ref[...] = x_new.reshape(nb, seq, d)
    u_ref[...] = _rmsnorm(x_new, gnext_ref[...]).astype(BF16).reshape(nb, seq, d)


def _xattn(x, mem_k, mem_v, w_q, w_xo, g_pre, g_post, g_next, *, n_heads, seq_tile):
    bsz, t, d = x.shape
    n_mem = mem_k.shape[1]
    nb = bsz if t <= seq_tile else 1
    seq = min(seq_tile, t)
    assert t % seq == 0 and seq % BF16_ROWS == 0 and bsz % nb == 0
    single = bsz // nb == 1
    mem_spec = (_resident((nb, n_mem, d)) if single else pl.BlockSpec((nb, n_mem, d), lambda b, i: (b, 0, 0)))
    row = lambda b, i: (b, i, 0)
    gain = pl.BlockSpec((1, d), lambda b, i: (0, 0))
    return pl.pallas_call(
        functools.partial(_xattn_kernel, n_heads=n_heads),
        grid=(bsz // nb, t // seq),
        in_specs=[pl.BlockSpec((nb, seq, d), row), mem_spec, mem_spec,
                  _resident(w_q.shape), _resident(w_xo.shape), gain, gain, gain],
        out_specs=[pl.BlockSpec((nb, seq, d), row), pl.BlockSpec((nb, seq, d), row)],
        out_shape=[jax.ShapeDtypeStruct((bsz, t, d), F32), jax.ShapeDtypeStruct((bsz, t, d), BF16)],
        scratch_shapes=[pltpu.VMEM((nb * seq, d), BF16), pltpu.VMEM((nb * seq, d), BF16)],
        compiler_params=_params("parallel", "parallel"),
        name="xattn",
    )(x, mem_k, mem_v, w_q, w_xo, g_pre, g_post, g_next)


def _up_geglu_kernel(u_ref, wg_ref, wv_ref, sg_ref, sv_ref, cwg_ref, cwv_ref, cbg_ref, cbv_ref,
                     hid_ref, nsg_ref, nsv_ref, wg_scr, wv_scr, pg_scr, pv_scr):
    nb, seq, k_dim = u_ref.shape
    tn = hid_ref.shape[2]
    width = cwg_ref.shape[0]
    top = SUBLANES
    halves = ((wg_ref, wg_scr, sg_ref, pg_scr, cwg_ref, cbg_ref, nsg_ref),
              (wv_ref, wv_scr, sv_ref, pv_scr, cwv_ref, cbv_ref, nsv_ref))

    @pl.when(pl.program_id(1) == 0)
    def _():
        for w_ref, w_scr, st_ref, p_scr, _, _, _ in halves:
            _cast_rows(w_ref, w_scr)
            p_scr[...] = jnp.zeros(p_scr.shape, F32)
            p_scr[:, top - (width - 1):top, :] = st_ref[...]

    chunk = min(COL_CHUNK, tn)
    u = u_ref[...].reshape(nb * seq, k_dim)
    for c in range(tn // chunk):
        cols = slice(c * chunk, (c + 1) * chunk)
        ys = []
        for _, w_scr, _, p_scr, cw_ref, cb_ref, ns_ref in halves:
            a = jnp.dot(u, w_scr[:, cols], preferred_element_type=F32).reshape(nb, seq, chunk)
            ys.append(_causal_conv(a, p_scr[:, :, cols], cw_ref[:, cols], cb_ref[:, cols]))
            p_scr[:, :, cols] = a[:, seq - top:, :]
            ns_ref[:, :, cols] = p_scr[:, top - (width - 1):top, cols]
        hid_ref[:, :, cols] = (_gelu_tanh(ys[0]) * ys[1]).astype(BF16)


def _up_geglu(u, w_up, state_f, conv_w, conv_b, *, tm, tn):
    bsz, t, d = u.shape
    d_ff = w_up.shape[1] // 2
    width = conv_w.shape[0]
    nb = bsz if t <= tm else 1
    seq = min(tm, t)
    n_j = d_ff // tn
    assert bsz == nb and t % seq == 0 and seq % BF16_ROWS == 0 and d_ff % tn == 0 and tn % min(COL_CHUNK, tn) == 0
    col_g = lambda j, i: (0, j)
    col_v = lambda j, i: (0, n_j + j)
    st_g = lambda j, i: (0, 0, j)
    st_v = lambda j, i: (0, 0, n_j + j)
    return pl.pallas_call(
        _up_geglu_kernel,
        grid=(n_j, t // seq),
        in_specs=[pl.BlockSpec((nb, seq, d), lambda j, i: (0, i, 0)),
                  pl.BlockSpec((d, tn), col_g), pl.BlockSpec((d, tn), col_v),
                  pl.BlockSpec((nb, width - 1, tn), st_g), pl.BlockSpec((nb, width - 1, tn), st_v),
                  pl.BlockSpec((width, tn), col_g), pl.BlockSpec((width, tn), col_v),
                  pl.BlockSpec((1, tn), col_g), pl.BlockSpec((1, tn), col_v)],
        out_specs=[pl.BlockSpec((nb, seq, tn), lambda j, i: (0, i, j)),
                   pl.BlockSpec((nb, width - 1, tn), st_g), pl.BlockSpec((nb, width - 1, tn), st_g)],
        out_shape=[jax.ShapeDtypeStruct((bsz, t, d_ff), BF16),
                   jax.ShapeDtypeStruct((bsz, width - 1, d_ff), F32),
                   jax.ShapeDtypeStruct((bsz, width - 1, d_ff), F32)],
        scratch_shapes=[pltpu.VMEM((d, tn), BF16), pltpu.VMEM((d, tn), BF16),
                        pltpu.VMEM((nb, SUBLANES, tn), F32), pltpu.VMEM((nb, SUBLANES, tn), F32)],
        compiler_params=_params("arbitrary", "arbitrary"),
        name="up_geglu",
    )(u, w_up, w_up, state_f, state_f, conv_w, conv_w, conv_b, conv_b)


def _ffn_down_kernel(hid_ref, x_ref, wd_ref, g_ref, o_ref):
    y = jnp.dot(hid_ref[...], wd_ref[...], preferred_element_type=F32)
    o_ref[...] = x_ref[...] + _rmsnorm(y, g_ref[...])


def _ffn_down(hid, x, w_down, g_post, *, tm):
    m, d = x.shape
    d_ff = hid.shape[1]
    tm = min(tm, m)
    assert m % tm == 0
    row = lambda i: (i, 0)
    return pl.pallas_call(
        _ffn_down_kernel,
        grid=(m // tm,),
        in_specs=[pl.BlockSpec((tm, d_ff), row), pl.BlockSpec((tm, d), row), _resident(w_down.shape),
                  pl.BlockSpec((1, d), lambda i: (0, 0))],
        out_specs=pl.BlockSpec((tm, d), row),
        out_shape=jax.ShapeDtypeStruct((m, d), F32),
        compiler_params=_params("parallel"),
        name="ffn_down",
    )(hid, x, w_down, g_post)


MM_TM, MM_TN = 2048, 1024
UP_TM, UP_TN = 1024, 512
SEQ_TILE = 256
ROW_TILE = 512
NORM_TM = 512
N_XHEADS = 4


def _layer(x, pos0, mem_k, mem_v, s_a, h0, s_b, s_f, p):
    bsz, t, d = x.shape
    m = bsz * t
    x2 = x.reshape(m, d)
    u0 = _rmsnorm_call(x2, p['g_mix_pre'], tm=NORM_TM)
    z = _proj(u0, p['w_in'], BF16, tm=MM_TM, tn=MM_TN)
    ya, yb, ns_a, nh, ns_b = _mixer_pre(z.reshape(bsz, t, -1), s_a, h0[:, None, :], s_b,
                                        p['conv_a_w'], p['conv_a_b'], p['w_ri'], p['b_r'], p['b_i'],
                                        p['lru_lambda'], p['conv_b_w'], seq_tile=SEQ_TILE, pos0=pos0)
    x1 = _mixer_post(ya.reshape(m, -1), yb.reshape(m, -1), z, x2, p['p_a'], p['p_b'], p['w_o'],
                     p['g_mix_post'], tm=ROW_TILE)
    x2a, u3 = _xattn(x1.reshape(bsz, t, d), mem_k, mem_v, p['w_q'], p['w_xo'], p['g_x_pre'], p['g_x_post'],
                     p['g_ffn_pre'], n_heads=N_XHEADS, seq_tile=ROW_TILE)
    hid, ns_fg, ns_fv = _up_geglu(u3, p['w_up'], s_f, p['ffn_conv_w'], p['ffn_conv_b'], tm=UP_TM, tn=UP_TN)
    x3 = _ffn_down(hid.reshape(m, -1), x2a.reshape(m, d), p['w_down'], p['g_ffn_post'], tm=ROW_TILE)
    return x3.reshape(bsz, t, d), ns_a, nh[:, 0, :], ns_b, jnp.concatenate([ns_fg, ns_fv], axis=-1)


def kernel(x_prompt, x_sample, mem_prompt, state_conv_a, state_rglru, state_conv_b, state_ffn_conv, cache_mem_k, cache_mem_v, g_mix_pre, g_mix_post, w_in, conv_a_w, conv_a_b, w_r, b_r, w_i, b_i, lru_lambda, conv_b_w, p_a, p_b, w_o, g_x_pre, g_x_post, g_mem, w_q, w_k, w_v, w_xo, g_ffn_pre, g_ffn_post, w_up, ffn_conv_w, ffn_conv_b, w_down):
    depth = w_in.shape[0]
    bsz, _, d = x_prompt.shape
    n_mem = mem_prompt.shape[1]
    yp, ys = x_prompt, x_sample
    outs = [[] for _ in range(10)]
    row = lambda v: v.reshape(1, -1).astype(F32)
    for l in range(depth):
        p = {'g_mix_pre': row(g_mix_pre[l]), 'g_mix_post': row(g_mix_post[l]), 'w_in': w_in[l],
             'conv_a_w': conv_a_w[l], 'conv_a_b': row(conv_a_b[l]),
             'w_ri': jnp.concatenate([w_r[l], w_i[l]], axis=-1).astype(BF16),
             'b_r': row(b_r[l]), 'b_i': row(b_i[l]), 'lru_lambda': row(lru_lambda[l]),
             'conv_b_w': conv_b_w[l], 'p_a': p_a[l].astype(BF16), 'p_b': p_b[l].astype(BF16),
             'w_o': w_o[l].astype(BF16), 'g_x_pre': row(g_x_pre[l]), 'g_x_post': row(g_x_post[l]),
             'w_q': w_q[l].astype(BF16), 'w_xo': w_xo[l].astype(BF16),
             'g_ffn_pre': row(g_ffn_pre[l]), 'g_ffn_post': row(g_ffn_post[l]), 'w_up': w_up[l],
             'ffn_conv_w': ffn_conv_w[l], 'ffn_conv_b': row(ffn_conv_b[l]), 'w_down': w_down[l].astype(BF16)}
        d_rnn, d_conv, d_up = conv_a_w.shape[2], conv_b_w.shape[2], ffn_conv_w.shape[2]
        mem_u = _rmsnorm_call(mem_prompt.reshape(bsz * n_mem, d), row(g_mem[l]), tm=NORM_TM)
        mk = _proj(mem_u, w_k[l], F32, tm=MM_TM, tn=MM_TN).reshape(bsz, n_mem, d)
        mv = _proj(mem_u, w_v[l], F32, tm=MM_TM, tn=MM_TN).reshape(bsz, n_mem, d)
        zeros = lambda *s: jnp.zeros(s, F32)
        yp, na, nh, nb_, nf = _layer(yp, 0, mk.astype(BF16), mv.astype(BF16),
                                     zeros(bsz, conv_a_w.shape[1] - 1, d_rnn), zeros(bsz, d_rnn),
                                     zeros(bsz, conv_b_w.shape[1] - 1, d_conv),
                                     zeros(bsz, ffn_conv_w.shape[1] - 1, d_up), p)
        for o, v in zip(outs[:6], (na, nh, nb_, nf, mk.reshape(bsz, n_mem, N_XHEADS, -1),
                                   mv.reshape(bsz, n_mem, N_XHEADS, -1))):
            o.append(v)
        dec_b = x_sample.shape[0]
        ck = cache_mem_k[l].reshape(dec_b, n_mem, d).astype(BF16)
        cv = cache_mem_v[l].reshape(dec_b, n_mem, d).astype(BF16)
        ys, na, nh, nb_, nf = _layer(ys, PAST_LEN, ck, cv, state_conv_a[l], state_rglru[l], state_conv_b[l],
                                     state_ffn_conv[l], p)
        for o, v in zip(outs[6:], (na, nh, nb_, nf)):
            o.append(v)
    return (yp, ys) + tuple(jnp.stack(o) for o in outs)
```

```python
import functools

import jax
import jax.numpy as jnp
from jax import lax
from jax.experimental import pallas as pl
from jax.experimental.pallas import tpu as pltpu

F32 = jnp.float32
BF16 = jnp.bfloat16

EPS = 1e-6
RG_C = 8.0
PAST_LEN = 2048

LANES = 128
SUBLANES = 8
BF16_ROWS = 16
MXU_COLS = 256
VMEM_LIMIT_BYTES = 60 * 1024 * 1024

ROW_CHUNK = 128
CAST_ROWS = 256
COL_CHUNK = MXU_COLS
SCAN_LANES = 512


def _params(*semantics):
    return pltpu.CompilerParams(dimension_semantics=semantics, vmem_limit_bytes=VMEM_LIMIT_BYTES)


def _resident(shape):
    nd = len(shape)
    return pl.BlockSpec(shape, lambda *_: (0,) * nd, pipeline_mode=pl.Buffered(1))


def _rmsnorm(x, g):
    y = x * lax.rsqrt(jnp.mean(x * x, axis=-1, keepdims=True) + EPS)
    return y * g


def _sigmoid(x):
    return 0.5 * jnp.tanh(0.5 * x) + 0.5


def _gelu_tanh(x):
    c1 = (2.0 / jnp.pi) ** 0.5
    return x * (0.5 * jnp.tanh(x * (c1 + (c1 * 0.044715) * (x * x))) + 0.5)


def _causal_conv(x, before, w, b):
    nb, t, c = x.shape
    width = w.shape[0]
    groups = t // SUBLANES
    full = jnp.concatenate([before, x], axis=1).reshape(nb * (groups + 1), SUBLANES, c)
    row = lax.broadcasted_iota(jnp.int32, (1, 1, SUBLANES, c), 2)
    y = None
    for k in range(width):
        lag = width - 1 - k
        if lag == 0:
            tap = x
        else:
            rot = pltpu.roll(full, lag, 1).reshape(nb, groups + 1, SUBLANES, c)
            tap = jnp.where(row >= lag, rot[:, 1:], rot[:, :-1]).reshape(nb, t, c)
        y = tap * w[k:k + 1, :] if y is None else y + tap * w[k:k + 1, :]
    return y if b is None else y + b


def _softplus(x):
    return jnp.maximum(x, 0.0) + jnp.log1p(jnp.exp(-jnp.abs(x)))


def _rmsnorm_kernel(x_ref, g_ref, u_ref):
    tm = x_ref.shape[0]
    chunk = min(ROW_CHUNK, tm)
    g = g_ref[...]

    def body(c, _):
        rows = pl.ds(pl.multiple_of(c * chunk, chunk), chunk)
        u_ref[rows, :] = _rmsnorm(x_ref[rows, :], g).astype(BF16)
        return None

    lax.fori_loop(0, tm // chunk, body, None)


def _rmsnorm_call(x, g, *, tm):
    m, d = x.shape
    tm = min(tm, m)
    assert m % tm == 0 and tm % min(ROW_CHUNK, tm) == 0
    return pl.pallas_call(
        _rmsnorm_kernel,
        grid=(m // tm,),
        in_specs=[pl.BlockSpec((tm, d), lambda i: (i, 0)), pl.BlockSpec((1, d), lambda i: (0, 0))],
        out_specs=pl.BlockSpec((tm, d), lambda i: (i, 0)),
        out_shape=jax.ShapeDtypeStruct((m, d), BF16),
        compiler_params=_params("parallel"),
        name="rmsnorm",
    )(x, g)


def _cast_rows(src_ref, dst_ref):
    rows_total = src_ref.shape[0]
    chunk = min(CAST_ROWS, rows_total)

    def body(c, _):
        rows = pl.ds(pl.multiple_of(c * chunk, chunk), chunk)
        dst_ref[rows, :] = src_ref[rows, :].astype(BF16)
        return None

    lax.fori_loop(0, rows_total // chunk, body, None)


def _group_steps(counts):
    starts = [sum(counts[:g]) for g in range(len(counts))]
    return starts, sum(counts)


def _group_tile(i, start, count):
    return jnp.clip(i - start, 0, count - 1)


def _proj_kernel(*refs, starts, counts):
    n_groups = len(counts)
    u_refs, w_ref, o_refs, w_scr = refs[:n_groups], refs[n_groups], refs[n_groups + 1:-1], refs[-1]
    i = pl.program_id(1)

    @pl.when(i == 0)
    def _():
        _cast_rows(w_ref, w_scr)

    for g in range(n_groups):
        @pl.when((i >= starts[g]) & (i < starts[g] + counts[g]))
        def _(g=g):
            o_refs[g][...] = jnp.dot(u_refs[g][...], w_scr[...],
                                     preferred_element_type=F32).astype(o_refs[g].dtype)


def _proj(us, w, out_dtype, *, tm, tn):
    d, n = w.shape
    tn = min(tn, n)
    tms = [min(tm, u.shape[0]) for u in us]
    counts = [u.shape[0] // t for u, t in zip(us, tms)]
    assert all(u.shape[0] % t == 0 for u, t in zip(us, tms)) and n % tn == 0 and d % min(CAST_ROWS, d) == 0
    starts, steps = _group_steps(counts)
    tile = lambda g: (lambda j, i: (_group_tile(i, starts[g], counts[g]), 0))
    out_tile = lambda g: (lambda j, i: (_group_tile(i, starts[g], counts[g]), j))
    return pl.pallas_call(
        functools.partial(_proj_kernel, starts=starts, counts=counts),
        grid=(n // tn, steps),
        in_specs=([pl.BlockSpec((tms[g], d), tile(g)) for g in range(len(us))]
                  + [pl.BlockSpec((d, tn), lambda j, i: (0, j))]),
        out_specs=[pl.BlockSpec((tms[g], tn), out_tile(g)) for g in range(len(us))],
        out_shape=[jax.ShapeDtypeStruct((u.shape[0], n), out_dtype) for u in us],
        scratch_shapes=[pltpu.VMEM((d, tn), BF16)],
        compiler_params=_params("arbitrary", "arbitrary"),
        name="proj",
    )(*us, w)


def _mixer_pre_kernel(z_ref, sa_ref, h0_ref, sb_ref, caw_ref, cab_ref, wri_ref, br_ref, bi_ref, lam_ref, cbw_ref,
                      ya_ref, yb_ref, nsa_ref, nh_ref, nsb_ref,
                      xa_scr, xc_scr, xcb_scr, ri_scr, cb_scr, h_scr, *, d_rnn, d_conv, pos0):
    t = pl.program_id(1)
    nb, seq, _ = ya_ref.shape
    wa = caw_ref.shape[0]
    wb = cbw_ref.shape[0]
    n_heads, head_dim, _ = wri_ref.shape
    top = SUBLANES

    @pl.when(t == 0)
    def _():
        xa_scr[:, top - (wa - 1):top, :] = sa_ref[...]
        cb_scr[:, top - (wb - 1):top, :] = sb_ref[...]
        h_scr[...] = jnp.broadcast_to(h0_ref[...], h_scr.shape)

    xa_scr[:, top:top + seq, :] = z_ref[:, :, 0:d_rnn].astype(F32)
    base = top - (wa - 1)
    xc = xa_scr[:, base:base + seq, :] * caw_ref[0:1, :]
    for k in range(1, wa):
        xc = xc + xa_scr[:, base + k:base + k + seq, :] * caw_ref[k:k + 1, :]
    xc = xc + cab_ref[...]
    xc_scr[...] = xc
    xcb_scr[...] = xc.reshape(nb * seq, d_rnn).astype(BF16)
    nsa_ref[...] = xa_scr[:, top + seq - (wa - 1):top + seq, :]
    xa_scr[:, 0:top, :] = xa_scr[:, seq:seq + top, :]

    for h in range(n_heads):
        cols = slice(h * head_dim, (h + 1) * head_dim)
        ri = jnp.dot(xcb_scr[:, cols], wri_ref[h], preferred_element_type=F32)
        ri_scr[:, :, cols] = ri[:, :head_dim].reshape(nb, seq, head_dim)
        ri_scr[:, :, d_rnn + h * head_dim:d_rnn + (h + 1) * head_dim] = ri[:, head_dim:].reshape(nb, seq, head_dim)

    strip = max(LANES, SCAN_LANES // nb)
    rows_per_iter = BF16_ROWS
    row_in_group = lax.broadcasted_iota(jnp.int32, (1, SUBLANES, strip), 1)
    for c in range(d_rnn // strip):
        cs = slice(c * strip, (c + 1) * strip)
        cs_i = slice(d_rnn + c * strip, d_rnn + (c + 1) * strip)
        cs_g = slice(d_rnn + c * strip, d_rnn + (c + 1) * strip)
        sp = _softplus(-lam_ref[:, cs])
        b_r = br_ref[:, cs]
        b_i = bi_ref[:, cs]

        def body(it, carry, cs=cs, cs_i=cs_i, cs_g=cs_g, sp=sp, b_r=b_r, b_i=b_i):
            r0 = pl.multiple_of(it * rows_per_iter, rows_per_iter)
            rows = pl.ds(r0, rows_per_iter)
            r = jax.nn.sigmoid(ri_scr[:, rows, cs] + b_r)
            i = jax.nn.sigmoid(ri_scr[:, rows, cs_i] + b_i)
            log_a = -RG_C * r * sp
            a = jnp.exp(log_a)
            mult = jnp.sqrt(-jnp.tanh(log_a) * (a * a + 1.0))
            if pos0 == 0:
                pos = t * seq + r0 + lax.broadcasted_iota(jnp.int32, (1, rows_per_iter, strip), 1)
                mult = jnp.where(pos == 0, 1.0, mult)
            b = mult * i * xc_scr[:, rows, cs]
            hs = []
            for sub in range(rows_per_iter // SUBLANES):
                a8 = a[:, sub * SUBLANES:(sub + 1) * SUBLANES, :]
                b8 = b[:, sub * SUBLANES:(sub + 1) * SUBLANES, :]
                for s in (1, 2, 4):
                    keep = row_in_group >= s
                    a_prev = pltpu.roll(a8, s, 1)
                    b_prev = pltpu.roll(b8, s, 1)
                    b8 = jnp.where(keep, a8 * b_prev + b8, b8)
                    a8 = jnp.where(keep, a8 * a_prev, a8)
                h8 = a8 * carry + b8
                carry = jnp.broadcast_to(h8[:, SUBLANES - 1:SUBLANES, :], h8.shape)
                hs.append(h8)
            h = jnp.concatenate(hs, axis=1)
            gate = jax.nn.gelu(z_ref[:, rows, cs_g].astype(F32), approximate=True)
            ya_ref[:, rows, cs] = (h * gate).astype(BF16)
            return carry

        last = lax.fori_loop(0, seq // rows_per_iter, body, h_scr[:, :, cs])
        h_scr[:, :, cs] = last
        nh_ref[:, :, cs] = last[:, 0:1, :]

    o_gb, o_gc, o_hb = 2 * d_rnn, 2 * d_rnn + d_conv, 2 * d_rnn + 2 * d_conv
    cb_scr[:, top:top + seq, :] = (z_ref[:, :, o_gc:o_gc + d_conv].astype(F32)
                                   * z_ref[:, :, o_hb:o_hb + d_conv].astype(F32))
    base = top - (wb - 1)
    co = cb_scr[:, base:base + seq, :] * cbw_ref[0:1, :]
    for k in range(1, wb):
        co = co + cb_scr[:, base + k:base + k + seq, :] * cbw_ref[k:k + 1, :]
    yb_ref[...] = (z_ref[:, :, o_gb:o_gb + d_conv].astype(F32) * co).astype(BF16)
    nsb_ref[...] = cb_scr[:, top + seq - (wb - 1):top + seq, :]
    cb_scr[:, 0:top, :] = cb_scr[:, seq:seq + top, :]


def _mixer_pre(z, state_a, h0, state_b, conv_a_w, conv_a_b, w_ri, b_r, b_i, lam, conv_b_w, *, seq_tile, pos0):
    bsz, t, _ = z.shape
    d_rnn = conv_a_w.shape[1]
    d_conv = conv_b_w.shape[1]
    n_pre = 2 * d_rnn + 3 * d_conv
    nb = bsz if t <= seq_tile else 1
    seq = min(seq_tile, t)
    assert t % seq == 0 and seq % BF16_ROWS == 0 and bsz % nb == 0
    wa, wb = conv_a_w.shape[0], conv_b_w.shape[0]
    kern = functools.partial(_mixer_pre_kernel, d_rnn=d_rnn, d_conv=d_conv, pos0=pos0)
    row = lambda b, i: (b, i, 0)
    per_b = lambda b, i: (b, 0, 0)
    const2 = lambda b, i: (0, 0)
    return pl.pallas_call(
        kern,
        grid=(bsz // nb, t // seq),
        in_specs=[pl.BlockSpec((nb, seq, n_pre), row),
                  pl.BlockSpec((nb, wa - 1, d_rnn), per_b),
                  pl.BlockSpec((nb, 1, d_rnn), per_b),
                  pl.BlockSpec((nb, wb - 1, d_conv), per_b),
                  pl.BlockSpec(conv_a_w.shape, const2),
                  pl.BlockSpec((1, d_rnn), const2),
                  pl.BlockSpec(w_ri.shape, lambda b, i: (0, 0, 0)),
                  pl.BlockSpec((1, d_rnn), const2),
                  pl.BlockSpec((1, d_rnn), const2),
                  pl.BlockSpec((1, d_rnn), const2),
                  pl.BlockSpec(conv_b_w.shape, const2)],
        out_specs=[pl.BlockSpec((nb, seq, d_rnn), row),
                   pl.BlockSpec((nb, seq, d_conv), row),
                   pl.BlockSpec((nb, wa - 1, d_rnn), per_b),
                   pl.BlockSpec((nb, 1, d_rnn), per_b),
                   pl.BlockSpec((nb, wb - 1, d_conv), per_b)],
        out_shape=[jax.ShapeDtypeStruct((bsz, t, d_rnn), BF16),
                   jax.ShapeDtypeStruct((bsz, t, d_conv), BF16),
                   jax.ShapeDtypeStruct((bsz, wa - 1, d_rnn), F32),
                   jax.ShapeDtypeStruct((bsz, 1, d_rnn), F32),
                   jax.ShapeDtypeStruct((bsz, wb - 1, d_conv), F32)],
        scratch_shapes=[pltpu.VMEM((nb, SUBLANES + seq, d_rnn), F32),
                        pltpu.VMEM((nb, seq, d_rnn), F32),
                        pltpu.VMEM((nb * seq, d_rnn), BF16),
                        pltpu.VMEM((nb, seq, 2 * d_rnn), F32),
                        pltpu.VMEM((nb, SUBLANES + seq, d_conv), F32),
                        pltpu.VMEM((nb, SUBLANES, d_rnn), F32)],
        compiler_params=_params("arbitrary", "arbitrary"),
        name="mixer_pre",
    )(z, state_a, h0, state_b, conv_a_w, conv_a_b, w_ri, b_r, b_i, lam, conv_b_w)


def _mixer_post_kernel(ya_ref, yb_ref, ga0_ref, ga1_ref, gb0_ref, gb1_ref, x_ref, pa_ref, pb_ref, wo_ref, g_ref,
                       o_ref, mix_scr):
    half = ga0_ref.shape[1]
    y_a = jnp.dot(ya_ref[...], pa_ref[...], preferred_element_type=F32)
    y_b = jnp.dot(yb_ref[...], pb_ref[...], preferred_element_type=F32)
    for c, (ga_ref, gb_ref) in enumerate(((ga0_ref, gb0_ref), (ga1_ref, gb1_ref))):
        cols = slice(c * half, (c + 1) * half)
        mix = (jax.nn.sigmoid(ga_ref[...].astype(F32)) * y_a[:, cols]
               + jax.nn.sigmoid(gb_ref[...].astype(F32)) * y_b[:, cols])
        mix_scr[:, cols] = mix.astype(BF16)
    out = jnp.dot(mix_scr[...], wo_ref[...], preferred_element_type=F32)
    o_ref[...] = x_ref[...] + _rmsnorm(out, g_ref[...])


def _mixer_post(ya, yb, z, x, p_a, p_b, w_o, g_post, *, tm):
    m, d = x.shape
    d_rnn, d_conv = ya.shape[1], yb.shape[1]
    tm = min(tm, m)
    half = d // 2
    gate0 = (2 * d_rnn + 3 * d_conv) // half
    assert m % tm == 0 and (2 * d_rnn + 3 * d_conv) % half == 0
    gate_spec = lambda k: pl.BlockSpec((tm, half), lambda i: (i, gate0 + k))
    row = lambda i: (i, 0)
    return pl.pallas_call(
        _mixer_post_kernel,
        grid=(m // tm,),
        in_specs=[pl.BlockSpec((tm, d_rnn), row), pl.BlockSpec((tm, d_conv), row),
                  gate_spec(0), gate_spec(1), gate_spec(2), gate_spec(3),
                  pl.BlockSpec((tm, d), row),
                  _resident(p_a.shape), _resident(p_b.shape), _resident(w_o.shape),
                  pl.BlockSpec((1, d), lambda i: (0, 0))],
        out_specs=pl.BlockSpec((tm, d), row),
        out_shape=jax.ShapeDtypeStruct((m, d), F32),
        scratch_shapes=[pltpu.VMEM((tm, d), BF16)],
        compiler_params=_params("parallel"),
        name="mixer_post",
    )(ya, yb, z, z, z, z, x, p_a, p_b, w_o, g_post)


def _xattn_kernel(x_ref, k_ref, v_ref, wq_ref, wxo_ref, gpre_ref, gpost_ref, gnext_ref, o_ref, u_ref, q_scr, o_scr,
                  *, n_heads):
    nb, seq, d = x_ref.shape
    hd = d // n_heads
    scale = hd ** -0.5
    x = x_ref[...].reshape(nb * seq, d)
    u = _rmsnorm(x, gpre_ref[...]).astype(BF16)
    q_scr[...] = jnp.dot(u, wq_ref[...], preferred_element_type=F32).astype(BF16)
    for b in range(nb):
        rows = slice(b * seq, (b + 1) * seq)
        for h in range(n_heads):
            cols = slice(h * hd, (h + 1) * hd)
            s = lax.dot_general(q_scr[rows, cols], k_ref[b, :, cols], (((1,), (1,)), ((), ())),
                                preferred_element_type=F32) * scale
            p = jnp.exp(s - jnp.max(s, axis=-1, keepdims=True))
            att = (p / jnp.sum(p, axis=-1, keepdims=True)).astype(BF16)
            o_scr[rows, cols] = jnp.dot(att, v_ref[b, :, cols], preferred_element_type=F32).astype(BF16)
    out = jnp.dot(o_scr[...], wxo_ref[...], preferred_element_type=F32)
    x_new = x + _rmsnorm(out, gpost_ref[...])
    o_ref[...] = x_new.reshape(nb, seq, d)
    u_ref[...] = _rmsnorm(x_new, gnext_ref[...]).astype(BF16).reshape(nb, seq, d)


def _xattn(x, mem_k, mem_v, w_q, w_xo, g_pre, g_post, g_next, *, n_heads, seq_tile):
    bsz, t, d = x.shape
    n_mem = mem_k.shape[1]
    nb = bsz if t <= seq_tile else 1
    seq = min(seq_tile, t)
    assert t % seq == 0 and seq % BF16_ROWS == 0 and bsz % nb == 0
    single = bsz // nb == 1
    mem_spec = (_resident((nb, n_mem, d)) if single else pl.BlockSpec((nb, n_mem, d), lambda b, i: (b, 0, 0)))
    row = lambda b, i: (b, i, 0)
    gain = pl.BlockSpec((1, d), lambda b, i: (0, 0))
    return pl.pallas_call(
        functools.partial(_xattn_kernel, n_heads=n_heads),
        grid=(bsz // nb, t // seq),
        in_specs=[pl.BlockSpec((nb, seq, d), row), mem_spec, mem_spec,
                  _resident(w_q.shape), _resident(w_xo.shape), gain, gain, gain],
        out_specs=[pl.BlockSpec((nb, seq, d), row), pl.BlockSpec((nb, seq, d), row)],
        out_shape=[jax.ShapeDtypeStruct((bsz, t, d), F32), jax.ShapeDtypeStruct((bsz, t, d), BF16)],
        scratch_shapes=[pltpu.VMEM((nb * seq, d), BF16), pltpu.VMEM((nb * seq, d), BF16)],
        compiler_params=_params("parallel", "parallel"),
        name="xattn",
    )(x, mem_k, mem_v, w_q, w_xo, g_pre, g_post, g_next)


def _up_geglu_kernel(*refs, starts, counts):
    n_groups = len(counts)
    ins, rest = refs[:3 * n_groups], refs[3 * n_groups:]
    wg_ref, wv_ref, cwg_ref, cwv_ref, cbg_ref, cbv_ref = rest[:6]
    outs, scr = rest[6:6 + 3 * n_groups], rest[6 + 3 * n_groups:]
    w_scrs, hists = scr[:2], scr[2:]
    width = cwg_ref.shape[0]
    top = SUBLANES
    i = pl.program_id(1)

    @pl.when(i == 0)
    def _():
        _cast_rows(wg_ref, w_scrs[0])
        _cast_rows(wv_ref, w_scrs[1])

    for g in range(n_groups):
        u_ref, st_refs = ins[3 * g], ins[3 * g + 1:3 * g + 3]
        hid_ref, ns_refs = outs[3 * g], outs[3 * g + 1:3 * g + 3]
        p_scrs = hists[2 * g:2 * g + 2]
        nb, seq, k_dim = u_ref.shape
        tn = hid_ref.shape[2]
        chunk = min(COL_CHUNK, tn)

        @pl.when(i == starts[g])
        def _(st_refs=st_refs, p_scrs=p_scrs):
            for st_ref, p_scr in zip(st_refs, p_scrs):
                p_scr[...] = jnp.zeros(p_scr.shape, F32)
                p_scr[:, top - (width - 1):top, :] = st_ref[...]

        @pl.when((i >= starts[g]) & (i < starts[g] + counts[g]))
        def _(u_ref=u_ref, hid_ref=hid_ref, ns_refs=ns_refs, p_scrs=p_scrs, nb=nb, seq=seq, k_dim=k_dim,
              tn=tn, chunk=chunk):
            u = u_ref[...].reshape(nb * seq, k_dim)
            for c in range(tn // chunk):
                cols = slice(c * chunk, (c + 1) * chunk)
                ys = []
                for w_scr, p_scr, cw_ref, cb_ref, ns_ref in zip(w_scrs, p_scrs, (cwg_ref, cwv_ref),
                                                                (cbg_ref, cbv_ref), ns_refs):
                    a = jnp.dot(u, w_scr[:, cols], preferred_element_type=F32).reshape(nb, seq, chunk)
                    ys.append(_causal_conv(a, p_scr[:, :, cols], cw_ref[:, cols], cb_ref[:, cols]))
                    p_scr[:, :, cols] = a[:, seq - top:, :]
                    ns_ref[:, :, cols] = p_scr[:, top - (width - 1):top, cols]
                hid_ref[:, :, cols] = (_gelu_tanh(ys[0]) * ys[1]).astype(BF16)


def _up_geglu(us, w_up, states, conv_w, conv_b, *, tm, tn):
    d = w_up.shape[0]
    d_ff = w_up.shape[1] // 2
    width = conv_w.shape[0]
    n_j = d_ff // tn
    nbs = [u.shape[0] if u.shape[1] <= tm else 1 for u in us]
    seqs = [min(tm, u.shape[1]) for u in us]
    counts = [u.shape[1] // s for u, s in zip(us, seqs)]
    assert all(u.shape[0] == nb and u.shape[1] % s == 0 and s % BF16_ROWS == 0 for u, nb, s in zip(us, nbs, seqs))
    assert d_ff % tn == 0 and tn % min(COL_CHUNK, tn) == 0
    starts, steps = _group_steps(counts)
    col_g = lambda j, i: (0, j)
    col_v = lambda j, i: (0, n_j + j)
    st_g = lambda j, i: (0, 0, j)
    st_v = lambda j, i: (0, 0, n_j + j)
    in_specs, out_specs, out_shape, hist, args = [], [], [], [], []
    for g, (u, st, nb, seq) in enumerate(zip(us, states, nbs, seqs)):
        row = lambda j, i, g=g: (0, _group_tile(i, starts[g], counts[g]), 0)
        out = lambda j, i, g=g: (0, _group_tile(i, starts[g], counts[g]), j)
        in_specs += [pl.BlockSpec((nb, seq, d), row),
                     pl.BlockSpec((nb, width - 1, tn), st_g), pl.BlockSpec((nb, width - 1, tn), st_v)]
        out_specs += [pl.BlockSpec((nb, seq, tn), out),
                      pl.BlockSpec((nb, width - 1, tn), st_g), pl.BlockSpec((nb, width - 1, tn), st_g)]
        out_shape += [jax.ShapeDtypeStruct((u.shape[0], u.shape[1], d_ff), BF16),
                      jax.ShapeDtypeStruct((u.shape[0], width - 1, d_ff), F32),
                      jax.ShapeDtypeStruct((u.shape[0], width - 1, d_ff), F32)]
        hist += [pltpu.VMEM((nb, SUBLANES, tn), F32), pltpu.VMEM((nb, SUBLANES, tn), F32)]
        args += [u, st, st]
    in_specs += [pl.BlockSpec((d, tn), col_g), pl.BlockSpec((d, tn), col_v),
                 pl.BlockSpec((width, tn), col_g), pl.BlockSpec((width, tn), col_v),
                 pl.BlockSpec((1, tn), col_g), pl.BlockSpec((1, tn), col_v)]
    res = pl.pallas_call(
        functools.partial(_up_geglu_kernel, starts=starts, counts=counts),
        grid=(n_j, steps),
        in_specs=in_specs,
        out_specs=out_specs,
        out_shape=out_shape,
        scratch_shapes=[pltpu.VMEM((d, tn), BF16), pltpu.VMEM((d, tn), BF16)] + hist,
        compiler_params=_params("arbitrary", "arbitrary"),
        name="up_geglu",
    )(*args, w_up, w_up, conv_w, conv_w, conv_b, conv_b)
    return [tuple(res[3 * g:3 * g + 3]) for g in range(len(us))]


def _ffn_down_kernel(hid_ref, x_ref, wd_ref, g_ref, o_ref):
    y = jnp.dot(hid_ref[...], wd_ref[...], preferred_element_type=F32)
    o_ref[...] = x_ref[...] + _rmsnorm(y, g_ref[...])


def _ffn_down(hid, x, w_down, g_post, *, tm):
    m, d = x.shape
    d_ff = hid.shape[1]
    tm = min(tm, m)
    assert m % tm == 0
    row = lambda i: (i, 0)
    return pl.pallas_call(
        _ffn_down_kernel,
        grid=(m // tm,),
        in_specs=[pl.BlockSpec((tm, d_ff), row), pl.BlockSpec((tm, d), row), _resident(w_down.shape),
                  pl.BlockSpec((1, d), lambda i: (0, 0))],
        out_specs=pl.BlockSpec((tm, d), row),
        out_shape=jax.ShapeDtypeStruct((m, d), F32),
        compiler_params=_params("parallel"),
        name="ffn_down",
    )(hid, x, w_down, g_post)


MM_TM, MM_TN = 2048, 1024
UP_TM, UP_TN = 1024, 512
SEQ_TILE = 256
ROW_TILE = 512
NORM_TM = 512
N_XHEADS = 4


def _layer(groups, p):
    flat = [x.reshape(-1, x.shape[-1]) for x, *_ in groups]
    u0s = [_rmsnorm_call(x2, p['g_mix_pre'], tm=NORM_TM) for x2 in flat]
    zs = _proj(u0s, p['w_in'], BF16, tm=MM_TM, tn=MM_TN)
    mids = []
    for (x, pos0, mem_k, mem_v, s_a, h0, s_b, _), x2, z in zip(groups, flat, zs):
        bsz, t, d = x.shape
        m = bsz * t
        ya, yb, ns_a, nh, ns_b = _mixer_pre(z.reshape(bsz, t, -1), s_a, h0[:, None, :], s_b,
                                            p['conv_a_w'], p['conv_a_b'], p['w_ri'], p['b_r'], p['b_i'],
                                            p['lru_lambda'], p['conv_b_w'], seq_tile=SEQ_TILE, pos0=pos0)
        x1 = _mixer_post(ya.reshape(m, -1), yb.reshape(m, -1), z, x2, p['p_a'], p['p_b'], p['w_o'],
                         p['g_mix_post'], tm=ROW_TILE)
        x2a, u3 = _xattn(x1.reshape(bsz, t, d), mem_k, mem_v, p['w_q'], p['w_xo'], p['g_x_pre'], p['g_x_post'],
                         p['g_ffn_pre'], n_heads=N_XHEADS, seq_tile=ROW_TILE)
        mids.append((x2a, u3, ns_a, nh[:, 0, :], ns_b))
    ffn = _up_geglu([u3 for _, u3, *_ in mids], p['w_up'], [g[7] for g in groups],
                    p['ffn_conv_w'], p['ffn_conv_b'], tm=UP_TM, tn=UP_TN)
    results = []
    for (x2a, _, ns_a, nh, ns_b), (hid, ns_fg, ns_fv) in zip(mids, ffn):
        bsz, t, d = x2a.shape
        x3 = _ffn_down(hid.reshape(bsz * t, -1), x2a.reshape(bsz * t, d), p['w_down'], p['g_ffn_post'], tm=ROW_TILE)
        results.append((x3.reshape(bsz, t, d), ns_a, nh, ns_b, jnp.concatenate([ns_fg, ns_fv], axis=-1)))
    return results


def kernel(x_prompt, x_sample, mem_prompt, state_conv_a, state_rglru, state_conv_b, state_ffn_conv, cache_mem_k, cache_mem_v, g_mix_pre, g_mix_post, w_in, conv_a_w, conv_a_b, w_r, b_r, w_i, b_i, lru_lambda, conv_b_w, p_a, p_b, w_o, g_x_pre, g_x_post, g_mem, w_q, w_k, w_v, w_xo, g_ffn_pre, g_ffn_post, w_up, ffn_conv_w, ffn_conv_b, w_down):
    depth = w_in.shape[0]
    bsz, _, d = x_prompt.shape
    n_mem = mem_prompt.shape[1]
    yp, ys = x_prompt, x_sample
    outs = [[] for _ in range(10)]
    row = lambda v: v.reshape(1, -1).astype(F32)
    for l in range(depth):
        p = {'g_mix_pre': row(g_mix_pre[l]), 'g_mix_post': row(g_mix_post[l]), 'w_in': w_in[l],
             'conv_a_w': conv_a_w[l], 'conv_a_b': row(conv_a_b[l]),
             'w_ri': jnp.concatenate([w_r[l], w_i[l]], axis=-1).astype(BF16),
             'b_r': row(b_r[l]), 'b_i': row(b_i[l]), 'lru_lambda': row(lru_lambda[l]),
             'conv_b_w': conv_b_w[l], 'p_a': p_a[l].astype(BF16), 'p_b': p_b[l].astype(BF16),
             'w_o': w_o[l].astype(BF16), 'g_x_pre': row(g_x_pre[l]), 'g_x_post': row(g_x_post[l]),
             'w_q': w_q[l].astype(BF16), 'w_xo': w_xo[l].astype(BF16),
             'g_ffn_pre': row(g_ffn_pre[l]), 'g_ffn_post': row(g_ffn_post[l]), 'w_up': w_up[l],
             'ffn_conv_w': ffn_conv_w[l], 'ffn_conv_b': row(ffn_conv_b[l]), 'w_down': w_down[l].astype(BF16)}
        d_rnn, d_conv, d_up = conv_a_w.shape[2], conv_b_w.shape[2], ffn_conv_w.shape[2]
        mem_u = _rmsnorm_call(mem_prompt.reshape(bsz * n_mem, d), row(g_mem[l]), tm=NORM_TM)
        mk = _proj([mem_u], w_k[l], F32, tm=MM_TM, tn=MM_TN)[0].reshape(bsz, n_mem, d)
        mv = _proj([mem_u], w_v[l], F32, tm=MM_TM, tn=MM_TN)[0].reshape(bsz, n_mem, d)
        zeros = lambda *s: jnp.zeros(s, F32)
        prompt = (yp, 0, mk.astype(BF16), mv.astype(BF16),
                  zeros(bsz, conv_a_w.shape[1] - 1, d_rnn), zeros(bsz, d_rnn),
                  zeros(bsz, conv_b_w.shape[1] - 1, d_conv), zeros(bsz, ffn_conv_w.shape[1] - 1, d_up))
        dec_b = x_sample.shape[0]
        ck = cache_mem_k[l].reshape(dec_b, n_mem, d).astype(BF16)
        cv = cache_mem_v[l].reshape(dec_b, n_mem, d).astype(BF16)
        sample = (ys, PAST_LEN, ck, cv, state_conv_a[l], state_rglru[l], state_conv_b[l], state_ffn_conv[l])
        (yp, *p_states), (ys, *s_states) = _layer([prompt, sample], p)
        for o, v in zip(outs, (*p_states, mk.reshape(bsz, n_mem, N_XHEADS, -1),
                               mv.reshape(bsz, n_mem, N_XHEADS, -1), *s_states)):
            o.append(v)
    return (yp, ys) + tuple(jnp.stack(o) for o in outs)
```

```python
import functools

import jax
import jax.numpy as jnp
from jax import lax
from jax.experimental import pallas as pl
from jax.experimental.pallas import tpu as pltpu

F32 = jnp.float32
BF16 = jnp.bfloat16

EPS = 1e-6
RG_C = 8.0
PAST_LEN = 2048

LANES = 128
SUBLANES = 8
BF16_ROWS = 16
MXU_COLS = 256
VMEM_LIMIT_BYTES = 60 * 1024 * 1024

ROW_CHUNK = 128
CAST_ROWS = 256
COL_CHUNK = MXU_COLS
SCAN_LANES = 512


def _params(*semantics):
    return pltpu.CompilerParams(dimension_semantics=semantics, vmem_limit_bytes=VMEM_LIMIT_BYTES)


def _resident(shape):
    nd = len(shape)
    return pl.BlockSpec(shape, lambda *_: (0,) * nd, pipeline_mode=pl.Buffered(1))


def _rmsnorm(x, g):
    y = x * lax.rsqrt(jnp.mean(x * x, axis=-1, keepdims=True) + EPS)
    return y * g


def _sigmoid(x):
    return 0.5 * jnp.tanh(0.5 * x) + 0.5


def _gelu_tanh(x):
    c1 = (2.0 / jnp.pi) ** 0.5
    return x * (0.5 * jnp.tanh(x * (c1 + (c1 * 0.044715) * (x * x))) + 0.5)


def _causal_conv(x, before, w, b):
    nb, t, c = x.shape
    width = w.shape[0]
    groups = t // SUBLANES
    full = jnp.concatenate([before, x], axis=1).reshape(nb * (groups + 1), SUBLANES, c)
    row = lax.broadcasted_iota(jnp.int32, (1, 1, SUBLANES, c), 2)
    y = None
    for k in range(width):
        lag = width - 1 - k
        if lag == 0:
            tap = x
        else:
            rot = pltpu.roll(full, lag, 1).reshape(nb, groups + 1, SUBLANES, c)
            tap = jnp.where(row >= lag, rot[:, 1:], rot[:, :-1]).reshape(nb, t, c)
        y = tap * w[k:k + 1, :] if y is None else y + tap * w[k:k + 1, :]
    return y if b is None else y + b


def _softplus(x):
    return jnp.maximum(x, 0.0) + jnp.log1p(jnp.exp(-jnp.abs(x)))


def _rmsnorm_kernel(x_ref, g_ref, u_ref):
    tm = x_ref.shape[0]
    chunk = min(ROW_CHUNK, tm)
    g = g_ref[...]

    def body(c, _):
        rows = pl.ds(pl.multiple_of(c * chunk, chunk), chunk)
        u_ref[rows, :] = _rmsnorm(x_ref[rows, :], g).astype(BF16)
        return None

    lax.fori_loop(0, tm // chunk, body, None)


def _rmsnorm_call(x, g, *, tm):
    m, d = x.shape
    tm = min(tm, m)
    assert m % tm == 0 and tm % min(ROW_CHUNK, tm) == 0
    return pl.pallas_call(
        _rmsnorm_kernel,
        grid=(m // tm,),
        in_specs=[pl.BlockSpec((tm, d), lambda i: (i, 0)), pl.BlockSpec((1, d), lambda i: (0, 0))],
        out_specs=pl.BlockSpec((tm, d), lambda i: (i, 0)),
        out_shape=jax.ShapeDtypeStruct((m, d), BF16),
        compiler_params=_params("parallel"),
        name="rmsnorm",
    )(x, g)


def _cast_rows(src_ref, dst_ref):
    rows_total = src_ref.shape[0]
    chunk = min(CAST_ROWS, rows_total)

    def body(c, _):
        rows = pl.ds(pl.multiple_of(c * chunk, chunk), chunk)
        dst_ref[rows, :] = src_ref[rows, :].astype(BF16)
        return None

    lax.fori_loop(0, rows_total // chunk, body, None)


def _group_steps(counts):
    starts = [sum(counts[:g]) for g in range(len(counts))]
    return starts, sum(counts)


def _group_tile(i, start, count):
    return jnp.clip(i - start, 0, count - 1)


def _proj_kernel(*refs, starts, counts):
    n_groups = len(counts)
    u_refs, w_ref, o_refs, w_scr = refs[:n_groups], refs[n_groups], refs[n_groups + 1:-1], refs[-1]
    i = pl.program_id(1)

    @pl.when(i == 0)
    def _():
        _cast_rows(w_ref, w_scr)

    for g in range(n_groups):
        @pl.when((i >= starts[g]) & (i < starts[g] + counts[g]))
        def _(g=g):
            o_refs[g][...] = jnp.dot(u_refs[g][...], w_scr[...],
                                     preferred_element_type=F32).astype(o_refs[g].dtype)


def _proj(us, w, out_dtype, *, tm, tn):
    d, n = w.shape
    tn = min(tn, n)
    tms = [min(tm, u.shape[0]) for u in us]
    counts = [u.shape[0] // t for u, t in zip(us, tms)]
    assert all(u.shape[0] % t == 0 for u, t in zip(us, tms)) and n % tn == 0 and d % min(CAST_ROWS, d) == 0
    starts, steps = _group_steps(counts)
    tile = lambda g: (lambda j, i: (_group_tile(i, starts[g], counts[g]), 0))
    out_tile = lambda g: (lambda j, i: (_group_tile(i, starts[g], counts[g]), j))
    return pl.pallas_call(
        functools.partial(_proj_kernel, starts=starts, counts=counts),
        grid=(n // tn, steps),
        in_specs=([pl.BlockSpec((tms[g], d), tile(g)) for g in range(len(us))]
                  + [pl.BlockSpec((d, tn), lambda j, i: (0, j))]),
        out_specs=[pl.BlockSpec((tms[g], tn), out_tile(g)) for g in range(len(us))],
        out_shape=[jax.ShapeDtypeStruct((u.shape[0], n), out_dtype) for u in us],
        scratch_shapes=[pltpu.VMEM((d, tn), BF16)],
        compiler_params=_params("arbitrary", "arbitrary"),
        name="proj",
    )(*us, w)


def _mixer_pre_kernel(z_ref, sa_ref, h0_ref, sb_ref, caw_ref, cab_ref, wri_ref, br_ref, bi_ref, lam_ref, cbw_ref,
                      ya_ref, yb_ref, nsa_ref, nh_ref, nsb_ref,
                      xa_scr, xc_scr, xcb_scr, ri_scr, cb_scr, h_scr, *, d_rnn, d_conv, pos0):
    t = pl.program_id(1)
    nb, seq, _ = ya_ref.shape
    wa = caw_ref.shape[0]
    wb = cbw_ref.shape[0]
    n_heads, head_dim, _ = wri_ref.shape
    top = SUBLANES

    @pl.when(t == 0)
    def _():
        xa_scr[:, top - (wa - 1):top, :] = sa_ref[...]
        cb_scr[:, top - (wb - 1):top, :] = sb_ref[...]
        h_scr[...] = jnp.broadcast_to(h0_ref[...], h_scr.shape)

    xa_scr[:, top:top + seq, :] = z_ref[:, :, 0:d_rnn].astype(F32)
    base = top - (wa - 1)
    xc = xa_scr[:, base:base + seq, :] * caw_ref[0:1, :]
    for k in range(1, wa):
        xc = xc + xa_scr[:, base + k:base + k + seq, :] * caw_ref[k:k + 1, :]
    xc = xc + cab_ref[...]
    xc_scr[...] = xc
    xcb_scr[...] = xc.reshape(nb * seq, d_rnn).astype(BF16)
    nsa_ref[...] = xa_scr[:, top + seq - (wa - 1):top + seq, :]
    xa_scr[:, 0:top, :] = xa_scr[:, seq:seq + top, :]

    for h in range(n_heads):
        cols = slice(h * head_dim, (h + 1) * head_dim)
        ri = jnp.dot(xcb_scr[:, cols], wri_ref[h], preferred_element_type=F32)
        ri_scr[:, :, cols] = ri[:, :head_dim].reshape(nb, seq, head_dim)
        ri_scr[:, :, d_rnn + h * head_dim:d_rnn + (h + 1) * head_dim] = ri[:, head_dim:].reshape(nb, seq, head_dim)

    strip = max(LANES, SCAN_LANES // nb)
    rows_per_iter = BF16_ROWS
    row_in_group = lax.broadcasted_iota(jnp.int32, (1, SUBLANES, strip), 1)
    for c in range(d_rnn // strip):
        cs = slice(c * strip, (c + 1) * strip)
        cs_i = slice(d_rnn + c * strip, d_rnn + (c + 1) * strip)
        cs_g = slice(d_rnn + c * strip, d_rnn + (c + 1) * strip)
        sp = _softplus(-lam_ref[:, cs])
        b_r = br_ref[:, cs]
        b_i = bi_ref[:, cs]

        def body(it, carry, cs=cs, cs_i=cs_i, cs_g=cs_g, sp=sp, b_r=b_r, b_i=b_i):
            r0 = pl.multiple_of(it * rows_per_iter, rows_per_iter)
            rows = pl.ds(r0, rows_per_iter)
            r = jax.nn.sigmoid(ri_scr[:, rows, cs] + b_r)
            i = jax.nn.sigmoid(ri_scr[:, rows, cs_i] + b_i)
            log_a = -RG_C * r * sp
            a = jnp.exp(log_a)
            mult = jnp.sqrt(-jnp.tanh(log_a) * (a * a + 1.0))
            if pos0 == 0:
                pos = t * seq + r0 + lax.broadcasted_iota(jnp.int32, (1, rows_per_iter, strip), 1)
                mult = jnp.where(pos == 0, 1.0, mult)
            b = mult * i * xc_scr[:, rows, cs]
            hs = []
            for sub in range(rows_per_iter // SUBLANES):
                a8 = a[:, sub * SUBLANES:(sub + 1) * SUBLANES, :]
                b8 = b[:, sub * SUBLANES:(sub + 1) * SUBLANES, :]
                for s in (1, 2, 4):
                    keep = row_in_group >= s
                    a_prev = pltpu.roll(a8, s, 1)
                    b_prev = pltpu.roll(b8, s, 1)
                    b8 = jnp.where(keep, a8 * b_prev + b8, b8)
                    a8 = jnp.where(keep, a8 * a_prev, a8)
                h8 = a8 * carry + b8
                carry = jnp.broadcast_to(h8[:, SUBLANES - 1:SUBLANES, :], h8.shape)
                hs.append(h8)
            h = jnp.concatenate(hs, axis=1)
            gate = jax.nn.gelu(z_ref[:, rows, cs_g].astype(F32), approximate=True)
            ya_ref[:, rows, cs] = (h * gate).astype(BF16)
            return carry

        last = lax.fori_loop(0, seq // rows_per_iter, body, h_scr[:, :, cs])
        h_scr[:, :, cs] = last
        nh_ref[:, :, cs] = last[:, 0:1, :]

    o_gb, o_gc, o_hb = 2 * d_rnn, 2 * d_rnn + d_conv, 2 * d_rnn + 2 * d_conv
    cb_scr[:, top:top + seq, :] = (z_ref[:, :, o_gc:o_gc + d_conv].astype(F32)
                                   * z_ref[:, :, o_hb:o_hb + d_conv].astype(F32))
    base = top - (wb - 1)
    co = cb_scr[:, base:base + seq, :] * cbw_ref[0:1, :]
    for k in range(1, wb):
        co = co + cb_scr[:, base + k:base + k + seq, :] * cbw_ref[k:k + 1, :]
    yb_ref[...] = (z_ref[:, :, o_gb:o_gb + d_conv].astype(F32) * co).astype(BF16)
    nsb_ref[...] = cb_scr[:, top + seq - (wb - 1):top + seq, :]
    cb_scr[:, 0:top, :] = cb_scr[:, seq:seq + top, :]


def _mixer_pre(z, state_a, h0, state_b, conv_a_w, conv_a_b, w_ri, b_r, b_i, lam, conv_b_w, *, seq_tile, pos0):
    bsz, t, _ = z.shape
    d_rnn = conv_a_w.shape[1]
    d_conv = conv_b_w.shape[1]
    n_pre = 2 * d_rnn + 3 * d_conv
    nb = bsz if t <= seq_tile else 1
    seq = min(seq_tile, t)
    assert t % seq == 0 and seq % BF16_ROWS == 0 and bsz % nb == 0
    wa, wb = conv_a_w.shape[0], conv_b_w.shape[0]
    kern = functools.partial(_mixer_pre_kernel, d_rnn=d_rnn, d_conv=d_conv, pos0=pos0)
    row = lambda b, i: (b, i, 0)
    per_b = lambda b, i: (b, 0, 0)
    const2 = lambda b, i: (0, 0)
    return pl.pallas_call(
        kern,
        grid=(bsz // nb, t // seq),
        in_specs=[pl.BlockSpec((nb, seq, n_pre), row),
                  pl.BlockSpec((nb, wa - 1, d_rnn), per_b),
                  pl.BlockSpec((nb, 1, d_rnn), per_b),
                  pl.BlockSpec((nb, wb - 1, d_conv), per_b),
                  pl.BlockSpec(conv_a_w.shape, const2),
                  pl.BlockSpec((1, d_rnn), const2),
                  pl.BlockSpec(w_ri.shape, lambda b, i: (0, 0, 0)),
                  pl.BlockSpec((1, d_rnn), const2),
                  pl.BlockSpec((1, d_rnn), const2),
                  pl.BlockSpec((1, d_rnn), const2),
                  pl.BlockSpec(conv_b_w.shape, const2)],
        out_specs=[pl.BlockSpec((nb, seq, d_rnn), row),
                   pl.BlockSpec((nb, seq, d_conv), row),
                   pl.BlockSpec((nb, wa - 1, d_rnn), per_b),
                   pl.BlockSpec((nb, 1, d_rnn), per_b),
                   pl.BlockSpec((nb, wb - 1, d_conv), per_b)],
        out_shape=[jax.ShapeDtypeStruct((bsz, t, d_rnn), BF16),
                   jax.ShapeDtypeStruct((bsz, t, d_conv), BF16),
                   jax.ShapeDtypeStruct((bsz, wa - 1, d_rnn), F32),
                   jax.ShapeDtypeStruct((bsz, 1, d_rnn), F32),
                   jax.ShapeDtypeStruct((bsz, wb - 1, d_conv), F32)],
        scratch_shapes=[pltpu.VMEM((nb, SUBLANES + seq, d_rnn), F32),
                        pltpu.VMEM((nb, seq, d_rnn), F32),
                        pltpu.VMEM((nb * seq, d_rnn), BF16),
                        pltpu.VMEM((nb, seq, 2 * d_rnn), F32),
                        pltpu.VMEM((nb, SUBLANES + seq, d_conv), F32),
                        pltpu.VMEM((nb, SUBLANES, d_rnn), F32)],
        compiler_params=_params("arbitrary", "arbitrary"),
        name="mixer_pre",
    )(z, state_a, h0, state_b, conv_a_w, conv_a_b, w_ri, b_r, b_i, lam, conv_b_w)


def _mixer_post_kernel(ya_ref, yb_ref, ga0_ref, ga1_ref, gb0_ref, gb1_ref, x_ref, pa_ref, pb_ref, wo_ref, g_ref,
                       o_ref, mix_scr):
    half = ga0_ref.shape[1]
    y_a = jnp.dot(ya_ref[...], pa_ref[...], preferred_element_type=F32)
    y_b = jnp.dot(yb_ref[...], pb_ref[...], preferred_element_type=F32)
    for c, (ga_ref, gb_ref) in enumerate(((ga0_ref, gb0_ref), (ga1_ref, gb1_ref))):
        cols = slice(c * half, (c + 1) * half)
        mix = (jax.nn.sigmoid(ga_ref[...].astype(F32)) * y_a[:, cols]
               + jax.nn.sigmoid(gb_ref[...].astype(F32)) * y_b[:, cols])
        mix_scr[:, cols] = mix.astype(BF16)
    out = jnp.dot(mix_scr[...], wo_ref[...], preferred_element_type=F32)
    o_ref[...] = x_ref[...] + _rmsnorm(out, g_ref[...])


def _mixer_post(ya, yb, z, x, p_a, p_b, w_o, g_post, *, tm):
    m, d = x.shape
    d_rnn, d_conv = ya.shape[1], yb.shape[1]
    tm = min(tm, m)
    half = d // 2
    gate0 = (2 * d_rnn + 3 * d_conv) // half
    assert m % tm == 0 and (2 * d_rnn + 3 * d_conv) % half == 0
    gate_spec = lambda k: pl.BlockSpec((tm, half), lambda i: (i, gate0 + k))
    row = lambda i: (i, 0)
    return pl.pallas_call(
        _mixer_post_kernel,
        grid=(m // tm,),
        in_specs=[pl.BlockSpec((tm, d_rnn), row), pl.BlockSpec((tm, d_conv), row),
                  gate_spec(0), gate_spec(1), gate_spec(2), gate_spec(3),
                  pl.BlockSpec((tm, d), row),
                  _resident(p_a.shape), _resident(p_b.shape), _resident(w_o.shape),
                  pl.BlockSpec((1, d), lambda i: (0, 0))],
        out_specs=pl.BlockSpec((tm, d), row),
        out_shape=jax.ShapeDtypeStruct((m, d), F32),
        scratch_shapes=[pltpu.VMEM((tm, d), BF16)],
        compiler_params=_params("parallel"),
        name="mixer_post",
    )(ya, yb, z, z, z, z, x, p_a, p_b, w_o, g_post)


def _xattn_kernel(x_ref, k_ref, v_ref, wq_ref, wxo_ref, gpre_ref, gpost_ref, gnext_ref, o_ref, u_ref, q_scr, o_scr,
                  *, n_heads):
    nb, seq, d = x_ref.shape
    hd = d // n_heads
    scale = hd ** -0.5
    x = x_ref[...].reshape(nb * seq, d)
    u = _rmsnorm(x, gpre_ref[...]).astype(BF16)
    q_scr[...] = jnp.dot(u, wq_ref[...], preferred_element_type=F32).astype(BF16)
    for b in range(nb):
        rows = slice(b * seq, (b + 1) * seq)
        for h in range(n_heads):
            cols = slice(h * hd, (h + 1) * hd)
            s = lax.dot_general(q_scr[rows, cols], k_ref[b, :, cols], (((1,), (1,)), ((), ())),
                                preferred_element_type=F32) * scale
            p = jnp.exp(s - jnp.max(s, axis=-1, keepdims=True))
            att = (p / jnp.sum(p, axis=-1, keepdims=True)).astype(BF16)
            o_scr[rows, cols] = jnp.dot(att, v_ref[b, :, cols], preferred_element_type=F32).astype(BF16)
    out = jnp.dot(o_scr[...], wxo_ref[...], preferred_element_type=F32)
    x_new = x + _rmsnorm(out, gpost_ref[...])
    o_ref[...] = x_new.reshape(nb, seq, d)
    u_ref[...] = _rmsnorm(x_new, gnext_ref[...]).astype(BF16).reshape(nb, seq, d)


def _xattn(x, mem_k, mem_v, w_q, w_xo, g_pre, g_post, g_next, *, n_heads, seq_tile):
    bsz, t, d = x.shape
    n_mem = mem_k.shape[1]
    nb = bsz if t <= seq_tile else 1
    seq = min(seq_tile, t)
    assert t % seq == 0 and seq % BF16_ROWS == 0 and bsz % nb == 0
    single = bsz // nb == 1
    mem_spec = (_resident((nb, n_mem, d)) if single else pl.BlockSpec((nb, n_mem, d), lambda b, i: (b, 0, 0)))
    row = lambda b, i: (b, i, 0)
    gain = pl.BlockSpec((1, d), lambda b, i: (0, 0))
    return pl.pallas_call(
        functools.partial(_xattn_kernel, n_heads=n_heads),
        grid=(bsz // nb, t // seq),
        in_specs=[pl.BlockSpec((nb, seq, d), row), mem_spec, mem_spec,
                  _resident(w_q.shape), _resident(w_xo.shape), gain, gain, gain],
        out_specs=[pl.BlockSpec((nb, seq, d), row), pl.BlockSpec((nb, seq, d), row)],
        out_shape=[jax.ShapeDtypeStruct((bsz, t, d), F32), jax.ShapeDtypeStruct((bsz, t, d), BF16)],
        scratch_shapes=[pltpu.VMEM((nb * seq, d), BF16), pltpu.VMEM((nb * seq, d), BF16)],
        compiler_params=_params("parallel", "parallel"),
        name="xattn",
    )(x, mem_k, mem_v, w_q, w_xo, g_pre, g_post, g_next)


def _up_geglu_kernel(*refs, starts, counts):
    n_groups = len(counts)
    ins, rest = refs[:3 * n_groups], refs[3 * n_groups:]
    wg_ref, wv_ref, cwg_ref, cwv_ref, cbg_ref, cbv_ref = rest[:6]
    outs, scr = rest[6:6 + 3 * n_groups], rest[6 + 3 * n_groups:]
    w_scrs, hists = scr[:2], scr[2:]
    width = cwg_ref.shape[0]
    top = SUBLANES
    i = pl.program_id(1)

    @pl.when(i == 0)
    def _():
        _cast_rows(wg_ref, w_scrs[0])
        _cast_rows(wv_ref, w_scrs[1])

    for g in range(n_groups):
        u_ref, st_refs = ins[3 * g], ins[3 * g + 1:3 * g + 3]
        hid_ref, ns_refs = outs[3 * g], outs[3 * g + 1:3 * g + 3]
        p_scrs = hists[2 * g:2 * g + 2]
        nb, seq, k_dim = u_ref.shape
        tn = hid_ref.shape[2]
        chunk = min(COL_CHUNK, tn)

        @pl.when(i == starts[g])
        def _(st_refs=st_refs, p_scrs=p_scrs):
            for st_ref, p_scr in zip(st_refs, p_scrs):
                p_scr[...] = jnp.zeros(p_scr.shape, F32)
                p_scr[:, top - (width - 1):top, :] = st_ref[...]

        @pl.when((i >= starts[g]) & (i < starts[g] + counts[g]))
        def _(u_ref=u_ref, hid_ref=hid_ref, ns_refs=ns_refs, p_scrs=p_scrs, nb=nb, seq=seq, k_dim=k_dim,
              tn=tn, chunk=chunk):
            u = u_ref[...].reshape(nb * seq, k_dim)
            for c in range(tn // chunk):
                cols = slice(c * chunk, (c + 1) * chunk)
                ys = []
                for w_scr, p_scr, cw_ref, cb_ref, ns_ref in zip(w_scrs, p_scrs, (cwg_ref, cwv_ref),
                                                                (cbg_ref, cbv_ref), ns_refs):
                    a = jnp.dot(u, w_scr[:, cols], preferred_element_type=F32).reshape(nb, seq, chunk)
                    ys.append(_causal_conv(a, p_scr[:, :, cols], cw_ref[:, cols], cb_ref[:, cols]))
                    p_scr[:, :, cols] = a[:, seq - top:, :]
                    ns_ref[:, :, cols] = p_scr[:, top - (width - 1):top, cols]
                hid_ref[:, :, cols] = (_gelu_tanh(ys[0]) * ys[1]).astype(BF16)


def _up_geglu(us, w_up, states, conv_w, conv_b, *, tm, tn):
    d = w_up.shape[0]
    d_ff = w_up.shape[1] // 2
    width = conv_w.shape[0]
    n_j = d_ff // tn
    nbs = [u.shape[0] if u.shape[1] <= tm else 1 for u in us]
    seqs = [min(tm, u.shape[1]) for u in us]
    counts = [u.shape[1] // s for u, s in zip(us, seqs)]
    assert all(u.shape[0] == nb and u.shape[1] % s == 0 and s % BF16_ROWS == 0 for u, nb, s in zip(us, nbs, seqs))
    assert d_ff % tn == 0 and tn % min(COL_CHUNK, tn) == 0
    starts, steps = _group_steps(counts)
    col_g = lambda j, i: (0, j)
    col_v = lambda j, i: (0, n_j + j)
    st_g = lambda j, i: (0, 0, j)
    st_v = lambda j, i: (0, 0, n_j + j)
    in_specs, out_specs, out_shape, hist, args = [], [], [], [], []
    for g, (u, st, nb, seq) in enumerate(zip(us, states, nbs, seqs)):
        row = lambda j, i, g=g: (0, _group_tile(i, starts[g], counts[g]), 0)
        out = lambda j, i, g=g: (0, _group_tile(i, starts[g], counts[g]), j)
        in_specs += [pl.BlockSpec((nb, seq, d), row),
                     pl.BlockSpec((nb, width - 1, tn), st_g), pl.BlockSpec((nb, width - 1, tn), st_v)]
        out_specs += [pl.BlockSpec((nb, seq, tn), out),
                      pl.BlockSpec((nb, width - 1, tn), st_g), pl.BlockSpec((nb, width - 1, tn), st_g)]
        out_shape += [jax.ShapeDtypeStruct((u.shape[0], u.shape[1], d_ff), BF16),
                      jax.ShapeDtypeStruct((u.shape[0], width - 1, d_ff), F32),
                      jax.ShapeDtypeStruct((u.shape[0], width - 1, d_ff), F32)]
        hist += [pltpu.VMEM((nb, SUBLANES, tn), F32), pltpu.VMEM((nb, SUBLANES, tn), F32)]
        args += [u, st, st]
    in_specs += [pl.BlockSpec((d, tn), col_g), pl.BlockSpec((d, tn), col_v),
                 pl.BlockSpec((width, tn), col_g), pl.BlockSpec((width, tn), col_v),
                 pl.BlockSpec((1, tn), col_g), pl.BlockSpec((1, tn), col_v)]
    res = pl.pallas_call(
        functools.partial(_up_geglu_kernel, starts=starts, counts=counts),
        grid=(n_j, steps),
        in_specs=in_specs,
        out_specs=out_specs,
        out_shape=out_shape,
        scratch_shapes=[pltpu.VMEM((d, tn), BF16), pltpu.VMEM((d, tn), BF16)] + hist,
        compiler_params=_params("arbitrary", "arbitrary"),
        name="up_geglu",
    )(*args, w_up, w_up, conv_w, conv_w, conv_b, conv_b)
    return [tuple(res[3 * g:3 * g + 3]) for g in range(len(us))]


def _ffn_down_kernel(hid_ref, x_ref, wd_ref, g_ref, o_ref):
    y = jnp.dot(hid_ref[...], wd_ref[...], preferred_element_type=F32)
    o_ref[...] = x_ref[...] + _rmsnorm(y, g_ref[...])


def _ffn_down(hid, x, w_down, g_post, *, tm):
    m, d = x.shape
    d_ff = hid.shape[1]
    tm = min(tm, m)
    assert m % tm == 0
    row = lambda i: (i, 0)
    return pl.pallas_call(
        _ffn_down_kernel,
        grid=(m // tm,),
        in_specs=[pl.BlockSpec((tm, d_ff), row), pl.BlockSpec((tm, d), row), _resident(w_down.shape),
                  pl.BlockSpec((1, d), lambda i: (0, 0))],
        out_specs=pl.BlockSpec((tm, d), row),
        out_shape=jax.ShapeDtypeStruct((m, d), F32),
        compiler_params=_params("parallel"),
        name="ffn_down",
    )(hid, x, w_down, g_post)


MM_TM, MM_TN = 2048, 1024
UP_TM, UP_TN = 1024, 512
SEQ_TILE = 256
ROW_TILE = 512
NORM_TM = 512
N_XHEADS = 4


def _layer(groups, p):
    flat = [x.reshape(-1, x.shape[-1]) for x, *_ in groups]
    u0s = [_rmsnorm_call(x2, p['g_mix_pre'], tm=NORM_TM) for x2 in flat]
    zs = _proj(u0s, p['w_in'], BF16, tm=MM_TM, tn=MM_TN)
    mids = []
    for (x, pos0, mem_k, mem_v, s_a, h0, s_b, _), x2, z in zip(groups, flat, zs):
        bsz, t, d = x.shape
        m = bsz * t
        ya, yb, ns_a, nh, ns_b = _mixer_pre(z.reshape(bsz, t, -1), s_a, h0[:, None, :], s_b,
                                            p['conv_a_w'], p['conv_a_b'], p['w_ri'], p['b_r'], p['b_i'],
                                            p['lru_lambda'], p['conv_b_w'], seq_tile=SEQ_TILE, pos0=pos0)
        x1 = _mixer_post(ya.reshape(m, -1), yb.reshape(m, -1), z, x2, p['p_a'], p['p_b'], p['w_o'],
                         p['g_mix_post'], tm=ROW_TILE)
        x2a, u3 = _xattn(x1.reshape(bsz, t, d), mem_k, mem_v, p['w_q'], p['w_xo'], p['g_x_pre'], p['g_x_post'],
                         p['g_ffn_pre'], n_heads=N_XHEADS, seq_tile=ROW_TILE)
        mids.append((x2a, u3, ns_a, nh[:, 0, :], ns_b))
    ffn = _up_geglu([u3 for _, u3, *_ in mids], p['w_up'], [g[7] for g in groups],
                    p['ffn_conv_w'], p['ffn_conv_b'], tm=UP_TM, tn=UP_TN)
    results = []
    for (x2a, _, ns_a, nh, ns_b), (hid, ns_fg, ns_fv) in zip(mids, ffn):
        bsz, t, d = x2a.shape
        x3 = _ffn_down(hid.reshape(bsz * t, -1), x2a.reshape(bsz * t, d), p['w_down'], p['g_ffn_post'], tm=ROW_TILE)
        results.append((x3.reshape(bsz, t, d), ns_a, nh, ns_b, jnp.concatenate([ns_fg, ns_fv], axis=-1)))
    return results


def kernel(x_prompt, x_sample, mem_prompt, state_conv_a, state_rglru, state_conv_b, state_ffn_conv, cache_mem_k, cache_mem_v, g_mix_pre, g_mix_post, w_in, conv_a_w, conv_a_b, w_r, b_r, w_i, b_i, lru_lambda, conv_b_w, p_a, p_b, w_o, g_x_pre, g_x_post, g_mem, w_q, w_k, w_v, w_xo, g_ffn_pre, g_ffn_post, w_up, ffn_conv_w, ffn_conv_b, w_down):
    depth = w_in.shape[0]
    bsz, _, d = x_prompt.shape
    n_mem = mem_prompt.shape[1]
    yp, ys = x_prompt, x_sample
    outs = [[] for _ in range(10)]
    row = lambda v: v.reshape(1, -1).astype(F32)
    for l in range(depth):
        p = {'g_mix_pre': row(g_mix_pre[l]), 'g_mix_post': row(g_mix_post[l]), 'w_in': w_in[l],
             'conv_a_w': conv_a_w[l], 'conv_a_b': row(conv_a_b[l]),
             'w_ri': jnp.concatenate([w_r[l], w_i[l]], axis=-1).astype(BF16),
             'b_r': row(b_r[l]), 'b_i': row(b_i[l]), 'lru_lambda': row(lru_lambda[l]),
             'conv_b_w': conv_b_w[l], 'p_a': p_a[l].astype(BF16), 'p_b': p_b[l].astype(BF16),
             'w_o': w_o[l].astype(BF16), 'g_x_pre': row(g_x_pre[l]), 'g_x_post': row(g_x_post[l]),
             'w_q': w_q[l].astype(BF16), 'w_xo': w_xo[l].astype(BF16),
             'g_ffn_pre': row(g_ffn_pre[l]), 'g_ffn_post': row(g_ffn_post[l]), 'w_up': w_up[l],
             'ffn_conv_w': ffn_conv_w[l], 'ffn_conv_b': row(ffn_conv_b[l]), 'w_down': w_down[l].astype(BF16)}
        d_rnn, d_conv, d_up = conv_a_w.shape[2], conv_b_w.shape[2], ffn_conv_w.shape[2]
        mem_u = _rmsnorm_call(mem_prompt.reshape(bsz * n_mem, d), row(g_mem[l]), tm=NORM_TM)
        mk = _proj([mem_u], w_k[l], F32, tm=MM_TM, tn=MM_TN)[0].reshape(bsz, n_mem, d)
        mv = _proj([mem_u], w_v[l], F32, tm=MM_TM, tn=MM_TN)[0].reshape(bsz, n_mem, d)
        zeros = lambda *s: jnp.zeros(s, F32)
        prompt = (yp, 0, mk.astype(BF16), mv.astype(BF16),
                  zeros(bsz, conv_a_w.shape[1] - 1, d_rnn), zeros(bsz, d_rnn),
                  zeros(bsz, conv_b_w.shape[1] - 1, d_conv), zeros(bsz, ffn_conv_w.shape[1] - 1, d_up))
        dec_b = x_sample.shape[0]
        ck = cache_mem_k[l].reshape(dec_b, n_mem, d).astype(BF16)
        cv = cache_mem_v[l].reshape(dec_b, n_mem, d).astype(BF16)
        sample = (ys, PAST_LEN, ck, cv, state_conv_a[l], state_rglru[l], state_conv_b[l], state_ffn_conv[l])
        (ys, *s_states), (yp, *p_states) = _layer([sample, prompt], p)
        for o, v in zip(outs, (*p_states, mk.reshape(bsz, n_mem, N_XHEADS, -1),
                               mv.reshape(bsz, n_mem, N_XHEADS, -1), *s_states)):
            o.append(v)
    return (yp, ys) + tuple(jnp.stack(o) for o in outs)
```

```python
import functools

import jax
import jax.numpy as jnp
from jax import lax
from jax.experimental import pallas as pl
from jax.experimental.pallas import tpu as pltpu

F32 = jnp.float32
BF16 = jnp.bfloat16

EPS = 1e-6
RG_C = 8.0
PAST_LEN = 2048

LANES = 128
SUBLANES = 8
BF16_ROWS = 16
MXU_COLS = 256
VMEM_LIMIT_BYTES = 60 * 1024 * 1024

ROW_CHUNK = 128
CAST_ROWS = 256
COL_CHUNK = MXU_COLS
SCAN_LANES = 512


def _params(*semantics):
    return pltpu.CompilerParams(dimension_semantics=semantics, vmem_limit_bytes=VMEM_LIMIT_BYTES)


def _resident(shape):
    nd = len(shape)
    return pl.BlockSpec(shape, lambda *_: (0,) * nd, pipeline_mode=pl.Buffered(1))


def _rmsnorm(x, g):
    y = x * lax.rsqrt(jnp.mean(x * x, axis=-1, keepdims=True) + EPS)
    return y * g


def _sigmoid(x):
    return 0.5 * jnp.tanh(0.5 * x) + 0.5


def _gelu_tanh(x):
    c1 = (2.0 / jnp.pi) ** 0.5
    return x * (0.5 * jnp.tanh(x * (c1 + (c1 * 0.044715) * (x * x))) + 0.5)


def _causal_conv(x, before, w, b):
    nb, t, c = x.shape
    width = w.shape[0]
    groups = t // SUBLANES
    full = jnp.concatenate([before, x], axis=1).reshape(nb * (groups + 1), SUBLANES, c)
    row = lax.broadcasted_iota(jnp.int32, (1, 1, SUBLANES, c), 2)
    y = None
    for k in range(width):
        lag = width - 1 - k
        if lag == 0:
            tap = x
        else:
            rot = pltpu.roll(full, lag, 1).reshape(nb, groups + 1, SUBLANES, c)
            tap = jnp.where(row >= lag, rot[:, 1:], rot[:, :-1]).reshape(nb, t, c)
        y = tap * w[k:k + 1, :] if y is None else y + tap * w[k:k + 1, :]
    return y if b is None else y + b


def _softplus(x):
    return jnp.maximum(x, 0.0) + jnp.log1p(jnp.exp(-jnp.abs(x)))


def _rmsnorm_kernel(x_ref, g_ref, u_ref):
    tm = x_ref.shape[0]
    chunk = min(ROW_CHUNK, tm)
    g = g_ref[...]

    def body(c, _):
        rows = pl.ds(pl.multiple_of(c * chunk, chunk), chunk)
        u_ref[rows, :] = _rmsnorm(x_ref[rows, :], g).astype(BF16)
        return None

    lax.fori_loop(0, tm // chunk, body, None)


def _rmsnorm_call(x, g, *, tm):
    m, d = x.shape
    tm = min(tm, m)
    assert m % tm == 0 and tm % min(ROW_CHUNK, tm) == 0
    return pl.pallas_call(
        _rmsnorm_kernel,
        grid=(m // tm,),
        in_specs=[pl.BlockSpec((tm, d), lambda i: (i, 0)), pl.BlockSpec((1, d), lambda i: (0, 0))],
        out_specs=pl.BlockSpec((tm, d), lambda i: (i, 0)),
        out_shape=jax.ShapeDtypeStruct((m, d), BF16),
        compiler_params=_params("parallel"),
        name="rmsnorm",
    )(x, g)


def _cast_rows(src_ref, dst_ref):
    rows_total = src_ref.shape[0]
    chunk = min(CAST_ROWS, rows_total)

    def body(c, _):
        rows = pl.ds(pl.multiple_of(c * chunk, chunk), chunk)
        dst_ref[rows, :] = src_ref[rows, :].astype(BF16)
        return None

    lax.fori_loop(0, rows_total // chunk, body, None)


def _side_cast_plan(arrays, n_steps, flat_step):
    in_specs, out_specs, out_shape = [], [], []
    for a in arrays:
        rows, cols = a.shape
        chunk = -(-rows // n_steps)
        chunk = -(-chunk // BF16_ROWS) * BF16_ROWS
        assert rows % chunk == 0
        last = rows // chunk - 1
        spec = pl.BlockSpec((chunk, cols), lambda *idx, last=last: (jnp.minimum(flat_step(*idx), last), 0))
        in_specs.append(spec)
        out_specs.append(spec)
        out_shape.append(jax.ShapeDtypeStruct((rows, cols), BF16))
    return in_specs, out_specs, out_shape


def _side_cast(src_refs, dst_refs):
    for src, dst in zip(src_refs, dst_refs):
        dst[...] = src[...].astype(BF16)


def _group_steps(counts):
    starts = [sum(counts[:g]) for g in range(len(counts))]
    return starts, sum(counts)


def _group_tile(i, start, count):
    return jnp.clip(i - start, 0, count - 1)


def _proj_kernel(*refs, starts, counts):
    n_groups = len(counts)
    u_refs, w_ref, o_refs, w_scr = refs[:n_groups], refs[n_groups], refs[n_groups + 1:-1], refs[-1]
    i = pl.program_id(1)

    @pl.when(i == 0)
    def _():
        _cast_rows(w_ref, w_scr)

    for g in range(n_groups):
        @pl.when((i >= starts[g]) & (i < starts[g] + counts[g]))
        def _(g=g):
            o_refs[g][...] = jnp.dot(u_refs[g][...], w_scr[...],
                                     preferred_element_type=F32).astype(o_refs[g].dtype)


def _proj(us, w, out_dtype, *, tm, tn):
    d, n = w.shape
    tn = min(tn, n)
    tms = [min(tm, u.shape[0]) for u in us]
    counts = [u.shape[0] // t for u, t in zip(us, tms)]
    assert all(u.shape[0] % t == 0 for u, t in zip(us, tms)) and n % tn == 0 and d % min(CAST_ROWS, d) == 0
    starts, steps = _group_steps(counts)
    tile = lambda g: (lambda j, i: (_group_tile(i, starts[g], counts[g]), 0))
    out_tile = lambda g: (lambda j, i: (_group_tile(i, starts[g], counts[g]), j))
    return pl.pallas_call(
        functools.partial(_proj_kernel, starts=starts, counts=counts),
        grid=(n // tn, steps),
        in_specs=([pl.BlockSpec((tms[g], d), tile(g)) for g in range(len(us))]
                  + [pl.BlockSpec((d, tn), lambda j, i: (0, j))]),
        out_specs=[pl.BlockSpec((tms[g], tn), out_tile(g)) for g in range(len(us))],
        out_shape=[jax.ShapeDtypeStruct((u.shape[0], n), out_dtype) for u in us],
        scratch_shapes=[pltpu.VMEM((d, tn), BF16)],
        compiler_params=_params("arbitrary", "arbitrary"),
        name="proj",
    )(*us, w)


def _mixer_pre_kernel(*refs, d_rnn, d_conv, pos0, n_side):
    (z_ref, sa_ref, h0_ref, sb_ref, caw_ref, cab_ref, wri_ref, br_ref, bi_ref, lam_ref, cbw_ref) = refs[:11]
    side_in = refs[11:11 + n_side]
    ya_ref, yb_ref, nsa_ref, nh_ref, nsb_ref = refs[11 + n_side:16 + n_side]
    side_out = refs[16 + n_side:16 + 2 * n_side]
    xa_scr, xc_scr, xcb_scr, ri_scr, cb_scr, h_scr = refs[16 + 2 * n_side:]
    _side_cast(side_in, side_out)
    t = pl.program_id(1)
    nb, seq, _ = ya_ref.shape
    wa = caw_ref.shape[0]
    wb = cbw_ref.shape[0]
    n_heads, head_dim, _ = wri_ref.shape
    top = SUBLANES

    @pl.when(t == 0)
    def _():
        xa_scr[:, top - (wa - 1):top, :] = sa_ref[...]
        cb_scr[:, top - (wb - 1):top, :] = sb_ref[...]
        h_scr[...] = jnp.broadcast_to(h0_ref[...], h_scr.shape)

    xa_scr[:, top:top + seq, :] = z_ref[:, :, 0:d_rnn].astype(F32)
    base = top - (wa - 1)
    xc = xa_scr[:, base:base + seq, :] * caw_ref[0:1, :]
    for k in range(1, wa):
        xc = xc + xa_scr[:, base + k:base + k + seq, :] * caw_ref[k:k + 1, :]
    xc = xc + cab_ref[...]
    xc_scr[...] = xc
    xcb_scr[...] = xc.reshape(nb * seq, d_rnn).astype(BF16)
    nsa_ref[...] = xa_scr[:, top + seq - (wa - 1):top + seq, :]
    xa_scr[:, 0:top, :] = xa_scr[:, seq:seq + top, :]

    for h in range(n_heads):
        cols = slice(h * head_dim, (h + 1) * head_dim)
        ri = jnp.dot(xcb_scr[:, cols], wri_ref[h], preferred_element_type=F32)
        ri_scr[:, :, cols] = ri[:, :head_dim].reshape(nb, seq, head_dim)
        ri_scr[:, :, d_rnn + h * head_dim:d_rnn + (h + 1) * head_dim] = ri[:, head_dim:].reshape(nb, seq, head_dim)

    strip = max(LANES, SCAN_LANES // nb)
    rows_per_iter = BF16_ROWS
    row_in_group = lax.broadcasted_iota(jnp.int32, (1, SUBLANES, strip), 1)
    for c in range(d_rnn // strip):
        cs = slice(c * strip, (c + 1) * strip)
        cs_i = slice(d_rnn + c * strip, d_rnn + (c + 1) * strip)
        cs_g = slice(d_rnn + c * strip, d_rnn + (c + 1) * strip)
        sp = _softplus(-lam_ref[:, cs])
        b_r = br_ref[:, cs]
        b_i = bi_ref[:, cs]

        def body(it, carry, cs=cs, cs_i=cs_i, cs_g=cs_g, sp=sp, b_r=b_r, b_i=b_i):
            r0 = pl.multiple_of(it * rows_per_iter, rows_per_iter)
            rows = pl.ds(r0, rows_per_iter)
            r = jax.nn.sigmoid(ri_scr[:, rows, cs] + b_r)
            i = jax.nn.sigmoid(ri_scr[:, rows, cs_i] + b_i)
            log_a = -RG_C * r * sp
            a = jnp.exp(log_a)
            mult = jnp.sqrt(-jnp.tanh(log_a) * (a * a + 1.0))
            if pos0 == 0:
                pos = t * seq + r0 + lax.broadcasted_iota(jnp.int32, (1, rows_per_iter, strip), 1)
                mult = jnp.where(pos == 0, 1.0, mult)
            b = mult * i * xc_scr[:, rows, cs]
            hs = []
            for sub in range(rows_per_iter // SUBLANES):
                a8 = a[:, sub * SUBLANES:(sub + 1) * SUBLANES, :]
                b8 = b[:, sub * SUBLANES:(sub + 1) * SUBLANES, :]
                for s in (1, 2, 4):
                    keep = row_in_group >= s
                    a_prev = pltpu.roll(a8, s, 1)
                    b_prev = pltpu.roll(b8, s, 1)
                    b8 = jnp.where(keep, a8 * b_prev + b8, b8)
                    a8 = jnp.where(keep, a8 * a_prev, a8)
                h8 = a8 * carry + b8
                carry = jnp.broadcast_to(h8[:, SUBLANES - 1:SUBLANES, :], h8.shape)
                hs.append(h8)
            h = jnp.concatenate(hs, axis=1)
            gate = jax.nn.gelu(z_ref[:, rows, cs_g].astype(F32), approximate=True)
            ya_ref[:, rows, cs] = (h * gate).astype(BF16)
            return carry

        last = lax.fori_loop(0, seq // rows_per_iter, body, h_scr[:, :, cs])
        h_scr[:, :, cs] = last
        nh_ref[:, :, cs] = last[:, 0:1, :]

    o_gb, o_gc, o_hb = 2 * d_rnn, 2 * d_rnn + d_conv, 2 * d_rnn + 2 * d_conv
    cb_scr[:, top:top + seq, :] = (z_ref[:, :, o_gc:o_gc + d_conv].astype(F32)
                                   * z_ref[:, :, o_hb:o_hb + d_conv].astype(F32))
    base = top - (wb - 1)
    co = cb_scr[:, base:base + seq, :] * cbw_ref[0:1, :]
    for k in range(1, wb):
        co = co + cb_scr[:, base + k:base + k + seq, :] * cbw_ref[k:k + 1, :]
    yb_ref[...] = (z_ref[:, :, o_gb:o_gb + d_conv].astype(F32) * co).astype(BF16)
    nsb_ref[...] = cb_scr[:, top + seq - (wb - 1):top + seq, :]
    cb_scr[:, 0:top, :] = cb_scr[:, seq:seq + top, :]


def _mixer_pre(z, state_a, h0, state_b, conv_a_w, conv_a_b, w_ri, b_r, b_i, lam, conv_b_w, *, seq_tile, pos0,
               to_round=()):
    bsz, t, _ = z.shape
    d_rnn = conv_a_w.shape[1]
    d_conv = conv_b_w.shape[1]
    n_pre = 2 * d_rnn + 3 * d_conv
    nb = bsz if t <= seq_tile else 1
    seq = min(seq_tile, t)
    assert t % seq == 0 and seq % BF16_ROWS == 0 and bsz % nb == 0
    wa, wb = conv_a_w.shape[0], conv_b_w.shape[0]
    kern = functools.partial(_mixer_pre_kernel, d_rnn=d_rnn, d_conv=d_conv, pos0=pos0, n_side=len(to_round))
    n_t = t // seq
    side_in, side_out, side_shape = _side_cast_plan(to_round, (bsz // nb) * n_t, lambda b, i: b * n_t + i)
    row = lambda b, i: (b, i, 0)
    per_b = lambda b, i: (b, 0, 0)
    const2 = lambda b, i: (0, 0)
    return pl.pallas_call(
        kern,
        grid=(bsz // nb, t // seq),
        in_specs=[pl.BlockSpec((nb, seq, n_pre), row),
                  pl.BlockSpec((nb, wa - 1, d_rnn), per_b),
                  pl.BlockSpec((nb, 1, d_rnn), per_b),
                  pl.BlockSpec((nb, wb - 1, d_conv), per_b),
                  pl.BlockSpec(conv_a_w.shape, const2),
                  pl.BlockSpec((1, d_rnn), const2),
                  pl.BlockSpec(w_ri.shape, lambda b, i: (0, 0, 0)),
                  pl.BlockSpec((1, d_rnn), const2),
                  pl.BlockSpec((1, d_rnn), const2),
                  pl.BlockSpec((1, d_rnn), const2),
                  pl.BlockSpec(conv_b_w.shape, const2)] + side_in,
        out_specs=[pl.BlockSpec((nb, seq, d_rnn), row),
                   pl.BlockSpec((nb, seq, d_conv), row),
                   pl.BlockSpec((nb, wa - 1, d_rnn), per_b),
                   pl.BlockSpec((nb, 1, d_rnn), per_b),
                   pl.BlockSpec((nb, wb - 1, d_conv), per_b)] + side_out,
        out_shape=[jax.ShapeDtypeStruct((bsz, t, d_rnn), BF16),
                   jax.ShapeDtypeStruct((bsz, t, d_conv), BF16),
                   jax.ShapeDtypeStruct((bsz, wa - 1, d_rnn), F32),
                   jax.ShapeDtypeStruct((bsz, 1, d_rnn), F32),
                   jax.ShapeDtypeStruct((bsz, wb - 1, d_conv), F32)] + side_shape,
        scratch_shapes=[pltpu.VMEM((nb, SUBLANES + seq, d_rnn), F32),
                        pltpu.VMEM((nb, seq, d_rnn), F32),
                        pltpu.VMEM((nb * seq, d_rnn), BF16),
                        pltpu.VMEM((nb, seq, 2 * d_rnn), F32),
                        pltpu.VMEM((nb, SUBLANES + seq, d_conv), F32),
                        pltpu.VMEM((nb, SUBLANES, d_rnn), F32)],
        compiler_params=_params("arbitrary", "arbitrary"),
        name="mixer_pre",
    )(z, state_a, h0, state_b, conv_a_w, conv_a_b, w_ri, b_r, b_i, lam, conv_b_w, *to_round)


def _mixer_post_kernel(ya_ref, yb_ref, ga0_ref, ga1_ref, gb0_ref, gb1_ref, x_ref, pa_ref, pb_ref, wo_ref, g_ref,
                       o_ref, mix_scr):
    half = ga0_ref.shape[1]
    y_a = jnp.dot(ya_ref[...], pa_ref[...], preferred_element_type=F32)
    y_b = jnp.dot(yb_ref[...], pb_ref[...], preferred_element_type=F32)
    for c, (ga_ref, gb_ref) in enumerate(((ga0_ref, gb0_ref), (ga1_ref, gb1_ref))):
        cols = slice(c * half, (c + 1) * half)
        mix = (jax.nn.sigmoid(ga_ref[...].astype(F32)) * y_a[:, cols]
               + jax.nn.sigmoid(gb_ref[...].astype(F32)) * y_b[:, cols])
        mix_scr[:, cols] = mix.astype(BF16)
    out = jnp.dot(mix_scr[...], wo_ref[...], preferred_element_type=F32)
    o_ref[...] = x_ref[...] + _rmsnorm(out, g_ref[...])


def _mixer_post(ya, yb, z, x, p_a, p_b, w_o, g_post, *, tm):
    m, d = x.shape
    d_rnn, d_conv = ya.shape[1], yb.shape[1]
    tm = min(tm, m)
    half = d // 2
    gate0 = (2 * d_rnn + 3 * d_conv) // half
    assert m % tm == 0 and (2 * d_rnn + 3 * d_conv) % half == 0
    gate_spec = lambda k: pl.BlockSpec((tm, half), lambda i: (i, gate0 + k))
    row = lambda i: (i, 0)
    return pl.pallas_call(
        _mixer_post_kernel,
        grid=(m // tm,),
        in_specs=[pl.BlockSpec((tm, d_rnn), row), pl.BlockSpec((tm, d_conv), row),
                  gate_spec(0), gate_spec(1), gate_spec(2), gate_spec(3),
                  pl.BlockSpec((tm, d), row),
                  _resident(p_a.shape), _resident(p_b.shape), _resident(w_o.shape),
                  pl.BlockSpec((1, d), lambda i: (0, 0))],
        out_specs=pl.BlockSpec((tm, d), row),
        out_shape=jax.ShapeDtypeStruct((m, d), F32),
        scratch_shapes=[pltpu.VMEM((tm, d), BF16)],
        compiler_params=_params("parallel"),
        name="mixer_post",
    )(ya, yb, z, z, z, z, x, p_a, p_b, w_o, g_post)


def _xattn_kernel(x_ref, k_ref, v_ref, wq_ref, wxo_ref, gpre_ref, gpost_ref, gnext_ref, o_ref, u_ref, q_scr, o_scr,
                  *, n_heads):
    nb, seq, d = x_ref.shape
    hd = d // n_heads
    scale = hd ** -0.5
    x = x_ref[...].reshape(nb * seq, d)
    u = _rmsnorm(x, gpre_ref[...]).astype(BF16)
    q_scr[...] = jnp.dot(u, wq_ref[...], preferred_element_type=F32).astype(BF16)
    for b in range(nb):
        rows = slice(b * seq, (b + 1) * seq)
        for h in range(n_heads):
            cols = slice(h * hd, (h + 1) * hd)
            s = lax.dot_general(q_scr[rows, cols], k_ref[b, :, cols], (((1,), (1,)), ((), ())),
                                preferred_element_type=F32) * scale
            p = jnp.exp(s - jnp.max(s, axis=-1, keepdims=True))
            att = (p / jnp.sum(p, axis=-1, keepdims=True)).astype(BF16)
            o_scr[rows, cols] = jnp.dot(att, v_ref[b, :, cols], preferred_element_type=F32).astype(BF16)
    out = jnp.dot(o_scr[...], wxo_ref[...], preferred_element_type=F32)
    x_new = x + _rmsnorm(out, gpost_ref[...])
    o_ref[...] = x_new.reshape(nb, seq, d)
    u_ref[...] = _rmsnorm(x_new, gnext_ref[...]).astype(BF16).reshape(nb, seq, d)


def _xattn(x, mem_k, mem_v, w_q, w_xo, g_pre, g_post, g_next, *, n_heads, seq_tile):
    bsz, t, d = x.shape
    n_mem = mem_k.shape[1]
    nb = bsz if t <= seq_tile else 1
    seq = min(seq_tile, t)
    assert t % seq == 0 and seq % BF16_ROWS == 0 and bsz % nb == 0
    single = bsz // nb == 1
    mem_spec = (_resident((nb, n_mem, d)) if single else pl.BlockSpec((nb, n_mem, d), lambda b, i: (b, 0, 0)))
    row = lambda b, i: (b, i, 0)
    gain = pl.BlockSpec((1, d), lambda b, i: (0, 0))
    return pl.pallas_call(
        functools.partial(_xattn_kernel, n_heads=n_heads),
        grid=(bsz // nb, t // seq),
        in_specs=[pl.BlockSpec((nb, seq, d), row), mem_spec, mem_spec,
                  _resident(w_q.shape), _resident(w_xo.shape), gain, gain, gain],
        out_specs=[pl.BlockSpec((nb, seq, d), row), pl.BlockSpec((nb, seq, d), row)],
        out_shape=[jax.ShapeDtypeStruct((bsz, t, d), F32), jax.ShapeDtypeStruct((bsz, t, d), BF16)],
        scratch_shapes=[pltpu.VMEM((nb * seq, d), BF16), pltpu.VMEM((nb * seq, d), BF16)],
        compiler_params=_params("parallel", "parallel"),
        name="xattn",
    )(x, mem_k, mem_v, w_q, w_xo, g_pre, g_post, g_next)


def _up_geglu_kernel(*refs, starts, counts, n_side):
    n_groups = len(counts)
    ins, rest = refs[:3 * n_groups], refs[3 * n_groups:]
    wg_ref, wv_ref, cwg_ref, cwv_ref, cbg_ref, cbv_ref = rest[:6]
    side_in, rest = rest[6:6 + n_side], rest[6 + n_side:]
    outs, rest = rest[:3 * n_groups], rest[3 * n_groups:]
    side_out, scr = rest[:n_side], rest[n_side:]
    w_scrs, hists = scr[:2], scr[2:]
    _side_cast(side_in, side_out)
    width = cwg_ref.shape[0]
    top = SUBLANES
    i = pl.program_id(1)

    @pl.when(i == 0)
    def _():
        _cast_rows(wg_ref, w_scrs[0])
        _cast_rows(wv_ref, w_scrs[1])

    for g in range(n_groups):
        u_ref, st_refs = ins[3 * g], ins[3 * g + 1:3 * g + 3]
        hid_ref, ns_refs = outs[3 * g], outs[3 * g + 1:3 * g + 3]
        p_scrs = hists[2 * g:2 * g + 2]
        nb, seq, k_dim = u_ref.shape
        tn = hid_ref.shape[2]
        chunk = min(COL_CHUNK, tn)

        @pl.when(i == starts[g])
        def _(st_refs=st_refs, p_scrs=p_scrs):
            for st_ref, p_scr in zip(st_refs, p_scrs):
                p_scr[...] = jnp.zeros(p_scr.shape, F32)
                p_scr[:, top - (width - 1):top, :] = st_ref[...]

        @pl.when((i >= starts[g]) & (i < starts[g] + counts[g]))
        def _(u_ref=u_ref, hid_ref=hid_ref, ns_refs=ns_refs, p_scrs=p_scrs, nb=nb, seq=seq, k_dim=k_dim,
              tn=tn, chunk=chunk):
            u = u_ref[...].reshape(nb * seq, k_dim)
            for c in range(tn // chunk):
                cols = slice(c * chunk, (c + 1) * chunk)
                ys = []
                for w_scr, p_scr, cw_ref, cb_ref, ns_ref in zip(w_scrs, p_scrs, (cwg_ref, cwv_ref),
                                                                (cbg_ref, cbv_ref), ns_refs):
                    a = jnp.dot(u, w_scr[:, cols], preferred_element_type=F32).reshape(nb, seq, chunk)
                    ys.append(_causal_conv(a, p_scr[:, :, cols], cw_ref[:, cols], cb_ref[:, cols]))
                    p_scr[:, :, cols] = a[:, seq - top:, :]
                    ns_ref[:, :, cols] = p_scr[:, top - (width - 1):top, cols]
                hid_ref[:, :, cols] = (_gelu_tanh(ys[0]) * ys[1]).astype(BF16)


def _up_geglu(us, w_up, states, conv_w, conv_b, *, tm, tn, to_round=()):
    d = w_up.shape[0]
    d_ff = w_up.shape[1] // 2
    width = conv_w.shape[0]
    n_j = d_ff // tn
    nbs = [u.shape[0] if u.shape[1] <= tm else 1 for u in us]
    seqs = [min(tm, u.shape[1]) for u in us]
    counts = [u.shape[1] // s for u, s in zip(us, seqs)]
    assert all(u.shape[0] == nb and u.shape[1] % s == 0 and s % BF16_ROWS == 0 for u, nb, s in zip(us, nbs, seqs))
    assert d_ff % tn == 0 and tn % min(COL_CHUNK, tn) == 0
    starts, steps = _group_steps(counts)
    col_g = lambda j, i: (0, j)
    col_v = lambda j, i: (0, n_j + j)
    st_g = lambda j, i: (0, 0, j)
    st_v = lambda j, i: (0, 0, n_j + j)
    in_specs, out_specs, out_shape, hist, args = [], [], [], [], []
    for g, (u, st, nb, seq) in enumerate(zip(us, states, nbs, seqs)):
        row = lambda j, i, g=g: (0, _group_tile(i, starts[g], counts[g]), 0)
        out = lambda j, i, g=g: (0, _group_tile(i, starts[g], counts[g]), j)
        in_specs += [pl.BlockSpec((nb, seq, d), row),
                     pl.BlockSpec((nb, width - 1, tn), st_g), pl.BlockSpec((nb, width - 1, tn), st_v)]
        out_specs += [pl.BlockSpec((nb, seq, tn), out),
                      pl.BlockSpec((nb, width - 1, tn), st_g), pl.BlockSpec((nb, width - 1, tn), st_g)]
        out_shape += [jax.ShapeDtypeStruct((u.shape[0], u.shape[1], d_ff), BF16),
                      jax.ShapeDtypeStruct((u.shape[0], width - 1, d_ff), F32),
                      jax.ShapeDtypeStruct((u.shape[0], width - 1, d_ff), F32)]
        hist += [pltpu.VMEM((nb, SUBLANES, tn), F32), pltpu.VMEM((nb, SUBLANES, tn), F32)]
        args += [u, st, st]
    side_in, side_out, side_shape = _side_cast_plan(to_round, n_j * steps, lambda j, i: j * steps + i)
    in_specs += [pl.BlockSpec((d, tn), col_g), pl.BlockSpec((d, tn), col_v),
                 pl.BlockSpec((width, tn), col_g), pl.BlockSpec((width, tn), col_v),
                 pl.BlockSpec((1, tn), col_g), pl.BlockSpec((1, tn), col_v)] + side_in
    res = pl.pallas_call(
        functools.partial(_up_geglu_kernel, starts=starts, counts=counts, n_side=len(to_round)),
        grid=(n_j, steps),
        in_specs=in_specs,
        out_specs=out_specs + side_out,
        out_shape=out_shape + side_shape,
        scratch_shapes=[pltpu.VMEM((d, tn), BF16), pltpu.VMEM((d, tn), BF16)] + hist,
        compiler_params=_params("arbitrary", "arbitrary"),
        name="up_geglu",
    )(*args, w_up, w_up, conv_w, conv_w, conv_b, conv_b, *to_round)
    return [tuple(res[3 * g:3 * g + 3]) for g in range(len(us))], list(res[3 * len(us):])


def _ffn_down_kernel(hid_ref, x_ref, wd_ref, g_ref, o_ref):
    y = jnp.dot(hid_ref[...], wd_ref[...], preferred_element_type=F32)
    o_ref[...] = x_ref[...] + _rmsnorm(y, g_ref[...])


def _ffn_down(hid, x, w_down, g_post, *, tm):
    m, d = x.shape
    d_ff = hid.shape[1]
    tm = min(tm, m)
    assert m % tm == 0
    row = lambda i: (i, 0)
    return pl.pallas_call(
        _ffn_down_kernel,
        grid=(m // tm,),
        in_specs=[pl.BlockSpec((tm, d_ff), row), pl.BlockSpec((tm, d), row), _resident(w_down.shape),
                  pl.BlockSpec((1, d), lambda i: (0, 0))],
        out_specs=pl.BlockSpec((tm, d), row),
        out_shape=jax.ShapeDtypeStruct((m, d), F32),
        compiler_params=_params("parallel"),
        name="ffn_down",
    )(hid, x, w_down, g_post)


MM_TM, MM_TN = 2048, 1024
UP_TM, UP_TN = 1024, 512
SEQ_TILE = 256
ROW_TILE = 512
NORM_TM = 512
N_XHEADS = 4


def _layer(groups, p):
    flat = [x.reshape(-1, x.shape[-1]) for x, *_ in groups]
    u0s = [_rmsnorm_call(x2, p['g_mix_pre'], tm=NORM_TM) for x2 in flat]
    zs = _proj(u0s, p['w_in'], BF16, tm=MM_TM, tn=MM_TN)
    big = max(range(len(groups)), key=lambda g: flat[g].shape[0])
    resident = [p['p_a'], p['p_b'], p['w_o'], p['w_q'], p['w_xo']]
    mems = [m.reshape(-1, m.shape[-1]) for g in groups for m in g[2:4]]
    pre = []
    for g, ((x, pos0, _, _, s_a, h0, s_b, _), z) in enumerate(zip(groups, zs)):
        bsz, t, _ = x.shape
        pre.append(_mixer_pre(z.reshape(bsz, t, -1), s_a, h0[:, None, :], s_b,
                              p['conv_a_w'], p['conv_a_b'], p['w_ri'], p['b_r'], p['b_i'],
                              p['lru_lambda'], p['conv_b_w'], seq_tile=SEQ_TILE, pos0=pos0,
                              to_round=resident + mems if g == big else ()))
    p_a, p_b, w_o, w_q, w_xo, *mems = pre[big][5:]
    mids = []
    for g, ((x, *_), x2, z) in enumerate(zip(groups, flat, zs)):
        bsz, t, d = x.shape
        m = bsz * t
        ya, yb, ns_a, nh, ns_b = pre[g][:5]
        mem_k, mem_v = (mm.reshape(bsz, -1, d) for mm in mems[2 * g:2 * g + 2])
        x1 = _mixer_post(ya.reshape(m, -1), yb.reshape(m, -1), z, x2, p_a, p_b, w_o, p['g_mix_post'], tm=ROW_TILE)
        x2a, u3 = _xattn(x1.reshape(bsz, t, d), mem_k, mem_v, w_q, w_xo, p['g_x_pre'], p['g_x_post'],
                         p['g_ffn_pre'], n_heads=N_XHEADS, seq_tile=ROW_TILE)
        mids.append((x2a, u3, ns_a, nh[:, 0, :], ns_b))
    ffn, (w_down,) = _up_geglu([u3 for _, u3, *_ in mids], p['w_up'], [g[7] for g in groups],
                               p['ffn_conv_w'], p['ffn_conv_b'], tm=UP_TM, tn=UP_TN, to_round=[p['w_down']])
    results = []
    for (x2a, _, ns_a, nh, ns_b), (hid, ns_fg, ns_fv) in zip(mids, ffn):
        bsz, t, d = x2a.shape
        x3 = _ffn_down(hid.reshape(bsz * t, -1), x2a.reshape(bsz * t, d), w_down, p['g_ffn_post'], tm=ROW_TILE)
        results.append((x3.reshape(bsz, t, d), ns_a, nh, ns_b, jnp.concatenate([ns_fg, ns_fv], axis=-1)))
    return results


def kernel(x_prompt, x_sample, mem_prompt, state_conv_a, state_rglru, state_conv_b, state_ffn_conv, cache_mem_k, cache_mem_v, g_mix_pre, g_mix_post, w_in, conv_a_w, conv_a_b, w_r, b_r, w_i, b_i, lru_lambda, conv_b_w, p_a, p_b, w_o, g_x_pre, g_x_post, g_mem, w_q, w_k, w_v, w_xo, g_ffn_pre, g_ffn_post, w_up, ffn_conv_w, ffn_conv_b, w_down):
    depth = w_in.shape[0]
    bsz, _, d = x_prompt.shape
    n_mem = mem_prompt.shape[1]
    yp, ys = x_prompt, x_sample
    outs = [[] for _ in range(10)]
    row = lambda v: v.reshape(1, -1).astype(F32)
    for l in range(depth):
        p = {'g_mix_pre': row(g_mix_pre[l]), 'g_mix_post': row(g_mix_post[l]), 'w_in': w_in[l],
             'conv_a_w': conv_a_w[l], 'conv_a_b': row(conv_a_b[l]),
             'w_ri': jnp.concatenate([w_r[l], w_i[l]], axis=-1).astype(BF16),
             'b_r': row(b_r[l]), 'b_i': row(b_i[l]), 'lru_lambda': row(lru_lambda[l]),
             'conv_b_w': conv_b_w[l], 'p_a': p_a[l], 'p_b': p_b[l],
             'w_o': w_o[l], 'g_x_pre': row(g_x_pre[l]), 'g_x_post': row(g_x_post[l]),
             'w_q': w_q[l], 'w_xo': w_xo[l],
             'g_ffn_pre': row(g_ffn_pre[l]), 'g_ffn_post': row(g_ffn_post[l]), 'w_up': w_up[l],
             'ffn_conv_w': ffn_conv_w[l], 'ffn_conv_b': row(ffn_conv_b[l]), 'w_down': w_down[l]}
        d_rnn, d_conv, d_up = conv_a_w.shape[2], conv_b_w.shape[2], ffn_conv_w.shape[2]
        mem_u = _rmsnorm_call(mem_prompt.reshape(bsz * n_mem, d), row(g_mem[l]), tm=NORM_TM)
        mk = _proj([mem_u], w_k[l], F32, tm=MM_TM, tn=MM_TN)[0].reshape(bsz, n_mem, d)
        mv = _proj([mem_u], w_v[l], F32, tm=MM_TM, tn=MM_TN)[0].reshape(bsz, n_mem, d)
        zeros = lambda *s: jnp.zeros(s, F32)
        prompt = (yp, 0, mk, mv,
                  zeros(bsz, conv_a_w.shape[1] - 1, d_rnn), zeros(bsz, d_rnn),
                  zeros(bsz, conv_b_w.shape[1] - 1, d_conv), zeros(bsz, ffn_conv_w.shape[1] - 1, d_up))
        dec_b = x_sample.shape[0]
        ck = cache_mem_k[l].reshape(dec_b, n_mem, d)
        cv = cache_mem_v[l].reshape(dec_b, n_mem, d)
        sample = (ys, PAST_LEN, ck, cv, state_conv_a[l], state_rglru[l], state_conv_b[l], state_ffn_conv[l])
        (ys, *s_states), (yp, *p_states) = _layer([sample, prompt], p)
        for o, v in zip(outs, (*p_states, mk.reshape(bsz, n_mem, N_XHEADS, -1),
                               mv.reshape(bsz, n_mem, N_XHEADS, -1), *s_states)):
            o.append(v)
    return (yp, ys) + tuple(jnp.stack(o) for o in outs)
```

```python
import functools

import jax
import jax.numpy as jnp
import numpy as np
from jax import lax
from jax.experimental import pallas as pl
from jax.experimental.pallas import tpu as pltpu

F32 = jnp.float32
BF16 = jnp.bfloat16

EPS = 1e-6
RG_C = 8.0
PAST_LEN = 2048

LANES = 128
SUBLANES = 8
BF16_ROWS = 16
MXU_COLS = 256
VMEM_LIMIT_BYTES = 60 * 1024 * 1024

ROW_CHUNK = 128
CAST_ROWS = 256
COL_CHUNK = MXU_COLS
SCAN_LANES = 2048


def _params(*semantics):
    return pltpu.CompilerParams(dimension_semantics=semantics, vmem_limit_bytes=VMEM_LIMIT_BYTES)


def _resident(shape):
    nd = len(shape)
    return pl.BlockSpec(shape, lambda *_: (0,) * nd, pipeline_mode=pl.Buffered(1))


def _rmsnorm(x, g):
    y = x * lax.rsqrt(jnp.mean(x * x, axis=-1, keepdims=True) + EPS)
    return y * g


def _sigmoid(x):
    return 0.5 * jnp.tanh(0.5 * x) + 0.5


def _gelu_tanh(x):
    c1 = (2.0 / jnp.pi) ** 0.5
    return x * (0.5 * jnp.tanh(x * (c1 + (c1 * 0.044715) * (x * x))) + 0.5)


def _causal_conv(x, before, w, b):
    nb, t, c = x.shape
    width = w.shape[0]
    groups = t // SUBLANES
    full = jnp.concatenate([before, x], axis=1).reshape(nb * (groups + 1), SUBLANES, c)
    row = lax.broadcasted_iota(jnp.int32, (1, 1, SUBLANES, c), 2)
    y = None
    for k in range(width):
        lag = width - 1 - k
        if lag == 0:
            tap = x
        else:
            rot = pltpu.roll(full, lag, 1).reshape(nb, groups + 1, SUBLANES, c)
            tap = jnp.where(row >= lag, rot[:, 1:], rot[:, :-1]).reshape(nb, t, c)
        y = tap * w[k:k + 1, :] if y is None else y + tap * w[k:k + 1, :]
    return y if b is None else y + b


def _segment_major_perm(n_seq, seq):
    n_groups = seq // SUBLANES
    row = np.arange(seq)
    time = (row % SUBLANES) * n_groups + row // SUBLANES
    p = np.zeros((seq, seq), np.float32)
    p[row, time] = 1.0
    return jnp.asarray(np.kron(np.eye(n_seq, dtype=np.float32), p), BF16)


def _seg_coords(n_groups, back):
    t = SUBLANES * n_groups - back
    return t % n_groups, t // n_groups


def _seg_hist_init(hist_ref, state_ref, n_groups):
    n_state = state_ref.shape[1]
    n_hist = hist_ref.shape[1] // SUBLANES
    hist_ref[...] = jnp.zeros(hist_ref.shape, F32)
    for back in range(1, n_state + 1):
        g, s = _seg_coords(n_groups, back)
        row = (g - (n_groups - n_hist)) * SUBLANES + s
        hist_ref[:, row:row + 1, :] = state_ref[:, n_state - back:n_state - back + 1, :]


def _seg_state_out(ns_ref, x_ref):
    n_state = ns_ref.shape[1]
    n_groups = x_ref.shape[1] // SUBLANES
    for back in range(1, n_state + 1):
        g, s = _seg_coords(n_groups, back)
        row = g * SUBLANES + s
        ns_ref[:, n_state - back:n_state - back + 1, :] = x_ref[:, row:row + 1, :]


def _seg_conv(x_ref, hist_ref, w_ref, bias):
    nb, seq, c = x_ref.shape
    n_groups = seq // SUBLANES
    n_hist = hist_ref.shape[1] // SUBLANES
    width = w_ref.shape[0]
    assert n_hist >= min(n_groups, width - 1)
    sub = lax.broadcasted_iota(jnp.int32, (1, SUBLANES, c), 1)
    y = None
    for k in range(width):
        lag = width - 1 - k
        if lag == 0:
            tap = x_ref[...]
        else:
            parts = []
            for g in range(min(lag, n_groups)):
                src = (g - lag) % n_groups
                crossed = (lag - g + n_groups - 1) // n_groups
                cur = x_ref[:, src * SUBLANES:(src + 1) * SUBLANES, :]
                h0 = (src - (n_groups - n_hist)) * SUBLANES
                prev = hist_ref[:, h0:h0 + SUBLANES, :]
                parts.append(pltpu.roll(jnp.where(sub >= SUBLANES - crossed, prev, cur), crossed, 1))
            if n_groups > lag:
                parts.append(x_ref[:, 0:(n_groups - lag) * SUBLANES, :])
            tap = jnp.concatenate(parts, axis=1)
        y = tap * w_ref[k:k + 1, :] if y is None else y + tap * w_ref[k:k + 1, :]
    return y if bias is None else y + bias


def _softplus(x):
    return jnp.maximum(x, 0.0) + jnp.log1p(jnp.exp(-jnp.abs(x)))


def _permute_rows(ref, perm_ref):
    blk = perm_ref.shape[0]
    for r in range(0, ref.shape[0], blk):
        ref[r:r + blk, :] = jnp.dot(perm_ref[...], ref[r:r + blk, :], preferred_element_type=F32).astype(BF16)


def _rmsnorm_kernel(x_ref, g_ref, *rest):
    u_ref = rest[-1]
    tm = x_ref.shape[0]
    chunk = min(ROW_CHUNK, tm)
    g = g_ref[...]

    def body(c, _):
        rows = pl.ds(pl.multiple_of(c * chunk, chunk), chunk)
        u_ref[rows, :] = _rmsnorm(x_ref[rows, :], g).astype(BF16)
        return None

    lax.fori_loop(0, tm // chunk, body, None)
    if len(rest) == 2:
        _permute_rows(u_ref, rest[0])


def _rmsnorm_call(x, g, *, tm, perm=None):
    m, d = x.shape
    tm = min(tm, m)
    assert m % tm == 0 and tm % min(ROW_CHUNK, tm) == 0 and (perm is None or tm % perm.shape[0] == 0)
    extra = [] if perm is None else [perm]
    return pl.pallas_call(
        _rmsnorm_kernel,
        grid=(m // tm,),
        in_specs=[pl.BlockSpec((tm, d), lambda i: (i, 0)), pl.BlockSpec((1, d), lambda i: (0, 0))]
        + [pl.BlockSpec(a.shape, lambda i: (0, 0)) for a in extra],
        out_specs=pl.BlockSpec((tm, d), lambda i: (i, 0)),
        out_shape=jax.ShapeDtypeStruct((m, d), BF16),
        compiler_params=_params("parallel"),
        name="rmsnorm",
    )(x, g, *extra)


def _cast_rows(src_ref, dst_ref):
    rows_total = src_ref.shape[0]
    chunk = min(CAST_ROWS, rows_total)

    def body(c, _):
        rows = pl.ds(pl.multiple_of(c * chunk, chunk), chunk)
        dst_ref[rows, :] = src_ref[rows, :].astype(BF16)
        return None

    lax.fori_loop(0, rows_total // chunk, body, None)


def _side_cast_plan(arrays, n_steps, flat_step):
    in_specs, out_specs, out_shape = [], [], []
    for a in arrays:
        rows, cols = a.shape
        chunk = -(-rows // n_steps)
        chunk = -(-chunk // BF16_ROWS) * BF16_ROWS
        assert rows % chunk == 0
        last = rows // chunk - 1
        spec = pl.BlockSpec((chunk, cols), lambda *idx, last=last: (jnp.minimum(flat_step(*idx), last), 0))
        in_specs.append(spec)
        out_specs.append(spec)
        out_shape.append(jax.ShapeDtypeStruct((rows, cols), BF16))
    return in_specs, out_specs, out_shape


def _side_cast(src_refs, dst_refs):
    for src, dst in zip(src_refs, dst_refs):
        dst[...] = src[...].astype(BF16)


def _group_steps(counts):
    starts = [sum(counts[:g]) for g in range(len(counts))]
    return starts, sum(counts)


def _group_tile(i, start, count):
    return jnp.clip(i - start, 0, count - 1)


def _proj_kernel(*refs, starts, counts):
    n_groups = len(counts)
    u_refs, w_ref, o_refs, w_scr = refs[:n_groups], refs[n_groups], refs[n_groups + 1:-1], refs[-1]
    i = pl.program_id(1)

    @pl.when(i == 0)
    def _():
        _cast_rows(w_ref, w_scr)

    for g in range(n_groups):
        @pl.when((i >= starts[g]) & (i < starts[g] + counts[g]))
        def _(g=g):
            o_refs[g][...] = jnp.dot(u_refs[g][...], w_scr[...],
                                     preferred_element_type=F32).astype(o_refs[g].dtype)


def _proj(us, w, out_dtype, *, tm, tn):
    d, n = w.shape
    tn = min(tn, n)
    tms = [min(tm, u.shape[0]) for u in us]
    counts = [u.shape[0] // t for u, t in zip(us, tms)]
    assert all(u.shape[0] % t == 0 for u, t in zip(us, tms)) and n % tn == 0 and d % min(CAST_ROWS, d) == 0
    starts, steps = _group_steps(counts)
    tile = lambda g: (lambda j, i: (_group_tile(i, starts[g], counts[g]), 0))
    out_tile = lambda g: (lambda j, i: (_group_tile(i, starts[g], counts[g]), j))
    return pl.pallas_call(
        functools.partial(_proj_kernel, starts=starts, counts=counts),
        grid=(n // tn, steps),
        in_specs=([pl.BlockSpec((tms[g], d), tile(g)) for g in range(len(us))]
                  + [pl.BlockSpec((d, tn), lambda j, i: (0, j))]),
        out_specs=[pl.BlockSpec((tms[g], tn), out_tile(g)) for g in range(len(us))],
        out_shape=[jax.ShapeDtypeStruct((u.shape[0], n), out_dtype) for u in us],
        scratch_shapes=[pltpu.VMEM((d, tn), BF16)],
        compiler_params=_params("arbitrary", "arbitrary"),
        name="proj",
    )(*us, w)


def _mixer_pre_kernel(*refs, d_rnn, d_conv, pos0, n_side):
    (z_ref, sa_ref, h0_ref, sb_ref, caw_ref, cab_ref, wri_ref, br_ref, bi_ref, lam_ref, cbw_ref) = refs[:11]
    side_in = refs[11:11 + n_side]
    ya_ref, yb_ref, nsa_ref, nh_ref, nsb_ref = refs[11 + n_side:16 + n_side]
    side_out = refs[16 + n_side:16 + 2 * n_side]
    xa_scr, ha_scr, xc_scr, xcb_scr, ri_scr, cb_scr, hb_scr, h_scr = refs[16 + 2 * n_side:]
    _side_cast(side_in, side_out)
    t = pl.program_id(1)
    nb, seq, _ = ya_ref.shape
    n_groups = seq // SUBLANES
    n_heads, head_dim, _ = wri_ref.shape

    @pl.when(t == 0)
    def _():
        _seg_hist_init(ha_scr, sa_ref, n_groups)
        _seg_hist_init(hb_scr, sb_ref, n_groups)
        h_scr[...] = jnp.broadcast_to(h0_ref[...], h_scr.shape)

    xa_scr[...] = z_ref[:, :, 0:d_rnn].astype(F32)
    xc = _seg_conv(xa_scr, ha_scr, caw_ref, cab_ref[...])
    xc_scr[...] = xc
    xcb_scr[...] = xc.reshape(nb * seq, d_rnn).astype(BF16)
    _seg_state_out(nsa_ref, xa_scr)
    ha_scr[...] = xa_scr[:, seq - ha_scr.shape[1]:, :]

    for h in range(n_heads):
        cols = slice(h * head_dim, (h + 1) * head_dim)
        ri = jnp.dot(xcb_scr[:, cols], wri_ref[h], preferred_element_type=F32)
        ri_scr[:, :, cols] = ri[:, :head_dim].reshape(nb, seq, head_dim)
        ri_scr[:, :, d_rnn + h * head_dim:d_rnn + (h + 1) * head_dim] = ri[:, head_dim:].reshape(nb, seq, head_dim)

    strip = max(LANES, SCAN_LANES // nb)
    pairs = seq // BF16_ROWS
    sub = lax.broadcasted_iota(jnp.int32, (1, SUBLANES, strip), 1)
    for c in range(d_rnn // strip):
        cs = slice(c * strip, (c + 1) * strip)
        cs_i = slice(d_rnn + c * strip, d_rnn + (c + 1) * strip)
        cs_g = slice(d_rnn + c * strip, d_rnn + (c + 1) * strip)
        sp = _softplus(-lam_ref[:, cs])
        b_r = br_ref[:, cs]
        b_i = bi_ref[:, cs]

        def pass1(it, carry, first=False, cs=cs, cs_i=cs_i, sp=sp, b_r=b_r, b_i=b_i):
            h_loc, a_run = carry
            for half in range(BF16_ROWS // SUBLANES):
                rows = pl.ds(pl.multiple_of(it * BF16_ROWS + half * SUBLANES, SUBLANES), SUBLANES)
                r = jax.nn.sigmoid(ri_scr[:, rows, cs] + b_r)
                i = jax.nn.sigmoid(ri_scr[:, rows, cs_i] + b_i)
                log_a = -RG_C * r * sp
                a = jnp.exp(log_a)
                mult = jnp.sqrt(-jnp.tanh(log_a) * (a * a + 1.0))
                if first and half == 0:
                    mult = jnp.where(jnp.logical_and(t == 0, sub == 0), 1.0, mult)
                b = mult * i * xc_scr[:, rows, cs]
                h_loc = a * h_loc + b
                a_run = a * a_run
                ri_scr[:, rows, cs] = h_loc
                ri_scr[:, rows, cs_i] = a_run
            return h_loc, a_run

        carry = (jnp.zeros((nb, SUBLANES, strip), F32), jnp.ones((nb, SUBLANES, strip), F32))
        start = 0
        if pos0 == 0:
            carry = pass1(0, carry, first=True)
            start = 1
        h_tot, a_tot = lax.fori_loop(start, pairs, pass1, carry)

        for s in (1, 2, 4):
            keep = sub >= s
            a_prev = pltpu.roll(a_tot, s, 1)
            h_prev = pltpu.roll(h_tot, s, 1)
            h_tot = jnp.where(keep, a_tot * h_prev + h_tot, h_tot)
            a_tot = jnp.where(keep, a_tot * a_prev, a_tot)
        h_in = h_scr[:, :, cs]
        seg_end = a_tot * h_in + h_tot
        seg_start = jnp.where(sub == 0, h_in, pltpu.roll(seg_end, 1, 1))
        h_last = jnp.broadcast_to(seg_end[:, SUBLANES - 1:SUBLANES, :], seg_end.shape)
        h_scr[:, :, cs] = h_last
        nh_ref[:, :, cs] = h_last[:, 0:1, :]
        start2 = jnp.concatenate([seg_start] * (BF16_ROWS // SUBLANES), axis=1)

        def pass2(it, _, cs=cs, cs_i=cs_i, cs_g=cs_g, start2=start2):
            rows = pl.ds(pl.multiple_of(it * BF16_ROWS, BF16_ROWS), BF16_ROWS)
            h = ri_scr[:, rows, cs] + ri_scr[:, rows, cs_i] * start2
            gate = _gelu_tanh(z_ref[:, rows, cs_g].astype(F32))
            ya_ref[:, rows, cs] = (h * gate).astype(BF16)
            return None

        lax.fori_loop(0, pairs, pass2, None)

    o_gb, o_gc, o_hb = 2 * d_rnn, 2 * d_rnn + d_conv, 2 * d_rnn + 2 * d_conv
    cb_scr[...] = z_ref[:, :, o_gc:o_gc + d_conv].astype(F32) * z_ref[:, :, o_hb:o_hb + d_conv].astype(F32)
    co = _seg_conv(cb_scr, hb_scr, cbw_ref, None)
    yb_ref[...] = (z_ref[:, :, o_gb:o_gb + d_conv].astype(F32) * co).astype(BF16)
    _seg_state_out(nsb_ref, cb_scr)
    hb_scr[...] = cb_scr[:, seq - hb_scr.shape[1]:, :]


def _mixer_pre(z, state_a, h0, state_b, conv_a_w, conv_a_b, w_ri, b_r, b_i, lam, conv_b_w, *, seq_tile, pos0,
               to_round=()):
    bsz, t, _ = z.shape
    d_rnn = conv_a_w.shape[1]
    d_conv = conv_b_w.shape[1]
    n_pre = 2 * d_rnn + 3 * d_conv
    nb = bsz if t <= seq_tile else 1
    seq = min(seq_tile, t)
    assert t % seq == 0 and seq % BF16_ROWS == 0 and bsz % nb == 0
    wa, wb = conv_a_w.shape[0], conv_b_w.shape[0]
    hist_a, hist_b = min(seq // SUBLANES, wa - 1), min(seq // SUBLANES, wb - 1)
    kern = functools.partial(_mixer_pre_kernel, d_rnn=d_rnn, d_conv=d_conv, pos0=pos0, n_side=len(to_round))
    n_t = t // seq
    side_in, side_out, side_shape = _side_cast_plan(to_round, (bsz // nb) * n_t, lambda b, i: b * n_t + i)
    row = lambda b, i: (b, i, 0)
    per_b = lambda b, i: (b, 0, 0)
    const2 = lambda b, i: (0, 0)
    return pl.pallas_call(
        kern,
        grid=(bsz // nb, t // seq),
        in_specs=[pl.BlockSpec((nb, seq, n_pre), row),
                  pl.BlockSpec((nb, wa - 1, d_rnn), per_b),
                  pl.BlockSpec((nb, 1, d_rnn), per_b),
                  pl.BlockSpec((nb, wb - 1, d_conv), per_b),
                  pl.BlockSpec(conv_a_w.shape, const2),
                  pl.BlockSpec((1, d_rnn), const2),
                  pl.BlockSpec(w_ri.shape, lambda b, i: (0, 0, 0)),
                  pl.BlockSpec((1, d_rnn), const2),
                  pl.BlockSpec((1, d_rnn), const2),
                  pl.BlockSpec((1, d_rnn), const2),
                  pl.BlockSpec(conv_b_w.shape, const2)] + side_in,
        out_specs=[pl.BlockSpec((nb, seq, d_rnn), row),
                   pl.BlockSpec((nb, seq, d_conv), row),
                   pl.BlockSpec((nb, wa - 1, d_rnn), per_b),
                   pl.BlockSpec((nb, 1, d_rnn), per_b),
                   pl.BlockSpec((nb, wb - 1, d_conv), per_b)] + side_out,
        out_shape=[jax.ShapeDtypeStruct((bsz, t, d_rnn), BF16),
                   jax.ShapeDtypeStruct((bsz, t, d_conv), BF16),
                   jax.ShapeDtypeStruct((bsz, wa - 1, d_rnn), F32),
                   jax.ShapeDtypeStruct((bsz, 1, d_rnn), F32),
                   jax.ShapeDtypeStruct((bsz, wb - 1, d_conv), F32)] + side_shape,
        scratch_shapes=[pltpu.VMEM((nb, seq, d_rnn), F32),
                        pltpu.VMEM((nb, hist_a * SUBLANES, d_rnn), F32),
                        pltpu.VMEM((nb, seq, d_rnn), F32),
                        pltpu.VMEM((nb * seq, d_rnn), BF16),
                        pltpu.VMEM((nb, seq, 2 * d_rnn), F32),
                        pltpu.VMEM((nb, seq, d_conv), F32),
                        pltpu.VMEM((nb, hist_b * SUBLANES, d_conv), F32),
                        pltpu.VMEM((nb, SUBLANES, d_rnn), F32)],
        compiler_params=_params("arbitrary", "arbitrary"),
        name="mixer_pre",
    )(z, state_a, h0, state_b, conv_a_w, conv_a_b, w_ri, b_r, b_i, lam, conv_b_w, *to_round)


def _mixer_post_kernel(ya_ref, yb_ref, ga0_ref, ga1_ref, gb0_ref, gb1_ref, x_ref, pa_ref, pb_ref, wo_ref, g_ref,
                       unperm_ref, o_ref, mix_scr):
    half = ga0_ref.shape[1]
    y_a = jnp.dot(ya_ref[...], pa_ref[...], preferred_element_type=F32)
    y_b = jnp.dot(yb_ref[...], pb_ref[...], preferred_element_type=F32)
    for c, (ga_ref, gb_ref) in enumerate(((ga0_ref, gb0_ref), (ga1_ref, gb1_ref))):
        cols = slice(c * half, (c + 1) * half)
        mix = (jax.nn.sigmoid(ga_ref[...].astype(F32)) * y_a[:, cols]
               + jax.nn.sigmoid(gb_ref[...].astype(F32)) * y_b[:, cols])
        mix_scr[:, cols] = mix.astype(BF16)
    _permute_rows(mix_scr, unperm_ref)
    out = jnp.dot(mix_scr[...], wo_ref[...], preferred_element_type=F32)
    o_ref[...] = x_ref[...] + _rmsnorm(out, g_ref[...])


def _mixer_post(ya, yb, z, x, p_a, p_b, w_o, g_post, unperm, *, tm):
    m, d = x.shape
    d_rnn, d_conv = ya.shape[1], yb.shape[1]
    tm = min(tm, m)
    half = d // 2
    gate0 = (2 * d_rnn + 3 * d_conv) // half
    assert m % tm == 0 and (2 * d_rnn + 3 * d_conv) % half == 0 and tm % unperm.shape[0] == 0
    gate_spec = lambda k: pl.BlockSpec((tm, half), lambda i: (i, gate0 + k))
    row = lambda i: (i, 0)
    return pl.pallas_call(
        _mixer_post_kernel,
        grid=(m // tm,),
        in_specs=[pl.BlockSpec((tm, d_rnn), row), pl.BlockSpec((tm, d_conv), row),
                  gate_spec(0), gate_spec(1), gate_spec(2), gate_spec(3),
                  pl.BlockSpec((tm, d), row),
                  _resident(p_a.shape), _resident(p_b.shape), _resident(w_o.shape),
                  pl.BlockSpec((1, d), lambda i: (0, 0)), pl.BlockSpec(unperm.shape, lambda i: (0, 0))],
        out_specs=pl.BlockSpec((tm, d), row),
        out_shape=jax.ShapeDtypeStruct((m, d), F32),
        scratch_shapes=[pltpu.VMEM((tm, d), BF16)],
        compiler_params=_params("parallel"),
        name="mixer_post",
    )(ya, yb, z, z, z, z, x, p_a, p_b, w_o, g_post, unperm)


def _xattn_kernel(x_ref, k_ref, v_ref, wq_ref, wxo_ref, gpre_ref, gpost_ref, gnext_ref, o_ref, u_ref, q_scr, o_scr,
                  *, n_heads):
    nb, seq, d = x_ref.shape
    hd = d // n_heads
    scale = hd ** -0.5
    x = x_ref[...].reshape(nb * seq, d)
    u = _rmsnorm(x, gpre_ref[...]).astype(BF16)
    q_scr[...] = jnp.dot(u, wq_ref[...], preferred_element_type=F32).astype(BF16)
    for b in range(nb):
        rows = slice(b * seq, (b + 1) * seq)
        for h in range(n_heads):
            cols = slice(h * hd, (h + 1) * hd)
            s = lax.dot_general(q_scr[rows, cols], k_ref[b, :, cols], (((1,), (1,)), ((), ())),
                                preferred_element_type=F32) * scale
            p = jnp.exp(s - jnp.max(s, axis=-1, keepdims=True))
            att = (p / jnp.sum(p, axis=-1, keepdims=True)).astype(BF16)
            o_scr[rows, cols] = jnp.dot(att, v_ref[b, :, cols], preferred_element_type=F32).astype(BF16)
    out = jnp.dot(o_scr[...], wxo_ref[...], preferred_element_type=F32)
    x_new = x + _rmsnorm(out, gpost_ref[...])
    o_ref[...] = x_new.reshape(nb, seq, d)
    u_ref[...] = _rmsnorm(x_new, gnext_ref[...]).astype(BF16).reshape(nb, seq, d)


def _xattn(x, mem_k, mem_v, w_q, w_xo, g_pre, g_post, g_next, *, n_heads, seq_tile):
    bsz, t, d = x.shape
    n_mem = mem_k.shape[1]
    nb = bsz if t <= seq_tile else 1
    seq = min(seq_tile, t)
    assert t % seq == 0 and seq % BF16_ROWS == 0 and bsz % nb == 0
    single = bsz // nb == 1
    mem_spec = (_resident((nb, n_mem, d)) if single else pl.BlockSpec((nb, n_mem, d), lambda b, i: (b, 0, 0)))
    row = lambda b, i: (b, i, 0)
    gain = pl.BlockSpec((1, d), lambda b, i: (0, 0))
    return pl.pallas_call(
        functools.partial(_xattn_kernel, n_heads=n_heads),
        grid=(bsz // nb, t // seq),
        in_specs=[pl.BlockSpec((nb, seq, d), row), mem_spec, mem_spec,
                  _resident(w_q.shape), _resident(w_xo.shape), gain, gain, gain],
        out_specs=[pl.BlockSpec((nb, seq, d), row), pl.BlockSpec((nb, seq, d), row)],
        out_shape=[jax.ShapeDtypeStruct((bsz, t, d), F32), jax.ShapeDtypeStruct((bsz, t, d), BF16)],
        scratch_shapes=[pltpu.VMEM((nb * seq, d), BF16), pltpu.VMEM((nb * seq, d), BF16)],
        compiler_params=_params("parallel", "parallel"),
        name="xattn",
    )(x, mem_k, mem_v, w_q, w_xo, g_pre, g_post, g_next)


def _up_geglu_kernel(*refs, starts, counts, n_side):
    n_groups = len(counts)
    ins, rest = refs[:3 * n_groups], refs[3 * n_groups:]
    wg_ref, wv_ref, cwg_ref, cwv_ref, cbg_ref, cbv_ref = rest[:6]
    side_in, rest = rest[6:6 + n_side], rest[6 + n_side:]
    outs, rest = rest[:3 * n_groups], rest[3 * n_groups:]
    side_out, scr = rest[:n_side], rest[n_side:]
    w_scrs, hists = scr[:2], scr[2:]
    _side_cast(side_in, side_out)
    width = cwg_ref.shape[0]
    top = SUBLANES
    i = pl.program_id(1)

    @pl.when(i == 0)
    def _():
        _cast_rows(wg_ref, w_scrs[0])
        _cast_rows(wv_ref, w_scrs[1])

    for g in range(n_groups):
        u_ref, st_refs = ins[3 * g], ins[3 * g + 1:3 * g + 3]
        hid_ref, ns_refs = outs[3 * g], outs[3 * g + 1:3 * g + 3]
        p_scrs = hists[2 * g:2 * g + 2]
        nb, seq, k_dim = u_ref.shape
        tn = hid_ref.shape[2]
        chunk = min(COL_CHUNK, tn)

        @pl.when(i == starts[g])
        def _(st_refs=st_refs, p_scrs=p_scrs):
            for st_ref, p_scr in zip(st_refs, p_scrs):
                p_scr[...] = jnp.zeros(p_scr.shape, F32)
                p_scr[:, top - (width - 1):top, :] = st_ref[...]

        @pl.when((i >= starts[g]) & (i < starts[g] + counts[g]))
        def _(u_ref=u_ref, hid_ref=hid_ref, ns_refs=ns_refs, p_scrs=p_scrs, nb=nb, seq=seq, k_dim=k_dim,
              tn=tn, chunk=chunk):
            u = u_ref[...].reshape(nb * seq, k_dim)
            for c in range(tn // chunk):
                cols = slice(c * chunk, (c + 1) * chunk)
                ys = []
                for w_scr, p_scr, cw_ref, cb_ref, ns_ref in zip(w_scrs, p_scrs, (cwg_ref, cwv_ref),
                                                                (cbg_ref, cbv_ref), ns_refs):
                    a = jnp.dot(u, w_scr[:, cols], preferred_element_type=F32).reshape(nb, seq, chunk)
                    ys.append(_causal_conv(a, p_scr[:, :, cols], cw_ref[:, cols], cb_ref[:, cols]))
                    p_scr[:, :, cols] = a[:, seq - top:, :]
                    ns_ref[:, :, cols] = p_scr[:, top - (width - 1):top, cols]
                hid_ref[:, :, cols] = (_gelu_tanh(ys[0]) * ys[1]).astype(BF16)


def _up_geglu(us, w_up, states, conv_w, conv_b, *, tm, tn, to_round=()):
    d = w_up.shape[0]
    d_ff = w_up.shape[1] // 2
    width = conv_w.shape[0]
    n_j = d_ff // tn
    nbs = [u.shape[0] if u.shape[1] <= tm else 1 for u in us]
    seqs = [min(tm, u.shape[1]) for u in us]
    counts = [u.shape[1] // s for u, s in zip(us, seqs)]
    assert all(u.shape[0] == nb and u.shape[1] % s == 0 and s % BF16_ROWS == 0 for u, nb, s in zip(us, nbs, seqs))
    assert d_ff % tn == 0 and tn % min(COL_CHUNK, tn) == 0
    starts, steps = _group_steps(counts)
    col_g = lambda j, i: (0, j)
    col_v = lambda j, i: (0, n_j + j)
    st_g = lambda j, i: (0, 0, j)
    st_v = lambda j, i: (0, 0, n_j + j)
    in_specs, out_specs, out_shape, hist, args = [], [], [], [], []
    for g, (u, st, nb, seq) in enumerate(zip(us, states, nbs, seqs)):
        row = lambda j, i, g=g: (0, _group_tile(i, starts[g], counts[g]), 0)
        out = lambda j, i, g=g: (0, _group_tile(i, starts[g], counts[g]), j)
        in_specs += [pl.BlockSpec((nb, seq, d), row),
                     pl.BlockSpec((nb, width - 1, tn), st_g), pl.BlockSpec((nb, width - 1, tn), st_v)]
        out_specs += [pl.BlockSpec((nb, seq, tn), out),
                      pl.BlockSpec((nb, width - 1, tn), st_g), pl.BlockSpec((nb, width - 1, tn), st_g)]
        out_shape += [jax.ShapeDtypeStruct((u.shape[0], u.shape[1], d_ff), BF16),
                      jax.ShapeDtypeStruct((u.shape[0], width - 1, d_ff), F32),
                      jax.ShapeDtypeStruct((u.shape[0], width - 1, d_ff), F32)]
        hist += [pltpu.VMEM((nb, SUBLANES, tn), F32), pltpu.VMEM((nb, SUBLANES, tn), F32)]
        args += [u, st, st]
    side_in, side_out, side_shape = _side_cast_plan(to_round, n_j * steps, lambda j, i: j * steps + i)
    in_specs += [pl.BlockSpec((d, tn), col_g), pl.BlockSpec((d, tn), col_v),
                 pl.BlockSpec((width, tn), col_g), pl.BlockSpec((width, tn), col_v),
                 pl.BlockSpec((1, tn), col_g), pl.BlockSpec((1, tn), col_v)] + side_in
    res = pl.pallas_call(
        functools.partial(_up_geglu_kernel, starts=starts, counts=counts, n_side=len(to_round)),
        grid=(n_j, steps),
        in_specs=in_specs,
        out_specs=out_specs + side_out,
        out_shape=out_shape + side_shape,
        scratch_shapes=[pltpu.VMEM((d, tn), BF16), pltpu.VMEM((d, tn), BF16)] + hist,
        compiler_params=_params("arbitrary", "arbitrary"),
        name="up_geglu",
    )(*args, w_up, w_up, conv_w, conv_w, conv_b, conv_b, *to_round)
    return [tuple(res[3 * g:3 * g + 3]) for g in range(len(us))], list(res[3 * len(us):])


def _ffn_down_kernel(hid_ref, x_ref, wd_ref, g_ref, o_ref):
    y = jnp.dot(hid_ref[...], wd_ref[...], preferred_element_type=F32)
    o_ref[...] = x_ref[...] + _rmsnorm(y, g_ref[...])


def _ffn_down(hid, x, w_down, g_post, *, tm):
    m, d = x.shape
    d_ff = hid.shape[1]
    tm = min(tm, m)
    assert m % tm == 0
    row = lambda i: (i, 0)
    return pl.pallas_call(
        _ffn_down_kernel,
        grid=(m // tm,),
        in_specs=[pl.BlockSpec((tm, d_ff), row), pl.BlockSpec((tm, d), row), _resident(w_down.shape),
                  pl.BlockSpec((1, d), lambda i: (0, 0))],
        out_specs=pl.BlockSpec((tm, d), row),
        out_shape=jax.ShapeDtypeStruct((m, d), F32),
        compiler_params=_params("parallel"),
        name="ffn_down",
    )(hid, x, w_down, g_post)


MM_TM, MM_TN = 2048, 1024
UP_TM, UP_TN = 1024, 512
SEQ_TILE = 256
ROW_TILE = 512
NORM_TM = 512
N_XHEADS = 4


def _layer(groups, p):
    flat = [x.reshape(-1, x.shape[-1]) for x, *_ in groups]
    perms = []
    for x, *_ in groups:
        bsz, t, _ = x.shape
        perms.append(_segment_major_perm(*((1, SEQ_TILE) if t > SEQ_TILE else (bsz, t))))
    u0s = [_rmsnorm_call(x2, p['g_mix_pre'], tm=NORM_TM, perm=pm) for x2, pm in zip(flat, perms)]
    zs = _proj(u0s, p['w_in'], BF16, tm=MM_TM, tn=MM_TN)
    big = max(range(len(groups)), key=lambda g: flat[g].shape[0])
    resident = [p['p_a'], p['p_b'], p['w_o'], p['w_q'], p['w_xo']]
    mems = [m.reshape(-1, m.shape[-1]) for g in groups for m in g[2:4]]
    pre = []
    for g, ((x, pos0, _, _, s_a, h0, s_b, _), z) in enumerate(zip(groups, zs)):
        bsz, t, _ = x.shape
        pre.append(_mixer_pre(z.reshape(bsz, t, -1), s_a, h0[:, None, :], s_b,
                              p['conv_a_w'], p['conv_a_b'], p['w_ri'], p['b_r'], p['b_i'],
                              p['lru_lambda'], p['conv_b_w'], seq_tile=SEQ_TILE, pos0=pos0,
                              to_round=resident + mems if g == big else ()))
    p_a, p_b, w_o, w_q, w_xo, *mems = pre[big][5:]
    mids = []
    for g, ((x, *_), x2, z) in enumerate(zip(groups, flat, zs)):
        bsz, t, d = x.shape
        m = bsz * t
        ya, yb, ns_a, nh, ns_b = pre[g][:5]
        mem_k, mem_v = (mm.reshape(bsz, -1, d) for mm in mems[2 * g:2 * g + 2])
        x1 = _mixer_post(ya.reshape(m, -1), yb.reshape(m, -1), z, x2, p_a, p_b, w_o, p['g_mix_post'],
                         perms[g].T, tm=ROW_TILE)
        x2a, u3 = _xattn(x1.reshape(bsz, t, d), mem_k, mem_v, w_q, w_xo, p['g_x_pre'], p['g_x_post'],
                         p['g_ffn_pre'], n_heads=N_XHEADS, seq_tile=ROW_TILE)
        mids.append((x2a, u3, ns_a, nh[:, 0, :], ns_b))
    ffn, (w_down,) = _up_geglu([u3 for _, u3, *_ in mids], p['w_up'], [g[7] for g in groups],
                               p['ffn_conv_w'], p['ffn_conv_b'], tm=UP_TM, tn=UP_TN, to_round=[p['w_down']])
    results = []
    for (x2a, _, ns_a, nh, ns_b), (hid, ns_fg, ns_fv) in zip(mids, ffn):
        bsz, t, d = x2a.shape
        x3 = _ffn_down(hid.reshape(bsz * t, -1), x2a.reshape(bsz * t, d), w_down, p['g_ffn_post'], tm=ROW_TILE)
        results.append((x3.reshape(bsz, t, d), ns_a, nh, ns_b, jnp.concatenate([ns_fg, ns_fv], axis=-1)))
    return results


def kernel(x_prompt, x_sample, mem_prompt, state_conv_a, state_rglru, state_conv_b, state_ffn_conv, cache_mem_k, cache_mem_v, g_mix_pre, g_mix_post, w_in, conv_a_w, conv_a_b, w_r, b_r, w_i, b_i, lru_lambda, conv_b_w, p_a, p_b, w_o, g_x_pre, g_x_post, g_mem, w_q, w_k, w_v, w_xo, g_ffn_pre, g_ffn_post, w_up, ffn_conv_w, ffn_conv_b, w_down):
    depth = w_in.shape[0]
    bsz, _, d = x_prompt.shape
    n_mem = mem_prompt.shape[1]
    yp, ys = x_prompt, x_sample
    outs = [[] for _ in range(10)]
    row = lambda v: v.reshape(1, -1).astype(F32)
    for l in range(depth):
        p = {'g_mix_pre': row(g_mix_pre[l]), 'g_mix_post': row(g_mix_post[l]), 'w_in': w_in[l],
             'conv_a_w': conv_a_w[l], 'conv_a_b': row(conv_a_b[l]),
             'w_ri': jnp.concatenate([w_r[l], w_i[l]], axis=-1).astype(BF16),
             'b_r': row(b_r[l]), 'b_i': row(b_i[l]), 'lru_lambda': row(lru_lambda[l]),
             'conv_b_w': conv_b_w[l], 'p_a': p_a[l], 'p_b': p_b[l],
             'w_o': w_o[l], 'g_x_pre': row(g_x_pre[l]), 'g_x_post': row(g_x_post[l]),
             'w_q': w_q[l], 'w_xo': w_xo[l],
             'g_ffn_pre': row(g_ffn_pre[l]), 'g_ffn_post': row(g_ffn_post[l]), 'w_up': w_up[l],
             'ffn_conv_w': ffn_conv_w[l], 'ffn_conv_b': row(ffn_conv_b[l]), 'w_down': w_down[l]}
        d_rnn, d_conv, d_up = conv_a_w.shape[2], conv_b_w.shape[2], ffn_conv_w.shape[2]
        mem_u = _rmsnorm_call(mem_prompt.reshape(bsz * n_mem, d), row(g_mem[l]), tm=NORM_TM)
        mk = _proj([mem_u], w_k[l], F32, tm=MM_TM, tn=MM_TN)[0].reshape(bsz, n_mem, d)
        mv = _proj([mem_u], w_v[l], F32, tm=MM_TM, tn=MM_TN)[0].reshape(bsz, n_mem, d)
        zeros = lambda *s: jnp.zeros(s, F32)
        prompt = (yp, 0, mk, mv,
                  zeros(bsz, conv_a_w.shape[1] - 1, d_rnn), zeros(bsz, d_rnn),
                  zeros(bsz, conv_b_w.shape[1] - 1, d_conv), zeros(bsz, ffn_conv_w.shape[1] - 1, d_up))
        dec_b = x_sample.shape[0]
        ck = cache_mem_k[l].reshape(dec_b, n_mem, d)
        cv = cache_mem_v[l].reshape(dec_b, n_mem, d)
        sample = (ys, PAST_LEN, ck, cv, state_conv_a[l], state_rglru[l], state_conv_b[l], state_ffn_conv[l])
        (ys, *s_states), (yp, *p_states) = _layer([sample, prompt], p)
        for o, v in zip(outs, (*p_states, mk.reshape(bsz, n_mem, N_XHEADS, -1),
                               mv.reshape(bsz, n_mem, N_XHEADS, -1), *s_states)):
            o.append(v)
    return (yp, ys) + tuple(jnp.stack(o) for o in outs)
```

```python
import functools

import jax
import jax.numpy as jnp
import numpy as np
from jax import lax
from jax.experimental import pallas as pl
from jax.experimental.pallas import tpu as pltpu

F32 = jnp.float32
BF16 = jnp.bfloat16

EPS = 1e-6
RG_C = 8.0
PAST_LEN = 2048

LANES = 128
SUBLANES = 8
BF16_ROWS = 16
MXU_COLS = 256
VMEM_LIMIT_BYTES = 60 * 1024 * 1024

ROW_CHUNK = 128
CAST_ROWS = 256
COL_CHUNK = MXU_COLS
SCAN_LANES = 2048


def _params(*semantics):
    return pltpu.CompilerParams(dimension_semantics=semantics, vmem_limit_bytes=VMEM_LIMIT_BYTES)


def _resident(shape):
    nd = len(shape)
    return pl.BlockSpec(shape, lambda *_: (0,) * nd, pipeline_mode=pl.Buffered(1))


def _rmsnorm(x, g):
    y = x * lax.rsqrt(jnp.mean(x * x, axis=-1, keepdims=True) + EPS)
    return y * g


def _sigmoid(x):
    return 0.5 * jnp.tanh(0.5 * x) + 0.5


def _gelu_tanh(x):
    c1 = (2.0 / jnp.pi) ** 0.5
    return x * (0.5 * jnp.tanh(x * (c1 + (c1 * 0.044715) * (x * x))) + 0.5)


def _causal_conv(x, before, w, b):
    nb, t, c = x.shape
    width = w.shape[0]
    groups = t // SUBLANES
    full = jnp.concatenate([before, x], axis=1).reshape(nb * (groups + 1), SUBLANES, c)
    row = lax.broadcasted_iota(jnp.int32, (1, 1, SUBLANES, c), 2)
    y = None
    for k in range(width):
        lag = width - 1 - k
        if lag == 0:
            tap = x
        else:
            rot = pltpu.roll(full, lag, 1).reshape(nb, groups + 1, SUBLANES, c)
            tap = jnp.where(row >= lag, rot[:, 1:], rot[:, :-1]).reshape(nb, t, c)
        y = tap * w[k:k + 1, :] if y is None else y + tap * w[k:k + 1, :]
    return y if b is None else y + b


def _segment_major_perm(n_seq, seq):
    n_groups = seq // SUBLANES
    row = np.arange(seq)
    time = (row % SUBLANES) * n_groups + row // SUBLANES
    p = np.zeros((seq, seq), np.float32)
    p[row, time] = 1.0
    return jnp.asarray(np.kron(np.eye(n_seq, dtype=np.float32), p), BF16)


def _seg_coords(n_groups, back):
    t = SUBLANES * n_groups - back
    return t % n_groups, t // n_groups


def _seg_hist_init(hist_ref, state_ref, n_groups):
    n_state = state_ref.shape[1]
    n_hist = hist_ref.shape[1] // SUBLANES
    hist_ref[...] = jnp.zeros(hist_ref.shape, F32)
    for back in range(1, n_state + 1):
        g, s = _seg_coords(n_groups, back)
        row = (g - (n_groups - n_hist)) * SUBLANES + s
        hist_ref[:, row:row + 1, :] = state_ref[:, n_state - back:n_state - back + 1, :]


def _seg_state_out(ns_ref, x_ref, cols=slice(None)):
    n_state = ns_ref.shape[1]
    n_groups = x_ref.shape[1] // SUBLANES
    for back in range(1, n_state + 1):
        g, s = _seg_coords(n_groups, back)
        row = g * SUBLANES + s
        ns_ref[:, n_state - back:n_state - back + 1, cols] = x_ref[:, row:row + 1, :]


def _seg_conv(x_ref, hist_ref, w_ref, bias):
    nb, seq, c = x_ref.shape
    n_groups = seq // SUBLANES
    n_hist = hist_ref.shape[1] // SUBLANES
    width = w_ref.shape[0]
    assert n_hist >= min(n_groups, width - 1)
    sub = lax.broadcasted_iota(jnp.int32, (1, SUBLANES, c), 1)
    y = None
    for k in range(width):
        lag = width - 1 - k
        if lag == 0:
            tap = x_ref[...]
        else:
            parts = []
            for g in range(min(lag, n_groups)):
                src = (g - lag) % n_groups
                crossed = (lag - g + n_groups - 1) // n_groups
                cur = x_ref[:, src * SUBLANES:(src + 1) * SUBLANES, :]
                h0 = (src - (n_groups - n_hist)) * SUBLANES
                prev = hist_ref[:, h0:h0 + SUBLANES, :]
                parts.append(pltpu.roll(jnp.where(sub >= SUBLANES - crossed, prev, cur), crossed, 1))
            if n_groups > lag:
                parts.append(x_ref[:, 0:(n_groups - lag) * SUBLANES, :])
            tap = jnp.concatenate(parts, axis=1)
        y = tap * w_ref[k:k + 1, :] if y is None else y + tap * w_ref[k:k + 1, :]
    return y if bias is None else y + bias


def _softplus(x):
    return jnp.maximum(x, 0.0) + jnp.log1p(jnp.exp(-jnp.abs(x)))


def _permute_rows(ref, perm_ref):
    blk = perm_ref.shape[0]
    for r in range(0, ref.shape[0], blk):
        ref[r:r + blk, :] = jnp.dot(perm_ref[...], ref[r:r + blk, :], preferred_element_type=F32).astype(BF16)


def _rmsnorm_kernel(x_ref, g_ref, *rest):
    u_ref = rest[-1]
    tm = x_ref.shape[0]
    chunk = min(ROW_CHUNK, tm)
    g = g_ref[...]

    def body(c, _):
        rows = pl.ds(pl.multiple_of(c * chunk, chunk), chunk)
        u_ref[rows, :] = _rmsnorm(x_ref[rows, :], g).astype(BF16)
        return None

    lax.fori_loop(0, tm // chunk, body, None)
    if len(rest) == 2:
        _permute_rows(u_ref, rest[0])


def _rmsnorm_call(x, g, *, tm, perm=None):
    m, d = x.shape
    tm = min(tm, m)
    assert m % tm == 0 and tm % min(ROW_CHUNK, tm) == 0 and (perm is None or tm % perm.shape[0] == 0)
    extra = [] if perm is None else [perm]
    return pl.pallas_call(
        _rmsnorm_kernel,
        grid=(m // tm,),
        in_specs=[pl.BlockSpec((tm, d), lambda i: (i, 0)), pl.BlockSpec((1, d), lambda i: (0, 0))]
        + [pl.BlockSpec(a.shape, lambda i: (0, 0)) for a in extra],
        out_specs=pl.BlockSpec((tm, d), lambda i: (i, 0)),
        out_shape=jax.ShapeDtypeStruct((m, d), BF16),
        compiler_params=_params("parallel"),
        name="rmsnorm",
    )(x, g, *extra)


def _cast_rows(src_ref, dst_ref):
    rows_total = src_ref.shape[0]
    chunk = min(CAST_ROWS, rows_total)

    def body(c, _):
        rows = pl.ds(pl.multiple_of(c * chunk, chunk), chunk)
        dst_ref[rows, :] = src_ref[rows, :].astype(BF16)
        return None

    lax.fori_loop(0, rows_total // chunk, body, None)


def _side_cast_plan(arrays, n_steps, flat_step):
    in_specs, out_specs, out_shape = [], [], []
    for a in arrays:
        rows, cols = a.shape
        chunk = -(-rows // n_steps)
        chunk = -(-chunk // BF16_ROWS) * BF16_ROWS
        assert rows % chunk == 0
        last = rows // chunk - 1
        spec = pl.BlockSpec((chunk, cols), lambda *idx, last=last: (jnp.minimum(flat_step(*idx), last), 0))
        in_specs.append(spec)
        out_specs.append(spec)
        out_shape.append(jax.ShapeDtypeStruct((rows, cols), BF16))
    return in_specs, out_specs, out_shape


def _side_cast(src_refs, dst_refs):
    for src, dst in zip(src_refs, dst_refs):
        dst[...] = src[...].astype(BF16)


def _group_steps(counts):
    starts = [sum(counts[:g]) for g in range(len(counts))]
    return starts, sum(counts)


def _group_tile(i, start, count):
    return jnp.clip(i - start, 0, count - 1)


def _proj_kernel(*refs, starts, counts):
    n_groups = len(counts)
    u_refs, w_ref, o_refs, w_scr = refs[:n_groups], refs[n_groups], refs[n_groups + 1:-1], refs[-1]
    i = pl.program_id(1)

    @pl.when(i == 0)
    def _():
        _cast_rows(w_ref, w_scr)

    for g in range(n_groups):
        @pl.when((i >= starts[g]) & (i < starts[g] + counts[g]))
        def _(g=g):
            o_refs[g][...] = jnp.dot(u_refs[g][...], w_scr[...],
                                     preferred_element_type=F32).astype(o_refs[g].dtype)


def _proj(us, w, out_dtype, *, tm, tn):
    d, n = w.shape
    tn = min(tn, n)
    tms = [min(tm, u.shape[0]) for u in us]
    counts = [u.shape[0] // t for u, t in zip(us, tms)]
    assert all(u.shape[0] % t == 0 for u, t in zip(us, tms)) and n % tn == 0 and d % min(CAST_ROWS, d) == 0
    starts, steps = _group_steps(counts)
    tile = lambda g: (lambda j, i: (_group_tile(i, starts[g], counts[g]), 0))
    out_tile = lambda g: (lambda j, i: (_group_tile(i, starts[g], counts[g]), j))
    return pl.pallas_call(
        functools.partial(_proj_kernel, starts=starts, counts=counts),
        grid=(n // tn, steps),
        in_specs=([pl.BlockSpec((tms[g], d), tile(g)) for g in range(len(us))]
                  + [pl.BlockSpec((d, tn), lambda j, i: (0, j))]),
        out_specs=[pl.BlockSpec((tms[g], tn), out_tile(g)) for g in range(len(us))],
        out_shape=[jax.ShapeDtypeStruct((u.shape[0], n), out_dtype) for u in us],
        scratch_shapes=[pltpu.VMEM((d, tn), BF16)],
        compiler_params=_params("arbitrary", "arbitrary"),
        name="proj",
    )(*us, w)


def _mixer_pre_kernel(*refs, d_rnn, d_conv, pos0, n_side):
    (z_ref, sa_ref, h0_ref, sb_ref, caw_ref, cab_ref, wri_ref, br_ref, bi_ref, lam_ref, cbw_ref) = refs[:11]
    side_in = refs[11:11 + n_side]
    ya_ref, yb_ref, nsa_ref, nh_ref, nsb_ref = refs[11 + n_side:16 + n_side]
    side_out = refs[16 + n_side:16 + 2 * n_side]
    xa_scr, ha_scr, xc_scr, xcb_scr, ri_scr, cb_scr, hb_scr, h_scr = refs[16 + 2 * n_side:]
    _side_cast(side_in, side_out)
    t = pl.program_id(1)
    nb, seq, _ = ya_ref.shape
    n_groups = seq // SUBLANES
    n_heads, head_dim, _ = wri_ref.shape

    @pl.when(t == 0)
    def _():
        _seg_hist_init(ha_scr, sa_ref, n_groups)
        _seg_hist_init(hb_scr, sb_ref, n_groups)
        h_scr[...] = jnp.broadcast_to(h0_ref[...], h_scr.shape)

    xa_scr[...] = z_ref[:, :, 0:d_rnn].astype(F32)
    xc = _seg_conv(xa_scr, ha_scr, caw_ref, cab_ref[...])
    xc_scr[...] = xc
    xcb_scr[...] = xc.reshape(nb * seq, d_rnn).astype(BF16)
    _seg_state_out(nsa_ref, xa_scr)
    ha_scr[...] = xa_scr[:, seq - ha_scr.shape[1]:, :]

    for h in range(n_heads):
        cols = slice(h * head_dim, (h + 1) * head_dim)
        ri = jnp.dot(xcb_scr[:, cols], wri_ref[h], preferred_element_type=F32)
        ri_scr[:, :, cols] = ri[:, :head_dim].reshape(nb, seq, head_dim)
        ri_scr[:, :, d_rnn + h * head_dim:d_rnn + (h + 1) * head_dim] = ri[:, head_dim:].reshape(nb, seq, head_dim)

    strip = max(LANES, SCAN_LANES // nb)
    pairs = seq // BF16_ROWS
    sub = lax.broadcasted_iota(jnp.int32, (1, SUBLANES, strip), 1)
    for c in range(d_rnn // strip):
        cs = slice(c * strip, (c + 1) * strip)
        cs_i = slice(d_rnn + c * strip, d_rnn + (c + 1) * strip)
        cs_g = slice(d_rnn + c * strip, d_rnn + (c + 1) * strip)
        sp = _softplus(-lam_ref[:, cs])
        b_r = br_ref[:, cs]
        b_i = bi_ref[:, cs]

        def pass1(it, carry, first=False, cs=cs, cs_i=cs_i, sp=sp, b_r=b_r, b_i=b_i):
            h_loc, a_run = carry
            for half in range(BF16_ROWS // SUBLANES):
                rows = pl.ds(pl.multiple_of(it * BF16_ROWS + half * SUBLANES, SUBLANES), SUBLANES)
                r = _sigmoid(ri_scr[:, rows, cs] + b_r)
                i = _sigmoid(ri_scr[:, rows, cs_i] + b_i)
                log_a = -RG_C * r * sp
                a = jnp.exp(log_a)
                mult = jnp.sqrt(-jnp.tanh(log_a) * (a * a + 1.0))
                if first and half == 0:
                    mult = jnp.where(jnp.logical_and(t == 0, sub == 0), 1.0, mult)
                b = mult * i * xc_scr[:, rows, cs]
                h_loc = a * h_loc + b
                a_run = a * a_run
                ri_scr[:, rows, cs] = h_loc
                ri_scr[:, rows, cs_i] = a_run
            return h_loc, a_run

        carry = (jnp.zeros((nb, SUBLANES, strip), F32), jnp.ones((nb, SUBLANES, strip), F32))
        start = 0
        if pos0 == 0:
            carry = pass1(0, carry, first=True)
            start = 1
        h_tot, a_tot = lax.fori_loop(start, pairs, pass1, carry)

        for s in (1, 2, 4):
            keep = sub >= s
            a_prev = pltpu.roll(a_tot, s, 1)
            h_prev = pltpu.roll(h_tot, s, 1)
            h_tot = jnp.where(keep, a_tot * h_prev + h_tot, h_tot)
            a_tot = jnp.where(keep, a_tot * a_prev, a_tot)
        h_in = h_scr[:, :, cs]
        seg_end = a_tot * h_in + h_tot
        seg_start = jnp.where(sub == 0, h_in, pltpu.roll(seg_end, 1, 1))
        h_last = jnp.broadcast_to(seg_end[:, SUBLANES - 1:SUBLANES, :], seg_end.shape)
        h_scr[:, :, cs] = h_last
        nh_ref[:, :, cs] = h_last[:, 0:1, :]
        start2 = jnp.concatenate([seg_start] * (BF16_ROWS // SUBLANES), axis=1)

        def pass2(it, _, cs=cs, cs_i=cs_i, cs_g=cs_g, start2=start2):
            rows = pl.ds(pl.multiple_of(it * BF16_ROWS, BF16_ROWS), BF16_ROWS)
            h = ri_scr[:, rows, cs] + ri_scr[:, rows, cs_i] * start2
            gate = _gelu_tanh(z_ref[:, rows, cs_g].astype(F32))
            ya_ref[:, rows, cs] = (h * gate).astype(BF16)
            return None

        lax.fori_loop(0, pairs, pass2, None)

    o_gb, o_gc, o_hb = 2 * d_rnn, 2 * d_rnn + d_conv, 2 * d_rnn + 2 * d_conv
    cb_scr[...] = z_ref[:, :, o_gc:o_gc + d_conv].astype(F32) * z_ref[:, :, o_hb:o_hb + d_conv].astype(F32)
    co = _seg_conv(cb_scr, hb_scr, cbw_ref, None)
    yb_ref[...] = (z_ref[:, :, o_gb:o_gb + d_conv].astype(F32) * co).astype(BF16)
    _seg_state_out(nsb_ref, cb_scr)
    hb_scr[...] = cb_scr[:, seq - hb_scr.shape[1]:, :]


def _mixer_pre(z, state_a, h0, state_b, conv_a_w, conv_a_b, w_ri, b_r, b_i, lam, conv_b_w, *, seq_tile, pos0,
               to_round=()):
    bsz, t, _ = z.shape
    d_rnn = conv_a_w.shape[1]
    d_conv = conv_b_w.shape[1]
    n_pre = 2 * d_rnn + 3 * d_conv
    nb = bsz if t <= seq_tile else 1
    seq = min(seq_tile, t)
    assert t % seq == 0 and seq % BF16_ROWS == 0 and bsz % nb == 0
    wa, wb = conv_a_w.shape[0], conv_b_w.shape[0]
    hist_a, hist_b = min(seq // SUBLANES, wa - 1), min(seq // SUBLANES, wb - 1)
    kern = functools.partial(_mixer_pre_kernel, d_rnn=d_rnn, d_conv=d_conv, pos0=pos0, n_side=len(to_round))
    n_t = t // seq
    side_in, side_out, side_shape = _side_cast_plan(to_round, (bsz // nb) * n_t, lambda b, i: b * n_t + i)
    row = lambda b, i: (b, i, 0)
    per_b = lambda b, i: (b, 0, 0)
    const2 = lambda b, i: (0, 0)
    return pl.pallas_call(
        kern,
        grid=(bsz // nb, t // seq),
        in_specs=[pl.BlockSpec((nb, seq, n_pre), row),
                  pl.BlockSpec((nb, wa - 1, d_rnn), per_b),
                  pl.BlockSpec((nb, 1, d_rnn), per_b),
                  pl.BlockSpec((nb, wb - 1, d_conv), per_b),
                  pl.BlockSpec(conv_a_w.shape, const2),
                  pl.BlockSpec((1, d_rnn), const2),
                  pl.BlockSpec(w_ri.shape, lambda b, i: (0, 0, 0)),
                  pl.BlockSpec((1, d_rnn), const2),
                  pl.BlockSpec((1, d_rnn), const2),
                  pl.BlockSpec((1, d_rnn), const2),
                  pl.BlockSpec(conv_b_w.shape, const2)] + side_in,
        out_specs=[pl.BlockSpec((nb, seq, d_rnn), row),
                   pl.BlockSpec((nb, seq, d_conv), row),
                   pl.BlockSpec((nb, wa - 1, d_rnn), per_b),
                   pl.BlockSpec((nb, 1, d_rnn), per_b),
                   pl.BlockSpec((nb, wb - 1, d_conv), per_b)] + side_out,
        out_shape=[jax.ShapeDtypeStruct((bsz, t, d_rnn), BF16),
                   jax.ShapeDtypeStruct((bsz, t, d_conv), BF16),
                   jax.ShapeDtypeStruct((bsz, wa - 1, d_rnn), F32),
                   jax.ShapeDtypeStruct((bsz, 1, d_rnn), F32),
                   jax.ShapeDtypeStruct((bsz, wb - 1, d_conv), F32)] + side_shape,
        scratch_shapes=[pltpu.VMEM((nb, seq, d_rnn), F32),
                        pltpu.VMEM((nb, hist_a * SUBLANES, d_rnn), F32),
                        pltpu.VMEM((nb, seq, d_rnn), F32),
                        pltpu.VMEM((nb * seq, d_rnn), BF16),
                        pltpu.VMEM((nb, seq, 2 * d_rnn), F32),
                        pltpu.VMEM((nb, seq, d_conv), F32),
                        pltpu.VMEM((nb, hist_b * SUBLANES, d_conv), F32),
                        pltpu.VMEM((nb, SUBLANES, d_rnn), F32)],
        compiler_params=_params("arbitrary", "arbitrary"),
        name="mixer_pre",
    )(z, state_a, h0, state_b, conv_a_w, conv_a_b, w_ri, b_r, b_i, lam, conv_b_w, *to_round)


def _mixer_post_kernel(ya_ref, yb_ref, ga0_ref, ga1_ref, gb0_ref, gb1_ref, x_ref, pa_ref, pb_ref, wo_ref, g_ref,
                       unperm_ref, o_ref, mix_scr):
    half = ga0_ref.shape[1]
    y_a = jnp.dot(ya_ref[...], pa_ref[...], preferred_element_type=F32)
    y_b = jnp.dot(yb_ref[...], pb_ref[...], preferred_element_type=F32)
    for c, (ga_ref, gb_ref) in enumerate(((ga0_ref, gb0_ref), (ga1_ref, gb1_ref))):
        cols = slice(c * half, (c + 1) * half)
        mix = (jax.nn.sigmoid(ga_ref[...].astype(F32)) * y_a[:, cols]
               + jax.nn.sigmoid(gb_ref[...].astype(F32)) * y_b[:, cols])
        mix_scr[:, cols] = mix.astype(BF16)
    _permute_rows(mix_scr, unperm_ref)
    out = jnp.dot(mix_scr[...], wo_ref[...], preferred_element_type=F32)
    o_ref[...] = x_ref[...] + _rmsnorm(out, g_ref[...])


def _mixer_post(ya, yb, z, x, p_a, p_b, w_o, g_post, unperm, *, tm):
    m, d = x.shape
    d_rnn, d_conv = ya.shape[1], yb.shape[1]
    tm = min(tm, m)
    half = d // 2
    gate0 = (2 * d_rnn + 3 * d_conv) // half
    assert m % tm == 0 and (2 * d_rnn + 3 * d_conv) % half == 0 and tm % unperm.shape[0] == 0
    gate_spec = lambda k: pl.BlockSpec((tm, half), lambda i: (i, gate0 + k))
    row = lambda i: (i, 0)
    return pl.pallas_call(
        _mixer_post_kernel,
        grid=(m // tm,),
        in_specs=[pl.BlockSpec((tm, d_rnn), row), pl.BlockSpec((tm, d_conv), row),
                  gate_spec(0), gate_spec(1), gate_spec(2), gate_spec(3),
                  pl.BlockSpec((tm, d), row),
                  _resident(p_a.shape), _resident(p_b.shape), _resident(w_o.shape),
                  pl.BlockSpec((1, d), lambda i: (0, 0)), pl.BlockSpec(unperm.shape, lambda i: (0, 0))],
        out_specs=pl.BlockSpec((tm, d), row),
        out_shape=jax.ShapeDtypeStruct((m, d), F32),
        scratch_shapes=[pltpu.VMEM((tm, d), BF16)],
        compiler_params=_params("parallel"),
        name="mixer_post",
    )(ya, yb, z, z, z, z, x, p_a, p_b, w_o, g_post, unperm)


def _xattn_kernel(x_ref, k_ref, v_ref, wq_ref, wxo_ref, gpre_ref, gpost_ref, gnext_ref, perm_ref, o_ref, u_ref,
                  q_scr, o_scr, *, n_heads):
    nb, seq, d = x_ref.shape
    hd = d // n_heads
    scale = hd ** -0.5
    x = x_ref[...].reshape(nb * seq, d)
    u = _rmsnorm(x, gpre_ref[...]).astype(BF16)
    q_scr[...] = jnp.dot(u, wq_ref[...], preferred_element_type=F32).astype(BF16)
    for b in range(nb):
        rows = slice(b * seq, (b + 1) * seq)
        for h in range(n_heads):
            cols = slice(h * hd, (h + 1) * hd)
            s = lax.dot_general(q_scr[rows, cols], k_ref[b, :, cols], (((1,), (1,)), ((), ())),
                                preferred_element_type=F32) * scale
            p = jnp.exp(s - jnp.max(s, axis=-1, keepdims=True))
            att = (p / jnp.sum(p, axis=-1, keepdims=True)).astype(BF16)
            o_scr[rows, cols] = jnp.dot(att, v_ref[b, :, cols], preferred_element_type=F32).astype(BF16)
    out = jnp.dot(o_scr[...], wxo_ref[...], preferred_element_type=F32)
    x_new = x + _rmsnorm(out, gpost_ref[...])
    o_ref[...] = x_new.reshape(nb, seq, d)
    u_next = _rmsnorm(x_new, gnext_ref[...]).astype(BF16)
    blk = perm_ref.shape[0]
    u_next = jnp.concatenate([jnp.dot(perm_ref[...], u_next[r:r + blk, :], preferred_element_type=F32)
                              for r in range(0, nb * seq, blk)], axis=0)
    u_ref[...] = u_next.astype(BF16).reshape(nb, seq, d)


def _xattn(x, mem_k, mem_v, w_q, w_xo, g_pre, g_post, g_next, perm, *, n_heads, seq_tile):
    bsz, t, d = x.shape
    n_mem = mem_k.shape[1]
    nb = bsz if t <= seq_tile else 1
    seq = min(seq_tile, t)
    assert t % seq == 0 and seq % BF16_ROWS == 0 and bsz % nb == 0 and (nb * seq) % perm.shape[0] == 0
    single = bsz // nb == 1
    mem_spec = (_resident((nb, n_mem, d)) if single else pl.BlockSpec((nb, n_mem, d), lambda b, i: (b, 0, 0)))
    row = lambda b, i: (b, i, 0)
    gain = pl.BlockSpec((1, d), lambda b, i: (0, 0))
    return pl.pallas_call(
        functools.partial(_xattn_kernel, n_heads=n_heads),
        grid=(bsz // nb, t // seq),
        in_specs=[pl.BlockSpec((nb, seq, d), row), mem_spec, mem_spec,
                  _resident(w_q.shape), _resident(w_xo.shape), gain, gain, gain,
                  pl.BlockSpec(perm.shape, lambda b, i: (0, 0))],
        out_specs=[pl.BlockSpec((nb, seq, d), row), pl.BlockSpec((nb, seq, d), row)],
        out_shape=[jax.ShapeDtypeStruct((bsz, t, d), F32), jax.ShapeDtypeStruct((bsz, t, d), BF16)],
        scratch_shapes=[pltpu.VMEM((nb * seq, d), BF16), pltpu.VMEM((nb * seq, d), BF16)],
        compiler_params=_params("parallel", "parallel"),
        name="xattn",
    )(x, mem_k, mem_v, w_q, w_xo, g_pre, g_post, g_next, perm)


def _up_geglu_kernel(*refs, starts, counts, blocks, n_side):
    n_groups = len(counts)
    ins, rest = refs[:3 * n_groups], refs[3 * n_groups:]
    wg_ref, wv_ref, cwg_ref, cwv_ref, cbg_ref, cbv_ref = rest[:6]
    side_in, rest = rest[6:6 + n_side], rest[6 + n_side:]
    outs, rest = rest[:3 * n_groups], rest[3 * n_groups:]
    side_out, scr = rest[:n_side], rest[n_side:]
    w_scrs, hists = scr[:2], scr[2:]
    _side_cast(side_in, side_out)
    i = pl.program_id(1)

    @pl.when(i == 0)
    def _():
        _cast_rows(wg_ref, w_scrs[0])
        _cast_rows(wv_ref, w_scrs[1])

    for g in range(n_groups):
        u_ref, st_refs = ins[3 * g], ins[3 * g + 1:3 * g + 3]
        hid_ref, ns_refs = outs[3 * g], outs[3 * g + 1:3 * g + 3]
        p_scrs = hists[2 * g:2 * g + 2]

        @pl.when(i == starts[g])
        def _(st_refs=st_refs, p_scrs=p_scrs, blk=blocks[g]):
            for st_ref, p_scr in zip(st_refs, p_scrs):
                _seg_hist_init(p_scr, st_ref, blk // SUBLANES)

        @pl.when((i >= starts[g]) & (i < starts[g] + counts[g]))
        def _(u_ref=u_ref, hid_ref=hid_ref, ns_refs=ns_refs, p_scrs=p_scrs, blk=blocks[g]):
            nb, seq, k_dim = u_ref.shape
            tn = hid_ref.shape[2]
            chunk = min(COL_CHUNK, tn)
            per_seq = seq // blk
            n_hist = p_scrs[0].shape[1]
            u = u_ref[...].reshape(nb * seq, k_dim)
            for c in range(tn // chunk):
                cols = slice(c * chunk, (c + 1) * chunk)
                ys = []
                for w_scr, p_scr, cw_ref, cb_ref, ns_ref in zip(w_scrs, p_scrs, (cwg_ref, cwv_ref),
                                                                (cbg_ref, cbv_ref), ns_refs):
                    a = jnp.dot(u, w_scr[:, cols], preferred_element_type=F32).reshape(nb * per_seq, blk, chunk)
                    tails = a[:, blk - n_hist:, :].reshape(nb, per_seq, n_hist, chunk)
                    hist = p_scr[:, :, cols][:, None]
                    if per_seq > 1:
                        hist = jnp.concatenate([hist, tails[:, :per_seq - 1]], axis=1)
                    hist = hist.reshape(nb * per_seq, n_hist, chunk)
                    ys.append(_seg_conv(a, hist, cw_ref[:, cols], cb_ref[:, cols]).reshape(nb, seq, chunk))
                    p_scr[:, :, cols] = tails[:, per_seq - 1]
                    _seg_state_out(ns_ref, a.reshape(nb, per_seq, blk, chunk)[:, per_seq - 1], cols)
                hid_ref[:, :, cols] = (_gelu_tanh(ys[0]) * ys[1]).astype(BF16)


def _up_geglu(us, w_up, states, conv_w, conv_b, blocks, *, tm, tn, to_round=()):
    d = w_up.shape[0]
    d_ff = w_up.shape[1] // 2
    width = conv_w.shape[0]
    n_j = d_ff // tn
    nbs = [u.shape[0] if u.shape[1] <= tm else 1 for u in us]
    seqs = [min(tm, u.shape[1]) for u in us]
    counts = [u.shape[1] // s for u, s in zip(us, seqs)]
    assert all(u.shape[0] == nb and u.shape[1] % s == 0 and s % BF16_ROWS == 0 for u, nb, s in zip(us, nbs, seqs))
    assert d_ff % tn == 0 and tn % min(COL_CHUNK, tn) == 0
    starts, steps = _group_steps(counts)
    col_g = lambda j, i: (0, j)
    col_v = lambda j, i: (0, n_j + j)
    st_g = lambda j, i: (0, 0, j)
    st_v = lambda j, i: (0, 0, n_j + j)
    in_specs, out_specs, out_shape, hist, args = [], [], [], [], []
    for g, (u, st, nb, seq) in enumerate(zip(us, states, nbs, seqs)):
        row = lambda j, i, g=g: (0, _group_tile(i, starts[g], counts[g]), 0)
        out = lambda j, i, g=g: (0, _group_tile(i, starts[g], counts[g]), j)
        in_specs += [pl.BlockSpec((nb, seq, d), row),
                     pl.BlockSpec((nb, width - 1, tn), st_g), pl.BlockSpec((nb, width - 1, tn), st_v)]
        out_specs += [pl.BlockSpec((nb, seq, tn), out),
                      pl.BlockSpec((nb, width - 1, tn), st_g), pl.BlockSpec((nb, width - 1, tn), st_g)]
        out_shape += [jax.ShapeDtypeStruct((u.shape[0], u.shape[1], d_ff), BF16),
                      jax.ShapeDtypeStruct((u.shape[0], width - 1, d_ff), F32),
                      jax.ShapeDtypeStruct((u.shape[0], width - 1, d_ff), F32)]
        assert seq % blocks[g] == 0 and blocks[g] % SUBLANES == 0
        n_hist = min(blocks[g] // SUBLANES, width - 1) * SUBLANES
        hist += [pltpu.VMEM((nb, n_hist, tn), F32), pltpu.VMEM((nb, n_hist, tn), F32)]
        args += [u, st, st]
    side_in, side_out, side_shape = _side_cast_plan(to_round, n_j * steps, lambda j, i: j * steps + i)
    in_specs += [pl.BlockSpec((d, tn), col_g), pl.BlockSpec((d, tn), col_v),
                 pl.BlockSpec((width, tn), col_g), pl.BlockSpec((width, tn), col_v),
                 pl.BlockSpec((1, tn), col_g), pl.BlockSpec((1, tn), col_v)] + side_in
    res = pl.pallas_call(
        functools.partial(_up_geglu_kernel, starts=starts, counts=counts, blocks=tuple(blocks),
                          n_side=len(to_round)),
        grid=(n_j, steps),
        in_specs=in_specs,
        out_specs=out_specs + side_out,
        out_shape=out_shape + side_shape,
        scratch_shapes=[pltpu.VMEM((d, tn), BF16), pltpu.VMEM((d, tn), BF16)] + hist,
        compiler_params=_params("arbitrary", "arbitrary"),
        name="up_geglu",
    )(*args, w_up, w_up, conv_w, conv_w, conv_b, conv_b, *to_round)
    return [tuple(res[3 * g:3 * g + 3]) for g in range(len(us))], list(res[3 * len(us):])


def _ffn_down_kernel(hid_ref, x_ref, wd_ref, g_ref, unperm_ref, o_ref):
    y = jnp.dot(hid_ref[...], wd_ref[...], preferred_element_type=F32)
    branch = _rmsnorm(y, g_ref[...]).astype(BF16)
    blk = unperm_ref.shape[0]
    for r in range(0, branch.shape[0], blk):
        o_ref[r:r + blk, :] = x_ref[r:r + blk, :] + jnp.dot(unperm_ref[...], branch[r:r + blk, :],
                                                            preferred_element_type=F32)


def _ffn_down(hid, x, w_down, g_post, unperm, *, tm):
    m, d = x.shape
    d_ff = hid.shape[1]
    tm = min(tm, m)
    assert m % tm == 0 and tm % unperm.shape[0] == 0
    row = lambda i: (i, 0)
    return pl.pallas_call(
        _ffn_down_kernel,
        grid=(m // tm,),
        in_specs=[pl.BlockSpec((tm, d_ff), row), pl.BlockSpec((tm, d), row), _resident(w_down.shape),
                  pl.BlockSpec((1, d), lambda i: (0, 0)), pl.BlockSpec(unperm.shape, lambda i: (0, 0))],
        out_specs=pl.BlockSpec((tm, d), row),
        out_shape=jax.ShapeDtypeStruct((m, d), F32),
        compiler_params=_params("parallel"),
        name="ffn_down",
    )(hid, x, w_down, g_post, unperm)


MM_TM, MM_TN = 2048, 1024
UP_TM, UP_TN = 1024, 512
SEQ_TILE = 256
ROW_TILE = 512
NORM_TM = 512
N_XHEADS = 4


def _layer(groups, p):
    flat = [x.reshape(-1, x.shape[-1]) for x, *_ in groups]
    perms, blocks = [], []
    for x, *_ in groups:
        bsz, t, _ = x.shape
        perms.append(_segment_major_perm(*((1, SEQ_TILE) if t > SEQ_TILE else (bsz, t))))
        blocks.append(min(t, SEQ_TILE))
    u0s = [_rmsnorm_call(x2, p['g_mix_pre'], tm=NORM_TM, perm=pm) for x2, pm in zip(flat, perms)]
    zs = _proj(u0s, p['w_in'], BF16, tm=MM_TM, tn=MM_TN)
    big = max(range(len(groups)), key=lambda g: flat[g].shape[0])
    resident = [p['p_a'], p['p_b'], p['w_o'], p['w_q'], p['w_xo']]
    mems = [m.reshape(-1, m.shape[-1]) for g in groups for m in g[2:4]]
    pre = []
    for g, ((x, pos0, _, _, s_a, h0, s_b, _), z) in enumerate(zip(groups, zs)):
        bsz, t, _ = x.shape
        pre.append(_mixer_pre(z.reshape(bsz, t, -1), s_a, h0[:, None, :], s_b,
                              p['conv_a_w'], p['conv_a_b'], p['w_ri'], p['b_r'], p['b_i'],
                              p['lru_lambda'], p['conv_b_w'], seq_tile=SEQ_TILE, pos0=pos0,
                              to_round=resident + mems if g == big else ()))
    p_a, p_b, w_o, w_q, w_xo, *mems = pre[big][5:]
    mids = []
    for g, ((x, *_), x2, z) in enumerate(zip(groups, flat, zs)):
        bsz, t, d = x.shape
        m = bsz * t
        ya, yb, ns_a, nh, ns_b = pre[g][:5]
        mem_k, mem_v = (mm.reshape(bsz, -1, d) for mm in mems[2 * g:2 * g + 2])
        x1 = _mixer_post(ya.reshape(m, -1), yb.reshape(m, -1), z, x2, p_a, p_b, w_o, p['g_mix_post'],
                         perms[g].T, tm=ROW_TILE)
        x2a, u3 = _xattn(x1.reshape(bsz, t, d), mem_k, mem_v, w_q, w_xo, p['g_x_pre'], p['g_x_post'],
                         p['g_ffn_pre'], perms[g], n_heads=N_XHEADS, seq_tile=ROW_TILE)
        mids.append((x2a, u3, ns_a, nh[:, 0, :], ns_b))
    ffn, (w_down,) = _up_geglu([u3 for _, u3, *_ in mids], p['w_up'], [g[7] for g in groups],
                               p['ffn_conv_w'], p['ffn_conv_b'], blocks, tm=UP_TM, tn=UP_TN,
                               to_round=[p['w_down']])
    results = []
    for g, ((x2a, _, ns_a, nh, ns_b), (hid, ns_fg, ns_fv)) in enumerate(zip(mids, ffn)):
        bsz, t, d = x2a.shape
        x3 = _ffn_down(hid.reshape(bsz * t, -1), x2a.reshape(bsz * t, d), w_down, p['g_ffn_post'], perms[g].T,
                       tm=ROW_TILE)
        results.append((x3.reshape(bsz, t, d), ns_a, nh, ns_b, jnp.concatenate([ns_fg, ns_fv], axis=-1)))
    return results


def kernel(x_prompt, x_sample, mem_prompt, state_conv_a, state_rglru, state_conv_b, state_ffn_conv, cache_mem_k, cache_mem_v, g_mix_pre, g_mix_post, w_in, conv_a_w, conv_a_b, w_r, b_r, w_i, b_i, lru_lambda, conv_b_w, p_a, p_b, w_o, g_x_pre, g_x_post, g_mem, w_q, w_k, w_v, w_xo, g_ffn_pre, g_ffn_post, w_up, ffn_conv_w, ffn_conv_b, w_down):
    depth = w_in.shape[0]
    bsz, _, d = x_prompt.shape
    n_mem = mem_prompt.shape[1]
    yp, ys = x_prompt, x_sample
    outs = [[] for _ in range(10)]
    row = lambda v: v.reshape(1, -1).astype(F32)
    for l in range(depth):
        p = {'g_mix_pre': row(g_mix_pre[l]), 'g_mix_post': row(g_mix_post[l]), 'w_in': w_in[l],
             'conv_a_w': conv_a_w[l], 'conv_a_b': row(conv_a_b[l]),
             'w_ri': jnp.concatenate([w_r[l], w_i[l]], axis=-1).astype(BF16),
             'b_r': row(b_r[l]), 'b_i': row(b_i[l]), 'lru_lambda': row(lru_lambda[l]),
             'conv_b_w': conv_b_w[l], 'p_a': p_a[l], 'p_b': p_b[l],
             'w_o': w_o[l], 'g_x_pre': row(g_x_pre[l]), 'g_x_post': row(g_x_post[l]),
             'w_q': w_q[l], 'w_xo': w_xo[l],
             'g_ffn_pre': row(g_ffn_pre[l]), 'g_ffn_post': row(g_ffn_post[l]), 'w_up': w_up[l],
             'ffn_conv_w': ffn_conv_w[l], 'ffn_conv_b': row(ffn_conv_b[l]), 'w_down': w_down[l]}
        d_rnn, d_conv, d_up = conv_a_w.shape[2], conv_b_w.shape[2], ffn_conv_w.shape[2]
        mem_u = _rmsnorm_call(mem_prompt.reshape(bsz * n_mem, d), row(g_mem[l]), tm=NORM_TM)
        mk = _proj([mem_u], w_k[l], F32, tm=MM_TM, tn=MM_TN)[0].reshape(bsz, n_mem, d)
        mv = _proj([mem_u], w_v[l], F32, tm=MM_TM, tn=MM_TN)[0].reshape(bsz, n_mem, d)
        zeros = lambda *s: jnp.zeros(s, F32)
        prompt = (yp, 0, mk, mv,
                  zeros(bsz, conv_a_w.shape[1] - 1, d_rnn), zeros(bsz, d_rnn),
                  zeros(bsz, conv_b_w.shape[1] - 1, d_conv), zeros(bsz, ffn_conv_w.shape[1] - 1, d_up))
        dec_b = x_sample.shape[0]
        ck = cache_mem_k[l].reshape(dec_b, n_mem, d)
        cv = cache_mem_v[l].reshape(dec_b, n_mem, d)
        sample = (ys, PAST_LEN, ck, cv, state_conv_a[l], state_rglru[l], state_conv_b[l], state_ffn_conv[l])
        (ys, *s_states), (yp, *p_states) = _layer([sample, prompt], p)
        for o, v in zip(outs, (*p_states, mk.reshape(bsz, n_mem, N_XHEADS, -1),
                               mv.reshape(bsz, n_mem, N_XHEADS, -1), *s_states)):
            o.append(v)
    return (yp, ys) + tuple(jnp.stack(o) for o in outs)
```

```python
import functools

import jax
import jax.numpy as jnp
import numpy as np
from jax import lax
from jax.experimental import pallas as pl
from jax.experimental.pallas import tpu as pltpu

F32 = jnp.float32
BF16 = jnp.bfloat16

EPS = 1e-6
RG_C = 8.0
PAST_LEN = 2048

LANES = 128
SUBLANES = 8
BF16_ROWS = 16
MXU_COLS = 256
VMEM_LIMIT_BYTES = 60 * 1024 * 1024

ROW_CHUNK = 128
CAST_ROWS = 256
COL_CHUNK = MXU_COLS
SCAN_LANES = 2048


def _params(*semantics):
    return pltpu.CompilerParams(dimension_semantics=semantics, vmem_limit_bytes=VMEM_LIMIT_BYTES)


def _resident(shape):
    nd = len(shape)
    return pl.BlockSpec(shape, lambda *_: (0,) * nd, pipeline_mode=pl.Buffered(1))


def _rmsnorm(x, g):
    y = x * lax.rsqrt(jnp.mean(x * x, axis=-1, keepdims=True) + EPS)
    return y * g


def _sigmoid(x):
    return 0.5 * jnp.tanh(0.5 * x) + 0.5


def _gelu_tanh(x):
    c1 = (2.0 / jnp.pi) ** 0.5
    return x * (0.5 * jnp.tanh(x * (c1 + (c1 * 0.044715) * (x * x))) + 0.5)


def _causal_conv(x, before, w, b):
    nb, t, c = x.shape
    width = w.shape[0]
    groups = t // SUBLANES
    full = jnp.concatenate([before, x], axis=1).reshape(nb * (groups + 1), SUBLANES, c)
    row = lax.broadcasted_iota(jnp.int32, (1, 1, SUBLANES, c), 2)
    y = None
    for k in range(width):
        lag = width - 1 - k
        if lag == 0:
            tap = x
        else:
            rot = pltpu.roll(full, lag, 1).reshape(nb, groups + 1, SUBLANES, c)
            tap = jnp.where(row >= lag, rot[:, 1:], rot[:, :-1]).reshape(nb, t, c)
        y = tap * w[k:k + 1, :] if y is None else y + tap * w[k:k + 1, :]
    return y if b is None else y + b


def _segment_major_perm(n_seq, seq):
    n_groups = seq // SUBLANES
    row = np.arange(seq)
    time = (row % SUBLANES) * n_groups + row // SUBLANES
    p = np.zeros((seq, seq), np.float32)
    p[row, time] = 1.0
    return jnp.asarray(np.kron(np.eye(n_seq, dtype=np.float32), p), BF16)


def _seg_coords(n_groups, back):
    t = SUBLANES * n_groups - back
    return t % n_groups, t // n_groups


def _seg_hist_init(hist_ref, state_ref, n_groups):
    n_state = state_ref.shape[1]
    n_hist = hist_ref.shape[1] // SUBLANES
    hist_ref[...] = jnp.zeros(hist_ref.shape, F32)
    for back in range(1, n_state + 1):
        g, s = _seg_coords(n_groups, back)
        row = (g - (n_groups - n_hist)) * SUBLANES + s
        hist_ref[:, row:row + 1, :] = state_ref[:, n_state - back:n_state - back + 1, :]


def _seg_state_out(ns_ref, x_ref, cols=slice(None)):
    n_state = ns_ref.shape[1]
    n_groups = x_ref.shape[1] // SUBLANES
    for back in range(1, n_state + 1):
        g, s = _seg_coords(n_groups, back)
        row = g * SUBLANES + s
        ns_ref[:, n_state - back:n_state - back + 1, cols] = x_ref[:, row:row + 1, :]


def _seg_conv(x_ref, hist_ref, w_ref, bias):
    nb, seq, c = x_ref.shape
    n_groups = seq // SUBLANES
    n_hist = hist_ref.shape[1] // SUBLANES
    width = w_ref.shape[0]
    assert n_hist >= min(n_groups, width - 1)
    sub = lax.broadcasted_iota(jnp.int32, (1, SUBLANES, c), 1)
    y = None
    for k in range(width):
        lag = width - 1 - k
        if lag == 0:
            tap = x_ref[...]
        else:
            parts = []
            for g in range(min(lag, n_groups)):
                src = (g - lag) % n_groups
                crossed = (lag - g + n_groups - 1) // n_groups
                cur = x_ref[:, src * SUBLANES:(src + 1) * SUBLANES, :]
                h0 = (src - (n_groups - n_hist)) * SUBLANES
                prev = hist_ref[:, h0:h0 + SUBLANES, :]
                parts.append(pltpu.roll(jnp.where(sub >= SUBLANES - crossed, prev, cur), crossed, 1))
            if n_groups > lag:
                parts.append(x_ref[:, 0:(n_groups - lag) * SUBLANES, :])
            tap = jnp.concatenate(parts, axis=1)
        y = tap * w_ref[k:k + 1, :] if y is None else y + tap * w_ref[k:k + 1, :]
    return y if bias is None else y + bias


def _softplus(x):
    return jnp.maximum(x, 0.0) + jnp.log1p(jnp.exp(-jnp.abs(x)))


def _permute_rows(ref, perm_ref):
    blk = perm_ref.shape[0]
    for r in range(0, ref.shape[0], blk):
        ref[r:r + blk, :] = jnp.dot(perm_ref[...], ref[r:r + blk, :], preferred_element_type=F32).astype(BF16)


def _rmsnorm_kernel(x_ref, g_ref, *rest):
    u_ref = rest[-1]
    tm = x_ref.shape[0]
    chunk = min(ROW_CHUNK, tm)
    g = g_ref[...]

    def body(c, _):
        rows = pl.ds(pl.multiple_of(c * chunk, chunk), chunk)
        u_ref[rows, :] = _rmsnorm(x_ref[rows, :], g).astype(BF16)
        return None

    lax.fori_loop(0, tm // chunk, body, None)
    if len(rest) == 2:
        _permute_rows(u_ref, rest[0])


def _rmsnorm_call(x, g, *, tm, perm=None):
    m, d = x.shape
    tm = min(tm, m)
    assert m % tm == 0 and tm % min(ROW_CHUNK, tm) == 0 and (perm is None or tm % perm.shape[0] == 0)
    extra = [] if perm is None else [perm]
    return pl.pallas_call(
        _rmsnorm_kernel,
        grid=(m // tm,),
        in_specs=[pl.BlockSpec((tm, d), lambda i: (i, 0)), pl.BlockSpec((1, d), lambda i: (0, 0))]
        + [pl.BlockSpec(a.shape, lambda i: (0, 0)) for a in extra],
        out_specs=pl.BlockSpec((tm, d), lambda i: (i, 0)),
        out_shape=jax.ShapeDtypeStruct((m, d), BF16),
        compiler_params=_params("parallel"),
        name="rmsnorm",
    )(x, g, *extra)


def _cast_rows(src_ref, dst_ref):
    rows_total = src_ref.shape[0]
    chunk = min(CAST_ROWS, rows_total)

    def body(c, _):
        rows = pl.ds(pl.multiple_of(c * chunk, chunk), chunk)
        dst_ref[rows, :] = src_ref[rows, :].astype(BF16)
        return None

    lax.fori_loop(0, rows_total // chunk, body, None)


def _side_cast_plan(arrays, n_steps, flat_step):
    in_specs, out_specs, out_shape = [], [], []
    for a in arrays:
        rows, cols = a.shape
        chunk = next(c for c in range(BF16_ROWS, rows + 1, BF16_ROWS) if rows % c == 0 and rows // c <= n_steps)
        last = rows // chunk - 1
        spec = pl.BlockSpec((chunk, cols), lambda *idx, last=last: (jnp.minimum(flat_step(*idx), last), 0))
        in_specs.append(spec)
        out_specs.append(spec)
        out_shape.append(jax.ShapeDtypeStruct((rows, cols), BF16))
    return in_specs, out_specs, out_shape


def _side_cast(src_refs, dst_refs):
    for src, dst in zip(src_refs, dst_refs):
        dst[...] = src[...].astype(BF16)


def _group_steps(counts):
    starts = [sum(counts[:g]) for g in range(len(counts))]
    return starts, sum(counts)


def _group_tile(i, start, count):
    return jnp.clip(i - start, 0, count - 1)


def _proj_kernel(*refs, starts, counts):
    n_groups = len(counts)
    u_refs, w_ref, o_refs, w_scr = refs[:n_groups], refs[n_groups], refs[n_groups + 1:-1], refs[-1]
    i = pl.program_id(1)

    @pl.when(i == 0)
    def _():
        _cast_rows(w_ref, w_scr)

    for g in range(n_groups):
        @pl.when((i >= starts[g]) & (i < starts[g] + counts[g]))
        def _(g=g):
            o_refs[g][...] = jnp.dot(u_refs[g][...], w_scr[...],
                                     preferred_element_type=F32).astype(o_refs[g].dtype)


def _proj(us, w, out_dtype, *, tm, tn):
    d, n = w.shape
    tn = min(tn, n)
    tms = [min(tm, u.shape[0]) for u in us]
    counts = [u.shape[0] // t for u, t in zip(us, tms)]
    assert all(u.shape[0] % t == 0 for u, t in zip(us, tms)) and n % tn == 0 and d % min(CAST_ROWS, d) == 0
    starts, steps = _group_steps(counts)
    tile = lambda g: (lambda j, i: (_group_tile(i, starts[g], counts[g]), 0))
    out_tile = lambda g: (lambda j, i: (_group_tile(i, starts[g], counts[g]), j))
    return pl.pallas_call(
        functools.partial(_proj_kernel, starts=starts, counts=counts),
        grid=(n // tn, steps),
        in_specs=([pl.BlockSpec((tms[g], d), tile(g)) for g in range(len(us))]
                  + [pl.BlockSpec((d, tn), lambda j, i: (0, j))]),
        out_specs=[pl.BlockSpec((tms[g], tn), out_tile(g)) for g in range(len(us))],
        out_shape=[jax.ShapeDtypeStruct((u.shape[0], n), out_dtype) for u in us],
        scratch_shapes=[pltpu.VMEM((d, tn), BF16)],
        compiler_params=_params("arbitrary", "arbitrary"),
        name="proj",
    )(*us, w)


def _mixer_pre_kernel(*refs, d_rnn, d_conv, pos0, n_side):
    (z_ref, sa_ref, h0_ref, sb_ref, caw_ref, cab_ref, wri_ref, br_ref, bi_ref, lam_ref, cbw_ref) = refs[:11]
    side_in = refs[11:11 + n_side]
    ya_ref, yb_ref, nsa_ref, nh_ref, nsb_ref = refs[11 + n_side:16 + n_side]
    side_out = refs[16 + n_side:16 + 2 * n_side]
    xa_scr, ha_scr, xc_scr, xcb_scr, ri_scr, cb_scr, hb_scr, h_scr = refs[16 + 2 * n_side:]
    _side_cast(side_in, side_out)
    t = pl.program_id(1)
    nb, seq, _ = ya_ref.shape
    n_groups = seq // SUBLANES
    n_heads, head_dim, _ = wri_ref.shape

    @pl.when(t == 0)
    def _():
        _seg_hist_init(ha_scr, sa_ref, n_groups)
        _seg_hist_init(hb_scr, sb_ref, n_groups)
        h_scr[...] = jnp.broadcast_to(h0_ref[...], h_scr.shape)

    xa_scr[...] = z_ref[:, :, 0:d_rnn].astype(F32)
    xc = _seg_conv(xa_scr, ha_scr, caw_ref, cab_ref[...])
    xc_scr[...] = xc
    xcb_scr[...] = xc.reshape(nb * seq, d_rnn).astype(BF16)
    _seg_state_out(nsa_ref, xa_scr)
    ha_scr[...] = xa_scr[:, seq - ha_scr.shape[1]:, :]

    for h in range(n_heads):
        cols = slice(h * head_dim, (h + 1) * head_dim)
        ri = jnp.dot(xcb_scr[:, cols], wri_ref[h], preferred_element_type=F32)
        ri_scr[:, :, cols] = ri[:, :head_dim].reshape(nb, seq, head_dim)
        ri_scr[:, :, d_rnn + h * head_dim:d_rnn + (h + 1) * head_dim] = ri[:, head_dim:].reshape(nb, seq, head_dim)

    strip = max(LANES, SCAN_LANES // nb)
    pairs = seq // BF16_ROWS
    sub = lax.broadcasted_iota(jnp.int32, (1, SUBLANES, strip), 1)
    for c in range(d_rnn // strip):
        cs = slice(c * strip, (c + 1) * strip)
        cs_i = slice(d_rnn + c * strip, d_rnn + (c + 1) * strip)
        cs_g = slice(d_rnn + c * strip, d_rnn + (c + 1) * strip)
        sp = _softplus(-lam_ref[:, cs])
        b_r = br_ref[:, cs]
        b_i = bi_ref[:, cs]

        def pass1(it, carry, first=False, cs=cs, cs_i=cs_i, sp=sp, b_r=b_r, b_i=b_i):
            h_loc, a_run = carry
            for half in range(BF16_ROWS // SUBLANES):
                rows = pl.ds(pl.multiple_of(it * BF16_ROWS + half * SUBLANES, SUBLANES), SUBLANES)
                r = _sigmoid(ri_scr[:, rows, cs] + b_r)
                i = _sigmoid(ri_scr[:, rows, cs_i] + b_i)
                log_a = -RG_C * r * sp
                a = jnp.exp(log_a)
                mult = jnp.sqrt(-jnp.tanh(log_a) * (a * a + 1.0))
                if first and half == 0:
                    mult = jnp.where(jnp.logical_and(t == 0, sub == 0), 1.0, mult)
                b = mult * i * xc_scr[:, rows, cs]
                h_loc = a * h_loc + b
                a_run = a * a_run
                ri_scr[:, rows, cs] = h_loc
                ri_scr[:, rows, cs_i] = a_run
            return h_loc, a_run

        carry = (jnp.zeros((nb, SUBLANES, strip), F32), jnp.ones((nb, SUBLANES, strip), F32))
        start = 0
        if pos0 == 0:
            carry = pass1(0, carry, first=True)
            start = 1
        h_tot, a_tot = lax.fori_loop(start, pairs, pass1, carry)

        for s in (1, 2, 4):
            keep = sub >= s
            a_prev = pltpu.roll(a_tot, s, 1)
            h_prev = pltpu.roll(h_tot, s, 1)
            h_tot = jnp.where(keep, a_tot * h_prev + h_tot, h_tot)
            a_tot = jnp.where(keep, a_tot * a_prev, a_tot)
        h_in = h_scr[:, :, cs]
        seg_end = a_tot * h_in + h_tot
        seg_start = jnp.where(sub == 0, h_in, pltpu.roll(seg_end, 1, 1))
        h_last = jnp.broadcast_to(seg_end[:, SUBLANES - 1:SUBLANES, :], seg_end.shape)
        h_scr[:, :, cs] = h_last
        nh_ref[:, :, cs] = h_last[:, 0:1, :]
        start2 = jnp.concatenate([seg_start] * (BF16_ROWS // SUBLANES), axis=1)

        def pass2(it, _, cs=cs, cs_i=cs_i, cs_g=cs_g, start2=start2):
            rows = pl.ds(pl.multiple_of(it * BF16_ROWS, BF16_ROWS), BF16_ROWS)
            h = ri_scr[:, rows, cs] + ri_scr[:, rows, cs_i] * start2
            gate = _gelu_tanh(z_ref[:, rows, cs_g].astype(F32))
            ya_ref[:, rows, cs] = (h * gate).astype(BF16)
            return None

        lax.fori_loop(0, pairs, pass2, None)

    o_gb, o_gc, o_hb = 2 * d_rnn, 2 * d_rnn + d_conv, 2 * d_rnn + 2 * d_conv
    cb_scr[...] = z_ref[:, :, o_gc:o_gc + d_conv].astype(F32) * z_ref[:, :, o_hb:o_hb + d_conv].astype(F32)
    co = _seg_conv(cb_scr, hb_scr, cbw_ref, None)
    yb_ref[...] = (z_ref[:, :, o_gb:o_gb + d_conv].astype(F32) * co).astype(BF16)
    _seg_state_out(nsb_ref, cb_scr)
    hb_scr[...] = cb_scr[:, seq - hb_scr.shape[1]:, :]


def _mixer_pre(z, state_a, h0, state_b, conv_a_w, conv_a_b, w_ri, b_r, b_i, lam, conv_b_w, *, seq_tile, pos0,
               to_round=()):
    bsz, t, _ = z.shape
    d_rnn = conv_a_w.shape[1]
    d_conv = conv_b_w.shape[1]
    n_pre = 2 * d_rnn + 3 * d_conv
    nb = bsz if t <= seq_tile else 1
    seq = min(seq_tile, t)
    assert t % seq == 0 and seq % BF16_ROWS == 0 and bsz % nb == 0
    wa, wb = conv_a_w.shape[0], conv_b_w.shape[0]
    hist_a, hist_b = min(seq // SUBLANES, wa - 1), min(seq // SUBLANES, wb - 1)
    kern = functools.partial(_mixer_pre_kernel, d_rnn=d_rnn, d_conv=d_conv, pos0=pos0, n_side=len(to_round))
    n_t = t // seq
    side_in, side_out, side_shape = _side_cast_plan(to_round, (bsz // nb) * n_t, lambda b, i: b * n_t + i)
    row = lambda b, i: (b, i, 0)
    per_b = lambda b, i: (b, 0, 0)
    const2 = lambda b, i: (0, 0)
    return pl.pallas_call(
        kern,
        grid=(bsz // nb, t // seq),
        in_specs=[pl.BlockSpec((nb, seq, n_pre), row),
                  pl.BlockSpec((nb, wa - 1, d_rnn), per_b),
                  pl.BlockSpec((nb, 1, d_rnn), per_b),
                  pl.BlockSpec((nb, wb - 1, d_conv), per_b),
                  pl.BlockSpec(conv_a_w.shape, const2),
                  pl.BlockSpec((1, d_rnn), const2),
                  pl.BlockSpec(w_ri.shape, lambda b, i: (0, 0, 0)),
                  pl.BlockSpec((1, d_rnn), const2),
                  pl.BlockSpec((1, d_rnn), const2),
                  pl.BlockSpec((1, d_rnn), const2),
                  pl.BlockSpec(conv_b_w.shape, const2)] + side_in,
        out_specs=[pl.BlockSpec((nb, seq, d_rnn), row),
                   pl.BlockSpec((nb, seq, d_conv), row),
                   pl.BlockSpec((nb, wa - 1, d_rnn), per_b),
                   pl.BlockSpec((nb, 1, d_rnn), per_b),
                   pl.BlockSpec((nb, wb - 1, d_conv), per_b)] + side_out,
        out_shape=[jax.ShapeDtypeStruct((bsz, t, d_rnn), BF16),
                   jax.ShapeDtypeStruct((bsz, t, d_conv), BF16),
                   jax.ShapeDtypeStruct((bsz, wa - 1, d_rnn), F32),
                   jax.ShapeDtypeStruct((bsz, 1, d_rnn), F32),
                   jax.ShapeDtypeStruct((bsz, wb - 1, d_conv), F32)] + side_shape,
        scratch_shapes=[pltpu.VMEM((nb, seq, d_rnn), F32),
                        pltpu.VMEM((nb, hist_a * SUBLANES, d_rnn), F32),
                        pltpu.VMEM((nb, seq, d_rnn), F32),
                        pltpu.VMEM((nb * seq, d_rnn), BF16),
                        pltpu.VMEM((nb, seq, 2 * d_rnn), F32),
                        pltpu.VMEM((nb, seq, d_conv), F32),
                        pltpu.VMEM((nb, hist_b * SUBLANES, d_conv), F32),
                        pltpu.VMEM((nb, SUBLANES, d_rnn), F32)],
        compiler_params=_params("arbitrary", "arbitrary"),
        name="mixer_pre",
    )(z, state_a, h0, state_b, conv_a_w, conv_a_b, w_ri, b_r, b_i, lam, conv_b_w, *to_round)


def _mixer_post_kernel(ya_ref, yb_ref, ga0_ref, ga1_ref, gb0_ref, gb1_ref, x_ref, pa_ref, pb_ref, wo_ref, g_ref,
                       unperm_ref, o_ref, mix_scr):
    half = ga0_ref.shape[1]
    y_a = jnp.dot(ya_ref[...], pa_ref[...], preferred_element_type=F32)
    y_b = jnp.dot(yb_ref[...], pb_ref[...], preferred_element_type=F32)
    for c, (ga_ref, gb_ref) in enumerate(((ga0_ref, gb0_ref), (ga1_ref, gb1_ref))):
        cols = slice(c * half, (c + 1) * half)
        mix = (jax.nn.sigmoid(ga_ref[...].astype(F32)) * y_a[:, cols]
               + jax.nn.sigmoid(gb_ref[...].astype(F32)) * y_b[:, cols])
        mix_scr[:, cols] = mix.astype(BF16)
    _permute_rows(mix_scr, unperm_ref)
    out = jnp.dot(mix_scr[...], wo_ref[...], preferred_element_type=F32)
    o_ref[...] = x_ref[...] + _rmsnorm(out, g_ref[...])


def _mixer_post(ya, yb, z, x, p_a, p_b, w_o, g_post, unperm, *, tm):
    m, d = x.shape
    d_rnn, d_conv = ya.shape[1], yb.shape[1]
    tm = min(tm, m)
    half = d // 2
    gate0 = (2 * d_rnn + 3 * d_conv) // half
    assert m % tm == 0 and (2 * d_rnn + 3 * d_conv) % half == 0 and tm % unperm.shape[0] == 0
    gate_spec = lambda k: pl.BlockSpec((tm, half), lambda i: (i, gate0 + k))
    row = lambda i: (i, 0)
    return pl.pallas_call(
        _mixer_post_kernel,
        grid=(m // tm,),
        in_specs=[pl.BlockSpec((tm, d_rnn), row), pl.BlockSpec((tm, d_conv), row),
                  gate_spec(0), gate_spec(1), gate_spec(2), gate_spec(3),
                  pl.BlockSpec((tm, d), row),
                  _resident(p_a.shape), _resident(p_b.shape), _resident(w_o.shape),
                  pl.BlockSpec((1, d), lambda i: (0, 0)), pl.BlockSpec(unperm.shape, lambda i: (0, 0))],
        out_specs=pl.BlockSpec((tm, d), row),
        out_shape=jax.ShapeDtypeStruct((m, d), F32),
        scratch_shapes=[pltpu.VMEM((tm, d), BF16)],
        compiler_params=_params("parallel"),
        name="mixer_post",
    )(ya, yb, z, z, z, z, x, p_a, p_b, w_o, g_post, unperm)


def _xattn_kernel(x_ref, k_ref, v_ref, wq_ref, wxo_ref, gpre_ref, gpost_ref, gnext_ref, perm_ref, o_ref, u_ref,
                  q_scr, o_scr, *, n_heads):
    nb, seq, d = x_ref.shape
    hd = d // n_heads
    scale = hd ** -0.5
    x = x_ref[...].reshape(nb * seq, d)
    u = _rmsnorm(x, gpre_ref[...]).astype(BF16)
    q_scr[...] = jnp.dot(u, wq_ref[...], preferred_element_type=F32).astype(BF16)
    for b in range(nb):
        rows = slice(b * seq, (b + 1) * seq)
        for h in range(n_heads):
            cols = slice(h * hd, (h + 1) * hd)
            s = lax.dot_general(q_scr[rows, cols], k_ref[b, :, cols], (((1,), (1,)), ((), ())),
                                preferred_element_type=F32) * scale
            p = jnp.exp(s - jnp.max(s, axis=-1, keepdims=True))
            att = (p / jnp.sum(p, axis=-1, keepdims=True)).astype(BF16)
            o_scr[rows, cols] = jnp.dot(att, v_ref[b, :, cols], preferred_element_type=F32).astype(BF16)
    out = jnp.dot(o_scr[...], wxo_ref[...], preferred_element_type=F32)
    x_new = x + _rmsnorm(out, gpost_ref[...])
    o_ref[...] = x_new.reshape(nb, seq, d)
    u_next = _rmsnorm(x_new, gnext_ref[...]).astype(BF16)
    blk = perm_ref.shape[0]
    u_next = jnp.concatenate([jnp.dot(perm_ref[...], u_next[r:r + blk, :], preferred_element_type=F32)
                              for r in range(0, nb * seq, blk)], axis=0)
    u_ref[...] = u_next.astype(BF16).reshape(nb, seq, d)


def _xattn(x, mem_k, mem_v, w_q, w_xo, g_pre, g_post, g_next, perm, *, n_heads, seq_tile):
    bsz, t, d = x.shape
    n_mem = mem_k.shape[1]
    nb = bsz if t <= seq_tile else 1
    seq = min(seq_tile, t)
    assert t % seq == 0 and seq % BF16_ROWS == 0 and bsz % nb == 0 and (nb * seq) % perm.shape[0] == 0
    single = bsz // nb == 1
    mem_spec = (_resident((nb, n_mem, d)) if single else pl.BlockSpec((nb, n_mem, d), lambda b, i: (b, 0, 0)))
    row = lambda b, i: (b, i, 0)
    gain = pl.BlockSpec((1, d), lambda b, i: (0, 0))
    return pl.pallas_call(
        functools.partial(_xattn_kernel, n_heads=n_heads),
        grid=(bsz // nb, t // seq),
        in_specs=[pl.BlockSpec((nb, seq, d), row), mem_spec, mem_spec,
                  _resident(w_q.shape), _resident(w_xo.shape), gain, gain, gain,
                  pl.BlockSpec(perm.shape, lambda b, i: (0, 0))],
        out_specs=[pl.BlockSpec((nb, seq, d), row), pl.BlockSpec((nb, seq, d), row)],
        out_shape=[jax.ShapeDtypeStruct((bsz, t, d), F32), jax.ShapeDtypeStruct((bsz, t, d), BF16)],
        scratch_shapes=[pltpu.VMEM((nb * seq, d), BF16), pltpu.VMEM((nb * seq, d), BF16)],
        compiler_params=_params("parallel", "parallel"),
        name="xattn",
    )(x, mem_k, mem_v, w_q, w_xo, g_pre, g_post, g_next, perm)


def _up_geglu_kernel(*refs, starts, counts, blocks, n_side):
    n_groups = len(counts)
    ins, rest = refs[:3 * n_groups], refs[3 * n_groups:]
    wg_ref, wv_ref, cwg_ref, cwv_ref, cbg_ref, cbv_ref = rest[:6]
    side_in, rest = rest[6:6 + n_side], rest[6 + n_side:]
    outs, rest = rest[:3 * n_groups], rest[3 * n_groups:]
    side_out, scr = rest[:n_side], rest[n_side:]
    w_scrs, hists = scr[:2], scr[2:]
    _side_cast(side_in, side_out)
    i = pl.program_id(1)

    @pl.when(i == 0)
    def _():
        _cast_rows(wg_ref, w_scrs[0])
        _cast_rows(wv_ref, w_scrs[1])

    for g in range(n_groups):
        u_ref, st_refs = ins[3 * g], ins[3 * g + 1:3 * g + 3]
        hid_ref, ns_refs = outs[3 * g], outs[3 * g + 1:3 * g + 3]
        p_scrs = hists[2 * g:2 * g + 2]

        @pl.when(i == starts[g])
        def _(st_refs=st_refs, p_scrs=p_scrs, blk=blocks[g]):
            for st_ref, p_scr in zip(st_refs, p_scrs):
                _seg_hist_init(p_scr, st_ref, blk // SUBLANES)

        @pl.when((i >= starts[g]) & (i < starts[g] + counts[g]))
        def _(u_ref=u_ref, hid_ref=hid_ref, ns_refs=ns_refs, p_scrs=p_scrs, blk=blocks[g]):
            nb, seq, k_dim = u_ref.shape
            tn = hid_ref.shape[2]
            chunk = min(COL_CHUNK, tn)
            per_seq = seq // blk
            n_hist = p_scrs[0].shape[1]
            u = u_ref[...].reshape(nb * seq, k_dim)
            for c in range(tn // chunk):
                cols = slice(c * chunk, (c + 1) * chunk)
                ys = []
                for w_scr, p_scr, cw_ref, cb_ref, ns_ref in zip(w_scrs, p_scrs, (cwg_ref, cwv_ref),
                                                                (cbg_ref, cbv_ref), ns_refs):
                    a = jnp.dot(u, w_scr[:, cols], preferred_element_type=F32).reshape(nb * per_seq, blk, chunk)
                    tails = a[:, blk - n_hist:, :].reshape(nb, per_seq, n_hist, chunk)
                    hist = p_scr[:, :, cols][:, None]
                    if per_seq > 1:
                        hist = jnp.concatenate([hist, tails[:, :per_seq - 1]], axis=1)
                    hist = hist.reshape(nb * per_seq, n_hist, chunk)
                    ys.append(_seg_conv(a, hist, cw_ref[:, cols], cb_ref[:, cols]).reshape(nb, seq, chunk))
                    p_scr[:, :, cols] = tails[:, per_seq - 1]
                    _seg_state_out(ns_ref, a.reshape(nb, per_seq, blk, chunk)[:, per_seq - 1], cols)
                hid_ref[:, :, cols] = (_gelu_tanh(ys[0]) * ys[1]).astype(BF16)


def _up_geglu(us, w_up, states, conv_w, conv_b, blocks, *, tm, tn, to_round=()):
    d = w_up.shape[0]
    d_ff = w_up.shape[1] // 2
    width = conv_w.shape[0]
    n_j = d_ff // tn
    nbs = [u.shape[0] if u.shape[1] <= tm else 1 for u in us]
    seqs = [min(tm, u.shape[1]) for u in us]
    counts = [u.shape[1] // s for u, s in zip(us, seqs)]
    assert all(u.shape[0] == nb and u.shape[1] % s == 0 and s % BF16_ROWS == 0 for u, nb, s in zip(us, nbs, seqs))
    assert d_ff % tn == 0 and tn % min(COL_CHUNK, tn) == 0
    starts, steps = _group_steps(counts)
    col_g = lambda j, i: (0, j)
    col_v = lambda j, i: (0, n_j + j)
    st_g = lambda j, i: (0, 0, j)
    st_v = lambda j, i: (0, 0, n_j + j)
    in_specs, out_specs, out_shape, hist, args = [], [], [], [], []
    for g, (u, st, nb, seq) in enumerate(zip(us, states, nbs, seqs)):
        row = lambda j, i, g=g: (0, _group_tile(i, starts[g], counts[g]), 0)
        out = lambda j, i, g=g: (0, _group_tile(i, starts[g], counts[g]), j)
        in_specs += [pl.BlockSpec((nb, seq, d), row),
                     pl.BlockSpec((nb, width - 1, tn), st_g), pl.BlockSpec((nb, width - 1, tn), st_v)]
        out_specs += [pl.BlockSpec((nb, seq, tn), out),
                      pl.BlockSpec((nb, width - 1, tn), st_g), pl.BlockSpec((nb, width - 1, tn), st_g)]
        out_shape += [jax.ShapeDtypeStruct((u.shape[0], u.shape[1], d_ff), BF16),
                      jax.ShapeDtypeStruct((u.shape[0], width - 1, d_ff), F32),
                      jax.ShapeDtypeStruct((u.shape[0], width - 1, d_ff), F32)]
        assert seq % blocks[g] == 0 and blocks[g] % SUBLANES == 0
        n_hist = min(blocks[g] // SUBLANES, width - 1) * SUBLANES
        hist += [pltpu.VMEM((nb, n_hist, tn), F32), pltpu.VMEM((nb, n_hist, tn), F32)]
        args += [u, st, st]
    side_in, side_out, side_shape = _side_cast_plan(to_round, n_j * steps, lambda j, i: j * steps + i)
    in_specs += [pl.BlockSpec((d, tn), col_g), pl.BlockSpec((d, tn), col_v),
                 pl.BlockSpec((width, tn), col_g), pl.BlockSpec((width, tn), col_v),
                 pl.BlockSpec((1, tn), col_g), pl.BlockSpec((1, tn), col_v)] + side_in
    res = pl.pallas_call(
        functools.partial(_up_geglu_kernel, starts=starts, counts=counts, blocks=tuple(blocks),
                          n_side=len(to_round)),
        grid=(n_j, steps),
        in_specs=in_specs,
        out_specs=out_specs + side_out,
        out_shape=out_shape + side_shape,
        scratch_shapes=[pltpu.VMEM((d, tn), BF16), pltpu.VMEM((d, tn), BF16)] + hist,
        compiler_params=_params("arbitrary", "arbitrary"),
        name="up_geglu",
    )(*args, w_up, w_up, conv_w, conv_w, conv_b, conv_b, *to_round)
    return [tuple(res[3 * g:3 * g + 3]) for g in range(len(us))], list(res[3 * len(us):])


def _ffn_down_kernel(*refs, starts, counts):
    n_groups = len(counts)
    wd_ref, g_ref = refs[3 * n_groups:3 * n_groups + 2]
    o_refs = refs[3 * n_groups + 2:]
    i = pl.program_id(0)
    for g in range(n_groups):
        hid_ref, x_ref, unperm_ref = refs[3 * g:3 * g + 3]

        @pl.when((i >= starts[g]) & (i < starts[g] + counts[g]))
        def _(hid_ref=hid_ref, x_ref=x_ref, unperm_ref=unperm_ref, o_ref=o_refs[g]):
            y = jnp.dot(hid_ref[...], wd_ref[...], preferred_element_type=F32)
            branch = _rmsnorm(y, g_ref[...]).astype(BF16)
            blk = unperm_ref.shape[0]
            for r in range(0, branch.shape[0], blk):
                o_ref[r:r + blk, :] = x_ref[r:r + blk, :] + jnp.dot(unperm_ref[...], branch[r:r + blk, :],
                                                                    preferred_element_type=F32)


def _ffn_down(hids, xs, w_down, g_post, unperms, *, tm):
    d_ff, d = w_down.shape
    tms = [min(tm, x.shape[0]) for x in xs]
    counts = [x.shape[0] // t for x, t in zip(xs, tms)]
    assert all(x.shape[0] % t == 0 and t % un.shape[0] == 0 for x, t, un in zip(xs, tms, unperms))
    starts, steps = _group_steps(counts)
    in_specs, args = [], []
    for g, (hid, x, un) in enumerate(zip(hids, xs, unperms)):
        row = lambda i, g=g: (_group_tile(i, starts[g], counts[g]), 0)
        in_specs += [pl.BlockSpec((tms[g], d_ff), row), pl.BlockSpec((tms[g], d), row),
                     pl.BlockSpec(un.shape, lambda i: (0, 0))]
        args += [hid, x, un]
    return pl.pallas_call(
        functools.partial(_ffn_down_kernel, starts=starts, counts=counts),
        grid=(steps,),
        in_specs=in_specs + [_resident(w_down.shape), pl.BlockSpec((1, d), lambda i: (0, 0))],
        out_specs=[pl.BlockSpec((tms[g], d), lambda i, g=g: (_group_tile(i, starts[g], counts[g]), 0))
                   for g in range(len(xs))],
        out_shape=[jax.ShapeDtypeStruct(x.shape, F32) for x in xs],
        compiler_params=_params("arbitrary"),
        name="ffn_down",
    )(*args, w_down, g_post)


MM_TM, MM_TN = 2048, 1024
UP_TM, UP_TN = 1024, 512
SEQ_TILE = 256
ROW_TILE = 512
NORM_TM = 1024
N_XHEADS = 4


def _layer(groups, p):
    flat = [x.reshape(-1, x.shape[-1]) for x, *_ in groups]
    perms, blocks = [], []
    for x, *_ in groups:
        bsz, t, _ = x.shape
        perms.append(_segment_major_perm(*((1, SEQ_TILE) if t > SEQ_TILE else (bsz, t))))
        blocks.append(min(t, SEQ_TILE))
    u0s = [_rmsnorm_call(x2, p['g_mix_pre'], tm=NORM_TM, perm=pm) for x2, pm in zip(flat, perms)]
    zs = _proj(u0s, p['w_in'], BF16, tm=MM_TM, tn=MM_TN)
    big = max(range(len(groups)), key=lambda g: flat[g].shape[0])
    resident = [p['p_a'], p['p_b'], p['w_o'], p['w_q'], p['w_xo']]
    mems = [m.reshape(-1, m.shape[-1]) for g in groups for m in g[2:4]]
    pre = []
    for g, ((x, pos0, _, _, s_a, h0, s_b, _), z) in enumerate(zip(groups, zs)):
        bsz, t, _ = x.shape
        pre.append(_mixer_pre(z.reshape(bsz, t, -1), s_a, h0[:, None, :], s_b,
                              p['conv_a_w'], p['conv_a_b'], p['w_ri'], p['b_r'], p['b_i'],
                              p['lru_lambda'], p['conv_b_w'], seq_tile=SEQ_TILE, pos0=pos0,
                              to_round=resident + mems if g == big else ()))
    p_a, p_b, w_o, w_q, w_xo, *mems = pre[big][5:]
    mids = []
    for g, ((x, *_), x2, z) in enumerate(zip(groups, flat, zs)):
        bsz, t, d = x.shape
        m = bsz * t
        ya, yb, ns_a, nh, ns_b = pre[g][:5]
        mem_k, mem_v = (mm.reshape(bsz, -1, d) for mm in mems[2 * g:2 * g + 2])
        x1 = _mixer_post(ya.reshape(m, -1), yb.reshape(m, -1), z, x2, p_a, p_b, w_o, p['g_mix_post'],
                         perms[g].T, tm=ROW_TILE)
        x2a, u3 = _xattn(x1.reshape(bsz, t, d), mem_k, mem_v, w_q, w_xo, p['g_x_pre'], p['g_x_post'],
                         p['g_ffn_pre'], perms[g], n_heads=N_XHEADS, seq_tile=ROW_TILE)
        mids.append((x2a, u3, ns_a, nh[:, 0, :], ns_b))
    ffn, (w_down,) = _up_geglu([u3 for _, u3, *_ in mids], p['w_up'], [g[7] for g in groups],
                               p['ffn_conv_w'], p['ffn_conv_b'], blocks, tm=UP_TM, tn=UP_TN,
                               to_round=[p['w_down']])
    x3s = _ffn_down([hid.reshape(-1, hid.shape[-1]) for hid, _, _ in ffn],
                    [x2a.reshape(-1, x2a.shape[-1]) for x2a, *_ in mids], w_down, p['g_ffn_post'],
                    [pm.T for pm in perms], tm=ROW_TILE)
    return [(x3.reshape(x2a.shape), ns_a, nh, ns_b, jnp.concatenate([ns_fg, ns_fv], axis=-1))
            for x3, (x2a, _, ns_a, nh, ns_b), (_, ns_fg, ns_fv) in zip(x3s, mids, ffn)]


def kernel(x_prompt, x_sample, mem_prompt, state_conv_a, state_rglru, state_conv_b, state_ffn_conv, cache_mem_k, cache_mem_v, g_mix_pre, g_mix_post, w_in, conv_a_w, conv_a_b, w_r, b_r, w_i, b_i, lru_lambda, conv_b_w, p_a, p_b, w_o, g_x_pre, g_x_post, g_mem, w_q, w_k, w_v, w_xo, g_ffn_pre, g_ffn_post, w_up, ffn_conv_w, ffn_conv_b, w_down):
    depth = w_in.shape[0]
    bsz, _, d = x_prompt.shape
    n_mem = mem_prompt.shape[1]
    yp, ys = x_prompt, x_sample
    outs = [[] for _ in range(10)]
    row = lambda v: v.reshape(1, -1).astype(F32)
    for l in range(depth):
        p = {'g_mix_pre': row(g_mix_pre[l]), 'g_mix_post': row(g_mix_post[l]), 'w_in': w_in[l],
             'conv_a_w': conv_a_w[l], 'conv_a_b': row(conv_a_b[l]),
             'w_ri': jnp.concatenate([w_r[l], w_i[l]], axis=-1).astype(BF16),
             'b_r': row(b_r[l]), 'b_i': row(b_i[l]), 'lru_lambda': row(lru_lambda[l]),
             'conv_b_w': conv_b_w[l], 'p_a': p_a[l], 'p_b': p_b[l],
             'w_o': w_o[l], 'g_x_pre': row(g_x_pre[l]), 'g_x_post': row(g_x_post[l]),
             'w_q': w_q[l], 'w_xo': w_xo[l],
             'g_ffn_pre': row(g_ffn_pre[l]), 'g_ffn_post': row(g_ffn_post[l]), 'w_up': w_up[l],
             'ffn_conv_w': ffn_conv_w[l], 'ffn_conv_b': row(ffn_conv_b[l]), 'w_down': w_down[l]}
        d_rnn, d_conv, d_up = conv_a_w.shape[2], conv_b_w.shape[2], ffn_conv_w.shape[2]
        mem_u = _rmsnorm_call(mem_prompt.reshape(bsz * n_mem, d), row(g_mem[l]), tm=NORM_TM)
        mk = _proj([mem_u], w_k[l], F32, tm=MM_TM, tn=MM_TN)[0].reshape(bsz, n_mem, d)
        mv = _proj([mem_u], w_v[l], F32, tm=MM_TM, tn=MM_TN)[0].reshape(bsz, n_mem, d)
        zeros = lambda *s: jnp.zeros(s, F32)
        prompt = (yp, 0, mk, mv,
                  zeros(bsz, conv_a_w.shape[1] - 1, d_rnn), zeros(bsz, d_rnn),
                  zeros(bsz, conv_b_w.shape[1] - 1, d_conv), zeros(bsz, ffn_conv_w.shape[1] - 1, d_up))
        dec_b = x_sample.shape[0]
        ck = cache_mem_k[l].reshape(dec_b, n_mem, d)
        cv = cache_mem_v[l].reshape(dec_b, n_mem, d)
        sample = (ys, PAST_LEN, ck, cv, state_conv_a[l], state_rglru[l], state_conv_b[l], state_ffn_conv[l])
        (ys, *s_states), (yp, *p_states) = _layer([sample, prompt], p)
        for o, v in zip(outs, (*p_states, mk.reshape(bsz, n_mem, N_XHEADS, -1),
                               mv.reshape(bsz, n_mem, N_XHEADS, -1), *s_states)):
            o.append(v)
    return (yp, ys) + tuple(jnp.stack(o) for o in outs)
```

```python
import functools

import jax
import jax.numpy as jnp
import numpy as np
from jax import lax
from jax.experimental import pallas as pl
from jax.experimental.pallas import tpu as pltpu

F32 = jnp.float32
BF16 = jnp.bfloat16

EPS = 1e-6
RG_C = 8.0
PAST_LEN = 2048

LANES = 128
SUBLANES = 8
BF16_ROWS = 16
VMEM_LIMIT_BYTES = 60 * 1024 * 1024

ROW_CHUNK = 128
CAST_ROWS = 256
SCAN_LANES = 2048


def _params(*semantics):
    return pltpu.CompilerParams(dimension_semantics=semantics, vmem_limit_bytes=VMEM_LIMIT_BYTES)


def _resident(shape):
    nd = len(shape)
    return pl.BlockSpec(shape, lambda *_: (0,) * nd, pipeline_mode=pl.Buffered(1))


def _rmsnorm(x, g):
    y = x * lax.rsqrt(jnp.mean(x * x, axis=-1, keepdims=True) + EPS)
    return y * g


def _sigmoid(x):
    return 0.5 * jnp.tanh(0.5 * x) + 0.5


def _gelu_tanh(x):
    c1 = (2.0 / jnp.pi) ** 0.5
    return x * (0.5 * jnp.tanh(x * (c1 + (c1 * 0.044715) * (x * x))) + 0.5)


def _causal_conv(x, before, w, b):
    nb, t, c = x.shape
    width = w.shape[0]
    groups = t // SUBLANES
    full = jnp.concatenate([before, x], axis=1).reshape(nb * (groups + 1), SUBLANES, c)
    row = lax.broadcasted_iota(jnp.int32, (1, 1, SUBLANES, c), 2)
    y = None
    for k in range(width):
        lag = width - 1 - k
        if lag == 0:
            tap = x
        else:
            rot = pltpu.roll(full, lag, 1).reshape(nb, groups + 1, SUBLANES, c)
            tap = jnp.where(row >= lag, rot[:, 1:], rot[:, :-1]).reshape(nb, t, c)
        y = tap * w[k:k + 1, :] if y is None else y + tap * w[k:k + 1, :]
    return y if b is None else y + b


def _segment_major_perm(n_seq, seq):
    n_groups = seq // SUBLANES
    row = np.arange(seq)
    time = (row % SUBLANES) * n_groups + row // SUBLANES
    p = np.zeros((seq, seq), np.float32)
    p[row, time] = 1.0
    return jnp.asarray(np.kron(np.eye(n_seq, dtype=np.float32), p), BF16)


def _seg_coords(n_groups, back):
    t = SUBLANES * n_groups - back
    return t % n_groups, t // n_groups


def _seg_hist_init(hist_ref, state_ref, n_groups):
    n_state = state_ref.shape[1]
    n_hist = hist_ref.shape[1] // SUBLANES
    hist_ref[...] = jnp.zeros(hist_ref.shape, F32)
    for back in range(1, n_state + 1):
        g, s = _seg_coords(n_groups, back)
        row = (g - (n_groups - n_hist)) * SUBLANES + s
        hist_ref[:, row:row + 1, :] = state_ref[:, n_state - back:n_state - back + 1, :]


def _seg_state_out(ns_ref, x_ref, cols=slice(None)):
    n_state = ns_ref.shape[1]
    n_groups = x_ref.shape[1] // SUBLANES
    for back in range(1, n_state + 1):
        g, s = _seg_coords(n_groups, back)
        row = g * SUBLANES + s
        ns_ref[:, n_state - back:n_state - back + 1, cols] = x_ref[:, row:row + 1, :]


def _seg_conv(x_ref, hist_ref, w_ref, bias):
    nb, seq, c = x_ref.shape
    n_groups = seq // SUBLANES
    n_hist = hist_ref.shape[1] // SUBLANES
    width = w_ref.shape[0]
    assert n_hist >= min(n_groups, width - 1)
    sub = lax.broadcasted_iota(jnp.int32, (1, SUBLANES, c), 1)
    y = None
    for k in range(width):
        lag = width - 1 - k
        if lag == 0:
            tap = x_ref[...]
        else:
            parts = []
            for g in range(min(lag, n_groups)):
                src = (g - lag) % n_groups
                crossed = (lag - g + n_groups - 1) // n_groups
                cur = x_ref[:, src * SUBLANES:(src + 1) * SUBLANES, :]
                h0 = (src - (n_groups - n_hist)) * SUBLANES
                prev = hist_ref[:, h0:h0 + SUBLANES, :]
                parts.append(pltpu.roll(jnp.where(sub >= SUBLANES - crossed, prev, cur), crossed, 1))
            if n_groups > lag:
                parts.append(x_ref[:, 0:(n_groups - lag) * SUBLANES, :])
            tap = jnp.concatenate(parts, axis=1)
        y = tap * w_ref[k:k + 1, :] if y is None else y + tap * w_ref[k:k + 1, :]
    return y if bias is None else y + bias


def _softplus(x):
    return jnp.maximum(x, 0.0) + jnp.log1p(jnp.exp(-jnp.abs(x)))


def _permute_rows(ref, perm_ref):
    blk = perm_ref.shape[0]
    for r in range(0, ref.shape[0], blk):
        ref[r:r + blk, :] = jnp.dot(perm_ref[...], ref[r:r + blk, :], preferred_element_type=F32).astype(BF16)


def _rmsnorm_kernel(x_ref, g_ref, *rest):
    u_ref = rest[-1]
    tm = x_ref.shape[0]
    chunk = min(ROW_CHUNK, tm)
    g = g_ref[...]

    def body(c, _):
        rows = pl.ds(pl.multiple_of(c * chunk, chunk), chunk)
        u_ref[rows, :] = _rmsnorm(x_ref[rows, :], g).astype(BF16)
        return None

    lax.fori_loop(0, tm // chunk, body, None)
    if len(rest) == 2:
        _permute_rows(u_ref, rest[0])


def _rmsnorm_call(x, g, *, tm, perm=None):
    m, d = x.shape
    tm = min(tm, m)
    assert m % tm == 0 and tm % min(ROW_CHUNK, tm) == 0 and (perm is None or tm % perm.shape[0] == 0)
    extra = [] if perm is None else [perm]
    return pl.pallas_call(
        _rmsnorm_kernel,
        grid=(m // tm,),
        in_specs=[pl.BlockSpec((tm, d), lambda i: (i, 0)), pl.BlockSpec((1, d), lambda i: (0, 0))]
        + [pl.BlockSpec(a.shape, lambda i: (0, 0)) for a in extra],
        out_specs=pl.BlockSpec((tm, d), lambda i: (i, 0)),
        out_shape=jax.ShapeDtypeStruct((m, d), BF16),
        compiler_params=_params("parallel"),
        name="rmsnorm",
    )(x, g, *extra)


def _cast_rows(src_ref, dst_ref, cols=slice(None)):
    rows_total = src_ref.shape[0]
    chunk = min(CAST_ROWS, rows_total)

    def body(c, _):
        rows = pl.ds(pl.multiple_of(c * chunk, chunk), chunk)
        dst_ref[rows, cols] = src_ref[rows, :].astype(BF16)
        return None

    lax.fori_loop(0, rows_total // chunk, body, None)


def _side_cast_plan(arrays, n_steps, flat_step):
    in_specs, out_specs, out_shape = [], [], []
    for a in arrays:
        rows, cols = a.shape
        chunk = next(c for c in range(BF16_ROWS, rows + 1, BF16_ROWS) if rows % c == 0 and rows // c <= n_steps)
        last = rows // chunk - 1
        spec = pl.BlockSpec((chunk, cols), lambda *idx, last=last: (jnp.minimum(flat_step(*idx), last), 0))
        in_specs.append(spec)
        out_specs.append(spec)
        out_shape.append(jax.ShapeDtypeStruct((rows, cols), BF16))
    return in_specs, out_specs, out_shape


def _side_cast(src_refs, dst_refs):
    for src, dst in zip(src_refs, dst_refs):
        dst[...] = src[...].astype(BF16)


def _group_steps(counts):
    starts = [sum(counts[:g]) for g in range(len(counts))]
    return starts, sum(counts)


def _group_tile(i, start, count):
    return jnp.clip(i - start, 0, count - 1)


def _proj_kernel(*refs, starts, counts):
    n_groups = len(counts)
    u_refs, w_ref, o_refs, w_scr = refs[:n_groups], refs[n_groups], refs[n_groups + 1:-1], refs[-1]
    i = pl.program_id(1)

    @pl.when(i == 0)
    def _():
        _cast_rows(w_ref, w_scr)

    for g in range(n_groups):
        @pl.when((i >= starts[g]) & (i < starts[g] + counts[g]))
        def _(g=g):
            o_refs[g][...] = jnp.dot(u_refs[g][...], w_scr[...],
                                     preferred_element_type=F32).astype(o_refs[g].dtype)


def _proj(us, w, out_dtype, *, tm, tn):
    d, n = w.shape
    tn = min(tn, n)
    tms = [min(tm, u.shape[0]) for u in us]
    counts = [u.shape[0] // t for u, t in zip(us, tms)]
    assert all(u.shape[0] % t == 0 for u, t in zip(us, tms)) and n % tn == 0 and d % min(CAST_ROWS, d) == 0
    starts, steps = _group_steps(counts)
    tile = lambda g: (lambda j, i: (_group_tile(i, starts[g], counts[g]), 0))
    out_tile = lambda g: (lambda j, i: (_group_tile(i, starts[g], counts[g]), j))
    return pl.pallas_call(
        functools.partial(_proj_kernel, starts=starts, counts=counts),
        grid=(n // tn, steps),
        in_specs=([pl.BlockSpec((tms[g], d), tile(g)) for g in range(len(us))]
                  + [pl.BlockSpec((d, tn), lambda j, i: (0, j))]),
        out_specs=[pl.BlockSpec((tms[g], tn), out_tile(g)) for g in range(len(us))],
        out_shape=[jax.ShapeDtypeStruct((u.shape[0], n), out_dtype) for u in us],
        scratch_shapes=[pltpu.VMEM((d, tn), BF16)],
        compiler_params=_params("arbitrary", "arbitrary"),
        name="proj",
    )(*us, w)


def _mixer_pre_kernel(*refs, d_rnn, d_conv, pos0, n_side):
    (z_ref, sa_ref, h0_ref, sb_ref, caw_ref, cab_ref, wri_ref, br_ref, bi_ref, lam_ref, cbw_ref) = refs[:11]
    side_in = refs[11:11 + n_side]
    ya_ref, yb_ref, nsa_ref, nh_ref, nsb_ref = refs[11 + n_side:16 + n_side]
    side_out = refs[16 + n_side:16 + 2 * n_side]
    xa_scr, ha_scr, xc_scr, xcb_scr, ri_scr, cb_scr, hb_scr, h_scr = refs[16 + 2 * n_side:]
    _side_cast(side_in, side_out)
    t = pl.program_id(1)
    nb, seq, _ = ya_ref.shape
    n_groups = seq // SUBLANES
    n_heads, head_dim, _ = wri_ref.shape

    @pl.when(t == 0)
    def _():
        _seg_hist_init(ha_scr, sa_ref, n_groups)
        _seg_hist_init(hb_scr, sb_ref, n_groups)
        h_scr[...] = jnp.broadcast_to(h0_ref[...], h_scr.shape)

    xa_scr[...] = z_ref[:, :, 0:d_rnn].astype(F32)
    xc = _seg_conv(xa_scr, ha_scr, caw_ref, cab_ref[...])
    xc_scr[...] = xc
    xcb_scr[...] = xc.reshape(nb * seq, d_rnn).astype(BF16)
    _seg_state_out(nsa_ref, xa_scr)
    ha_scr[...] = xa_scr[:, seq - ha_scr.shape[1]:, :]

    for h in range(n_heads):
        cols = slice(h * head_dim, (h + 1) * head_dim)
        ri = jnp.dot(xcb_scr[:, cols], wri_ref[h], preferred_element_type=F32)
        ri_scr[:, :, cols] = ri[:, :head_dim].reshape(nb, seq, head_dim)
        ri_scr[:, :, d_rnn + h * head_dim:d_rnn + (h + 1) * head_dim] = ri[:, head_dim:].reshape(nb, seq, head_dim)

    strip = max(LANES, SCAN_LANES // nb)
    pairs = seq // BF16_ROWS
    sub = lax.broadcasted_iota(jnp.int32, (1, SUBLANES, strip), 1)
    for c in range(d_rnn // strip):
        cs = slice(c * strip, (c + 1) * strip)
        cs_i = slice(d_rnn + c * strip, d_rnn + (c + 1) * strip)
        cs_g = slice(d_rnn + c * strip, d_rnn + (c + 1) * strip)
        sp = _softplus(-lam_ref[:, cs])
        b_r = br_ref[:, cs]
        b_i = bi_ref[:, cs]

        def pass1(it, carry, first=False, cs=cs, cs_i=cs_i, sp=sp, b_r=b_r, b_i=b_i):
            h_loc, a_run = carry
            for half in range(BF16_ROWS // SUBLANES):
                rows = pl.ds(pl.multiple_of(it * BF16_ROWS + half * SUBLANES, SUBLANES), SUBLANES)
                r = _sigmoid(ri_scr[:, rows, cs] + b_r)
                i = _sigmoid(ri_scr[:, rows, cs_i] + b_i)
                log_a = -RG_C * r * sp
                a = jnp.exp(log_a)
                mult = jnp.sqrt(-jnp.tanh(log_a) * (a * a + 1.0))
                if first and half == 0:
                    mult = jnp.where(jnp.logical_and(t == 0, sub == 0), 1.0, mult)
                b = mult * i * xc_scr[:, rows, cs]
                h_loc = a * h_loc + b
                a_run = a * a_run
                ri_scr[:, rows, cs] = h_loc
                ri_scr[:, rows, cs_i] = a_run
            return h_loc, a_run

        carry = (jnp.zeros((nb, SUBLANES, strip), F32), jnp.ones((nb, SUBLANES, strip), F32))
        start = 0
        if pos0 == 0:
            carry = pass1(0, carry, first=True)
            start = 1
        h_tot, a_tot = lax.fori_loop(start, pairs, pass1, carry)

        for s in (1, 2, 4):
            keep = sub >= s
            a_prev = pltpu.roll(a_tot, s, 1)
            h_prev = pltpu.roll(h_tot, s, 1)
            h_tot = jnp.where(keep, a_tot * h_prev + h_tot, h_tot)
            a_tot = jnp.where(keep, a_tot * a_prev, a_tot)
        h_in = h_scr[:, :, cs]
        seg_end = a_tot * h_in + h_tot
        seg_start = jnp.where(sub == 0, h_in, pltpu.roll(seg_end, 1, 1))
        h_last = jnp.broadcast_to(seg_end[:, SUBLANES - 1:SUBLANES, :], seg_end.shape)
        h_scr[:, :, cs] = h_last
        nh_ref[:, :, cs] = h_last[:, 0:1, :]
        start2 = jnp.concatenate([seg_start] * (BF16_ROWS // SUBLANES), axis=1)

        def pass2(it, _, cs=cs, cs_i=cs_i, cs_g=cs_g, start2=start2):
            rows = pl.ds(pl.multiple_of(it * BF16_ROWS, BF16_ROWS), BF16_ROWS)
            h = ri_scr[:, rows, cs] + ri_scr[:, rows, cs_i] * start2
            gate = _gelu_tanh(z_ref[:, rows, cs_g].astype(F32))
            ya_ref[:, rows, cs] = (h * gate).astype(BF16)
            return None

        lax.fori_loop(0, pairs, pass2, None)

    o_gb, o_gc, o_hb = 2 * d_rnn, 2 * d_rnn + d_conv, 2 * d_rnn + 2 * d_conv
    cb_scr[...] = z_ref[:, :, o_gc:o_gc + d_conv].astype(F32) * z_ref[:, :, o_hb:o_hb + d_conv].astype(F32)
    co = _seg_conv(cb_scr, hb_scr, cbw_ref, None)
    yb_ref[...] = (z_ref[:, :, o_gb:o_gb + d_conv].astype(F32) * co).astype(BF16)
    _seg_state_out(nsb_ref, cb_scr)
    hb_scr[...] = cb_scr[:, seq - hb_scr.shape[1]:, :]


def _mixer_pre(z, state_a, h0, state_b, conv_a_w, conv_a_b, w_ri, b_r, b_i, lam, conv_b_w, *, seq_tile, pos0,
               to_round=()):
    bsz, t, _ = z.shape
    d_rnn = conv_a_w.shape[1]
    d_conv = conv_b_w.shape[1]
    n_pre = 2 * d_rnn + 3 * d_conv
    nb = bsz if t <= seq_tile else 1
    seq = min(seq_tile, t)
    assert t % seq == 0 and seq % BF16_ROWS == 0 and bsz % nb == 0
    wa, wb = conv_a_w.shape[0], conv_b_w.shape[0]
    hist_a, hist_b = min(seq // SUBLANES, wa - 1), min(seq // SUBLANES, wb - 1)
    kern = functools.partial(_mixer_pre_kernel, d_rnn=d_rnn, d_conv=d_conv, pos0=pos0, n_side=len(to_round))
    n_t = t // seq
    side_in, side_out, side_shape = _side_cast_plan(to_round, (bsz // nb) * n_t, lambda b, i: b * n_t + i)
    row = lambda b, i: (b, i, 0)
    per_b = lambda b, i: (b, 0, 0)
    const2 = lambda b, i: (0, 0)
    return pl.pallas_call(
        kern,
        grid=(bsz // nb, t // seq),
        in_specs=[pl.BlockSpec((nb, seq, n_pre), row),
                  pl.BlockSpec((nb, wa - 1, d_rnn), per_b),
                  pl.BlockSpec((nb, 1, d_rnn), per_b),
                  pl.BlockSpec((nb, wb - 1, d_conv), per_b),
                  pl.BlockSpec(conv_a_w.shape, const2),
                  pl.BlockSpec((1, d_rnn), const2),
                  pl.BlockSpec(w_ri.shape, lambda b, i: (0, 0, 0)),
                  pl.BlockSpec((1, d_rnn), const2),
                  pl.BlockSpec((1, d_rnn), const2),
                  pl.BlockSpec((1, d_rnn), const2),
                  pl.BlockSpec(conv_b_w.shape, const2)] + side_in,
        out_specs=[pl.BlockSpec((nb, seq, d_rnn), row),
                   pl.BlockSpec((nb, seq, d_conv), row),
                   pl.BlockSpec((nb, wa - 1, d_rnn), per_b),
                   pl.BlockSpec((nb, 1, d_rnn), per_b),
                   pl.BlockSpec((nb, wb - 1, d_conv), per_b)] + side_out,
        out_shape=[jax.ShapeDtypeStruct((bsz, t, d_rnn), BF16),
                   jax.ShapeDtypeStruct((bsz, t, d_conv), BF16),
                   jax.ShapeDtypeStruct((bsz, wa - 1, d_rnn), F32),
                   jax.ShapeDtypeStruct((bsz, 1, d_rnn), F32),
                   jax.ShapeDtypeStruct((bsz, wb - 1, d_conv), F32)] + side_shape,
        scratch_shapes=[pltpu.VMEM((nb, seq, d_rnn), F32),
                        pltpu.VMEM((nb, hist_a * SUBLANES, d_rnn), F32),
                        pltpu.VMEM((nb, seq, d_rnn), F32),
                        pltpu.VMEM((nb * seq, d_rnn), BF16),
                        pltpu.VMEM((nb, seq, 2 * d_rnn), F32),
                        pltpu.VMEM((nb, seq, d_conv), F32),
                        pltpu.VMEM((nb, hist_b * SUBLANES, d_conv), F32),
                        pltpu.VMEM((nb, SUBLANES, d_rnn), F32)],
        compiler_params=_params("arbitrary", "arbitrary"),
        name="mixer_pre",
    )(z, state_a, h0, state_b, conv_a_w, conv_a_b, w_ri, b_r, b_i, lam, conv_b_w, *to_round)


def _mixer_post_kernel(ya_ref, yb_ref, ga0_ref, ga1_ref, gb0_ref, gb1_ref, x_ref, pa_ref, pb_ref, wo_ref, g_ref,
                       unperm_ref, o_ref, mix_scr):
    half = ga0_ref.shape[1]
    y_a = jnp.dot(ya_ref[...], pa_ref[...], preferred_element_type=F32)
    y_b = jnp.dot(yb_ref[...], pb_ref[...], preferred_element_type=F32)
    for c, (ga_ref, gb_ref) in enumerate(((ga0_ref, gb0_ref), (ga1_ref, gb1_ref))):
        cols = slice(c * half, (c + 1) * half)
        mix = (jax.nn.sigmoid(ga_ref[...].astype(F32)) * y_a[:, cols]
               + jax.nn.sigmoid(gb_ref[...].astype(F32)) * y_b[:, cols])
        mix_scr[:, cols] = mix.astype(BF16)
    _permute_rows(mix_scr, unperm_ref)
    out = jnp.dot(mix_scr[...], wo_ref[...], preferred_element_type=F32)
    o_ref[...] = x_ref[...] + _rmsnorm(out, g_ref[...])


def _mixer_post(ya, yb, z, x, p_a, p_b, w_o, g_post, unperm, *, tm):
    m, d = x.shape
    d_rnn, d_conv = ya.shape[1], yb.shape[1]
    tm = min(tm, m)
    half = d // 2
    gate0 = (2 * d_rnn + 3 * d_conv) // half
    assert m % tm == 0 and (2 * d_rnn + 3 * d_conv) % half == 0 and tm % unperm.shape[0] == 0
    gate_spec = lambda k: pl.BlockSpec((tm, half), lambda i: (i, gate0 + k))
    row = lambda i: (i, 0)
    return pl.pallas_call(
        _mixer_post_kernel,
        grid=(m // tm,),
        in_specs=[pl.BlockSpec((tm, d_rnn), row), pl.BlockSpec((tm, d_conv), row),
                  gate_spec(0), gate_spec(1), gate_spec(2), gate_spec(3),
                  pl.BlockSpec((tm, d), row),
                  _resident(p_a.shape), _resident(p_b.shape), _resident(w_o.shape),
                  pl.BlockSpec((1, d), lambda i: (0, 0)), pl.BlockSpec(unperm.shape, lambda i: (0, 0))],
        out_specs=pl.BlockSpec((tm, d), row),
        out_shape=jax.ShapeDtypeStruct((m, d), F32),
        scratch_shapes=[pltpu.VMEM((tm, d), BF16)],
        compiler_params=_params("parallel"),
        name="mixer_post",
    )(ya, yb, z, z, z, z, x, p_a, p_b, w_o, g_post, unperm)


def _xattn_kernel(x_ref, k_ref, v_ref, wq_ref, wxo_ref, gpre_ref, gpost_ref, gnext_ref, perm_ref, o_ref, u_ref,
                  q_scr, o_scr, *, n_heads):
    nb, seq, d = x_ref.shape
    hd = d // n_heads
    scale = hd ** -0.5
    x = x_ref[...].reshape(nb * seq, d)
    u = _rmsnorm(x, gpre_ref[...]).astype(BF16)
    q_scr[...] = jnp.dot(u, wq_ref[...], preferred_element_type=F32).astype(BF16)
    for b in range(nb):
        rows = slice(b * seq, (b + 1) * seq)
        for h in range(n_heads):
            cols = slice(h * hd, (h + 1) * hd)
            s = lax.dot_general(q_scr[rows, cols], k_ref[b, :, cols], (((1,), (1,)), ((), ())),
                                preferred_element_type=F32) * scale
            p = jnp.exp(s - jnp.max(s, axis=-1, keepdims=True))
            att = (p / jnp.sum(p, axis=-1, keepdims=True)).astype(BF16)
            o_scr[rows, cols] = jnp.dot(att, v_ref[b, :, cols], preferred_element_type=F32).astype(BF16)
    out = jnp.dot(o_scr[...], wxo_ref[...], preferred_element_type=F32)
    x_new = x + _rmsnorm(out, gpost_ref[...])
    o_ref[...] = x_new.reshape(nb, seq, d)
    u_next = _rmsnorm(x_new, gnext_ref[...]).astype(BF16)
    blk = perm_ref.shape[0]
    u_next = jnp.concatenate([jnp.dot(perm_ref[...], u_next[r:r + blk, :], preferred_element_type=F32)
                              for r in range(0, nb * seq, blk)], axis=0)
    u_ref[...] = u_next.astype(BF16).reshape(nb, seq, d)


def _xattn(x, mem_k, mem_v, w_q, w_xo, g_pre, g_post, g_next, perm, *, n_heads, seq_tile):
    bsz, t, d = x.shape
    n_mem = mem_k.shape[1]
    nb = bsz if t <= seq_tile else 1
    seq = min(seq_tile, t)
    assert t % seq == 0 and seq % BF16_ROWS == 0 and bsz % nb == 0 and (nb * seq) % perm.shape[0] == 0
    single = bsz // nb == 1
    mem_spec = (_resident((nb, n_mem, d)) if single else pl.BlockSpec((nb, n_mem, d), lambda b, i: (b, 0, 0)))
    row = lambda b, i: (b, i, 0)
    gain = pl.BlockSpec((1, d), lambda b, i: (0, 0))
    return pl.pallas_call(
        functools.partial(_xattn_kernel, n_heads=n_heads),
        grid=(bsz // nb, t // seq),
        in_specs=[pl.BlockSpec((nb, seq, d), row), mem_spec, mem_spec,
                  _resident(w_q.shape), _resident(w_xo.shape), gain, gain, gain,
                  pl.BlockSpec(perm.shape, lambda b, i: (0, 0))],
        out_specs=[pl.BlockSpec((nb, seq, d), row), pl.BlockSpec((nb, seq, d), row)],
        out_shape=[jax.ShapeDtypeStruct((bsz, t, d), F32), jax.ShapeDtypeStruct((bsz, t, d), BF16)],
        scratch_shapes=[pltpu.VMEM((nb * seq, d), BF16), pltpu.VMEM((nb * seq, d), BF16)],
        compiler_params=_params("parallel", "parallel"),
        name="xattn",
    )(x, mem_k, mem_v, w_q, w_xo, g_pre, g_post, g_next, perm)


def _up_geglu_kernel(*refs, starts, counts, blocks, n_side):
    n_groups = len(counts)
    ins, rest = refs[:3 * n_groups], refs[3 * n_groups:]
    wg_ref, wv_ref, cwg_ref, cwv_ref, cbg_ref, cbv_ref = rest[:6]
    side_in, rest = rest[6:6 + n_side], rest[6 + n_side:]
    outs, rest = rest[:3 * n_groups], rest[3 * n_groups:]
    side_out, scr = rest[:n_side], rest[n_side:]
    w_scr, hists = scr[0], scr[1:]
    _side_cast(side_in, side_out)
    i = pl.program_id(1)
    tn = wg_ref.shape[1]

    @pl.when(i == 0)
    def _():
        _cast_rows(wg_ref, w_scr, slice(0, tn))
        _cast_rows(wv_ref, w_scr, slice(tn, 2 * tn))

    for g in range(n_groups):
        u_ref, st_refs = ins[3 * g], ins[3 * g + 1:3 * g + 3]
        hid_ref, ns_refs = outs[3 * g], outs[3 * g + 1:3 * g + 3]
        p_scrs = hists[2 * g:2 * g + 2]

        @pl.when(i == starts[g])
        def _(st_refs=st_refs, p_scrs=p_scrs, blk=blocks[g]):
            for st_ref, p_scr in zip(st_refs, p_scrs):
                _seg_hist_init(p_scr, st_ref, blk // SUBLANES)

        @pl.when((i >= starts[g]) & (i < starts[g] + counts[g]))
        def _(u_ref=u_ref, hid_ref=hid_ref, ns_refs=ns_refs, p_scrs=p_scrs, blk=blocks[g]):
            nb, seq, k_dim = u_ref.shape
            per_seq = seq // blk
            n_hist = p_scrs[0].shape[1]
            both = jnp.dot(u_ref[...].reshape(nb * seq, k_dim), w_scr[...], preferred_element_type=F32)
            ys = []
            for half, (p_scr, cw_ref, cb_ref, ns_ref) in enumerate(zip(p_scrs, (cwg_ref, cwv_ref),
                                                                     (cbg_ref, cbv_ref), ns_refs)):
                a = both[:, half * tn:(half + 1) * tn].reshape(nb * per_seq, blk, tn)
                tails = a[:, blk - n_hist:, :].reshape(nb, per_seq, n_hist, tn)
                hist = p_scr[...][:, None]
                if per_seq > 1:
                    hist = jnp.concatenate([hist, tails[:, :per_seq - 1]], axis=1)
                hist = hist.reshape(nb * per_seq, n_hist, tn)
                ys.append(_seg_conv(a, hist, cw_ref, cb_ref[...]).reshape(nb, seq, tn))
                p_scr[...] = tails[:, per_seq - 1]
                _seg_state_out(ns_ref, a.reshape(nb, per_seq, blk, tn)[:, per_seq - 1])
            hid_ref[...] = (_gelu_tanh(ys[0]) * ys[1]).astype(BF16)


def _up_geglu(us, w_up, states, conv_w, conv_b, blocks, *, tm, tn, to_round=()):
    d = w_up.shape[0]
    d_ff = w_up.shape[1] // 2
    width = conv_w.shape[0]
    n_j = d_ff // tn
    nbs = [u.shape[0] if u.shape[1] <= tm else 1 for u in us]
    seqs = [min(tm, u.shape[1]) for u in us]
    counts = [u.shape[1] // s for u, s in zip(us, seqs)]
    assert all(u.shape[0] == nb and u.shape[1] % s == 0 and s % BF16_ROWS == 0 for u, nb, s in zip(us, nbs, seqs))
    assert d_ff % tn == 0 and tn % LANES == 0
    starts, steps = _group_steps(counts)
    col_g = lambda j, i: (0, j)
    col_v = lambda j, i: (0, n_j + j)
    st_g = lambda j, i: (0, 0, j)
    st_v = lambda j, i: (0, 0, n_j + j)
    in_specs, out_specs, out_shape, hist, args = [], [], [], [], []
    for g, (u, st, nb, seq) in enumerate(zip(us, states, nbs, seqs)):
        row = lambda j, i, g=g: (0, _group_tile(i, starts[g], counts[g]), 0)
        out = lambda j, i, g=g: (0, _group_tile(i, starts[g], counts[g]), j)
        in_specs += [pl.BlockSpec((nb, seq, d), row),
                     pl.BlockSpec((nb, width - 1, tn), st_g), pl.BlockSpec((nb, width - 1, tn), st_v)]
        out_specs += [pl.BlockSpec((nb, seq, tn), out),
                      pl.BlockSpec((nb, width - 1, tn), st_g), pl.BlockSpec((nb, width - 1, tn), st_g)]
        out_shape += [jax.ShapeDtypeStruct((u.shape[0], u.shape[1], d_ff), BF16),
                      jax.ShapeDtypeStruct((u.shape[0], width - 1, d_ff), F32),
                      jax.ShapeDtypeStruct((u.shape[0], width - 1, d_ff), F32)]
        assert seq % blocks[g] == 0 and blocks[g] % SUBLANES == 0
        n_hist = min(blocks[g] // SUBLANES, width - 1) * SUBLANES
        hist += [pltpu.VMEM((nb, n_hist, tn), F32), pltpu.VMEM((nb, n_hist, tn), F32)]
        args += [u, st, st]
    side_in, side_out, side_shape = _side_cast_plan(to_round, n_j * steps, lambda j, i: j * steps + i)
    in_specs += [pl.BlockSpec((d, tn), col_g), pl.BlockSpec((d, tn), col_v),
                 pl.BlockSpec((width, tn), col_g), pl.BlockSpec((width, tn), col_v),
                 pl.BlockSpec((1, tn), col_g), pl.BlockSpec((1, tn), col_v)] + side_in
    res = pl.pallas_call(
        functools.partial(_up_geglu_kernel, starts=starts, counts=counts, blocks=tuple(blocks),
                          n_side=len(to_round)),
        grid=(n_j, steps),
        in_specs=in_specs,
        out_specs=out_specs + side_out,
        out_shape=out_shape + side_shape,
        scratch_shapes=[pltpu.VMEM((d, 2 * tn), BF16)] + hist,
        compiler_params=_params("arbitrary", "arbitrary"),
        name="up_geglu",
    )(*args, w_up, w_up, conv_w, conv_w, conv_b, conv_b, *to_round)
    return [tuple(res[3 * g:3 * g + 3]) for g in range(len(us))], list(res[3 * len(us):])


def _ffn_down_kernel(*refs, starts, counts):
    n_groups = len(counts)
    wd_ref, g_ref = refs[3 * n_groups:3 * n_groups + 2]
    o_refs = refs[3 * n_groups + 2:]
    i = pl.program_id(0)
    for g in range(n_groups):
        hid_ref, x_ref, unperm_ref = refs[3 * g:3 * g + 3]

        @pl.when((i >= starts[g]) & (i < starts[g] + counts[g]))
        def _(hid_ref=hid_ref, x_ref=x_ref, unperm_ref=unperm_ref, o_ref=o_refs[g]):
            y = jnp.dot(hid_ref[...], wd_ref[...], preferred_element_type=F32)
            branch = _rmsnorm(y, g_ref[...]).astype(BF16)
            blk = unperm_ref.shape[0]
            for r in range(0, branch.shape[0], blk):
                o_ref[r:r + blk, :] = x_ref[r:r + blk, :] + jnp.dot(unperm_ref[...], branch[r:r + blk, :],
                                                                    preferred_element_type=F32)


def _ffn_down(hids, xs, w_down, g_post, unperms, *, tm):
    d_ff, d = w_down.shape
    tms = [min(tm, x.shape[0]) for x in xs]
    counts = [x.shape[0] // t for x, t in zip(xs, tms)]
    assert all(x.shape[0] % t == 0 and t % un.shape[0] == 0 for x, t, un in zip(xs, tms, unperms))
    starts, steps = _group_steps(counts)
    in_specs, args = [], []
    for g, (hid, x, un) in enumerate(zip(hids, xs, unperms)):
        row = lambda i, g=g: (_group_tile(i, starts[g], counts[g]), 0)
        in_specs += [pl.BlockSpec((tms[g], d_ff), row), pl.BlockSpec((tms[g], d), row),
                     pl.BlockSpec(un.shape, lambda i: (0, 0))]
        args += [hid, x, un]
    return pl.pallas_call(
        functools.partial(_ffn_down_kernel, starts=starts, counts=counts),
        grid=(steps,),
        in_specs=in_specs + [_resident(w_down.shape), pl.BlockSpec((1, d), lambda i: (0, 0))],
        out_specs=[pl.BlockSpec((tms[g], d), lambda i, g=g: (_group_tile(i, starts[g], counts[g]), 0))
                   for g in range(len(xs))],
        out_shape=[jax.ShapeDtypeStruct(x.shape, F32) for x in xs],
        compiler_params=_params("arbitrary"),
        name="ffn_down",
    )(*args, w_down, g_post)


MM_TM, MM_TN = 2048, 1024
UP_TM, UP_TN = 1024, 512
SEQ_TILE = 256
ROW_TILE = 512
NORM_TM = 1024
N_XHEADS = 4


def _layer(groups, p):
    flat = [x.reshape(-1, x.shape[-1]) for x, *_ in groups]
    perms, blocks = [], []
    for x, *_ in groups:
        bsz, t, _ = x.shape
        perms.append(_segment_major_perm(*((1, SEQ_TILE) if t > SEQ_TILE else (bsz, t))))
        blocks.append(min(t, SEQ_TILE))
    u0s = [_rmsnorm_call(x2, p['g_mix_pre'], tm=NORM_TM, perm=pm) for x2, pm in zip(flat, perms)]
    zs = _proj(u0s, p['w_in'], BF16, tm=MM_TM, tn=MM_TN)
    big = max(range(len(groups)), key=lambda g: flat[g].shape[0])
    resident = [p['p_a'], p['p_b'], p['w_o'], p['w_q'], p['w_xo']]
    mems = [m.reshape(-1, m.shape[-1]) for g in groups for m in g[2:4]]
    pre = []
    for g, ((x, pos0, _, _, s_a, h0, s_b, _), z) in enumerate(zip(groups, zs)):
        bsz, t, _ = x.shape
        pre.append(_mixer_pre(z.reshape(bsz, t, -1), s_a, h0[:, None, :], s_b,
                              p['conv_a_w'], p['conv_a_b'], p['w_ri'], p['b_r'], p['b_i'],
                              p['lru_lambda'], p['conv_b_w'], seq_tile=SEQ_TILE, pos0=pos0,
                              to_round=resident + mems if g == big else ()))
    p_a, p_b, w_o, w_q, w_xo, *mems = pre[big][5:]
    mids = []
    for g, ((x, *_), x2, z) in enumerate(zip(groups, flat, zs)):
        bsz, t, d = x.shape
        m = bsz * t
        ya, yb, ns_a, nh, ns_b = pre[g][:5]
        mem_k, mem_v = (mm.reshape(bsz, -1, d) for mm in mems[2 * g:2 * g + 2])
        x1 = _mixer_post(ya.reshape(m, -1), yb.reshape(m, -1), z, x2, p_a, p_b, w_o, p['g_mix_post'],
                         perms[g].T, tm=ROW_TILE)
        x2a, u3 = _xattn(x1.reshape(bsz, t, d), mem_k, mem_v, w_q, w_xo, p['g_x_pre'], p['g_x_post'],
                         p['g_ffn_pre'], perms[g], n_heads=N_XHEADS, seq_tile=ROW_TILE)
        mids.append((x2a, u3, ns_a, nh[:, 0, :], ns_b))
    ffn, (w_down,) = _up_geglu([u3 for _, u3, *_ in mids], p['w_up'], [g[7] for g in groups],
                               p['ffn_conv_w'], p['ffn_conv_b'], blocks, tm=UP_TM, tn=UP_TN,
                               to_round=[p['w_down']])
    x3s = _ffn_down([hid.reshape(-1, hid.shape[-1]) for hid, _, _ in ffn],
                    [x2a.reshape(-1, x2a.shape[-1]) for x2a, *_ in mids], w_down, p['g_ffn_post'],
                    [pm.T for pm in perms], tm=ROW_TILE)
    return [(x3.reshape(x2a.shape), ns_a, nh, ns_b, jnp.concatenate([ns_fg, ns_fv], axis=-1))
            for x3, (x2a, _, ns_a, nh, ns_b), (_, ns_fg, ns_fv) in zip(x3s, mids, ffn)]


def kernel(x_prompt, x_sample, mem_prompt, state_conv_a, state_rglru, state_conv_b, state_ffn_conv, cache_mem_k, cache_mem_v, g_mix_pre, g_mix_post, w_in, conv_a_w, conv_a_b, w_r, b_r, w_i, b_i, lru_lambda, conv_b_w, p_a, p_b, w_o, g_x_pre, g_x_post, g_mem, w_q, w_k, w_v, w_xo, g_ffn_pre, g_ffn_post, w_up, ffn_conv_w, ffn_conv_b, w_down):
    depth = w_in.shape[0]
    bsz, _, d = x_prompt.shape
    n_mem = mem_prompt.shape[1]
    yp, ys = x_prompt, x_sample
    outs = [[] for _ in range(10)]
    row = lambda v: v.reshape(1, -1).astype(F32)
    for l in range(depth):
        p = {'g_mix_pre': row(g_mix_pre[l]), 'g_mix_post': row(g_mix_post[l]), 'w_in': w_in[l],
             'conv_a_w': conv_a_w[l], 'conv_a_b': row(conv_a_b[l]),
             'w_ri': jnp.concatenate([w_r[l], w_i[l]], axis=-1).astype(BF16),
             'b_r': row(b_r[l]), 'b_i': row(b_i[l]), 'lru_lambda': row(lru_lambda[l]),
             'conv_b_w': conv_b_w[l], 'p_a': p_a[l], 'p_b': p_b[l],
             'w_o': w_o[l], 'g_x_pre': row(g_x_pre[l]), 'g_x_post': row(g_x_post[l]),
             'w_q': w_q[l], 'w_xo': w_xo[l],
             'g_ffn_pre': row(g_ffn_pre[l]), 'g_ffn_post': row(g_ffn_post[l]), 'w_up': w_up[l],
             'ffn_conv_w': ffn_conv_w[l], 'ffn_conv_b': row(ffn_conv_b[l]), 'w_down': w_down[l]}
        d_rnn, d_conv, d_up = conv_a_w.shape[2], conv_b_w.shape[2], ffn_conv_w.shape[2]
        mem_u = _rmsnorm_call(mem_prompt.reshape(bsz * n_mem, d), row(g_mem[l]), tm=NORM_TM)
        mk = _proj([mem_u], w_k[l], F32, tm=MM_TM, tn=MM_TN)[0].reshape(bsz, n_mem, d)
        mv = _proj([mem_u], w_v[l], F32, tm=MM_TM, tn=MM_TN)[0].reshape(bsz, n_mem, d)
        zeros = lambda *s: jnp.zeros(s, F32)
        prompt = (yp, 0, mk, mv,
                  zeros(bsz, conv_a_w.shape[1] - 1, d_rnn), zeros(bsz, d_rnn),
                  zeros(bsz, conv_b_w.shape[1] - 1, d_conv), zeros(bsz, ffn_conv_w.shape[1] - 1, d_up))
        dec_b = x_sample.shape[0]
        ck = cache_mem_k[l].reshape(dec_b, n_mem, d)
        cv = cache_mem_v[l].reshape(dec_b, n_mem, d)
        sample = (ys, PAST_LEN, ck, cv, state_conv_a[l], state_rglru[l], state_conv_b[l], state_ffn_conv[l])
        (ys, *s_states), (yp, *p_states) = _layer([sample, prompt], p)
        for o, v in zip(outs, (*p_states, mk.reshape(bsz, n_mem, N_XHEADS, -1),
                               mv.reshape(bsz, n_mem, N_XHEADS, -1), *s_states)):
            o.append(v)
    return (yp, ys) + tuple(jnp.stack(o) for o in outs)
```

```python
import functools

import jax
import jax.numpy as jnp
import numpy as np
from jax import lax
from jax.experimental import pallas as pl
from jax.experimental.pallas import tpu as pltpu

F32 = jnp.float32
BF16 = jnp.bfloat16

EPS = 1e-6
RG_C = 8.0
PAST_LEN = 2048

LANES = 128
SUBLANES = 8
BF16_ROWS = 16
VMEM_LIMIT_BYTES = 60 * 1024 * 1024

ROW_CHUNK = 128
CAST_ROWS = 256
SCAN_LANES = 2048


def _params(*semantics):
    return pltpu.CompilerParams(dimension_semantics=semantics, vmem_limit_bytes=VMEM_LIMIT_BYTES)


def _resident(shape):
    nd = len(shape)
    return pl.BlockSpec(shape, lambda *_: (0,) * nd, pipeline_mode=pl.Buffered(1))


def _rmsnorm(x, g):
    y = x * lax.rsqrt(jnp.mean(x * x, axis=-1, keepdims=True) + EPS)
    return y * g


def _sigmoid(x):
    return 0.5 * jnp.tanh(0.5 * x) + 0.5


def _gelu_tanh(x):
    c1 = (2.0 / jnp.pi) ** 0.5
    return x * (0.5 * jnp.tanh(x * (c1 + (c1 * 0.044715) * (x * x))) + 0.5)


def _segment_major_perm(n_seq, seq):
    n_groups = seq // SUBLANES
    row = np.arange(seq)
    time = (row % SUBLANES) * n_groups + row // SUBLANES
    p = np.zeros((seq, seq), np.float32)
    p[row, time] = 1.0
    return jnp.asarray(np.kron(np.eye(n_seq, dtype=np.float32), p), BF16)


def _seg_coords(n_groups, back):
    t = SUBLANES * n_groups - back
    return t % n_groups, t // n_groups


def _seg_hist_init(hist_ref, state_ref, n_groups):
    n_state = state_ref.shape[1]
    n_hist = hist_ref.shape[1] // SUBLANES
    hist_ref[...] = jnp.zeros(hist_ref.shape, F32)
    for back in range(1, n_state + 1):
        g, s = _seg_coords(n_groups, back)
        row = (g - (n_groups - n_hist)) * SUBLANES + s
        hist_ref[:, row:row + 1, :] = state_ref[:, n_state - back:n_state - back + 1, :]


def _seg_state_out(ns_ref, x_ref, cols=slice(None)):
    n_state = ns_ref.shape[1]
    n_groups = x_ref.shape[1] // SUBLANES
    for back in range(1, n_state + 1):
        g, s = _seg_coords(n_groups, back)
        row = g * SUBLANES + s
        ns_ref[:, n_state - back:n_state - back + 1, cols] = x_ref[:, row:row + 1, :]


def _seg_conv(x_ref, hist_ref, w_ref, bias):
    nb, seq, c = x_ref.shape
    n_groups = seq // SUBLANES
    n_hist = hist_ref.shape[1] // SUBLANES
    width = w_ref.shape[0]
    assert n_hist >= min(n_groups, width - 1)
    sub = lax.broadcasted_iota(jnp.int32, (1, SUBLANES, c), 1)
    y = None
    for k in range(width):
        lag = width - 1 - k
        if lag == 0:
            tap = x_ref[...]
        else:
            parts = []
            for g in range(min(lag, n_groups)):
                src = (g - lag) % n_groups
                crossed = (lag - g + n_groups - 1) // n_groups
                cur = x_ref[:, src * SUBLANES:(src + 1) * SUBLANES, :]
                h0 = (src - (n_groups - n_hist)) * SUBLANES
                prev = hist_ref[:, h0:h0 + SUBLANES, :]
                parts.append(pltpu.roll(jnp.where(sub >= SUBLANES - crossed, prev, cur), crossed, 1))
            if n_groups > lag:
                parts.append(x_ref[:, 0:(n_groups - lag) * SUBLANES, :])
            tap = jnp.concatenate(parts, axis=1)
        y = tap * w_ref[k:k + 1, :] if y is None else y + tap * w_ref[k:k + 1, :]
    return y if bias is None else y + bias


def _softplus(x):
    return jnp.maximum(x, 0.0) + jnp.log1p(jnp.exp(-jnp.abs(x)))


def _permute_rows(ref, perm_ref):
    blk = perm_ref.shape[0]
    for r in range(0, ref.shape[0], blk):
        ref[r:r + blk, :] = jnp.dot(perm_ref[...], ref[r:r + blk, :], preferred_element_type=F32).astype(BF16)


def _rmsnorm_kernel(x_ref, g_ref, *rest):
    u_ref = rest[-1]
    tm = x_ref.shape[0]
    chunk = min(ROW_CHUNK, tm)
    g = g_ref[...]

    def body(c, _):
        rows = pl.ds(pl.multiple_of(c * chunk, chunk), chunk)
        u_ref[rows, :] = _rmsnorm(x_ref[rows, :], g).astype(BF16)
        return None

    lax.fori_loop(0, tm // chunk, body, None)
    if len(rest) == 2:
        _permute_rows(u_ref, rest[0])


def _rmsnorm_call(x, g, *, tm, perm=None):
    m, d = x.shape
    tm = min(tm, m)
    assert m % tm == 0 and tm % min(ROW_CHUNK, tm) == 0 and (perm is None or tm % perm.shape[0] == 0)
    extra = [] if perm is None else [perm]
    return pl.pallas_call(
        _rmsnorm_kernel,
        grid=(m // tm,),
        in_specs=[pl.BlockSpec((tm, d), lambda i: (i, 0)), pl.BlockSpec((1, d), lambda i: (0, 0))]
        + [pl.BlockSpec(a.shape, lambda i: (0, 0)) for a in extra],
        out_specs=pl.BlockSpec((tm, d), lambda i: (i, 0)),
        out_shape=jax.ShapeDtypeStruct((m, d), BF16),
        compiler_params=_params("parallel"),
        name="rmsnorm",
    )(x, g, *extra)


def _cast_rows(src_ref, dst_ref, cols=slice(None)):
    rows_total = src_ref.shape[0]
    chunk = min(CAST_ROWS, rows_total)

    def body(c, _):
        rows = pl.ds(pl.multiple_of(c * chunk, chunk), chunk)
        dst_ref[rows, cols] = src_ref[rows, :].astype(BF16)
        return None

    lax.fori_loop(0, rows_total // chunk, body, None)


def _side_cast_plan(arrays, n_steps, flat_step):
    in_specs, out_specs, out_shape = [], [], []
    for a in arrays:
        rows, cols = a.shape
        chunk = next(c for c in range(BF16_ROWS, rows + 1, BF16_ROWS) if rows % c == 0 and rows // c <= n_steps)
        last = rows // chunk - 1
        spec = pl.BlockSpec((chunk, cols), lambda *idx, last=last: (jnp.minimum(flat_step(*idx), last), 0))
        in_specs.append(spec)
        out_specs.append(spec)
        out_shape.append(jax.ShapeDtypeStruct((rows, cols), BF16))
    return in_specs, out_specs, out_shape


def _side_cast(src_refs, dst_refs):
    for src, dst in zip(src_refs, dst_refs):
        dst[...] = src[...].astype(BF16)


def _group_steps(counts):
    starts = [sum(counts[:g]) for g in range(len(counts))]
    return starts, sum(counts)


def _group_tile(i, start, count):
    return jnp.clip(i - start, 0, count - 1)


def _proj_kernel(*refs, starts, counts):
    n_groups = len(counts)
    u_refs, w_ref, o_refs, w_scr = refs[:n_groups], refs[n_groups], refs[n_groups + 1:-1], refs[-1]
    i = pl.program_id(1)

    @pl.when(i == 0)
    def _():
        _cast_rows(w_ref, w_scr)

    for g in range(n_groups):
        @pl.when((i >= starts[g]) & (i < starts[g] + counts[g]))
        def _(g=g):
            o_refs[g][...] = jnp.dot(u_refs[g][...], w_scr[...],
                                     preferred_element_type=F32).astype(o_refs[g].dtype)


def _proj(us, w, out_dtype, *, tm, tn):
    d, n = w.shape
    tn = min(tn, n)
    tms = [min(tm, u.shape[0]) for u in us]
    counts = [u.shape[0] // t for u, t in zip(us, tms)]
    assert all(u.shape[0] % t == 0 for u, t in zip(us, tms)) and n % tn == 0 and d % min(CAST_ROWS, d) == 0
    starts, steps = _group_steps(counts)
    tile = lambda g: (lambda j, i: (_group_tile(i, starts[g], counts[g]), 0))
    out_tile = lambda g: (lambda j, i: (_group_tile(i, starts[g], counts[g]), j))
    return pl.pallas_call(
        functools.partial(_proj_kernel, starts=starts, counts=counts),
        grid=(n // tn, steps),
        in_specs=([pl.BlockSpec((tms[g], d), tile(g)) for g in range(len(us))]
                  + [pl.BlockSpec((d, tn), lambda j, i: (0, j))]),
        out_specs=[pl.BlockSpec((tms[g], tn), out_tile(g)) for g in range(len(us))],
        out_shape=[jax.ShapeDtypeStruct((u.shape[0], n), out_dtype) for u in us],
        scratch_shapes=[pltpu.VMEM((d, tn), BF16)],
        compiler_params=_params("arbitrary", "arbitrary"),
        name="proj",
    )(*us, w)


def _mixer_pre_kernel(*refs, d_rnn, d_conv, pos0, n_side):
    (z_ref, sa_ref, h0_ref, sb_ref, caw_ref, cab_ref, wri_ref, br_ref, bi_ref, lam_ref, cbw_ref) = refs[:11]
    side_in = refs[11:11 + n_side]
    ya_ref, yb_ref, nsa_ref, nh_ref, nsb_ref = refs[11 + n_side:16 + n_side]
    side_out = refs[16 + n_side:16 + 2 * n_side]
    xa_scr, ha_scr, xc_scr, xcb_scr, ri_scr, cb_scr, hb_scr, h_scr = refs[16 + 2 * n_side:]
    _side_cast(side_in, side_out)
    t = pl.program_id(1)
    nb, seq, _ = ya_ref.shape
    n_groups = seq // SUBLANES
    n_heads, head_dim, _ = wri_ref.shape

    @pl.when(t == 0)
    def _():
        _seg_hist_init(ha_scr, sa_ref, n_groups)
        _seg_hist_init(hb_scr, sb_ref, n_groups)
        h_scr[...] = jnp.broadcast_to(h0_ref[...], h_scr.shape)

    xa_scr[...] = z_ref[:, :, 0:d_rnn].astype(F32)
    xc = _seg_conv(xa_scr, ha_scr, caw_ref, cab_ref[...])
    xc_scr[...] = xc
    xcb_scr[...] = xc.reshape(nb * seq, d_rnn).astype(BF16)
    _seg_state_out(nsa_ref, xa_scr)
    ha_scr[...] = xa_scr[:, seq - ha_scr.shape[1]:, :]

    for h in range(n_heads):
        cols = slice(h * head_dim, (h + 1) * head_dim)
        ri = jnp.dot(xcb_scr[:, cols], wri_ref[h], preferred_element_type=F32)
        ri_scr[:, :, cols] = ri[:, :head_dim].reshape(nb, seq, head_dim)
        ri_scr[:, :, d_rnn + h * head_dim:d_rnn + (h + 1) * head_dim] = ri[:, head_dim:].reshape(nb, seq, head_dim)

    strip = max(LANES, SCAN_LANES // nb)
    pairs = seq // BF16_ROWS
    sub = lax.broadcasted_iota(jnp.int32, (1, SUBLANES, strip), 1)
    for c in range(d_rnn // strip):
        cs = slice(c * strip, (c + 1) * strip)
        cs_i = slice(d_rnn + c * strip, d_rnn + (c + 1) * strip)
        cs_g = slice(d_rnn + c * strip, d_rnn + (c + 1) * strip)
        rate = RG_C * _softplus(-lam_ref[:, cs])
        b_r = br_ref[:, cs]
        b_i = bi_ref[:, cs]

        def pass1(it, carry, first=False, cs=cs, cs_i=cs_i, rate=rate, b_r=b_r, b_i=b_i):
            h_loc, a_run = carry
            for half in range(BF16_ROWS // SUBLANES):
                rows = pl.ds(pl.multiple_of(it * BF16_ROWS + half * SUBLANES, SUBLANES), SUBLANES)
                r = _sigmoid(ri_scr[:, rows, cs] + b_r)
                i = _sigmoid(ri_scr[:, rows, cs_i] + b_i)
                neg_log_a = r * rate
                a = jnp.exp(-neg_log_a)
                mult = jnp.sqrt(jnp.tanh(neg_log_a) * (a * a + 1.0))
                if first and half == 0:
                    mult = jnp.where(jnp.logical_and(t == 0, sub == 0), 1.0, mult)
                b = mult * i * xc_scr[:, rows, cs]
                h_loc = a * h_loc + b
                a_run = a * a_run
                ri_scr[:, rows, cs] = h_loc
                ri_scr[:, rows, cs_i] = a_run
            return h_loc, a_run

        carry = (jnp.zeros((nb, SUBLANES, strip), F32), jnp.ones((nb, SUBLANES, strip), F32))
        start = 0
        if pos0 == 0:
            carry = pass1(0, carry, first=True)
            start = 1
        h_tot, a_tot = lax.fori_loop(start, pairs, pass1, carry)

        for s in (1, 2, 4):
            keep = sub >= s
            a_prev = pltpu.roll(a_tot, s, 1)
            h_prev = pltpu.roll(h_tot, s, 1)
            h_tot = jnp.where(keep, a_tot * h_prev + h_tot, h_tot)
            a_tot = jnp.where(keep, a_tot * a_prev, a_tot)
        h_in = h_scr[:, :, cs]
        seg_end = a_tot * h_in + h_tot
        seg_start = jnp.where(sub == 0, h_in, pltpu.roll(seg_end, 1, 1))
        h_last = jnp.broadcast_to(seg_end[:, SUBLANES - 1:SUBLANES, :], seg_end.shape)
        h_scr[:, :, cs] = h_last
        nh_ref[:, :, cs] = h_last[:, 0:1, :]
        start2 = jnp.concatenate([seg_start] * (BF16_ROWS // SUBLANES), axis=1)

        def pass2(it, _, cs=cs, cs_i=cs_i, cs_g=cs_g, start2=start2):
            rows = pl.ds(pl.multiple_of(it * BF16_ROWS, BF16_ROWS), BF16_ROWS)
            h = ri_scr[:, rows, cs] + ri_scr[:, rows, cs_i] * start2
            gate = _gelu_tanh(z_ref[:, rows, cs_g].astype(F32))
            ya_ref[:, rows, cs] = (h * gate).astype(BF16)
            return None

        lax.fori_loop(0, pairs, pass2, None)

    o_gb, o_gc, o_hb = 2 * d_rnn, 2 * d_rnn + d_conv, 2 * d_rnn + 2 * d_conv
    cb_scr[...] = z_ref[:, :, o_gc:o_gc + d_conv].astype(F32) * z_ref[:, :, o_hb:o_hb + d_conv].astype(F32)
    co = _seg_conv(cb_scr, hb_scr, cbw_ref, None)
    yb_ref[...] = (z_ref[:, :, o_gb:o_gb + d_conv].astype(F32) * co).astype(BF16)
    _seg_state_out(nsb_ref, cb_scr)
    hb_scr[...] = cb_scr[:, seq - hb_scr.shape[1]:, :]


def _mixer_pre(z, state_a, h0, state_b, conv_a_w, conv_a_b, w_ri, b_r, b_i, lam, conv_b_w, *, seq_tile, pos0,
               to_round=()):
    bsz, t, _ = z.shape
    d_rnn = conv_a_w.shape[1]
    d_conv = conv_b_w.shape[1]
    n_pre = 2 * d_rnn + 3 * d_conv
    nb = bsz if t <= seq_tile else 1
    seq = min(seq_tile, t)
    assert t % seq == 0 and seq % BF16_ROWS == 0 and bsz % nb == 0
    wa, wb = conv_a_w.shape[0], conv_b_w.shape[0]
    hist_a, hist_b = min(seq // SUBLANES, wa - 1), min(seq // SUBLANES, wb - 1)
    kern = functools.partial(_mixer_pre_kernel, d_rnn=d_rnn, d_conv=d_conv, pos0=pos0, n_side=len(to_round))
    n_t = t // seq
    side_in, side_out, side_shape = _side_cast_plan(to_round, (bsz // nb) * n_t, lambda b, i: b * n_t + i)
    row = lambda b, i: (b, i, 0)
    per_b = lambda b, i: (b, 0, 0)
    const2 = lambda b, i: (0, 0)
    return pl.pallas_call(
        kern,
        grid=(bsz // nb, t // seq),
        in_specs=[pl.BlockSpec((nb, seq, n_pre), row),
                  pl.BlockSpec((nb, wa - 1, d_rnn), per_b),
                  pl.BlockSpec((nb, 1, d_rnn), per_b),
                  pl.BlockSpec((nb, wb - 1, d_conv), per_b),
                  pl.BlockSpec(conv_a_w.shape, const2),
                  pl.BlockSpec((1, d_rnn), const2),
                  pl.BlockSpec(w_ri.shape, lambda b, i: (0, 0, 0)),
                  pl.BlockSpec((1, d_rnn), const2),
                  pl.BlockSpec((1, d_rnn), const2),
                  pl.BlockSpec((1, d_rnn), const2),
                  pl.BlockSpec(conv_b_w.shape, const2)] + side_in,
        out_specs=[pl.BlockSpec((nb, seq, d_rnn), row),
                   pl.BlockSpec((nb, seq, d_conv), row),
                   pl.BlockSpec((nb, wa - 1, d_rnn), per_b),
                   pl.BlockSpec((nb, 1, d_rnn), per_b),
                   pl.BlockSpec((nb, wb - 1, d_conv), per_b)] + side_out,
        out_shape=[jax.ShapeDtypeStruct((bsz, t, d_rnn), BF16),
                   jax.ShapeDtypeStruct((bsz, t, d_conv), BF16),
                   jax.ShapeDtypeStruct((bsz, wa - 1, d_rnn), F32),
                   jax.ShapeDtypeStruct((bsz, 1, d_rnn), F32),
                   jax.ShapeDtypeStruct((bsz, wb - 1, d_conv), F32)] + side_shape,
        scratch_shapes=[pltpu.VMEM((nb, seq, d_rnn), F32),
                        pltpu.VMEM((nb, hist_a * SUBLANES, d_rnn), F32),
                        pltpu.VMEM((nb, seq, d_rnn), F32),
                        pltpu.VMEM((nb * seq, d_rnn), BF16),
                        pltpu.VMEM((nb, seq, 2 * d_rnn), F32),
                        pltpu.VMEM((nb, seq, d_conv), F32),
                        pltpu.VMEM((nb, hist_b * SUBLANES, d_conv), F32),
                        pltpu.VMEM((nb, SUBLANES, d_rnn), F32)],
        compiler_params=_params("arbitrary", "arbitrary"),
        name="mixer_pre",
    )(z, state_a, h0, state_b, conv_a_w, conv_a_b, w_ri, b_r, b_i, lam, conv_b_w, *to_round)


def _mixer_post_kernel(ya_ref, yb_ref, ga0_ref, ga1_ref, gb0_ref, gb1_ref, x_ref, pa_ref, pb_ref, wo_ref, g_ref,
                       unperm_ref, o_ref, mix_scr):
    half = ga0_ref.shape[1]
    y_a = jnp.dot(ya_ref[...], pa_ref[...], preferred_element_type=F32)
    y_b = jnp.dot(yb_ref[...], pb_ref[...], preferred_element_type=F32)
    for c, (ga_ref, gb_ref) in enumerate(((ga0_ref, gb0_ref), (ga1_ref, gb1_ref))):
        cols = slice(c * half, (c + 1) * half)
        mix = (jax.nn.sigmoid(ga_ref[...].astype(F32)) * y_a[:, cols]
               + jax.nn.sigmoid(gb_ref[...].astype(F32)) * y_b[:, cols])
        mix_scr[:, cols] = mix.astype(BF16)
    _permute_rows(mix_scr, unperm_ref)
    out = jnp.dot(mix_scr[...], wo_ref[...], preferred_element_type=F32)
    o_ref[...] = x_ref[...] + _rmsnorm(out, g_ref[...])


def _mixer_post(ya, yb, z, x, p_a, p_b, w_o, g_post, unperm, *, tm):
    m, d = x.shape
    d_rnn, d_conv = ya.shape[1], yb.shape[1]
    tm = min(tm, m)
    half = d // 2
    gate0 = (2 * d_rnn + 3 * d_conv) // half
    assert m % tm == 0 and (2 * d_rnn + 3 * d_conv) % half == 0 and tm % unperm.shape[0] == 0
    gate_spec = lambda k: pl.BlockSpec((tm, half), lambda i: (i, gate0 + k))
    row = lambda i: (i, 0)
    return pl.pallas_call(
        _mixer_post_kernel,
        grid=(m // tm,),
        in_specs=[pl.BlockSpec((tm, d_rnn), row), pl.BlockSpec((tm, d_conv), row),
                  gate_spec(0), gate_spec(1), gate_spec(2), gate_spec(3),
                  pl.BlockSpec((tm, d), row),
                  _resident(p_a.shape), _resident(p_b.shape), _resident(w_o.shape),
                  pl.BlockSpec((1, d), lambda i: (0, 0)), pl.BlockSpec(unperm.shape, lambda i: (0, 0))],
        out_specs=pl.BlockSpec((tm, d), row),
        out_shape=jax.ShapeDtypeStruct((m, d), F32),
        scratch_shapes=[pltpu.VMEM((tm, d), BF16)],
        compiler_params=_params("parallel"),
        name="mixer_post",
    )(ya, yb, z, z, z, z, x, p_a, p_b, w_o, g_post, unperm)


def _xattn_kernel(x_ref, k_ref, v_ref, wq_ref, wxo_ref, gpre_ref, gpost_ref, gnext_ref, perm_ref, o_ref, u_ref,
                  q_scr, o_scr, *, n_heads):
    nb, seq, d = x_ref.shape
    hd = d // n_heads
    scale = hd ** -0.5
    x = x_ref[...].reshape(nb * seq, d)
    u = _rmsnorm(x, gpre_ref[...]).astype(BF16)
    q_scr[...] = jnp.dot(u, wq_ref[...], preferred_element_type=F32).astype(BF16)
    for b in range(nb):
        rows = slice(b * seq, (b + 1) * seq)
        for h in range(n_heads):
            cols = slice(h * hd, (h + 1) * hd)
            s = lax.dot_general(q_scr[rows, cols], k_ref[b, :, cols], (((1,), (1,)), ((), ())),
                                preferred_element_type=F32) * scale
            p = jnp.exp(s - jnp.max(s, axis=-1, keepdims=True))
            att = (p / jnp.sum(p, axis=-1, keepdims=True)).astype(BF16)
            o_scr[rows, cols] = jnp.dot(att, v_ref[b, :, cols], preferred_element_type=F32).astype(BF16)
    out = jnp.dot(o_scr[...], wxo_ref[...], preferred_element_type=F32)
    x_new = x + _rmsnorm(out, gpost_ref[...])
    o_ref[...] = x_new.reshape(nb, seq, d)
    u_next = _rmsnorm(x_new, gnext_ref[...]).astype(BF16)
    blk = perm_ref.shape[0]
    u_next = jnp.concatenate([jnp.dot(perm_ref[...], u_next[r:r + blk, :], preferred_element_type=F32)
                              for r in range(0, nb * seq, blk)], axis=0)
    u_ref[...] = u_next.astype(BF16).reshape(nb, seq, d)


def _xattn(x, mem_k, mem_v, w_q, w_xo, g_pre, g_post, g_next, perm, *, n_heads, seq_tile):
    bsz, t, d = x.shape
    n_mem = mem_k.shape[1]
    nb = bsz if t <= seq_tile else 1
    seq = min(seq_tile, t)
    assert t % seq == 0 and seq % BF16_ROWS == 0 and bsz % nb == 0 and (nb * seq) % perm.shape[0] == 0
    single = bsz // nb == 1
    mem_spec = (_resident((nb, n_mem, d)) if single else pl.BlockSpec((nb, n_mem, d), lambda b, i: (b, 0, 0)))
    row = lambda b, i: (b, i, 0)
    gain = pl.BlockSpec((1, d), lambda b, i: (0, 0))
    return pl.pallas_call(
        functools.partial(_xattn_kernel, n_heads=n_heads),
        grid=(bsz // nb, t // seq),
        in_specs=[pl.BlockSpec((nb, seq, d), row), mem_spec, mem_spec,
                  _resident(w_q.shape), _resident(w_xo.shape), gain, gain, gain,
                  pl.BlockSpec(perm.shape, lambda b, i: (0, 0))],
        out_specs=[pl.BlockSpec((nb, seq, d), row), pl.BlockSpec((nb, seq, d), row)],
        out_shape=[jax.ShapeDtypeStruct((bsz, t, d), F32), jax.ShapeDtypeStruct((bsz, t, d), BF16)],
        scratch_shapes=[pltpu.VMEM((nb * seq, d), BF16), pltpu.VMEM((nb * seq, d), BF16)],
        compiler_params=_params("parallel", "parallel"),
        name="xattn",
    )(x, mem_k, mem_v, w_q, w_xo, g_pre, g_post, g_next, perm)


def _up_geglu_kernel(*refs, starts, counts, blocks, n_side):
    n_groups = len(counts)
    ins, rest = refs[:3 * n_groups], refs[3 * n_groups:]
    wg_ref, wv_ref, cwg_ref, cwv_ref, cbg_ref, cbv_ref = rest[:6]
    side_in, rest = rest[6:6 + n_side], rest[6 + n_side:]
    outs, rest = rest[:3 * n_groups], rest[3 * n_groups:]
    side_out, scr = rest[:n_side], rest[n_side:]
    w_scr, hists = scr[0], scr[1:]
    _side_cast(side_in, side_out)
    i = pl.program_id(1)
    tn = wg_ref.shape[1]

    @pl.when(i == 0)
    def _():
        _cast_rows(wg_ref, w_scr, slice(0, tn))
        _cast_rows(wv_ref, w_scr, slice(tn, 2 * tn))

    for g in range(n_groups):
        u_ref, st_refs = ins[3 * g], ins[3 * g + 1:3 * g + 3]
        hid_ref, ns_refs = outs[3 * g], outs[3 * g + 1:3 * g + 3]
        p_scrs = hists[2 * g:2 * g + 2]

        @pl.when(i == starts[g])
        def _(st_refs=st_refs, p_scrs=p_scrs, blk=blocks[g]):
            for st_ref, p_scr in zip(st_refs, p_scrs):
                _seg_hist_init(p_scr, st_ref, blk // SUBLANES)

        @pl.when((i >= starts[g]) & (i < starts[g] + counts[g]))
        def _(u_ref=u_ref, hid_ref=hid_ref, ns_refs=ns_refs, p_scrs=p_scrs, blk=blocks[g]):
            nb, seq, k_dim = u_ref.shape
            per_seq = seq // blk
            n_hist = p_scrs[0].shape[1]
            both = jnp.dot(u_ref[...].reshape(nb * seq, k_dim), w_scr[...], preferred_element_type=F32)
            ys = []
            for half, (p_scr, cw_ref, cb_ref, ns_ref) in enumerate(zip(p_scrs, (cwg_ref, cwv_ref),
                                                                     (cbg_ref, cbv_ref), ns_refs)):
                a = both[:, half * tn:(half + 1) * tn].reshape(nb * per_seq, blk, tn)
                tails = a[:, blk - n_hist:, :].reshape(nb, per_seq, n_hist, tn)
                hist = p_scr[...][:, None]
                if per_seq > 1:
                    hist = jnp.concatenate([hist, tails[:, :per_seq - 1]], axis=1)
                hist = hist.reshape(nb * per_seq, n_hist, tn)
                ys.append(_seg_conv(a, hist, cw_ref, cb_ref[...]).reshape(nb, seq, tn))
                p_scr[...] = tails[:, per_seq - 1]
                _seg_state_out(ns_ref, a.reshape(nb, per_seq, blk, tn)[:, per_seq - 1])
            hid_ref[...] = (_gelu_tanh(ys[0]) * ys[1]).astype(BF16)


def _up_geglu(us, w_up, states, conv_w, conv_b, blocks, *, tm, tn, to_round=()):
    d = w_up.shape[0]
    d_ff = w_up.shape[1] // 2
    width = conv_w.shape[0]
    n_j = d_ff // tn
    nbs = [u.shape[0] if u.shape[1] <= tm else 1 for u in us]
    seqs = [min(tm, u.shape[1]) for u in us]
    counts = [u.shape[1] // s for u, s in zip(us, seqs)]
    assert all(u.shape[0] == nb and u.shape[1] % s == 0 and s % BF16_ROWS == 0 for u, nb, s in zip(us, nbs, seqs))
    assert d_ff % tn == 0 and tn % LANES == 0
    starts, steps = _group_steps(counts)
    col_g = lambda j, i: (0, j)
    col_v = lambda j, i: (0, n_j + j)
    st_g = lambda j, i: (0, 0, j)
    st_v = lambda j, i: (0, 0, n_j + j)
    in_specs, out_specs, out_shape, hist, args = [], [], [], [], []
    for g, (u, st, nb, seq) in enumerate(zip(us, states, nbs, seqs)):
        row = lambda j, i, g=g: (0, _group_tile(i, starts[g], counts[g]), 0)
        out = lambda j, i, g=g: (0, _group_tile(i, starts[g], counts[g]), j)
        in_specs += [pl.BlockSpec((nb, seq, d), row),
                     pl.BlockSpec((nb, width - 1, tn), st_g), pl.BlockSpec((nb, width - 1, tn), st_v)]
        out_specs += [pl.BlockSpec((nb, seq, tn), out),
                      pl.BlockSpec((nb, width - 1, tn), st_g), pl.BlockSpec((nb, width - 1, tn), st_g)]
        out_shape += [jax.ShapeDtypeStruct((u.shape[0], u.shape[1], d_ff), BF16),
                      jax.ShapeDtypeStruct((u.shape[0], width - 1, d_ff), F32),
                      jax.ShapeDtypeStruct((u.shape[0], width - 1, d_ff), F32)]
        assert seq % blocks[g] == 0 and blocks[g] % SUBLANES == 0
        n_hist = min(blocks[g] // SUBLANES, width - 1) * SUBLANES
        hist += [pltpu.VMEM((nb, n_hist, tn), F32), pltpu.VMEM((nb, n_hist, tn), F32)]
        args += [u, st, st]
    side_in, side_out, side_shape = _side_cast_plan(to_round, n_j * steps, lambda j, i: j * steps + i)
    in_specs += [pl.BlockSpec((d, tn), col_g), pl.BlockSpec((d, tn), col_v),
                 pl.BlockSpec((width, tn), col_g), pl.BlockSpec((width, tn), col_v),
                 pl.BlockSpec((1, tn), col_g), pl.BlockSpec((1, tn), col_v)] + side_in
    res = pl.pallas_call(
        functools.partial(_up_geglu_kernel, starts=starts, counts=counts, blocks=tuple(blocks),
                          n_side=len(to_round)),
        grid=(n_j, steps),
        in_specs=in_specs,
        out_specs=out_specs + side_out,
        out_shape=out_shape + side_shape,
        scratch_shapes=[pltpu.VMEM((d, 2 * tn), BF16)] + hist,
        compiler_params=_params("arbitrary", "arbitrary"),
        name="up_geglu",
    )(*args, w_up, w_up, conv_w, conv_w, conv_b, conv_b, *to_round)
    return [tuple(res[3 * g:3 * g + 3]) for g in range(len(us))], list(res[3 * len(us):])


def _ffn_down_kernel(*refs, starts, counts):
    n_groups = len(counts)
    wd_ref, g_ref = refs[3 * n_groups:3 * n_groups + 2]
    o_refs = refs[3 * n_groups + 2:]
    i = pl.program_id(0)
    for g in range(n_groups):
        hid_ref, x_ref, unperm_ref = refs[3 * g:3 * g + 3]

        @pl.when((i >= starts[g]) & (i < starts[g] + counts[g]))
        def _(hid_ref=hid_ref, x_ref=x_ref, unperm_ref=unperm_ref, o_ref=o_refs[g]):
            y = jnp.dot(hid_ref[...], wd_ref[...], preferred_element_type=F32)
            branch = _rmsnorm(y, g_ref[...]).astype(BF16)
            blk = unperm_ref.shape[0]
            for r in range(0, branch.shape[0], blk):
                o_ref[r:r + blk, :] = x_ref[r:r + blk, :] + jnp.dot(unperm_ref[...], branch[r:r + blk, :],
                                                                    preferred_element_type=F32)


def _ffn_down(hids, xs, w_down, g_post, unperms, *, tm):
    d_ff, d = w_down.shape
    tms = [min(tm, x.shape[0]) for x in xs]
    counts = [x.shape[0] // t for x, t in zip(xs, tms)]
    assert all(x.shape[0] % t == 0 and t % un.shape[0] == 0 for x, t, un in zip(xs, tms, unperms))
    starts, steps = _group_steps(counts)
    in_specs, args = [], []
    for g, (hid, x, un) in enumerate(zip(hids, xs, unperms)):
        row = lambda i, g=g: (_group_tile(i, starts[g], counts[g]), 0)
        in_specs += [pl.BlockSpec((tms[g], d_ff), row), pl.BlockSpec((tms[g], d), row),
                     pl.BlockSpec(un.shape, lambda i: (0, 0))]
        args += [hid, x, un]
    return pl.pallas_call(
        functools.partial(_ffn_down_kernel, starts=starts, counts=counts),
        grid=(steps,),
        in_specs=in_specs + [_resident(w_down.shape), pl.BlockSpec((1, d), lambda i: (0, 0))],
        out_specs=[pl.BlockSpec((tms[g], d), lambda i, g=g: (_group_tile(i, starts[g], counts[g]), 0))
                   for g in range(len(xs))],
        out_shape=[jax.ShapeDtypeStruct(x.shape, F32) for x in xs],
        compiler_params=_params("arbitrary"),
        name="ffn_down",
    )(*args, w_down, g_post)


MM_TM, MM_TN = 2048, 1024
UP_TM, UP_TN = 1024, 512
SEQ_TILE = 256
ROW_TILE = 512
NORM_TM = 1024
N_XHEADS = 4


def _layer(groups, p):
    flat = [x.reshape(-1, x.shape[-1]) for x, *_ in groups]
    perms, blocks = [], []
    for x, *_ in groups:
        bsz, t, _ = x.shape
        perms.append(_segment_major_perm(*((1, SEQ_TILE) if t > SEQ_TILE else (bsz, t))))
        blocks.append(min(t, SEQ_TILE))
    u0s = [_rmsnorm_call(x2, p['g_mix_pre'], tm=NORM_TM, perm=pm) for x2, pm in zip(flat, perms)]
    zs = _proj(u0s, p['w_in'], BF16, tm=MM_TM, tn=MM_TN)
    big = max(range(len(groups)), key=lambda g: flat[g].shape[0])
    resident = [p['p_a'], p['p_b'], p['w_o'], p['w_q'], p['w_xo']]
    mems = [m.reshape(-1, m.shape[-1]) for g in groups for m in g[2:4]]
    pre = []
    for g, ((x, pos0, _, _, s_a, h0, s_b, _), z) in enumerate(zip(groups, zs)):
        bsz, t, _ = x.shape
        pre.append(_mixer_pre(z.reshape(bsz, t, -1), s_a, h0[:, None, :], s_b,
                              p['conv_a_w'], p['conv_a_b'], p['w_ri'], p['b_r'], p['b_i'],
                              p['lru_lambda'], p['conv_b_w'], seq_tile=SEQ_TILE, pos0=pos0,
                              to_round=resident + mems if g == big else ()))
    p_a, p_b, w_o, w_q, w_xo, *mems = pre[big][5:]
    mids = []
    for g, ((x, *_), x2, z) in enumerate(zip(groups, flat, zs)):
        bsz, t, d = x.shape
        m = bsz * t
        ya, yb, ns_a, nh, ns_b = pre[g][:5]
        mem_k, mem_v = (mm.reshape(bsz, -1, d) for mm in mems[2 * g:2 * g + 2])
        x1 = _mixer_post(ya.reshape(m, -1), yb.reshape(m, -1), z, x2, p_a, p_b, w_o, p['g_mix_post'],
                         perms[g].T, tm=ROW_TILE)
        x2a, u3 = _xattn(x1.reshape(bsz, t, d), mem_k, mem_v, w_q, w_xo, p['g_x_pre'], p['g_x_post'],
                         p['g_ffn_pre'], perms[g], n_heads=N_XHEADS, seq_tile=ROW_TILE)
        mids.append((x2a, u3, ns_a, nh[:, 0, :], ns_b))
    ffn, (w_down,) = _up_geglu([u3 for _, u3, *_ in mids], p['w_up'], [g[7] for g in groups],
                               p['ffn_conv_w'], p['ffn_conv_b'], blocks, tm=UP_TM, tn=UP_TN,
                               to_round=[p['w_down']])
    x3s = _ffn_down([hid.reshape(-1, hid.shape[-1]) for hid, _, _ in ffn],
                    [x2a.reshape(-1, x2a.shape[-1]) for x2a, *_ in mids], w_down, p['g_ffn_post'],
                    [pm.T for pm in perms], tm=ROW_TILE)
    return [(x3.reshape(x2a.shape), ns_a, nh, ns_b, jnp.concatenate([ns_fg, ns_fv], axis=-1))
            for x3, (x2a, _, ns_a, nh, ns_b), (_, ns_fg, ns_fv) in zip(x3s, mids, ffn)]


def kernel(x_prompt, x_sample, mem_prompt, state_conv_a, state_rglru, state_conv_b, state_ffn_conv, cache_mem_k, cache_mem_v, g_mix_pre, g_mix_post, w_in, conv_a_w, conv_a_b, w_r, b_r, w_i, b_i, lru_lambda, conv_b_w, p_a, p_b, w_o, g_x_pre, g_x_post, g_mem, w_q, w_k, w_v, w_xo, g_ffn_pre, g_ffn_post, w_up, ffn_conv_w, ffn_conv_b, w_down):
    depth = w_in.shape[0]
    bsz, _, d = x_prompt.shape
    n_mem = mem_prompt.shape[1]
    yp, ys = x_prompt, x_sample
    outs = [[] for _ in range(10)]
    row = lambda v: v.reshape(1, -1).astype(F32)
    for l in range(depth):
        p = {'g_mix_pre': row(g_mix_pre[l]), 'g_mix_post': row(g_mix_post[l]), 'w_in': w_in[l],
             'conv_a_w': conv_a_w[l], 'conv_a_b': row(conv_a_b[l]),
             'w_ri': jnp.concatenate([w_r[l], w_i[l]], axis=-1).astype(BF16),
             'b_r': row(b_r[l]), 'b_i': row(b_i[l]), 'lru_lambda': row(lru_lambda[l]),
             'conv_b_w': conv_b_w[l], 'p_a': p_a[l], 'p_b': p_b[l],
             'w_o': w_o[l], 'g_x_pre': row(g_x_pre[l]), 'g_x_post': row(g_x_post[l]),
             'w_q': w_q[l], 'w_xo': w_xo[l],
             'g_ffn_pre': row(g_ffn_pre[l]), 'g_ffn_post': row(g_ffn_post[l]), 'w_up': w_up[l],
             'ffn_conv_w': ffn_conv_w[l], 'ffn_conv_b': row(ffn_conv_b[l]), 'w_down': w_down[l]}
        d_rnn, d_conv, d_up = conv_a_w.shape[2], conv_b_w.shape[2], ffn_conv_w.shape[2]
        mem_u = _rmsnorm_call(mem_prompt.reshape(bsz * n_mem, d), row(g_mem[l]), tm=NORM_TM)
        mk = _proj([mem_u], w_k[l], F32, tm=MM_TM, tn=MM_TN)[0].reshape(bsz, n_mem, d)
        mv = _proj([mem_u], w_v[l], F32, tm=MM_TM, tn=MM_TN)[0].reshape(bsz, n_mem, d)
        zeros = lambda *s: jnp.zeros(s, F32)
        prompt = (yp, 0, mk, mv,
                  zeros(bsz, conv_a_w.shape[1] - 1, d_rnn), zeros(bsz, d_rnn),
                  zeros(bsz, conv_b_w.shape[1] - 1, d_conv), zeros(bsz, ffn_conv_w.shape[1] - 1, d_up))
        dec_b = x_sample.shape[0]
        ck = cache_mem_k[l].reshape(dec_b, n_mem, d)
        cv = cache_mem_v[l].reshape(dec_b, n_mem, d)
        sample = (ys, PAST_LEN, ck, cv, state_conv_a[l], state_rglru[l], state_conv_b[l], state_ffn_conv[l])
        (ys, *s_states), (yp, *p_states) = _layer([sample, prompt], p)
        for o, v in zip(outs, (*p_states, mk.reshape(bsz, n_mem, N_XHEADS, -1),
                               mv.reshape(bsz, n_mem, N_XHEADS, -1), *s_states)):
            o.append(v)
    return (yp, ys) + tuple(jnp.stack(o) for o in outs)
```

```python
import functools

import jax
import jax.numpy as jnp
import numpy as np
from jax import lax
from jax.experimental import pallas as pl
from jax.experimental.pallas import tpu as pltpu

F32 = jnp.float32
BF16 = jnp.bfloat16

EPS = 1e-6
RG_C = 8.0
PAST_LEN = 2048

LANES = 128
SUBLANES = 8
BF16_ROWS = 16
VMEM_LIMIT_BYTES = 60 * 1024 * 1024

ROW_CHUNK = 128
CAST_ROWS = 256
SCAN_LANES = 2048


def _params(*semantics):
    return pltpu.CompilerParams(dimension_semantics=semantics, vmem_limit_bytes=VMEM_LIMIT_BYTES)


def _resident(shape):
    nd = len(shape)
    return pl.BlockSpec(shape, lambda *_: (0,) * nd, pipeline_mode=pl.Buffered(1))


def _rmsnorm(x, g):
    y = x * lax.rsqrt(jnp.mean(x * x, axis=-1, keepdims=True) + EPS)
    return y * g


def _sigmoid(x):
    return 0.5 * jnp.tanh(0.5 * x) + 0.5


def _gelu_tanh(x):
    c1 = (2.0 / jnp.pi) ** 0.5
    return x * (0.5 * jnp.tanh(x * (c1 + (c1 * 0.044715) * (x * x))) + 0.5)


def _segment_major_perm(n_seq, seq):
    n_groups = seq // SUBLANES
    row = np.arange(seq)
    time = (row % SUBLANES) * n_groups + row // SUBLANES
    p = np.zeros((seq, seq), np.float32)
    p[row, time] = 1.0
    return jnp.asarray(np.kron(np.eye(n_seq, dtype=np.float32), p), BF16)


def _seg_coords(n_groups, back):
    t = SUBLANES * n_groups - back
    return t % n_groups, t // n_groups


def _seg_hist_init(hist_ref, state_ref, n_groups):
    n_state = state_ref.shape[1]
    n_hist = hist_ref.shape[1] // SUBLANES
    hist_ref[...] = jnp.zeros(hist_ref.shape, F32)
    for back in range(1, n_state + 1):
        g, s = _seg_coords(n_groups, back)
        row = (g - (n_groups - n_hist)) * SUBLANES + s
        hist_ref[:, row:row + 1, :] = state_ref[:, n_state - back:n_state - back + 1, :]


def _seg_state_out(ns_ref, x_ref, cols=slice(None)):
    n_state = ns_ref.shape[1]
    n_groups = x_ref.shape[1] // SUBLANES
    for back in range(1, n_state + 1):
        g, s = _seg_coords(n_groups, back)
        row = g * SUBLANES + s
        ns_ref[:, n_state - back:n_state - back + 1, cols] = x_ref[:, row:row + 1, :]


def _seg_conv(x_ref, hist_ref, w_ref, bias):
    nb, seq, c = x_ref.shape
    n_groups = seq // SUBLANES
    n_hist = hist_ref.shape[1] // SUBLANES
    width = w_ref.shape[0]
    assert n_hist >= min(n_groups, width - 1)
    sub = lax.broadcasted_iota(jnp.int32, (1, SUBLANES, c), 1)
    y = None
    for k in range(width):
        lag = width - 1 - k
        if lag == 0:
            tap = x_ref[...]
        else:
            parts = []
            for g in range(min(lag, n_groups)):
                src = (g - lag) % n_groups
                crossed = (lag - g + n_groups - 1) // n_groups
                cur = x_ref[:, src * SUBLANES:(src + 1) * SUBLANES, :]
                h0 = (src - (n_groups - n_hist)) * SUBLANES
                prev = hist_ref[:, h0:h0 + SUBLANES, :]
                parts.append(pltpu.roll(jnp.where(sub >= SUBLANES - crossed, prev, cur), crossed, 1))
            if n_groups > lag:
                parts.append(x_ref[:, 0:(n_groups - lag) * SUBLANES, :])
            tap = jnp.concatenate(parts, axis=1)
        y = tap * w_ref[k:k + 1, :] if y is None else y + tap * w_ref[k:k + 1, :]
    return y if bias is None else y + bias


def _softplus(x):
    return jnp.maximum(x, 0.0) + jnp.log1p(jnp.exp(-jnp.abs(x)))


def _permute_rows(ref, perm_ref):
    blk = perm_ref.shape[0]
    for r in range(0, ref.shape[0], blk):
        ref[r:r + blk, :] = jnp.dot(perm_ref[...], ref[r:r + blk, :], preferred_element_type=F32).astype(BF16)


def _rmsnorm_kernel(x_ref, g_ref, *rest):
    u_ref = rest[-1]
    tm = x_ref.shape[0]
    chunk = min(ROW_CHUNK, tm)
    g = g_ref[...]

    def body(c, _):
        rows = pl.ds(pl.multiple_of(c * chunk, chunk), chunk)
        u_ref[rows, :] = _rmsnorm(x_ref[rows, :], g).astype(BF16)
        return None

    lax.fori_loop(0, tm // chunk, body, None)
    if len(rest) == 2:
        _permute_rows(u_ref, rest[0])


def _rmsnorm_call(x, g, *, tm, perm=None):
    m, d = x.shape
    tm = min(tm, m)
    assert m % tm == 0 and tm % min(ROW_CHUNK, tm) == 0 and (perm is None or tm % perm.shape[0] == 0)
    extra = [] if perm is None else [perm]
    return pl.pallas_call(
        _rmsnorm_kernel,
        grid=(m // tm,),
        in_specs=[pl.BlockSpec((tm, d), lambda i: (i, 0)), pl.BlockSpec((1, d), lambda i: (0, 0))]
        + [pl.BlockSpec(a.shape, lambda i: (0, 0)) for a in extra],
        out_specs=pl.BlockSpec((tm, d), lambda i: (i, 0)),
        out_shape=jax.ShapeDtypeStruct((m, d), BF16),
        compiler_params=_params("parallel"),
        name="rmsnorm",
    )(x, g, *extra)


def _cast_rows(src_ref, dst_ref, cols=slice(None)):
    rows_total = src_ref.shape[0]
    chunk = min(CAST_ROWS, rows_total)

    def body(c, _):
        rows = pl.ds(pl.multiple_of(c * chunk, chunk), chunk)
        dst_ref[rows, cols] = src_ref[rows, :].astype(BF16)
        return None

    lax.fori_loop(0, rows_total // chunk, body, None)


def _side_cast_plan(arrays, n_steps, flat_step):
    in_specs, out_specs, out_shape = [], [], []
    for a in arrays:
        rows, cols = a.shape
        chunk = next(c for c in range(BF16_ROWS, rows + 1, BF16_ROWS) if rows % c == 0 and rows // c <= n_steps)
        last = rows // chunk - 1
        spec = pl.BlockSpec((chunk, cols), lambda *idx, last=last: (jnp.minimum(flat_step(*idx), last), 0))
        in_specs.append(spec)
        out_specs.append(spec)
        out_shape.append(jax.ShapeDtypeStruct((rows, cols), BF16))
    return in_specs, out_specs, out_shape


def _side_cast(src_refs, dst_refs):
    for src, dst in zip(src_refs, dst_refs):
        dst[...] = src[...].astype(BF16)


def _group_steps(counts):
    starts = [sum(counts[:g]) for g in range(len(counts))]
    return starts, sum(counts)


def _group_tile(i, start, count):
    return jnp.clip(i - start, 0, count - 1)


def _proj_kernel(*refs, starts, counts):
    n_groups = len(counts)
    u_refs, w_ref, o_refs, w_scr = refs[:n_groups], refs[n_groups], refs[n_groups + 1:-1], refs[-1]
    i = pl.program_id(1)

    @pl.when(i == 0)
    def _():
        _cast_rows(w_ref, w_scr)

    for g in range(n_groups):
        @pl.when((i >= starts[g]) & (i < starts[g] + counts[g]))
        def _(g=g):
            o_refs[g][...] = jnp.dot(u_refs[g][...], w_scr[...],
                                     preferred_element_type=F32).astype(o_refs[g].dtype)


def _proj(us, w, out_dtype, *, tm, tn):
    d, n = w.shape
    tn = min(tn, n)
    tms = [min(tm, u.shape[0]) for u in us]
    counts = [u.shape[0] // t for u, t in zip(us, tms)]
    assert all(u.shape[0] % t == 0 for u, t in zip(us, tms)) and n % tn == 0 and d % min(CAST_ROWS, d) == 0
    starts, steps = _group_steps(counts)
    tile = lambda g: (lambda j, i: (_group_tile(i, starts[g], counts[g]), 0))
    out_tile = lambda g: (lambda j, i: (_group_tile(i, starts[g], counts[g]), j))
    return pl.pallas_call(
        functools.partial(_proj_kernel, starts=starts, counts=counts),
        grid=(n // tn, steps),
        in_specs=([pl.BlockSpec((tms[g], d), tile(g)) for g in range(len(us))]
                  + [pl.BlockSpec((d, tn), lambda j, i: (0, j))]),
        out_specs=[pl.BlockSpec((tms[g], tn), out_tile(g)) for g in range(len(us))],
        out_shape=[jax.ShapeDtypeStruct((u.shape[0], n), out_dtype) for u in us],
        scratch_shapes=[pltpu.VMEM((d, tn), BF16)],
        compiler_params=_params("arbitrary", "arbitrary"),
        name="proj",
    )(*us, w)


def _mixer_pre_kernel(*refs, d_rnn, d_conv, pos0, n_side):
    (z_ref, sa_ref, h0_ref, sb_ref, caw_ref, cab_ref, wri_ref, br_ref, bi_ref, lam_ref, cbw_ref) = refs[:11]
    side_in = refs[11:11 + n_side]
    ya_ref, yb_ref, nsa_ref, nh_ref, nsb_ref = refs[11 + n_side:16 + n_side]
    side_out = refs[16 + n_side:16 + 2 * n_side]
    xa_scr, ha_scr, xc_scr, xcb_scr, ri_scr, cb_scr, hb_scr, h_scr = refs[16 + 2 * n_side:]
    _side_cast(side_in, side_out)
    t = pl.program_id(1)
    nb, seq, _ = ya_ref.shape
    n_groups = seq // SUBLANES
    n_heads, head_dim, _ = wri_ref.shape

    @pl.when(t == 0)
    def _():
        _seg_hist_init(ha_scr, sa_ref, n_groups)
        _seg_hist_init(hb_scr, sb_ref, n_groups)
        h_scr[...] = jnp.broadcast_to(h0_ref[...], h_scr.shape)

    xa_scr[...] = z_ref[:, :, 0:d_rnn].astype(F32)
    xc = _seg_conv(xa_scr, ha_scr, caw_ref, cab_ref[...])
    xc_scr[...] = xc
    xcb_scr[...] = xc.reshape(nb * seq, d_rnn).astype(BF16)
    _seg_state_out(nsa_ref, xa_scr)
    ha_scr[...] = xa_scr[:, seq - ha_scr.shape[1]:, :]

    for h in range(n_heads):
        cols = slice(h * head_dim, (h + 1) * head_dim)
        ri = jnp.dot(xcb_scr[:, cols], wri_ref[h], preferred_element_type=F32)
        ri_scr[:, :, cols] = ri[:, :head_dim].reshape(nb, seq, head_dim)
        ri_scr[:, :, d_rnn + h * head_dim:d_rnn + (h + 1) * head_dim] = ri[:, head_dim:].reshape(nb, seq, head_dim)

    strip = max(LANES, SCAN_LANES // nb)
    pairs = seq // BF16_ROWS
    sub = lax.broadcasted_iota(jnp.int32, (1, SUBLANES, strip), 1)
    for c in range(d_rnn // strip):
        cs = slice(c * strip, (c + 1) * strip)
        cs_i = slice(d_rnn + c * strip, d_rnn + (c + 1) * strip)
        cs_g = slice(d_rnn + c * strip, d_rnn + (c + 1) * strip)
        rate = RG_C * _softplus(-lam_ref[:, cs])
        b_r = br_ref[:, cs]
        b_i = bi_ref[:, cs]

        def pass1(it, carry, first=False, cs=cs, cs_i=cs_i, rate=rate, b_r=b_r, b_i=b_i):
            h_loc, a_run = carry
            for half in range(BF16_ROWS // SUBLANES):
                rows = pl.ds(pl.multiple_of(it * BF16_ROWS + half * SUBLANES, SUBLANES), SUBLANES)
                r = _sigmoid(ri_scr[:, rows, cs] + b_r)
                i = _sigmoid(ri_scr[:, rows, cs_i] + b_i)
                neg_log_a = r * rate
                a = jnp.exp(-neg_log_a)
                mult = jnp.sqrt(jnp.tanh(neg_log_a) * (a * a + 1.0))
                if first and half == 0:
                    mult = jnp.where(jnp.logical_and(t == 0, sub == 0), 1.0, mult)
                b = mult * i * xc_scr[:, rows, cs]
                h_loc = a * h_loc + b
                a_run = a * a_run
                ri_scr[:, rows, cs] = h_loc
                ri_scr[:, rows, cs_i] = a_run
            return h_loc, a_run

        carry = (jnp.zeros((nb, SUBLANES, strip), F32), jnp.ones((nb, SUBLANES, strip), F32))
        start = 0
        if pos0 == 0:
            carry = pass1(0, carry, first=True)
            start = 1
        h_tot, a_tot = lax.fori_loop(start, pairs, pass1, carry)

        for s in (1, 2, 4):
            keep = sub >= s
            a_prev = pltpu.roll(a_tot, s, 1)
            h_prev = pltpu.roll(h_tot, s, 1)
            h_tot = jnp.where(keep, a_tot * h_prev + h_tot, h_tot)
            a_tot = jnp.where(keep, a_tot * a_prev, a_tot)
        h_in = h_scr[:, :, cs]
        seg_end = a_tot * h_in + h_tot
        seg_start = jnp.where(sub == 0, h_in, pltpu.roll(seg_end, 1, 1))
        h_last = jnp.broadcast_to(seg_end[:, SUBLANES - 1:SUBLANES, :], seg_end.shape)
        h_scr[:, :, cs] = h_last
        nh_ref[:, :, cs] = h_last[:, 0:1, :]
        start2 = jnp.concatenate([seg_start] * (BF16_ROWS // SUBLANES), axis=1)

        def pass2(it, _, cs=cs, cs_i=cs_i, cs_g=cs_g, start2=start2):
            rows = pl.ds(pl.multiple_of(it * BF16_ROWS, BF16_ROWS), BF16_ROWS)
            h = ri_scr[:, rows, cs] + ri_scr[:, rows, cs_i] * start2
            gate = _gelu_tanh(z_ref[:, rows, cs_g].astype(F32))
            ya_ref[:, rows, cs] = (h * gate).astype(BF16)
            return None

        lax.fori_loop(0, pairs, pass2, None)

    o_gb, o_gc, o_hb = 2 * d_rnn, 2 * d_rnn + d_conv, 2 * d_rnn + 2 * d_conv
    cb_scr[...] = z_ref[:, :, o_gc:o_gc + d_conv].astype(F32) * z_ref[:, :, o_hb:o_hb + d_conv].astype(F32)
    co = _seg_conv(cb_scr, hb_scr, cbw_ref, None)
    yb_ref[...] = (z_ref[:, :, o_gb:o_gb + d_conv].astype(F32) * co).astype(BF16)
    _seg_state_out(nsb_ref, cb_scr)
    hb_scr[...] = cb_scr[:, seq - hb_scr.shape[1]:, :]


def _mixer_pre(z, state_a, h0, state_b, conv_a_w, conv_a_b, w_ri, b_r, b_i, lam, conv_b_w, *, seq_tile, pos0,
               to_round=()):
    bsz, t, _ = z.shape
    d_rnn = conv_a_w.shape[1]
    d_conv = conv_b_w.shape[1]
    n_pre = 2 * d_rnn + 3 * d_conv
    nb = bsz if t <= seq_tile else 1
    seq = min(seq_tile, t)
    assert t % seq == 0 and seq % BF16_ROWS == 0 and bsz % nb == 0
    wa, wb = conv_a_w.shape[0], conv_b_w.shape[0]
    hist_a, hist_b = min(seq // SUBLANES, wa - 1), min(seq // SUBLANES, wb - 1)
    kern = functools.partial(_mixer_pre_kernel, d_rnn=d_rnn, d_conv=d_conv, pos0=pos0, n_side=len(to_round))
    n_t = t // seq
    side_in, side_out, side_shape = _side_cast_plan(to_round, (bsz // nb) * n_t, lambda b, i: b * n_t + i)
    row = lambda b, i: (b, i, 0)
    per_b = lambda b, i: (b, 0, 0)
    const2 = lambda b, i: (0, 0)
    return pl.pallas_call(
        kern,
        grid=(bsz // nb, t // seq),
        in_specs=[pl.BlockSpec((nb, seq, n_pre), row),
                  pl.BlockSpec((nb, wa - 1, d_rnn), per_b),
                  pl.BlockSpec((nb, 1, d_rnn), per_b),
                  pl.BlockSpec((nb, wb - 1, d_conv), per_b),
                  pl.BlockSpec(conv_a_w.shape, const2),
                  pl.BlockSpec((1, d_rnn), const2),
                  pl.BlockSpec(w_ri.shape, lambda b, i: (0, 0, 0)),
                  pl.BlockSpec((1, d_rnn), const2),
                  pl.BlockSpec((1, d_rnn), const2),
                  pl.BlockSpec((1, d_rnn), const2),
                  pl.BlockSpec(conv_b_w.shape, const2)] + side_in,
        out_specs=[pl.BlockSpec((nb, seq, d_rnn), row),
                   pl.BlockSpec((nb, seq, d_conv), row),
                   pl.BlockSpec((nb, wa - 1, d_rnn), per_b),
                   pl.BlockSpec((nb, 1, d_rnn), per_b),
                   pl.BlockSpec((nb, wb - 1, d_conv), per_b)] + side_out,
        out_shape=[jax.ShapeDtypeStruct((bsz, t, d_rnn), BF16),
                   jax.ShapeDtypeStruct((bsz, t, d_conv), BF16),
                   jax.ShapeDtypeStruct((bsz, wa - 1, d_rnn), F32),
                   jax.ShapeDtypeStruct((bsz, 1, d_rnn), F32),
                   jax.ShapeDtypeStruct((bsz, wb - 1, d_conv), F32)] + side_shape,
        scratch_shapes=[pltpu.VMEM((nb, seq, d_rnn), F32),
                        pltpu.VMEM((nb, hist_a * SUBLANES, d_rnn), F32),
                        pltpu.VMEM((nb, seq, d_rnn), F32),
                        pltpu.VMEM((nb * seq, d_rnn), BF16),
                        pltpu.VMEM((nb, seq, 2 * d_rnn), F32),
                        pltpu.VMEM((nb, seq, d_conv), F32),
                        pltpu.VMEM((nb, hist_b * SUBLANES, d_conv), F32),
                        pltpu.VMEM((nb, SUBLANES, d_rnn), F32)],
        compiler_params=_params("arbitrary", "arbitrary"),
        name="mixer_pre",
    )(z, state_a, h0, state_b, conv_a_w, conv_a_b, w_ri, b_r, b_i, lam, conv_b_w, *to_round)


def _mixer_post_kernel(ya_ref, yb_ref, ga0_ref, ga1_ref, gb0_ref, gb1_ref, x_ref, pa_ref, pb_ref, wo_ref, g_ref,
                       unperm_ref, o_ref, mix_scr):
    half = ga0_ref.shape[1]
    y_a = jnp.dot(ya_ref[...], pa_ref[...], preferred_element_type=F32)
    y_b = jnp.dot(yb_ref[...], pb_ref[...], preferred_element_type=F32)
    for c, (ga_ref, gb_ref) in enumerate(((ga0_ref, gb0_ref), (ga1_ref, gb1_ref))):
        cols = slice(c * half, (c + 1) * half)
        mix = (jax.nn.sigmoid(ga_ref[...].astype(F32)) * y_a[:, cols]
               + jax.nn.sigmoid(gb_ref[...].astype(F32)) * y_b[:, cols])
        mix_scr[:, cols] = mix.astype(BF16)
    _permute_rows(mix_scr, unperm_ref)
    out = jnp.dot(mix_scr[...], wo_ref[...], preferred_element_type=F32)
    o_ref[...] = x_ref[...] + _rmsnorm(out, g_ref[...])


def _mixer_post(ya, yb, z, x, p_a, p_b, w_o, g_post, unperm, *, tm):
    m, d = x.shape
    d_rnn, d_conv = ya.shape[1], yb.shape[1]
    tm = min(tm, m)
    half = d // 2
    gate0 = (2 * d_rnn + 3 * d_conv) // half
    assert m % tm == 0 and (2 * d_rnn + 3 * d_conv) % half == 0 and tm % unperm.shape[0] == 0
    gate_spec = lambda k: pl.BlockSpec((tm, half), lambda i: (i, gate0 + k))
    row = lambda i: (i, 0)
    return pl.pallas_call(
        _mixer_post_kernel,
        grid=(m // tm,),
        in_specs=[pl.BlockSpec((tm, d_rnn), row), pl.BlockSpec((tm, d_conv), row),
                  gate_spec(0), gate_spec(1), gate_spec(2), gate_spec(3),
                  pl.BlockSpec((tm, d), row),
                  _resident(p_a.shape), _resident(p_b.shape), _resident(w_o.shape),
                  pl.BlockSpec((1, d), lambda i: (0, 0)), pl.BlockSpec(unperm.shape, lambda i: (0, 0))],
        out_specs=pl.BlockSpec((tm, d), row),
        out_shape=jax.ShapeDtypeStruct((m, d), F32),
        scratch_shapes=[pltpu.VMEM((tm, d), BF16)],
        compiler_params=_params("parallel"),
        name="mixer_post",
    )(ya, yb, z, z, z, z, x, p_a, p_b, w_o, g_post, unperm)


def _xattn_kernel(x_ref, k_ref, v_ref, wq_ref, wxo_ref, gpre_ref, gpost_ref, gnext_ref, perm_ref, o_ref, u_ref,
                  q_scr, o_scr, *, n_heads):
    nb, seq, d = x_ref.shape
    hd = d // n_heads
    scale = hd ** -0.5
    x = x_ref[...].reshape(nb * seq, d)
    u = _rmsnorm(x, gpre_ref[...]).astype(BF16)
    q_scr[...] = jnp.dot(u, wq_ref[...], preferred_element_type=F32).astype(BF16)
    for b in range(nb):
        rows = slice(b * seq, (b + 1) * seq)
        for h in range(n_heads):
            cols = slice(h * hd, (h + 1) * hd)
            s = lax.dot_general(q_scr[rows, cols], k_ref[b, :, cols], (((1,), (1,)), ((), ())),
                                preferred_element_type=F32) * scale
            p = jnp.exp(s - jnp.max(s, axis=-1, keepdims=True))
            att = (p / jnp.sum(p, axis=-1, keepdims=True)).astype(BF16)
            o_scr[rows, cols] = jnp.dot(att, v_ref[b, :, cols], preferred_element_type=F32).astype(BF16)
    out = jnp.dot(o_scr[...], wxo_ref[...], preferred_element_type=F32)
    x_new = x + _rmsnorm(out, gpost_ref[...])
    o_ref[...] = x_new.reshape(nb, seq, d)
    u_next = _rmsnorm(x_new, gnext_ref[...]).astype(BF16)
    blk = perm_ref.shape[0]
    u_next = jnp.concatenate([jnp.dot(perm_ref[...], u_next[r:r + blk, :], preferred_element_type=F32)
                              for r in range(0, nb * seq, blk)], axis=0)
    u_ref[...] = u_next.astype(BF16).reshape(nb, seq, d)


def _xattn(x, mem_k, mem_v, w_q, w_xo, g_pre, g_post, g_next, perm, *, n_heads, seq_tile):
    bsz, t, d = x.shape
    n_mem = mem_k.shape[1]
    nb = bsz if t <= seq_tile else 1
    seq = min(seq_tile, t)
    assert t % seq == 0 and seq % BF16_ROWS == 0 and bsz % nb == 0 and (nb * seq) % perm.shape[0] == 0
    single = bsz // nb == 1
    mem_spec = (_resident((nb, n_mem, d)) if single else pl.BlockSpec((nb, n_mem, d), lambda b, i: (b, 0, 0)))
    row = lambda b, i: (b, i, 0)
    gain = pl.BlockSpec((1, d), lambda b, i: (0, 0))
    return pl.pallas_call(
        functools.partial(_xattn_kernel, n_heads=n_heads),
        grid=(bsz // nb, t // seq),
        in_specs=[pl.BlockSpec((nb, seq, d), row), mem_spec, mem_spec,
                  _resident(w_q.shape), _resident(w_xo.shape), gain, gain, gain,
                  pl.BlockSpec(perm.shape, lambda b, i: (0, 0))],
        out_specs=[pl.BlockSpec((nb, seq, d), row), pl.BlockSpec((nb, seq, d), row)],
        out_shape=[jax.ShapeDtypeStruct((bsz, t, d), F32), jax.ShapeDtypeStruct((bsz, t, d), BF16)],
        scratch_shapes=[pltpu.VMEM((nb * seq, d), BF16), pltpu.VMEM((nb * seq, d), BF16)],
        compiler_params=_params("parallel", "parallel"),
        name="xattn",
    )(x, mem_k, mem_v, w_q, w_xo, g_pre, g_post, g_next, perm)


def _up_geglu_kernel(*refs, starts, counts, blocks, n_side):
    n_groups = len(counts)
    ins, rest = refs[:3 * n_groups], refs[3 * n_groups:]
    wg_ref, wv_ref, cwg_ref, cwv_ref, cbg_ref, cbv_ref = rest[:6]
    side_in, rest = rest[6:6 + n_side], rest[6 + n_side:]
    outs, rest = rest[:3 * n_groups], rest[3 * n_groups:]
    side_out, scr = rest[:n_side], rest[n_side:]
    w_scr, hists = scr[0], scr[1:]
    _side_cast(side_in, side_out)
    i = pl.program_id(1)
    tn = wg_ref.shape[1]

    @pl.when(i == 0)
    def _():
        _cast_rows(wg_ref, w_scr, slice(0, tn))
        _cast_rows(wv_ref, w_scr, slice(tn, 2 * tn))

    for g in range(n_groups):
        u_ref, st_refs = ins[3 * g], ins[3 * g + 1:3 * g + 3]
        hid_ref, ns_refs = outs[3 * g], outs[3 * g + 1:3 * g + 3]
        p_scrs = hists[2 * g:2 * g + 2]

        @pl.when(i == starts[g])
        def _(st_refs=st_refs, p_scrs=p_scrs, blk=blocks[g]):
            for st_ref, p_scr in zip(st_refs, p_scrs):
                _seg_hist_init(p_scr, st_ref, blk // SUBLANES)

        @pl.when((i >= starts[g]) & (i < starts[g] + counts[g]))
        def _(u_ref=u_ref, hid_ref=hid_ref, ns_refs=ns_refs, p_scrs=p_scrs, blk=blocks[g]):
            nb, seq, k_dim = u_ref.shape
            per_seq = seq // blk
            n_hist = p_scrs[0].shape[1]
            both = jnp.dot(u_ref[...].reshape(nb * seq, k_dim), w_scr[...], preferred_element_type=F32)
            ys = []
            for half, (p_scr, cw_ref, cb_ref, ns_ref) in enumerate(zip(p_scrs, (cwg_ref, cwv_ref),
                                                                     (cbg_ref, cbv_ref), ns_refs)):
                a = both[:, half * tn:(half + 1) * tn].reshape(nb * per_seq, blk, tn)
                tails = a[:, blk - n_hist:, :].reshape(nb, per_seq, n_hist, tn)
                hist = p_scr[...][:, None]
                if per_seq > 1:
                    hist = jnp.concatenate([hist, tails[:, :per_seq - 1]], axis=1)
                hist = hist.reshape(nb * per_seq, n_hist, tn)
                ys.append(_seg_conv(a, hist, cw_ref, cb_ref[...]).reshape(nb, seq, tn))
                p_scr[...] = tails[:, per_seq - 1]
                _seg_state_out(ns_ref, a.reshape(nb, per_seq, blk, tn)[:, per_seq - 1])
            hid_ref[...] = (_gelu_tanh(ys[0]) * ys[1]).astype(BF16)


def _up_geglu(us, w_up, states, conv_w, conv_b, blocks, *, tm, tn, to_round=()):
    d = w_up.shape[0]
    d_ff = w_up.shape[1] // 2
    width = conv_w.shape[0]
    n_j = d_ff // tn
    nbs = [u.shape[0] if u.shape[1] <= tm else 1 for u in us]
    seqs = [min(tm, u.shape[1]) for u in us]
    counts = [u.shape[1] // s for u, s in zip(us, seqs)]
    assert all(u.shape[0] == nb and u.shape[1] % s == 0 and s % BF16_ROWS == 0 for u, nb, s in zip(us, nbs, seqs))
    assert d_ff % tn == 0 and tn % LANES == 0
    starts, steps = _group_steps(counts)
    col_g = lambda j, i: (0, j)
    col_v = lambda j, i: (0, n_j + j)
    st_g = lambda j, i: (0, 0, j)
    st_v = lambda j, i: (0, 0, n_j + j)
    in_specs, out_specs, out_shape, hist, args = [], [], [], [], []
    for g, (u, st, nb, seq) in enumerate(zip(us, states, nbs, seqs)):
        row = lambda j, i, g=g: (0, _group_tile(i, starts[g], counts[g]), 0)
        out = lambda j, i, g=g: (0, _group_tile(i, starts[g], counts[g]), j)
        in_specs += [pl.BlockSpec((nb, seq, d), row),
                     pl.BlockSpec((nb, width - 1, tn), st_g), pl.BlockSpec((nb, width - 1, tn), st_v)]
        out_specs += [pl.BlockSpec((nb, seq, tn), out),
                      pl.BlockSpec((nb, width - 1, tn), st_g), pl.BlockSpec((nb, width - 1, tn), st_g)]
        out_shape += [jax.ShapeDtypeStruct((u.shape[0], u.shape[1], d_ff), BF16),
                      jax.ShapeDtypeStruct((u.shape[0], width - 1, d_ff), F32),
                      jax.ShapeDtypeStruct((u.shape[0], width - 1, d_ff), F32)]
        assert seq % blocks[g] == 0 and blocks[g] % SUBLANES == 0
        n_hist = min(blocks[g] // SUBLANES, width - 1) * SUBLANES
        hist += [pltpu.VMEM((nb, n_hist, tn), F32), pltpu.VMEM((nb, n_hist, tn), F32)]
        args += [u, st, st]
    side_in, side_out, side_shape = _side_cast_plan(to_round, n_j * steps, lambda j, i: j * steps + i)
    in_specs += [pl.BlockSpec((d, tn), col_g), pl.BlockSpec((d, tn), col_v),
                 pl.BlockSpec((width, tn), col_g), pl.BlockSpec((width, tn), col_v),
                 pl.BlockSpec((1, tn), col_g), pl.BlockSpec((1, tn), col_v)] + side_in
    res = pl.pallas_call(
        functools.partial(_up_geglu_kernel, starts=starts, counts=counts, blocks=tuple(blocks),
                          n_side=len(to_round)),
        grid=(n_j, steps),
        in_specs=in_specs,
        out_specs=out_specs + side_out,
        out_shape=out_shape + side_shape,
        scratch_shapes=[pltpu.VMEM((d, 2 * tn), BF16)] + hist,
        compiler_params=_params("arbitrary", "arbitrary"),
        name="up_geglu",
    )(*args, w_up, w_up, conv_w, conv_w, conv_b, conv_b, *to_round)
    return [tuple(res[3 * g:3 * g + 3]) for g in range(len(us))], list(res[3 * len(us):])


def _ffn_down_kernel(*refs, starts, counts):
    n_groups = len(counts)
    wd_ref, g_ref = refs[3 * n_groups:3 * n_groups + 2]
    o_refs = refs[3 * n_groups + 2:]
    i = pl.program_id(0)
    for g in range(n_groups):
        hid_ref, x_ref, unperm_ref = refs[3 * g:3 * g + 3]

        @pl.when((i >= starts[g]) & (i < starts[g] + counts[g]))
        def _(hid_ref=hid_ref, x_ref=x_ref, unperm_ref=unperm_ref, o_ref=o_refs[g]):
            y = jnp.dot(hid_ref[...], wd_ref[...], preferred_element_type=F32)
            branch = _rmsnorm(y, g_ref[...]).astype(BF16)
            blk = unperm_ref.shape[0]
            for r in range(0, branch.shape[0], blk):
                o_ref[r:r + blk, :] = x_ref[r:r + blk, :] + jnp.dot(unperm_ref[...], branch[r:r + blk, :],
                                                                    preferred_element_type=F32)


def _ffn_down(hids, xs, w_down, g_post, unperms, *, tm):
    d_ff, d = w_down.shape
    tms = [min(tm, x.shape[0]) for x in xs]
    counts = [x.shape[0] // t for x, t in zip(xs, tms)]
    assert all(x.shape[0] % t == 0 and t % un.shape[0] == 0 for x, t, un in zip(xs, tms, unperms))
    starts, steps = _group_steps(counts)
    in_specs, args = [], []
    for g, (hid, x, un) in enumerate(zip(hids, xs, unperms)):
        row = lambda i, g=g: (_group_tile(i, starts[g], counts[g]), 0)
        in_specs += [pl.BlockSpec((tms[g], d_ff), row), pl.BlockSpec((tms[g], d), row),
                     pl.BlockSpec(un.shape, lambda i: (0, 0))]
        args += [hid, x, un]
    return pl.pallas_call(
        functools.partial(_ffn_down_kernel, starts=starts, counts=counts),
        grid=(steps,),
        in_specs=in_specs + [_resident(w_down.shape), pl.BlockSpec((1, d), lambda i: (0, 0))],
        out_specs=[pl.BlockSpec((tms[g], d), lambda i, g=g: (_group_tile(i, starts[g], counts[g]), 0))
                   for g in range(len(xs))],
        out_shape=[jax.ShapeDtypeStruct(x.shape, F32) for x in xs],
        compiler_params=_params("arbitrary"),
        name="ffn_down",
    )(*args, w_down, g_post)


MM_TM, MM_TN = 2048, 1024
UP_TM, UP_TN = 1024, 512
SEQ_TILE = 256
ROW_TILE = 512
NORM_TM = 1024
N_XHEADS = 4


def _layer(groups, p):
    flat = [x.reshape(-1, x.shape[-1]) for x, *_ in groups]
    perms, blocks = [], []
    for x, *_ in groups:
        bsz, t, _ = x.shape
        perms.append(_segment_major_perm(*((1, SEQ_TILE) if t > SEQ_TILE else (bsz, t))))
        blocks.append(min(t, SEQ_TILE))
    u0s = [_rmsnorm_call(x2, p['g_mix_pre'], tm=NORM_TM, perm=pm) for x2, pm in zip(flat, perms)]
    raw_mems, u0s = lax.optimization_barrier(([m for g in groups for m in g[2:4]], u0s))
    mems = [m.reshape(-1, flat[0].shape[-1]) for m in raw_mems]
    zs = _proj(u0s, p['w_in'], BF16, tm=MM_TM, tn=MM_TN)
    big = max(range(len(groups)), key=lambda g: flat[g].shape[0])
    resident = [p['p_a'], p['p_b'], p['w_o'], p['w_q'], p['w_xo']]
    pre = []
    for g, ((x, pos0, _, _, s_a, h0, s_b, _), z) in enumerate(zip(groups, zs)):
        bsz, t, _ = x.shape
        pre.append(_mixer_pre(z.reshape(bsz, t, -1), s_a, h0[:, None, :], s_b,
                              p['conv_a_w'], p['conv_a_b'], p['w_ri'], p['b_r'], p['b_i'],
                              p['lru_lambda'], p['conv_b_w'], seq_tile=SEQ_TILE, pos0=pos0,
                              to_round=resident + mems if g == big else ()))
    p_a, p_b, w_o, w_q, w_xo, *mems = pre[big][5:]
    mids = []
    for g, ((x, *_), x2, z) in enumerate(zip(groups, flat, zs)):
        bsz, t, d = x.shape
        m = bsz * t
        ya, yb, ns_a, nh, ns_b = pre[g][:5]
        mem_k, mem_v = (mm.reshape(bsz, -1, d) for mm in mems[2 * g:2 * g + 2])
        x1 = _mixer_post(ya.reshape(m, -1), yb.reshape(m, -1), z, x2, p_a, p_b, w_o, p['g_mix_post'],
                         perms[g].T, tm=ROW_TILE)
        x2a, u3 = _xattn(x1.reshape(bsz, t, d), mem_k, mem_v, w_q, w_xo, p['g_x_pre'], p['g_x_post'],
                         p['g_ffn_pre'], perms[g], n_heads=N_XHEADS, seq_tile=ROW_TILE)
        mids.append((x2a, u3, ns_a, nh[:, 0, :], ns_b))
    ffn, (w_down,) = _up_geglu([u3 for _, u3, *_ in mids], p['w_up'], [g[7] for g in groups],
                               p['ffn_conv_w'], p['ffn_conv_b'], blocks, tm=UP_TM, tn=UP_TN,
                               to_round=[p['w_down']])
    x3s = _ffn_down([hid.reshape(-1, hid.shape[-1]) for hid, _, _ in ffn],
                    [x2a.reshape(-1, x2a.shape[-1]) for x2a, *_ in mids], w_down, p['g_ffn_post'],
                    [pm.T for pm in perms], tm=ROW_TILE)
    return [(x3.reshape(x2a.shape), ns_a, nh, ns_b, jnp.concatenate([ns_fg, ns_fv], axis=-1))
            for x3, (x2a, _, ns_a, nh, ns_b), (_, ns_fg, ns_fv) in zip(x3s, mids, ffn)]


def kernel(x_prompt, x_sample, mem_prompt, state_conv_a, state_rglru, state_conv_b, state_ffn_conv, cache_mem_k, cache_mem_v, g_mix_pre, g_mix_post, w_in, conv_a_w, conv_a_b, w_r, b_r, w_i, b_i, lru_lambda, conv_b_w, p_a, p_b, w_o, g_x_pre, g_x_post, g_mem, w_q, w_k, w_v, w_xo, g_ffn_pre, g_ffn_post, w_up, ffn_conv_w, ffn_conv_b, w_down):
    depth = w_in.shape[0]
    bsz, _, d = x_prompt.shape
    n_mem = mem_prompt.shape[1]
    yp, ys = x_prompt, x_sample
    outs = [[] for _ in range(10)]
    row = lambda v: v.reshape(1, -1).astype(F32)
    for l in range(depth):
        p = {'g_mix_pre': row(g_mix_pre[l]), 'g_mix_post': row(g_mix_post[l]), 'w_in': w_in[l],
             'conv_a_w': conv_a_w[l], 'conv_a_b': row(conv_a_b[l]),
             'w_ri': jnp.concatenate([w_r[l], w_i[l]], axis=-1).astype(BF16),
             'b_r': row(b_r[l]), 'b_i': row(b_i[l]), 'lru_lambda': row(lru_lambda[l]),
             'conv_b_w': conv_b_w[l], 'p_a': p_a[l], 'p_b': p_b[l],
             'w_o': w_o[l], 'g_x_pre': row(g_x_pre[l]), 'g_x_post': row(g_x_post[l]),
             'w_q': w_q[l], 'w_xo': w_xo[l],
             'g_ffn_pre': row(g_ffn_pre[l]), 'g_ffn_post': row(g_ffn_post[l]), 'w_up': w_up[l],
             'ffn_conv_w': ffn_conv_w[l], 'ffn_conv_b': row(ffn_conv_b[l]), 'w_down': w_down[l]}
        d_rnn, d_conv, d_up = conv_a_w.shape[2], conv_b_w.shape[2], ffn_conv_w.shape[2]
        mem_u = _rmsnorm_call(mem_prompt.reshape(bsz * n_mem, d), row(g_mem[l]), tm=NORM_TM)
        mk = _proj([mem_u], w_k[l], F32, tm=MM_TM, tn=MM_TN)[0].reshape(bsz, n_mem, d)
        mv = _proj([mem_u], w_v[l], F32, tm=MM_TM, tn=MM_TN)[0].reshape(bsz, n_mem, d)
        zeros = lambda *s: jnp.zeros(s, F32)
        prompt = (yp, 0, mk, mv,
                  zeros(bsz, conv_a_w.shape[1] - 1, d_rnn), zeros(bsz, d_rnn),
                  zeros(bsz, conv_b_w.shape[1] - 1, d_conv), zeros(bsz, ffn_conv_w.shape[1] - 1, d_up))
        sample = (ys, PAST_LEN, cache_mem_k[l], cache_mem_v[l], state_conv_a[l], state_rglru[l], state_conv_b[l],
                  state_ffn_conv[l])
        (ys, *s_states), (yp, *p_states) = _layer([sample, prompt], p)
        for o, v in zip(outs, (*p_states, mk.reshape(bsz, n_mem, N_XHEADS, -1),
                               mv.reshape(bsz, n_mem, N_XHEADS, -1), *s_states)):
            o.append(v)
    return (yp, ys) + tuple(jnp.stack(o) for o in outs)
```

```python
import functools

import jax
import jax.numpy as jnp
import numpy as np
from jax import lax
from jax.experimental import pallas as pl
from jax.experimental.pallas import tpu as pltpu

F32 = jnp.float32
BF16 = jnp.bfloat16

EPS = 1e-6
RG_C = 8.0
PAST_LEN = 2048

LANES = 128
SUBLANES = 8
BF16_ROWS = 16
VMEM_LIMIT_BYTES = 60 * 1024 * 1024

ROW_CHUNK = 128
CAST_ROWS = 256
SCAN_LANES = 2048


def _params(*semantics):
    return pltpu.CompilerParams(dimension_semantics=semantics, vmem_limit_bytes=VMEM_LIMIT_BYTES)


def _resident(shape):
    nd = len(shape)
    return pl.BlockSpec(shape, lambda *_: (0,) * nd, pipeline_mode=pl.Buffered(1))


def _rmsnorm(x, g):
    y = x * lax.rsqrt(jnp.mean(x * x, axis=-1, keepdims=True) + EPS)
    return y * g


def _sigmoid(x):
    return 0.5 * jnp.tanh(0.5 * x) + 0.5


def _gelu_tanh(x):
    c1 = (2.0 / jnp.pi) ** 0.5
    return x * (0.5 * jnp.tanh(x * (c1 + (c1 * 0.044715) * (x * x))) + 0.5)


def _segment_major_perm(n_seq, seq):
    n_groups = seq // SUBLANES
    row = np.arange(seq)
    time = (row % SUBLANES) * n_groups + row // SUBLANES
    p = np.zeros((seq, seq), np.float32)
    p[row, time] = 1.0
    return jnp.asarray(np.kron(np.eye(n_seq, dtype=np.float32), p), BF16)


def _seg_coords(n_groups, back):
    t = SUBLANES * n_groups - back
    return t % n_groups, t // n_groups


def _seg_hist_init(hist_ref, state_ref, n_groups):
    n_state = state_ref.shape[1]
    n_hist = hist_ref.shape[1] // SUBLANES
    hist_ref[...] = jnp.zeros(hist_ref.shape, F32)
    for back in range(1, n_state + 1):
        g, s = _seg_coords(n_groups, back)
        row = (g - (n_groups - n_hist)) * SUBLANES + s
        hist_ref[:, row:row + 1, :] = state_ref[:, n_state - back:n_state - back + 1, :]


def _seg_state_out(ns_ref, x_ref, cols=slice(None)):
    n_state = ns_ref.shape[1]
    n_groups = x_ref.shape[1] // SUBLANES
    for back in range(1, n_state + 1):
        g, s = _seg_coords(n_groups, back)
        row = g * SUBLANES + s
        ns_ref[:, n_state - back:n_state - back + 1, cols] = x_ref[:, row:row + 1, :]


def _seg_conv(x_ref, hist_ref, w_ref, bias):
    nb, seq, c = x_ref.shape
    n_groups = seq // SUBLANES
    n_hist = hist_ref.shape[1] // SUBLANES
    width = w_ref.shape[0]
    assert n_hist >= min(n_groups, width - 1)
    sub = lax.broadcasted_iota(jnp.int32, (1, SUBLANES, c), 1)
    y = None
    for k in range(width):
        lag = width - 1 - k
        if lag == 0:
            tap = x_ref[...]
        else:
            parts = []
            for g in range(min(lag, n_groups)):
                src = (g - lag) % n_groups
                crossed = (lag - g + n_groups - 1) // n_groups
                cur = x_ref[:, src * SUBLANES:(src + 1) * SUBLANES, :]
                h0 = (src - (n_groups - n_hist)) * SUBLANES
                prev = hist_ref[:, h0:h0 + SUBLANES, :]
                parts.append(pltpu.roll(jnp.where(sub >= SUBLANES - crossed, prev, cur), crossed, 1))
            if n_groups > lag:
                parts.append(x_ref[:, 0:(n_groups - lag) * SUBLANES, :])
            tap = jnp.concatenate(parts, axis=1)
        y = tap * w_ref[k:k + 1, :] if y is None else y + tap * w_ref[k:k + 1, :]
    return y if bias is None else y + bias


def _softplus(x):
    return jnp.maximum(x, 0.0) + jnp.log1p(jnp.exp(-jnp.abs(x)))


def _permute_rows(ref, perm_ref):
    blk = perm_ref.shape[0]
    for r in range(0, ref.shape[0], blk):
        ref[r:r + blk, :] = jnp.dot(perm_ref[...], ref[r:r + blk, :], preferred_element_type=F32).astype(BF16)


def _rmsnorm_kernel(x_ref, g_ref, *rest):
    u_ref = rest[-1]
    tm = x_ref.shape[0]
    chunk = min(ROW_CHUNK, tm)
    g = g_ref[...]

    def body(c, _):
        rows = pl.ds(pl.multiple_of(c * chunk, chunk), chunk)
        u_ref[rows, :] = _rmsnorm(x_ref[rows, :], g).astype(BF16)
        return None

    lax.fori_loop(0, tm // chunk, body, None)
    if len(rest) == 2:
        _permute_rows(u_ref, rest[0])


def _rmsnorm_call(x, g, *, tm, perm=None):
    m, d = x.shape
    tm = min(tm, m)
    assert m % tm == 0 and tm % min(ROW_CHUNK, tm) == 0 and (perm is None or tm % perm.shape[0] == 0)
    extra = [] if perm is None else [perm]
    return pl.pallas_call(
        _rmsnorm_kernel,
        grid=(m // tm,),
        in_specs=[pl.BlockSpec((tm, d), lambda i: (i, 0)), pl.BlockSpec((1, d), lambda i: (0, 0))]
        + [pl.BlockSpec(a.shape, lambda i: (0, 0)) for a in extra],
        out_specs=pl.BlockSpec((tm, d), lambda i: (i, 0)),
        out_shape=jax.ShapeDtypeStruct((m, d), BF16),
        compiler_params=_params("parallel"),
        name="rmsnorm",
    )(x, g, *extra)


def _cast_rows(src_ref, dst_ref, cols=slice(None)):
    rows_total = src_ref.shape[0]
    chunk = min(CAST_ROWS, rows_total)

    def body(c, _):
        rows = pl.ds(pl.multiple_of(c * chunk, chunk), chunk)
        dst_ref[rows, cols] = src_ref[rows, :].astype(BF16)
        return None

    lax.fori_loop(0, rows_total // chunk, body, None)


def _side_cast_plan(arrays, n_steps, flat_step):
    in_specs, out_specs, out_shape = [], [], []
    for a in arrays:
        rows, cols = a.shape
        chunk = next(c for c in range(BF16_ROWS, rows + 1, BF16_ROWS) if rows % c == 0 and rows // c <= n_steps)
        last = rows // chunk - 1
        spec = pl.BlockSpec((chunk, cols), lambda *idx, last=last: (jnp.minimum(flat_step(*idx), last), 0))
        in_specs.append(spec)
        out_specs.append(spec)
        out_shape.append(jax.ShapeDtypeStruct((rows, cols), BF16))
    return in_specs, out_specs, out_shape


def _side_cast(src_refs, dst_refs):
    for src, dst in zip(src_refs, dst_refs):
        dst[...] = src[...].astype(BF16)


def _group_steps(counts):
    starts = [sum(counts[:g]) for g in range(len(counts))]
    return starts, sum(counts)


def _group_tile(i, start, count):
    return jnp.clip(i - start, 0, count - 1)


_COLUMN_ACTS = {'id': lambda a: a, 'gelu': _gelu_tanh}


def _proj_kernel(*refs, starts, counts, col_acts):
    n_groups = len(counts)
    u_refs, w_ref, o_refs, w_scr = refs[:n_groups], refs[n_groups], refs[n_groups + 1:-1], refs[-1]
    j = pl.program_id(0)
    i = pl.program_id(1)

    @pl.when(i == 0)
    def _():
        _cast_rows(w_ref, w_scr)

    for g in range(n_groups):
        in_group = (i >= starts[g]) & (i < starts[g] + counts[g])
        for act in sorted(set(col_acts)):
            on_tile = functools.reduce(jnp.logical_or, [j == t for t, a in enumerate(col_acts) if a == act])

            @pl.when(in_group & on_tile)
            def _(g=g, act=act):
                acc = jnp.dot(u_refs[g][...], w_scr[...], preferred_element_type=F32)
                o_refs[g][...] = _COLUMN_ACTS[act](acc).astype(o_refs[g].dtype)


def _proj(us, w, out_dtype, *, tm, tn, col_acts=None):
    d, n = w.shape
    tn = min(tn, n)
    tms = [min(tm, u.shape[0]) for u in us]
    counts = [u.shape[0] // t for u, t in zip(us, tms)]
    assert all(u.shape[0] % t == 0 for u, t in zip(us, tms)) and n % tn == 0 and d % min(CAST_ROWS, d) == 0
    col_acts = tuple(col_acts) if col_acts is not None else ('id',) * (n // tn)
    assert len(col_acts) == n // tn
    starts, steps = _group_steps(counts)
    tile = lambda g: (lambda j, i: (_group_tile(i, starts[g], counts[g]), 0))
    out_tile = lambda g: (lambda j, i: (_group_tile(i, starts[g], counts[g]), j))
    return pl.pallas_call(
        functools.partial(_proj_kernel, starts=starts, counts=counts, col_acts=col_acts),
        grid=(n // tn, steps),
        in_specs=([pl.BlockSpec((tms[g], d), tile(g)) for g in range(len(us))]
                  + [pl.BlockSpec((d, tn), lambda j, i: (0, j))]),
        out_specs=[pl.BlockSpec((tms[g], tn), out_tile(g)) for g in range(len(us))],
        out_shape=[jax.ShapeDtypeStruct((u.shape[0], n), out_dtype) for u in us],
        scratch_shapes=[pltpu.VMEM((d, tn), BF16)],
        compiler_params=_params("arbitrary", "arbitrary"),
        name="proj",
    )(*us, w)


def _mixer_pre_kernel(*refs, d_rnn, d_conv, pos0, n_side):
    (z_ref, sa_ref, h0_ref, sb_ref, caw_ref, cab_ref, wri_ref, br_ref, bi_ref, lam_ref, cbw_ref) = refs[:11]
    side_in = refs[11:11 + n_side]
    ya_ref, yb_ref, nsa_ref, nh_ref, nsb_ref = refs[11 + n_side:16 + n_side]
    side_out = refs[16 + n_side:16 + 2 * n_side]
    xa_scr, ha_scr, xc_scr, xcb_scr, ri_scr, cb_scr, hb_scr, h_scr = refs[16 + 2 * n_side:]
    _side_cast(side_in, side_out)
    t = pl.program_id(1)
    nb, seq, _ = ya_ref.shape
    n_groups = seq // SUBLANES
    n_heads, head_dim, _ = wri_ref.shape

    @pl.when(t == 0)
    def _():
        _seg_hist_init(ha_scr, sa_ref, n_groups)
        _seg_hist_init(hb_scr, sb_ref, n_groups)
        h_scr[...] = jnp.broadcast_to(h0_ref[...], h_scr.shape)

    xa_scr[...] = z_ref[:, :, 0:d_rnn].astype(F32)
    xc = _seg_conv(xa_scr, ha_scr, caw_ref, cab_ref[...])
    xc_scr[...] = xc
    xcb_scr[...] = xc.reshape(nb * seq, d_rnn).astype(BF16)
    _seg_state_out(nsa_ref, xa_scr)
    ha_scr[...] = xa_scr[:, seq - ha_scr.shape[1]:, :]

    for h in range(n_heads):
        cols = slice(h * head_dim, (h + 1) * head_dim)
        ri = jnp.dot(xcb_scr[:, cols], wri_ref[h], preferred_element_type=F32)
        ri_scr[:, :, cols] = ri[:, :head_dim].reshape(nb, seq, head_dim)
        ri_scr[:, :, d_rnn + h * head_dim:d_rnn + (h + 1) * head_dim] = ri[:, head_dim:].reshape(nb, seq, head_dim)

    strip = max(LANES, SCAN_LANES // nb)
    pairs = seq // BF16_ROWS
    sub = lax.broadcasted_iota(jnp.int32, (1, SUBLANES, strip), 1)
    for c in range(d_rnn // strip):
        cs = slice(c * strip, (c + 1) * strip)
        cs_i = slice(d_rnn + c * strip, d_rnn + (c + 1) * strip)
        cs_g = slice(d_rnn + c * strip, d_rnn + (c + 1) * strip)
        rate = RG_C * _softplus(-lam_ref[:, cs])
        b_r = br_ref[:, cs]
        b_i = bi_ref[:, cs]

        def pass1(it, carry, first=False, cs=cs, cs_i=cs_i, rate=rate, b_r=b_r, b_i=b_i):
            h_loc, a_run = carry
            for half in range(BF16_ROWS // SUBLANES):
                rows = pl.ds(pl.multiple_of(it * BF16_ROWS + half * SUBLANES, SUBLANES), SUBLANES)
                r = _sigmoid(ri_scr[:, rows, cs] + b_r)
                i = _sigmoid(ri_scr[:, rows, cs_i] + b_i)
                neg_log_a = r * rate
                a = jnp.exp(-neg_log_a)
                mult = jnp.sqrt(jnp.tanh(neg_log_a) * (a * a + 1.0))
                if first and half == 0:
                    mult = jnp.where(jnp.logical_and(t == 0, sub == 0), 1.0, mult)
                b = mult * i * xc_scr[:, rows, cs]
                h_loc = a * h_loc + b
                a_run = a * a_run
                ri_scr[:, rows, cs] = h_loc
                ri_scr[:, rows, cs_i] = a_run
            return h_loc, a_run

        carry = (jnp.zeros((nb, SUBLANES, strip), F32), jnp.ones((nb, SUBLANES, strip), F32))
        start = 0
        if pos0 == 0:
            carry = pass1(0, carry, first=True)
            start = 1
        h_tot, a_tot = lax.fori_loop(start, pairs, pass1, carry)

        for s in (1, 2, 4):
            keep = sub >= s
            a_prev = pltpu.roll(a_tot, s, 1)
            h_prev = pltpu.roll(h_tot, s, 1)
            h_tot = jnp.where(keep, a_tot * h_prev + h_tot, h_tot)
            a_tot = jnp.where(keep, a_tot * a_prev, a_tot)
        h_in = h_scr[:, :, cs]
        seg_end = a_tot * h_in + h_tot
        seg_start = jnp.where(sub == 0, h_in, pltpu.roll(seg_end, 1, 1))
        h_last = jnp.broadcast_to(seg_end[:, SUBLANES - 1:SUBLANES, :], seg_end.shape)
        h_scr[:, :, cs] = h_last
        nh_ref[:, :, cs] = h_last[:, 0:1, :]
        start2 = jnp.concatenate([seg_start] * (BF16_ROWS // SUBLANES), axis=1)

        def pass2(it, _, cs=cs, cs_i=cs_i, cs_g=cs_g, start2=start2):
            rows = pl.ds(pl.multiple_of(it * BF16_ROWS, BF16_ROWS), BF16_ROWS)
            h = ri_scr[:, rows, cs] + ri_scr[:, rows, cs_i] * start2
            ya_ref[:, rows, cs] = (h * z_ref[:, rows, cs_g].astype(F32)).astype(BF16)
            return None

        lax.fori_loop(0, pairs, pass2, None)

    o_gb, o_gc, o_hb = 2 * d_rnn, 2 * d_rnn + d_conv, 2 * d_rnn + 2 * d_conv
    cb_scr[...] = z_ref[:, :, o_gc:o_gc + d_conv].astype(F32) * z_ref[:, :, o_hb:o_hb + d_conv].astype(F32)
    co = _seg_conv(cb_scr, hb_scr, cbw_ref, None)
    yb_ref[...] = (z_ref[:, :, o_gb:o_gb + d_conv].astype(F32) * co).astype(BF16)
    _seg_state_out(nsb_ref, cb_scr)
    hb_scr[...] = cb_scr[:, seq - hb_scr.shape[1]:, :]


def _mixer_pre(z, state_a, h0, state_b, conv_a_w, conv_a_b, w_ri, b_r, b_i, lam, conv_b_w, *, seq_tile, pos0,
               to_round=()):
    bsz, t, _ = z.shape
    d_rnn = conv_a_w.shape[1]
    d_conv = conv_b_w.shape[1]
    n_pre = 2 * d_rnn + 3 * d_conv
    nb = bsz if t <= seq_tile else 1
    seq = min(seq_tile, t)
    assert t % seq == 0 and seq % BF16_ROWS == 0 and bsz % nb == 0
    wa, wb = conv_a_w.shape[0], conv_b_w.shape[0]
    hist_a, hist_b = min(seq // SUBLANES, wa - 1), min(seq // SUBLANES, wb - 1)
    kern = functools.partial(_mixer_pre_kernel, d_rnn=d_rnn, d_conv=d_conv, pos0=pos0, n_side=len(to_round))
    n_t = t // seq
    side_in, side_out, side_shape = _side_cast_plan(to_round, (bsz // nb) * n_t, lambda b, i: b * n_t + i)
    row = lambda b, i: (b, i, 0)
    per_b = lambda b, i: (b, 0, 0)
    const2 = lambda b, i: (0, 0)
    return pl.pallas_call(
        kern,
        grid=(bsz // nb, t // seq),
        in_specs=[pl.BlockSpec((nb, seq, n_pre), row),
                  pl.BlockSpec((nb, wa - 1, d_rnn), per_b),
                  pl.BlockSpec((nb, 1, d_rnn), per_b),
                  pl.BlockSpec((nb, wb - 1, d_conv), per_b),
                  pl.BlockSpec(conv_a_w.shape, const2),
                  pl.BlockSpec((1, d_rnn), const2),
                  pl.BlockSpec(w_ri.shape, lambda b, i: (0, 0, 0)),
                  pl.BlockSpec((1, d_rnn), const2),
                  pl.BlockSpec((1, d_rnn), const2),
                  pl.BlockSpec((1, d_rnn), const2),
                  pl.BlockSpec(conv_b_w.shape, const2)] + side_in,
        out_specs=[pl.BlockSpec((nb, seq, d_rnn), row),
                   pl.BlockSpec((nb, seq, d_conv), row),
                   pl.BlockSpec((nb, wa - 1, d_rnn), per_b),
                   pl.BlockSpec((nb, 1, d_rnn), per_b),
                   pl.BlockSpec((nb, wb - 1, d_conv), per_b)] + side_out,
        out_shape=[jax.ShapeDtypeStruct((bsz, t, d_rnn), BF16),
                   jax.ShapeDtypeStruct((bsz, t, d_conv), BF16),
                   jax.ShapeDtypeStruct((bsz, wa - 1, d_rnn), F32),
                   jax.ShapeDtypeStruct((bsz, 1, d_rnn), F32),
                   jax.ShapeDtypeStruct((bsz, wb - 1, d_conv), F32)] + side_shape,
        scratch_shapes=[pltpu.VMEM((nb, seq, d_rnn), F32),
                        pltpu.VMEM((nb, hist_a * SUBLANES, d_rnn), F32),
                        pltpu.VMEM((nb, seq, d_rnn), F32),
                        pltpu.VMEM((nb * seq, d_rnn), BF16),
                        pltpu.VMEM((nb, seq, 2 * d_rnn), F32),
                        pltpu.VMEM((nb, seq, d_conv), F32),
                        pltpu.VMEM((nb, hist_b * SUBLANES, d_conv), F32),
                        pltpu.VMEM((nb, SUBLANES, d_rnn), F32)],
        compiler_params=_params("arbitrary", "arbitrary"),
        name="mixer_pre",
    )(z, state_a, h0, state_b, conv_a_w, conv_a_b, w_ri, b_r, b_i, lam, conv_b_w, *to_round)


def _mixer_post_kernel(ya_ref, yb_ref, ga0_ref, ga1_ref, gb0_ref, gb1_ref, x_ref, pa_ref, pb_ref, wo_ref, g_ref,
                       unperm_ref, o_ref, mix_scr):
    half = ga0_ref.shape[1]
    y_a = jnp.dot(ya_ref[...], pa_ref[...], preferred_element_type=F32)
    y_b = jnp.dot(yb_ref[...], pb_ref[...], preferred_element_type=F32)
    for c, (ga_ref, gb_ref) in enumerate(((ga0_ref, gb0_ref), (ga1_ref, gb1_ref))):
        cols = slice(c * half, (c + 1) * half)
        mix = (jax.nn.sigmoid(ga_ref[...].astype(F32)) * y_a[:, cols]
               + jax.nn.sigmoid(gb_ref[...].astype(F32)) * y_b[:, cols])
        mix_scr[:, cols] = mix.astype(BF16)
    _permute_rows(mix_scr, unperm_ref)
    out = jnp.dot(mix_scr[...], wo_ref[...], preferred_element_type=F32)
    o_ref[...] = x_ref[...] + _rmsnorm(out, g_ref[...])


def _mixer_post(ya, yb, z, x, p_a, p_b, w_o, g_post, unperm, *, tm):
    m, d = x.shape
    d_rnn, d_conv = ya.shape[1], yb.shape[1]
    tm = min(tm, m)
    half = d // 2
    gate0 = (2 * d_rnn + 3 * d_conv) // half
    assert m % tm == 0 and (2 * d_rnn + 3 * d_conv) % half == 0 and tm % unperm.shape[0] == 0
    gate_spec = lambda k: pl.BlockSpec((tm, half), lambda i: (i, gate0 + k))
    row = lambda i: (i, 0)
    return pl.pallas_call(
        _mixer_post_kernel,
        grid=(m // tm,),
        in_specs=[pl.BlockSpec((tm, d_rnn), row), pl.BlockSpec((tm, d_conv), row),
                  gate_spec(0), gate_spec(1), gate_spec(2), gate_spec(3),
                  pl.BlockSpec((tm, d), row),
                  _resident(p_a.shape), _resident(p_b.shape), _resident(w_o.shape),
                  pl.BlockSpec((1, d), lambda i: (0, 0)), pl.BlockSpec(unperm.shape, lambda i: (0, 0))],
        out_specs=pl.BlockSpec((tm, d), row),
        out_shape=jax.ShapeDtypeStruct((m, d), F32),
        scratch_shapes=[pltpu.VMEM((tm, d), BF16)],
        compiler_params=_params("parallel"),
        name="mixer_post",
    )(ya, yb, z, z, z, z, x, p_a, p_b, w_o, g_post, unperm)


def _xattn_kernel(x_ref, k_ref, v_ref, wq_ref, wxo_ref, gpre_ref, gpost_ref, gnext_ref, perm_ref, o_ref, u_ref,
                  q_scr, o_scr, *, n_heads):
    nb, seq, d = x_ref.shape
    hd = d // n_heads
    scale = hd ** -0.5
    x = x_ref[...].reshape(nb * seq, d)
    u = _rmsnorm(x, gpre_ref[...]).astype(BF16)
    q_scr[...] = jnp.dot(u, wq_ref[...], preferred_element_type=F32).astype(BF16)
    for b in range(nb):
        rows = slice(b * seq, (b + 1) * seq)
        for h in range(n_heads):
            cols = slice(h * hd, (h + 1) * hd)
            s = lax.dot_general(q_scr[rows, cols], k_ref[b, :, cols], (((1,), (1,)), ((), ())),
                                preferred_element_type=F32) * scale
            p = jnp.exp(s - jnp.max(s, axis=-1, keepdims=True))
            att = (p / jnp.sum(p, axis=-1, keepdims=True)).astype(BF16)
            o_scr[rows, cols] = jnp.dot(att, v_ref[b, :, cols], preferred_element_type=F32).astype(BF16)
    out = jnp.dot(o_scr[...], wxo_ref[...], preferred_element_type=F32)
    x_new = x + _rmsnorm(out, gpost_ref[...])
    o_ref[...] = x_new.reshape(nb, seq, d)
    u_next = _rmsnorm(x_new, gnext_ref[...]).astype(BF16)
    blk = perm_ref.shape[0]
    u_next = jnp.concatenate([jnp.dot(perm_ref[...], u_next[r:r + blk, :], preferred_element_type=F32)
                              for r in range(0, nb * seq, blk)], axis=0)
    u_ref[...] = u_next.astype(BF16).reshape(nb, seq, d)


def _xattn(x, mem_k, mem_v, w_q, w_xo, g_pre, g_post, g_next, perm, *, n_heads, seq_tile):
    bsz, t, d = x.shape
    n_mem = mem_k.shape[1]
    nb = bsz if t <= seq_tile else 1
    seq = min(seq_tile, t)
    assert t % seq == 0 and seq % BF16_ROWS == 0 and bsz % nb == 0 and (nb * seq) % perm.shape[0] == 0
    single = bsz // nb == 1
    mem_spec = (_resident((nb, n_mem, d)) if single else pl.BlockSpec((nb, n_mem, d), lambda b, i: (b, 0, 0)))
    row = lambda b, i: (b, i, 0)
    gain = pl.BlockSpec((1, d), lambda b, i: (0, 0))
    return pl.pallas_call(
        functools.partial(_xattn_kernel, n_heads=n_heads),
        grid=(bsz // nb, t // seq),
        in_specs=[pl.BlockSpec((nb, seq, d), row), mem_spec, mem_spec,
                  _resident(w_q.shape), _resident(w_xo.shape), gain, gain, gain,
                  pl.BlockSpec(perm.shape, lambda b, i: (0, 0))],
        out_specs=[pl.BlockSpec((nb, seq, d), row), pl.BlockSpec((nb, seq, d), row)],
        out_shape=[jax.ShapeDtypeStruct((bsz, t, d), F32), jax.ShapeDtypeStruct((bsz, t, d), BF16)],
        scratch_shapes=[pltpu.VMEM((nb * seq, d), BF16), pltpu.VMEM((nb * seq, d), BF16)],
        compiler_params=_params("parallel", "parallel"),
        name="xattn",
    )(x, mem_k, mem_v, w_q, w_xo, g_pre, g_post, g_next, perm)


def _up_geglu_kernel(*refs, starts, counts, blocks, n_side):
    n_groups = len(counts)
    ins, rest = refs[:3 * n_groups], refs[3 * n_groups:]
    wg_ref, wv_ref, cwg_ref, cwv_ref, cbg_ref, cbv_ref = rest[:6]
    side_in, rest = rest[6:6 + n_side], rest[6 + n_side:]
    outs, rest = rest[:3 * n_groups], rest[3 * n_groups:]
    side_out, scr = rest[:n_side], rest[n_side:]
    w_scr, hists = scr[0], scr[1:]
    _side_cast(side_in, side_out)
    i = pl.program_id(1)
    tn = wg_ref.shape[1]

    @pl.when(i == 0)
    def _():
        _cast_rows(wg_ref, w_scr, slice(0, tn))
        _cast_rows(wv_ref, w_scr, slice(tn, 2 * tn))

    for g in range(n_groups):
        u_ref, st_refs = ins[3 * g], ins[3 * g + 1:3 * g + 3]
        hid_ref, ns_refs = outs[3 * g], outs[3 * g + 1:3 * g + 3]
        p_scrs = hists[2 * g:2 * g + 2]

        @pl.when(i == starts[g])
        def _(st_refs=st_refs, p_scrs=p_scrs, blk=blocks[g]):
            for st_ref, p_scr in zip(st_refs, p_scrs):
                _seg_hist_init(p_scr, st_ref, blk // SUBLANES)

        @pl.when((i >= starts[g]) & (i < starts[g] + counts[g]))
        def _(u_ref=u_ref, hid_ref=hid_ref, ns_refs=ns_refs, p_scrs=p_scrs, blk=blocks[g]):
            nb, seq, k_dim = u_ref.shape
            per_seq = seq // blk
            n_hist = p_scrs[0].shape[1]
            both = jnp.dot(u_ref[...].reshape(nb * seq, k_dim), w_scr[...], preferred_element_type=F32)
            ys = []
            for half, (p_scr, cw_ref, cb_ref, ns_ref) in enumerate(zip(p_scrs, (cwg_ref, cwv_ref),
                                                                     (cbg_ref, cbv_ref), ns_refs)):
                a = both[:, half * tn:(half + 1) * tn].reshape(nb * per_seq, blk, tn)
                tails = a[:, blk - n_hist:, :].reshape(nb, per_seq, n_hist, tn)
                hist = p_scr[...][:, None]
                if per_seq > 1:
                    hist = jnp.concatenate([hist, tails[:, :per_seq - 1]], axis=1)
                hist = hist.reshape(nb * per_seq, n_hist, tn)
                ys.append(_seg_conv(a, hist, cw_ref, cb_ref[...]).reshape(nb, seq, tn))
                p_scr[...] = tails[:, per_seq - 1]
                _seg_state_out(ns_ref, a.reshape(nb, per_seq, blk, tn)[:, per_seq - 1])
            hid_ref[...] = (_gelu_tanh(ys[0]) * ys[1]).astype(BF16)


def _up_geglu(us, w_up, states, conv_w, conv_b, blocks, *, tm, tn, to_round=()):
    d = w_up.shape[0]
    d_ff = w_up.shape[1] // 2
    width = conv_w.shape[0]
    n_j = d_ff // tn
    nbs = [u.shape[0] if u.shape[1] <= tm else 1 for u in us]
    seqs = [min(tm, u.shape[1]) for u in us]
    counts = [u.shape[1] // s for u, s in zip(us, seqs)]
    assert all(u.shape[0] == nb and u.shape[1] % s == 0 and s % BF16_ROWS == 0 for u, nb, s in zip(us, nbs, seqs))
    assert d_ff % tn == 0 and tn % LANES == 0
    starts, steps = _group_steps(counts)
    col_g = lambda j, i: (0, j)
    col_v = lambda j, i: (0, n_j + j)
    st_g = lambda j, i: (0, 0, j)
    st_v = lambda j, i: (0, 0, n_j + j)
    in_specs, out_specs, out_shape, hist, args = [], [], [], [], []
    for g, (u, st, nb, seq) in enumerate(zip(us, states, nbs, seqs)):
        row = lambda j, i, g=g: (0, _group_tile(i, starts[g], counts[g]), 0)
        out = lambda j, i, g=g: (0, _group_tile(i, starts[g], counts[g]), j)
        in_specs += [pl.BlockSpec((nb, seq, d), row),
                     pl.BlockSpec((nb, width - 1, tn), st_g), pl.BlockSpec((nb, width - 1, tn), st_v)]
        out_specs += [pl.BlockSpec((nb, seq, tn), out),
                      pl.BlockSpec((nb, width - 1, tn), st_g), pl.BlockSpec((nb, width - 1, tn), st_g)]
        out_shape += [jax.ShapeDtypeStruct((u.shape[0], u.shape[1], d_ff), BF16),
                      jax.ShapeDtypeStruct((u.shape[0], width - 1, d_ff), F32),
                      jax.ShapeDtypeStruct((u.shape[0], width - 1, d_ff), F32)]
        assert seq % blocks[g] == 0 and blocks[g] % SUBLANES == 0
        n_hist = min(blocks[g] // SUBLANES, width - 1) * SUBLANES
        hist += [pltpu.VMEM((nb, n_hist, tn), F32), pltpu.VMEM((nb, n_hist, tn), F32)]
        args += [u, st, st]
    side_in, side_out, side_shape = _side_cast_plan(to_round, n_j * steps, lambda j, i: j * steps + i)
    in_specs += [pl.BlockSpec((d, tn), col_g), pl.BlockSpec((d, tn), col_v),
                 pl.BlockSpec((width, tn), col_g), pl.BlockSpec((width, tn), col_v),
                 pl.BlockSpec((1, tn), col_g), pl.BlockSpec((1, tn), col_v)] + side_in
    res = pl.pallas_call(
        functools.partial(_up_geglu_kernel, starts=starts, counts=counts, blocks=tuple(blocks),
                          n_side=len(to_round)),
        grid=(n_j, steps),
        in_specs=in_specs,
        out_specs=out_specs + side_out,
        out_shape=out_shape + side_shape,
        scratch_shapes=[pltpu.VMEM((d, 2 * tn), BF16)] + hist,
        compiler_params=_params("arbitrary", "arbitrary"),
        name="up_geglu",
    )(*args, w_up, w_up, conv_w, conv_w, conv_b, conv_b, *to_round)
    return [tuple(res[3 * g:3 * g + 3]) for g in range(len(us))], list(res[3 * len(us):])


def _ffn_down_kernel(*refs, starts, counts):
    n_groups = len(counts)
    wd_ref, g_ref = refs[3 * n_groups:3 * n_groups + 2]
    o_refs = refs[3 * n_groups + 2:]
    i = pl.program_id(0)
    for g in range(n_groups):
        hid_ref, x_ref, unperm_ref = refs[3 * g:3 * g + 3]

        @pl.when((i >= starts[g]) & (i < starts[g] + counts[g]))
        def _(hid_ref=hid_ref, x_ref=x_ref, unperm_ref=unperm_ref, o_ref=o_refs[g]):
            y = jnp.dot(hid_ref[...], wd_ref[...], preferred_element_type=F32)
            branch = _rmsnorm(y, g_ref[...]).astype(BF16)
            blk = unperm_ref.shape[0]
            for r in range(0, branch.shape[0], blk):
                o_ref[r:r + blk, :] = x_ref[r:r + blk, :] + jnp.dot(unperm_ref[...], branch[r:r + blk, :],
                                                                    preferred_element_type=F32)


def _ffn_down(hids, xs, w_down, g_post, unperms, *, tm):
    d_ff, d = w_down.shape
    tms = [min(tm, x.shape[0]) for x in xs]
    counts = [x.shape[0] // t for x, t in zip(xs, tms)]
    assert all(x.shape[0] % t == 0 and t % un.shape[0] == 0 for x, t, un in zip(xs, tms, unperms))
    starts, steps = _group_steps(counts)
    in_specs, args = [], []
    for g, (hid, x, un) in enumerate(zip(hids, xs, unperms)):
        row = lambda i, g=g: (_group_tile(i, starts[g], counts[g]), 0)
        in_specs += [pl.BlockSpec((tms[g], d_ff), row), pl.BlockSpec((tms[g], d), row),
                     pl.BlockSpec(un.shape, lambda i: (0, 0))]
        args += [hid, x, un]
    return pl.pallas_call(
        functools.partial(_ffn_down_kernel, starts=starts, counts=counts),
        grid=(steps,),
        in_specs=in_specs + [_resident(w_down.shape), pl.BlockSpec((1, d), lambda i: (0, 0))],
        out_specs=[pl.BlockSpec((tms[g], d), lambda i, g=g: (_group_tile(i, starts[g], counts[g]), 0))
                   for g in range(len(xs))],
        out_shape=[jax.ShapeDtypeStruct(x.shape, F32) for x in xs],
        compiler_params=_params("arbitrary"),
        name="ffn_down",
    )(*args, w_down, g_post)


MM_TM, MM_TN = 2048, 1024
UP_TM, UP_TN = 1024, 512
SEQ_TILE = 256
ROW_TILE = 512
NORM_TM = 1024
N_XHEADS = 4


def _layer(groups, p):
    flat = [x.reshape(-1, x.shape[-1]) for x, *_ in groups]
    perms, blocks = [], []
    for x, *_ in groups:
        bsz, t, _ = x.shape
        perms.append(_segment_major_perm(*((1, SEQ_TILE) if t > SEQ_TILE else (bsz, t))))
        blocks.append(min(t, SEQ_TILE))
    u0s = [_rmsnorm_call(x2, p['g_mix_pre'], tm=NORM_TM, perm=pm) for x2, pm in zip(flat, perms)]
    raw_mems, u0s = lax.optimization_barrier(([m for g in groups for m in g[2:4]], u0s))
    mems = [m.reshape(-1, flat[0].shape[-1]) for m in raw_mems]
    d_rnn = p['conv_a_w'].shape[1]
    regions = [(d_rnn, 'id'), (d_rnn, 'gelu'), (p['w_in'].shape[1] - 2 * d_rnn, 'id')]
    assert all(width % MM_TN == 0 for width, _ in regions)
    col_acts = [act for width, act in regions for _ in range(width // MM_TN)]
    zs = _proj(u0s, p['w_in'], BF16, tm=MM_TM, tn=MM_TN, col_acts=col_acts)
    big = max(range(len(groups)), key=lambda g: flat[g].shape[0])
    resident = [p['p_a'], p['p_b'], p['w_o'], p['w_q'], p['w_xo']]
    pre = []
    for g, ((x, pos0, _, _, s_a, h0, s_b, _), z) in enumerate(zip(groups, zs)):
        bsz, t, _ = x.shape
        pre.append(_mixer_pre(z.reshape(bsz, t, -1), s_a, h0[:, None, :], s_b,
                              p['conv_a_w'], p['conv_a_b'], p['w_ri'], p['b_r'], p['b_i'],
                              p['lru_lambda'], p['conv_b_w'], seq_tile=SEQ_TILE, pos0=pos0,
                              to_round=resident + mems if g == big else ()))
    p_a, p_b, w_o, w_q, w_xo, *mems = pre[big][5:]
    mids = []
    for g, ((x, *_), x2, z) in enumerate(zip(groups, flat, zs)):
        bsz, t, d = x.shape
        m = bsz * t
        ya, yb, ns_a, nh, ns_b = pre[g][:5]
        mem_k, mem_v = (mm.reshape(bsz, -1, d) for mm in mems[2 * g:2 * g + 2])
        x1 = _mixer_post(ya.reshape(m, -1), yb.reshape(m, -1), z, x2, p_a, p_b, w_o, p['g_mix_post'],
                         perms[g].T, tm=ROW_TILE)
        x2a, u3 = _xattn(x1.reshape(bsz, t, d), mem_k, mem_v, w_q, w_xo, p['g_x_pre'], p['g_x_post'],
                         p['g_ffn_pre'], perms[g], n_heads=N_XHEADS, seq_tile=ROW_TILE)
        mids.append((x2a, u3, ns_a, nh[:, 0, :], ns_b))
    ffn, (w_down,) = _up_geglu([u3 for _, u3, *_ in mids], p['w_up'], [g[7] for g in groups],
                               p['ffn_conv_w'], p['ffn_conv_b'], blocks, tm=UP_TM, tn=UP_TN,
                               to_round=[p['w_down']])
    x3s = _ffn_down([hid.reshape(-1, hid.shape[-1]) for hid, _, _ in ffn],
                    [x2a.reshape(-1, x2a.shape[-1]) for x2a, *_ in mids], w_down, p['g_ffn_post'],
                    [pm.T for pm in perms], tm=ROW_TILE)
    return [(x3.reshape(x2a.shape), ns_a, nh, ns_b, jnp.concatenate([ns_fg, ns_fv], axis=-1))
            for x3, (x2a, _, ns_a, nh, ns_b), (_, ns_fg, ns_fv) in zip(x3s, mids, ffn)]


def kernel(x_prompt, x_sample, mem_prompt, state_conv_a, state_rglru, state_conv_b, state_ffn_conv, cache_mem_k, cache_mem_v, g_mix_pre, g_mix_post, w_in, conv_a_w, conv_a_b, w_r, b_r, w_i, b_i, lru_lambda, conv_b_w, p_a, p_b, w_o, g_x_pre, g_x_post, g_mem, w_q, w_k, w_v, w_xo, g_ffn_pre, g_ffn_post, w_up, ffn_conv_w, ffn_conv_b, w_down):
    depth = w_in.shape[0]
    bsz, _, d = x_prompt.shape
    n_mem = mem_prompt.shape[1]
    yp, ys = x_prompt, x_sample
    outs = [[] for _ in range(10)]
    row = lambda v: v.reshape(1, -1).astype(F32)
    for l in range(depth):
        p = {'g_mix_pre': row(g_mix_pre[l]), 'g_mix_post': row(g_mix_post[l]), 'w_in': w_in[l],
             'conv_a_w': conv_a_w[l], 'conv_a_b': row(conv_a_b[l]),
             'w_ri': jnp.concatenate([w_r[l], w_i[l]], axis=-1).astype(BF16),
             'b_r': row(b_r[l]), 'b_i': row(b_i[l]), 'lru_lambda': row(lru_lambda[l]),
             'conv_b_w': conv_b_w[l], 'p_a': p_a[l], 'p_b': p_b[l],
             'w_o': w_o[l], 'g_x_pre': row(g_x_pre[l]), 'g_x_post': row(g_x_post[l]),
             'w_q': w_q[l], 'w_xo': w_xo[l],
             'g_ffn_pre': row(g_ffn_pre[l]), 'g_ffn_post': row(g_ffn_post[l]), 'w_up': w_up[l],
             'ffn_conv_w': ffn_conv_w[l], 'ffn_conv_b': row(ffn_conv_b[l]), 'w_down': w_down[l]}
        d_rnn, d_conv, d_up = conv_a_w.shape[2], conv_b_w.shape[2], ffn_conv_w.shape[2]
        mem_u = _rmsnorm_call(mem_prompt.reshape(bsz * n_mem, d), row(g_mem[l]), tm=NORM_TM)
        mk = _proj([mem_u], w_k[l], F32, tm=MM_TM, tn=MM_TN)[0].reshape(bsz, n_mem, d)
        mv = _proj([mem_u], w_v[l], F32, tm=MM_TM, tn=MM_TN)[0].reshape(bsz, n_mem, d)
        zeros = lambda *s: jnp.zeros(s, F32)
        prompt = (yp, 0, mk, mv,
                  zeros(bsz, conv_a_w.shape[1] - 1, d_rnn), zeros(bsz, d_rnn),
                  zeros(bsz, conv_b_w.shape[1] - 1, d_conv), zeros(bsz, ffn_conv_w.shape[1] - 1, d_up))
        sample = (ys, PAST_LEN, cache_mem_k[l], cache_mem_v[l], state_conv_a[l], state_rglru[l], state_conv_b[l],
                  state_ffn_conv[l])
        (ys, *s_states), (yp, *p_states) = _layer([sample, prompt], p)
        for o, v in zip(outs, (*p_states, mk.reshape(bsz, n_mem, N_XHEADS, -1),
                               mv.reshape(bsz, n_mem, N_XHEADS, -1), *s_states)):
            o.append(v)
    return (yp, ys) + tuple(jnp.stack(o) for o in outs)
```

```python
import functools

import jax
import jax.numpy as jnp
import numpy as np
from jax import lax
from jax.experimental import pallas as pl
from jax.experimental.pallas import tpu as pltpu

F32 = jnp.float32
BF16 = jnp.bfloat16

EPS = 1e-6
RG_C = 8.0
PAST_LEN = 2048

LANES = 128
SUBLANES = 8
BF16_ROWS = 16
VMEM_LIMIT_BYTES = 60 * 1024 * 1024

ROW_CHUNK = 256
CAST_ROWS = 256
SCAN_LANES = 2048


def _params(*semantics):
    return pltpu.CompilerParams(dimension_semantics=semantics, vmem_limit_bytes=VMEM_LIMIT_BYTES)


def _resident(shape):
    nd = len(shape)
    return pl.BlockSpec(shape, lambda *_: (0,) * nd, pipeline_mode=pl.Buffered(1))


def _rmsnorm(x, g):
    y = x * lax.rsqrt(jnp.mean(x * x, axis=-1, keepdims=True) + EPS)
    return y * g


def _sigmoid(x):
    return 0.5 * jnp.tanh(0.5 * x) + 0.5


def _gelu_tanh(x):
    c1 = (2.0 / jnp.pi) ** 0.5
    return x * (0.5 * jnp.tanh(x * (c1 + (c1 * 0.044715) * (x * x))) + 0.5)


def _segment_major_perm(n_seq, seq):
    n_groups = seq // SUBLANES
    row = np.arange(seq)
    time = (row % SUBLANES) * n_groups + row // SUBLANES
    p = np.zeros((seq, seq), np.float32)
    p[row, time] = 1.0
    return jnp.asarray(np.kron(np.eye(n_seq, dtype=np.float32), p), BF16)


def _seg_coords(n_groups, back):
    t = SUBLANES * n_groups - back
    return t % n_groups, t // n_groups


def _seg_hist_init(hist_ref, state_ref, n_groups):
    n_state = state_ref.shape[1]
    n_hist = hist_ref.shape[1] // SUBLANES
    hist_ref[...] = jnp.zeros(hist_ref.shape, F32)
    for back in range(1, n_state + 1):
        g, s = _seg_coords(n_groups, back)
        row = (g - (n_groups - n_hist)) * SUBLANES + s
        hist_ref[:, row:row + 1, :] = state_ref[:, n_state - back:n_state - back + 1, :]


def _seg_state_out(ns_ref, x_ref, cols=slice(None)):
    n_state = ns_ref.shape[1]
    n_groups = x_ref.shape[1] // SUBLANES
    for back in range(1, n_state + 1):
        g, s = _seg_coords(n_groups, back)
        row = g * SUBLANES + s
        ns_ref[:, n_state - back:n_state - back + 1, cols] = x_ref[:, row:row + 1, :]


def _seg_conv(x_ref, hist_ref, w_ref, bias):
    nb, seq, c = x_ref.shape
    n_groups = seq // SUBLANES
    n_hist = hist_ref.shape[1] // SUBLANES
    width = w_ref.shape[0]
    assert n_hist >= min(n_groups, width - 1)
    sub = lax.broadcasted_iota(jnp.int32, (1, SUBLANES, c), 1)
    y = None
    for k in range(width):
        lag = width - 1 - k
        if lag == 0:
            tap = x_ref[...]
        else:
            parts = []
            for g in range(min(lag, n_groups)):
                src = (g - lag) % n_groups
                crossed = (lag - g + n_groups - 1) // n_groups
                cur = x_ref[:, src * SUBLANES:(src + 1) * SUBLANES, :]
                h0 = (src - (n_groups - n_hist)) * SUBLANES
                prev = hist_ref[:, h0:h0 + SUBLANES, :]
                parts.append(pltpu.roll(jnp.where(sub >= SUBLANES - crossed, prev, cur), crossed, 1))
            if n_groups > lag:
                parts.append(x_ref[:, 0:(n_groups - lag) * SUBLANES, :])
            tap = jnp.concatenate(parts, axis=1)
        y = tap * w_ref[k:k + 1, :] if y is None else y + tap * w_ref[k:k + 1, :]
    return y if bias is None else y + bias


def _softplus(x):
    return jnp.maximum(x, 0.0) + jnp.log1p(jnp.exp(-jnp.abs(x)))


def _permute_rows(ref, perm_ref):
    blk = perm_ref.shape[0]
    for r in range(0, ref.shape[0], blk):
        ref[r:r + blk, :] = jnp.dot(perm_ref[...], ref[r:r + blk, :], preferred_element_type=F32).astype(BF16)


def _rmsnorm_kernel(x_ref, g_ref, *rest):
    u_ref = rest[-1]
    tm = x_ref.shape[0]
    chunk = min(ROW_CHUNK, tm)
    g = g_ref[...]

    def body(c, _):
        rows = pl.ds(pl.multiple_of(c * chunk, chunk), chunk)
        u_ref[rows, :] = _rmsnorm(x_ref[rows, :], g).astype(BF16)
        return None

    lax.fori_loop(0, tm // chunk, body, None)
    if len(rest) == 2:
        _permute_rows(u_ref, rest[0])


def _rmsnorm_call(x, g, *, tm, perm=None):
    m, d = x.shape
    tm = min(tm, m)
    assert m % tm == 0 and tm % min(ROW_CHUNK, tm) == 0 and (perm is None or tm % perm.shape[0] == 0)
    extra = [] if perm is None else [perm]
    return pl.pallas_call(
        _rmsnorm_kernel,
        grid=(m // tm,),
        in_specs=[pl.BlockSpec((tm, d), lambda i: (i, 0)), pl.BlockSpec((1, d), lambda i: (0, 0))]
        + [pl.BlockSpec(a.shape, lambda i: (0, 0)) for a in extra],
        out_specs=pl.BlockSpec((tm, d), lambda i: (i, 0)),
        out_shape=jax.ShapeDtypeStruct((m, d), BF16),
        compiler_params=_params("parallel"),
        name="rmsnorm",
    )(x, g, *extra)


def _cast_rows(src_ref, dst_ref, cols=slice(None)):
    rows_total = src_ref.shape[0]
    chunk = min(CAST_ROWS, rows_total)

    def body(c, _):
        rows = pl.ds(pl.multiple_of(c * chunk, chunk), chunk)
        dst_ref[rows, cols] = src_ref[rows, :].astype(BF16)
        return None

    lax.fori_loop(0, rows_total // chunk, body, None)


def _side_cast_plan(arrays, n_steps, flat_step):
    in_specs, out_specs, out_shape = [], [], []
    for a in arrays:
        rows, cols = a.shape
        chunk = next(c for c in range(BF16_ROWS, rows + 1, BF16_ROWS) if rows % c == 0 and rows // c <= n_steps)
        last = rows // chunk - 1
        spec = pl.BlockSpec((chunk, cols), lambda *idx, last=last: (jnp.minimum(flat_step(*idx), last), 0))
        in_specs.append(spec)
        out_specs.append(spec)
        out_shape.append(jax.ShapeDtypeStruct((rows, cols), BF16))
    return in_specs, out_specs, out_shape


def _side_cast(src_refs, dst_refs):
    for src, dst in zip(src_refs, dst_refs):
        dst[...] = src[...].astype(BF16)


def _group_steps(counts):
    starts = [sum(counts[:g]) for g in range(len(counts))]
    return starts, sum(counts)


def _group_tile(i, start, count):
    return jnp.clip(i - start, 0, count - 1)


def _proj_kernel(*refs, starts, counts):
    n_groups = len(counts)
    u_refs, w_ref, o_refs, w_scr = refs[:n_groups], refs[n_groups], refs[n_groups + 1:-1], refs[-1]
    i = pl.program_id(1)

    @pl.when(i == 0)
    def _():
        _cast_rows(w_ref, w_scr)

    for g in range(n_groups):
        @pl.when((i >= starts[g]) & (i < starts[g] + counts[g]))
        def _(g=g):
            o_refs[g][...] = jnp.dot(u_refs[g][...], w_scr[...],
                                     preferred_element_type=F32).astype(o_refs[g].dtype)


def _proj(us, w, out_dtype, *, tm, tn):
    d, n = w.shape
    tn = min(tn, n)
    tms = [min(tm, u.shape[0]) for u in us]
    counts = [u.shape[0] // t for u, t in zip(us, tms)]
    assert all(u.shape[0] % t == 0 for u, t in zip(us, tms)) and n % tn == 0 and d % min(CAST_ROWS, d) == 0
    starts, steps = _group_steps(counts)
    tile = lambda g: (lambda j, i: (_group_tile(i, starts[g], counts[g]), 0))
    out_tile = lambda g: (lambda j, i: (_group_tile(i, starts[g], counts[g]), j))
    return pl.pallas_call(
        functools.partial(_proj_kernel, starts=starts, counts=counts),
        grid=(n // tn, steps),
        in_specs=([pl.BlockSpec((tms[g], d), tile(g)) for g in range(len(us))]
                  + [pl.BlockSpec((d, tn), lambda j, i: (0, j))]),
        out_specs=[pl.BlockSpec((tms[g], tn), out_tile(g)) for g in range(len(us))],
        out_shape=[jax.ShapeDtypeStruct((u.shape[0], n), out_dtype) for u in us],
        scratch_shapes=[pltpu.VMEM((d, tn), BF16)],
        compiler_params=_params("arbitrary", "arbitrary"),
        name="proj",
    )(*us, w)


def _mixer_pre_kernel(*refs, d_rnn, d_conv, pos0, n_side):
    (z_ref, sa_ref, h0_ref, sb_ref, caw_ref, cab_ref, wri_ref, br_ref, bi_ref, lam_ref, cbw_ref) = refs[:11]
    side_in = refs[11:11 + n_side]
    ya_ref, yb_ref, nsa_ref, nh_ref, nsb_ref = refs[11 + n_side:16 + n_side]
    side_out = refs[16 + n_side:16 + 2 * n_side]
    xa_scr, ha_scr, xc_scr, xcb_scr, ri_scr, cb_scr, hb_scr, h_scr = refs[16 + 2 * n_side:]
    _side_cast(side_in, side_out)
    t = pl.program_id(1)
    nb, seq, _ = ya_ref.shape
    n_groups = seq // SUBLANES
    n_heads, head_dim, _ = wri_ref.shape

    @pl.when(t == 0)
    def _():
        _seg_hist_init(ha_scr, sa_ref, n_groups)
        _seg_hist_init(hb_scr, sb_ref, n_groups)
        h_scr[...] = jnp.broadcast_to(h0_ref[...], h_scr.shape)

    xa_scr[...] = z_ref[:, :, 0:d_rnn].astype(F32)
    xc = _seg_conv(xa_scr, ha_scr, caw_ref, cab_ref[...])
    xc_scr[...] = xc
    xcb_scr[...] = xc.reshape(nb * seq, d_rnn).astype(BF16)
    _seg_state_out(nsa_ref, xa_scr)
    ha_scr[...] = xa_scr[:, seq - ha_scr.shape[1]:, :]

    for h in range(n_heads):
        cols = slice(h * head_dim, (h + 1) * head_dim)
        ri = jnp.dot(xcb_scr[:, cols], wri_ref[h], preferred_element_type=F32)
        ri_scr[:, :, cols] = ri[:, :head_dim].reshape(nb, seq, head_dim)
        ri_scr[:, :, d_rnn + h * head_dim:d_rnn + (h + 1) * head_dim] = ri[:, head_dim:].reshape(nb, seq, head_dim)

    strip = max(LANES, SCAN_LANES // nb)
    pairs = seq // BF16_ROWS
    sub = lax.broadcasted_iota(jnp.int32, (1, SUBLANES, strip), 1)
    for c in range(d_rnn // strip):
        cs = slice(c * strip, (c + 1) * strip)
        cs_i = slice(d_rnn + c * strip, d_rnn + (c + 1) * strip)
        cs_g = slice(d_rnn + c * strip, d_rnn + (c + 1) * strip)
        rate = RG_C * _softplus(-lam_ref[:, cs])
        b_r = br_ref[:, cs]
        b_i = bi_ref[:, cs]

        def pass1(it, carry, first=False, cs=cs, cs_i=cs_i, rate=rate, b_r=b_r, b_i=b_i):
            h_loc, a_run = carry
            for half in range(BF16_ROWS // SUBLANES):
                rows = pl.ds(pl.multiple_of(it * BF16_ROWS + half * SUBLANES, SUBLANES), SUBLANES)
                r = _sigmoid(ri_scr[:, rows, cs] + b_r)
                i = _sigmoid(ri_scr[:, rows, cs_i] + b_i)
                neg_log_a = r * rate
                a = jnp.exp(-neg_log_a)
                mult = jnp.sqrt(jnp.tanh(neg_log_a) * (a * a + 1.0))
                if first and half == 0:
                    mult = jnp.where(jnp.logical_and(t == 0, sub == 0), 1.0, mult)
                b = mult * i * xc_scr[:, rows, cs]
                h_loc = a * h_loc + b
                a_run = a * a_run
                ri_scr[:, rows, cs] = h_loc
                ri_scr[:, rows, cs_i] = a_run
            return h_loc, a_run

        carry = (jnp.zeros((nb, SUBLANES, strip), F32), jnp.ones((nb, SUBLANES, strip), F32))
        start = 0
        if pos0 == 0:
            carry = pass1(0, carry, first=True)
            start = 1
        h_tot, a_tot = lax.fori_loop(start, pairs, pass1, carry)

        for s in (1, 2, 4):
            keep = sub >= s
            a_prev = pltpu.roll(a_tot, s, 1)
            h_prev = pltpu.roll(h_tot, s, 1)
            h_tot = jnp.where(keep, a_tot * h_prev + h_tot, h_tot)
            a_tot = jnp.where(keep, a_tot * a_prev, a_tot)
        h_in = h_scr[:, :, cs]
        seg_end = a_tot * h_in + h_tot
        seg_start = jnp.where(sub == 0, h_in, pltpu.roll(seg_end, 1, 1))
        h_last = jnp.broadcast_to(seg_end[:, SUBLANES - 1:SUBLANES, :], seg_end.shape)
        h_scr[:, :, cs] = h_last
        nh_ref[:, :, cs] = h_last[:, 0:1, :]
        start2 = jnp.concatenate([seg_start] * (BF16_ROWS // SUBLANES), axis=1)

        def pass2(it, _, cs=cs, cs_i=cs_i, cs_g=cs_g, start2=start2):
            rows = pl.ds(pl.multiple_of(it * BF16_ROWS, BF16_ROWS), BF16_ROWS)
            h = ri_scr[:, rows, cs] + ri_scr[:, rows, cs_i] * start2
            gate = _gelu_tanh(z_ref[:, rows, cs_g].astype(F32))
            ya_ref[:, rows, cs] = (h * gate).astype(BF16)
            return None

        lax.fori_loop(0, pairs, pass2, None)

    o_gb, o_gc, o_hb = 2 * d_rnn, 2 * d_rnn + d_conv, 2 * d_rnn + 2 * d_conv
    cb_scr[...] = z_ref[:, :, o_gc:o_gc + d_conv].astype(F32) * z_ref[:, :, o_hb:o_hb + d_conv].astype(F32)
    co = _seg_conv(cb_scr, hb_scr, cbw_ref, None)
    yb_ref[...] = (z_ref[:, :, o_gb:o_gb + d_conv].astype(F32) * co).astype(BF16)
    _seg_state_out(nsb_ref, cb_scr)
    hb_scr[...] = cb_scr[:, seq - hb_scr.shape[1]:, :]


def _mixer_pre(z, state_a, h0, state_b, conv_a_w, conv_a_b, w_ri, b_r, b_i, lam, conv_b_w, *, seq_tile, pos0,
               to_round=()):
    bsz, t, _ = z.shape
    d_rnn = conv_a_w.shape[1]
    d_conv = conv_b_w.shape[1]
    n_pre = 2 * d_rnn + 3 * d_conv
    nb = bsz if t <= seq_tile else 1
    seq = min(seq_tile, t)
    assert t % seq == 0 and seq % BF16_ROWS == 0 and bsz % nb == 0
    wa, wb = conv_a_w.shape[0], conv_b_w.shape[0]
    hist_a, hist_b = min(seq // SUBLANES, wa - 1), min(seq // SUBLANES, wb - 1)
    kern = functools.partial(_mixer_pre_kernel, d_rnn=d_rnn, d_conv=d_conv, pos0=pos0, n_side=len(to_round))
    n_t = t // seq
    side_in, side_out, side_shape = _side_cast_plan(to_round, (bsz // nb) * n_t, lambda b, i: b * n_t + i)
    row = lambda b, i: (b, i, 0)
    per_b = lambda b, i: (b, 0, 0)
    const2 = lambda b, i: (0, 0)
    return pl.pallas_call(
        kern,
        grid=(bsz // nb, t // seq),
        in_specs=[pl.BlockSpec((nb, seq, n_pre), row),
                  pl.BlockSpec((nb, wa - 1, d_rnn), per_b),
                  pl.BlockSpec((nb, 1, d_rnn), per_b),
                  pl.BlockSpec((nb, wb - 1, d_conv), per_b),
                  pl.BlockSpec(conv_a_w.shape, const2),
                  pl.BlockSpec((1, d_rnn), const2),
                  pl.BlockSpec(w_ri.shape, lambda b, i: (0, 0, 0)),
                  pl.BlockSpec((1, d_rnn), const2),
                  pl.BlockSpec((1, d_rnn), const2),
                  pl.BlockSpec((1, d_rnn), const2),
                  pl.BlockSpec(conv_b_w.shape, const2)] + side_in,
        out_specs=[pl.BlockSpec((nb, seq, d_rnn), row),
                   pl.BlockSpec((nb, seq, d_conv), row),
                   pl.BlockSpec((nb, wa - 1, d_rnn), per_b),
                   pl.BlockSpec((nb, 1, d_rnn), per_b),
                   pl.BlockSpec((nb, wb - 1, d_conv), per_b)] + side_out,
        out_shape=[jax.ShapeDtypeStruct((bsz, t, d_rnn), BF16),
                   jax.ShapeDtypeStruct((bsz, t, d_conv), BF16),
                   jax.ShapeDtypeStruct((bsz, wa - 1, d_rnn), F32),
                   jax.ShapeDtypeStruct((bsz, 1, d_rnn), F32),
                   jax.ShapeDtypeStruct((bsz, wb - 1, d_conv), F32)] + side_shape,
        scratch_shapes=[pltpu.VMEM((nb, seq, d_rnn), F32),
                        pltpu.VMEM((nb, hist_a * SUBLANES, d_rnn), F32),
                        pltpu.VMEM((nb, seq, d_rnn), F32),
                        pltpu.VMEM((nb * seq, d_rnn), BF16),
                        pltpu.VMEM((nb, seq, 2 * d_rnn), F32),
                        pltpu.VMEM((nb, seq, d_conv), F32),
                        pltpu.VMEM((nb, hist_b * SUBLANES, d_conv), F32),
                        pltpu.VMEM((nb, SUBLANES, d_rnn), F32)],
        compiler_params=_params("arbitrary", "arbitrary"),
        name="mixer_pre",
    )(z, state_a, h0, state_b, conv_a_w, conv_a_b, w_ri, b_r, b_i, lam, conv_b_w, *to_round)


def _mixer_post_kernel(ya_ref, yb_ref, ga0_ref, ga1_ref, gb0_ref, gb1_ref, x_ref, pa_ref, pb_ref, wo_ref, g_ref,
                       unperm_ref, o_ref, mix_scr):
    half = ga0_ref.shape[1]
    y_a = jnp.dot(ya_ref[...], pa_ref[...], preferred_element_type=F32)
    y_b = jnp.dot(yb_ref[...], pb_ref[...], preferred_element_type=F32)
    for c, (ga_ref, gb_ref) in enumerate(((ga0_ref, gb0_ref), (ga1_ref, gb1_ref))):
        cols = slice(c * half, (c + 1) * half)
        mix = (jax.nn.sigmoid(ga_ref[...].astype(F32)) * y_a[:, cols]
               + jax.nn.sigmoid(gb_ref[...].astype(F32)) * y_b[:, cols])
        mix_scr[:, cols] = mix.astype(BF16)
    _permute_rows(mix_scr, unperm_ref)
    out = jnp.dot(mix_scr[...], wo_ref[...], preferred_element_type=F32)
    o_ref[...] = x_ref[...] + _rmsnorm(out, g_ref[...])


def _mixer_post(ya, yb, z, x, p_a, p_b, w_o, g_post, unperm, *, tm):
    m, d = x.shape
    d_rnn, d_conv = ya.shape[1], yb.shape[1]
    tm = min(tm, m)
    half = d // 2
    gate0 = (2 * d_rnn + 3 * d_conv) // half
    assert m % tm == 0 and (2 * d_rnn + 3 * d_conv) % half == 0 and tm % unperm.shape[0] == 0
    gate_spec = lambda k: pl.BlockSpec((tm, half), lambda i: (i, gate0 + k))
    row = lambda i: (i, 0)
    return pl.pallas_call(
        _mixer_post_kernel,
        grid=(m // tm,),
        in_specs=[pl.BlockSpec((tm, d_rnn), row), pl.BlockSpec((tm, d_conv), row),
                  gate_spec(0), gate_spec(1), gate_spec(2), gate_spec(3),
                  pl.BlockSpec((tm, d), row),
                  _resident(p_a.shape), _resident(p_b.shape), _resident(w_o.shape),
                  pl.BlockSpec((1, d), lambda i: (0, 0)), pl.BlockSpec(unperm.shape, lambda i: (0, 0))],
        out_specs=pl.BlockSpec((tm, d), row),
        out_shape=jax.ShapeDtypeStruct((m, d), F32),
        scratch_shapes=[pltpu.VMEM((tm, d), BF16)],
        compiler_params=_params("parallel"),
        name="mixer_post",
    )(ya, yb, z, z, z, z, x, p_a, p_b, w_o, g_post, unperm)


def _xattn_kernel(x_ref, k_ref, v_ref, wq_ref, wxo_ref, gpre_ref, gpost_ref, gnext_ref, perm_ref, o_ref, u_ref,
                  q_scr, o_scr, *, n_heads):
    nb, seq, d = x_ref.shape
    hd = d // n_heads
    scale = hd ** -0.5
    x = x_ref[...].reshape(nb * seq, d)
    u = _rmsnorm(x, gpre_ref[...]).astype(BF16)
    q_scr[...] = jnp.dot(u, wq_ref[...], preferred_element_type=F32).astype(BF16)
    for b in range(nb):
        rows = slice(b * seq, (b + 1) * seq)
        for h in range(n_heads):
            cols = slice(h * hd, (h + 1) * hd)
            s = lax.dot_general(q_scr[rows, cols], k_ref[b, :, cols], (((1,), (1,)), ((), ())),
                                preferred_element_type=F32) * scale
            p = jnp.exp(s - jnp.max(s, axis=-1, keepdims=True))
            att = (p / jnp.sum(p, axis=-1, keepdims=True)).astype(BF16)
            o_scr[rows, cols] = jnp.dot(att, v_ref[b, :, cols], preferred_element_type=F32).astype(BF16)
    out = jnp.dot(o_scr[...], wxo_ref[...], preferred_element_type=F32)
    x_new = x + _rmsnorm(out, gpost_ref[...])
    o_ref[...] = x_new.reshape(nb, seq, d)
    u_next = _rmsnorm(x_new, gnext_ref[...]).astype(BF16)
    blk = perm_ref.shape[0]
    u_next = jnp.concatenate([jnp.dot(perm_ref[...], u_next[r:r + blk, :], preferred_element_type=F32)
                              for r in range(0, nb * seq, blk)], axis=0)
    u_ref[...] = u_next.astype(BF16).reshape(nb, seq, d)


def _xattn(x, mem_k, mem_v, w_q, w_xo, g_pre, g_post, g_next, perm, *, n_heads, seq_tile):
    bsz, t, d = x.shape
    n_mem = mem_k.shape[1]
    nb = bsz if t <= seq_tile else 1
    seq = min(seq_tile, t)
    assert t % seq == 0 and seq % BF16_ROWS == 0 and bsz % nb == 0 and (nb * seq) % perm.shape[0] == 0
    single = bsz // nb == 1
    mem_spec = (_resident((nb, n_mem, d)) if single else pl.BlockSpec((nb, n_mem, d), lambda b, i: (b, 0, 0)))
    row = lambda b, i: (b, i, 0)
    gain = pl.BlockSpec((1, d), lambda b, i: (0, 0))
    return pl.pallas_call(
        functools.partial(_xattn_kernel, n_heads=n_heads),
        grid=(bsz // nb, t // seq),
        in_specs=[pl.BlockSpec((nb, seq, d), row), mem_spec, mem_spec,
                  _resident(w_q.shape), _resident(w_xo.shape), gain, gain, gain,
                  pl.BlockSpec(perm.shape, lambda b, i: (0, 0))],
        out_specs=[pl.BlockSpec((nb, seq, d), row), pl.BlockSpec((nb, seq, d), row)],
        out_shape=[jax.ShapeDtypeStruct((bsz, t, d), F32), jax.ShapeDtypeStruct((bsz, t, d), BF16)],
        scratch_shapes=[pltpu.VMEM((nb * seq, d), BF16), pltpu.VMEM((nb * seq, d), BF16)],
        compiler_params=_params("parallel", "parallel"),
        name="xattn",
    )(x, mem_k, mem_v, w_q, w_xo, g_pre, g_post, g_next, perm)


def _up_geglu_kernel(*refs, starts, counts, blocks, n_side):
    n_groups = len(counts)
    ins, rest = refs[:3 * n_groups], refs[3 * n_groups:]
    wg_ref, wv_ref, cwg_ref, cwv_ref, cbg_ref, cbv_ref = rest[:6]
    side_in, rest = rest[6:6 + n_side], rest[6 + n_side:]
    outs, rest = rest[:3 * n_groups], rest[3 * n_groups:]
    side_out, scr = rest[:n_side], rest[n_side:]
    w_scr, hists = scr[0], scr[1:]
    _side_cast(side_in, side_out)
    i = pl.program_id(1)
    tn = wg_ref.shape[1]

    @pl.when(i == 0)
    def _():
        _cast_rows(wg_ref, w_scr, slice(0, tn))
        _cast_rows(wv_ref, w_scr, slice(tn, 2 * tn))

    for g in range(n_groups):
        u_ref, st_refs = ins[3 * g], ins[3 * g + 1:3 * g + 3]
        hid_ref, ns_refs = outs[3 * g], outs[3 * g + 1:3 * g + 3]
        p_scrs = hists[2 * g:2 * g + 2]

        @pl.when(i == starts[g])
        def _(st_refs=st_refs, p_scrs=p_scrs, blk=blocks[g]):
            for st_ref, p_scr in zip(st_refs, p_scrs):
                _seg_hist_init(p_scr, st_ref, blk // SUBLANES)

        @pl.when((i >= starts[g]) & (i < starts[g] + counts[g]))
        def _(u_ref=u_ref, hid_ref=hid_ref, ns_refs=ns_refs, p_scrs=p_scrs, blk=blocks[g]):
            nb, seq, k_dim = u_ref.shape
            per_seq = seq // blk
            n_hist = p_scrs[0].shape[1]
            both = jnp.dot(u_ref[...].reshape(nb * seq, k_dim), w_scr[...], preferred_element_type=F32)
            ys = []
            for half, (p_scr, cw_ref, cb_ref, ns_ref) in enumerate(zip(p_scrs, (cwg_ref, cwv_ref),
                                                                     (cbg_ref, cbv_ref), ns_refs)):
                a = both[:, half * tn:(half + 1) * tn].reshape(nb * per_seq, blk, tn)
                tails = a[:, blk - n_hist:, :].reshape(nb, per_seq, n_hist, tn)
                hist = p_scr[...][:, None]
                if per_seq > 1:
                    hist = jnp.concatenate([hist, tails[:, :per_seq - 1]], axis=1)
                hist = hist.reshape(nb * per_seq, n_hist, tn)
                ys.append(_seg_conv(a, hist, cw_ref, cb_ref[...]).reshape(nb, seq, tn))
                p_scr[...] = tails[:, per_seq - 1]
                _seg_state_out(ns_ref, a.reshape(nb, per_seq, blk, tn)[:, per_seq - 1])
            hid_ref[...] = (_gelu_tanh(ys[0]) * ys[1]).astype(BF16)


def _up_geglu(us, w_up, states, conv_w, conv_b, blocks, *, tm, tn, to_round=()):
    d = w_up.shape[0]
    d_ff = w_up.shape[1] // 2
    width = conv_w.shape[0]
    n_j = d_ff // tn
    nbs = [u.shape[0] if u.shape[1] <= tm else 1 for u in us]
    seqs = [min(tm, u.shape[1]) for u in us]
    counts = [u.shape[1] // s for u, s in zip(us, seqs)]
    assert all(u.shape[0] == nb and u.shape[1] % s == 0 and s % BF16_ROWS == 0 for u, nb, s in zip(us, nbs, seqs))
    assert d_ff % tn == 0 and tn % LANES == 0
    starts, steps = _group_steps(counts)
    col_g = lambda j, i: (0, j)
    col_v = lambda j, i: (0, n_j + j)
    st_g = lambda j, i: (0, 0, j)
    st_v = lambda j, i: (0, 0, n_j + j)
    in_specs, out_specs, out_shape, hist, args = [], [], [], [], []
    for g, (u, st, nb, seq) in enumerate(zip(us, states, nbs, seqs)):
        row = lambda j, i, g=g: (0, _group_tile(i, starts[g], counts[g]), 0)
        out = lambda j, i, g=g: (0, _group_tile(i, starts[g], counts[g]), j)
        in_specs += [pl.BlockSpec((nb, seq, d), row),
                     pl.BlockSpec((nb, width - 1, tn), st_g), pl.BlockSpec((nb, width - 1, tn), st_v)]
        out_specs += [pl.BlockSpec((nb, seq, tn), out),
                      pl.BlockSpec((nb, width - 1, tn), st_g), pl.BlockSpec((nb, width - 1, tn), st_g)]
        out_shape += [jax.ShapeDtypeStruct((u.shape[0], u.shape[1], d_ff), BF16),
                      jax.ShapeDtypeStruct((u.shape[0], width - 1, d_ff), F32),
                      jax.ShapeDtypeStruct((u.shape[0], width - 1, d_ff), F32)]
        assert seq % blocks[g] == 0 and blocks[g] % SUBLANES == 0
        n_hist = min(blocks[g] // SUBLANES, width - 1) * SUBLANES
        hist += [pltpu.VMEM((nb, n_hist, tn), F32), pltpu.VMEM((nb, n_hist, tn), F32)]
        args += [u, st, st]
    side_in, side_out, side_shape = _side_cast_plan(to_round, n_j * steps, lambda j, i: j * steps + i)
    in_specs += [pl.BlockSpec((d, tn), col_g), pl.BlockSpec((d, tn), col_v),
                 pl.BlockSpec((width, tn), col_g), pl.BlockSpec((width, tn), col_v),
                 pl.BlockSpec((1, tn), col_g), pl.BlockSpec((1, tn), col_v)] + side_in
    res = pl.pallas_call(
        functools.partial(_up_geglu_kernel, starts=starts, counts=counts, blocks=tuple(blocks),
                          n_side=len(to_round)),
        grid=(n_j, steps),
        in_specs=in_specs,
        out_specs=out_specs + side_out,
        out_shape=out_shape + side_shape,
        scratch_shapes=[pltpu.VMEM((d, 2 * tn), BF16)] + hist,
        compiler_params=_params("arbitrary", "arbitrary"),
        name="up_geglu",
    )(*args, w_up, w_up, conv_w, conv_w, conv_b, conv_b, *to_round)
    return [tuple(res[3 * g:3 * g + 3]) for g in range(len(us))], list(res[3 * len(us):])


def _ffn_down_kernel(*refs, starts, counts):
    n_groups = len(counts)
    wd_ref, g_ref = refs[3 * n_groups:3 * n_groups + 2]
    o_refs = refs[3 * n_groups + 2:]
    i = pl.program_id(0)
    for g in range(n_groups):
        hid_ref, x_ref, unperm_ref = refs[3 * g:3 * g + 3]

        @pl.when((i >= starts[g]) & (i < starts[g] + counts[g]))
        def _(hid_ref=hid_ref, x_ref=x_ref, unperm_ref=unperm_ref, o_ref=o_refs[g]):
            y = jnp.dot(hid_ref[...], wd_ref[...], preferred_element_type=F32)
            branch = _rmsnorm(y, g_ref[...]).astype(BF16)
            blk = unperm_ref.shape[0]
            for r in range(0, branch.shape[0], blk):
                o_ref[r:r + blk, :] = x_ref[r:r + blk, :] + jnp.dot(unperm_ref[...], branch[r:r + blk, :],
                                                                    preferred_element_type=F32)


def _ffn_down(hids, xs, w_down, g_post, unperms, *, tm):
    d_ff, d = w_down.shape
    tms = [min(tm, x.shape[0]) for x in xs]
    counts = [x.shape[0] // t for x, t in zip(xs, tms)]
    assert all(x.shape[0] % t == 0 and t % un.shape[0] == 0 for x, t, un in zip(xs, tms, unperms))
    starts, steps = _group_steps(counts)
    in_specs, args = [], []
    for g, (hid, x, un) in enumerate(zip(hids, xs, unperms)):
        row = lambda i, g=g: (_group_tile(i, starts[g], counts[g]), 0)
        in_specs += [pl.BlockSpec((tms[g], d_ff), row), pl.BlockSpec((tms[g], d), row),
                     pl.BlockSpec(un.shape, lambda i: (0, 0))]
        args += [hid, x, un]
    return pl.pallas_call(
        functools.partial(_ffn_down_kernel, starts=starts, counts=counts),
        grid=(steps,),
        in_specs=in_specs + [_resident(w_down.shape), pl.BlockSpec((1, d), lambda i: (0, 0))],
        out_specs=[pl.BlockSpec((tms[g], d), lambda i, g=g: (_group_tile(i, starts[g], counts[g]), 0))
                   for g in range(len(xs))],
        out_shape=[jax.ShapeDtypeStruct(x.shape, F32) for x in xs],
        compiler_params=_params("arbitrary"),
        name="ffn_down",
    )(*args, w_down, g_post)


MM_TM, MM_TN = 2048, 1024
UP_TM, UP_TN = 1024, 512
SEQ_TILE = 256
ROW_TILE = 512
NORM_TM = 2048
N_XHEADS = 4


def _layer(groups, p):
    flat = [x.reshape(-1, x.shape[-1]) for x, *_ in groups]
    perms, blocks = [], []
    for x, *_ in groups:
        bsz, t, _ = x.shape
        perms.append(_segment_major_perm(*((1, SEQ_TILE) if t > SEQ_TILE else (bsz, t))))
        blocks.append(min(t, SEQ_TILE))
    u0s = [_rmsnorm_call(x2, p['g_mix_pre'], tm=NORM_TM, perm=pm) for x2, pm in zip(flat, perms)]
    raw_mems, u0s = lax.optimization_barrier(([m for g in groups for m in g[2:4]], u0s))
    mems = [m.reshape(-1, flat[0].shape[-1]) for m in raw_mems]
    zs = _proj(u0s, p['w_in'], BF16, tm=MM_TM, tn=MM_TN)
    big = max(range(len(groups)), key=lambda g: flat[g].shape[0])
    resident = [p['p_a'], p['p_b'], p['w_o'], p['w_q'], p['w_xo']]
    pre = []
    for g, ((x, pos0, _, _, s_a, h0, s_b, _), z) in enumerate(zip(groups, zs)):
        bsz, t, _ = x.shape
        pre.append(_mixer_pre(z.reshape(bsz, t, -1), s_a, h0[:, None, :], s_b,
                              p['conv_a_w'], p['conv_a_b'], p['w_ri'], p['b_r'], p['b_i'],
                              p['lru_lambda'], p['conv_b_w'], seq_tile=SEQ_TILE, pos0=pos0,
                              to_round=resident + mems if g == big else ()))
    p_a, p_b, w_o, w_q, w_xo, *mems = pre[big][5:]
    mids = []
    for g, ((x, *_), x2, z) in enumerate(zip(groups, flat, zs)):
        bsz, t, d = x.shape
        m = bsz * t
        ya, yb, ns_a, nh, ns_b = pre[g][:5]
        mem_k, mem_v = (mm.reshape(bsz, -1, d) for mm in mems[2 * g:2 * g + 2])
        x1 = _mixer_post(ya.reshape(m, -1), yb.reshape(m, -1), z, x2, p_a, p_b, w_o, p['g_mix_post'],
                         perms[g].T, tm=ROW_TILE)
        x2a, u3 = _xattn(x1.reshape(bsz, t, d), mem_k, mem_v, w_q, w_xo, p['g_x_pre'], p['g_x_post'],
                         p['g_ffn_pre'], perms[g], n_heads=N_XHEADS, seq_tile=ROW_TILE)
        mids.append((x2a, u3, ns_a, nh[:, 0, :], ns_b))
    ffn, (w_down,) = _up_geglu([u3 for _, u3, *_ in mids], p['w_up'], [g[7] for g in groups],
                               p['ffn_conv_w'], p['ffn_conv_b'], blocks, tm=UP_TM, tn=UP_TN,
                               to_round=[p['w_down']])
    x3s = _ffn_down([hid.reshape(-1, hid.shape[-1]) for hid, _, _ in ffn],
                    [x2a.reshape(-1, x2a.shape[-1]) for x2a, *_ in mids], w_down, p['g_ffn_post'],
                    [pm.T for pm in perms], tm=ROW_TILE)
    return [(x3.reshape(x2a.shape), ns_a, nh, ns_b, jnp.concatenate([ns_fg, ns_fv], axis=-1))
            for x3, (x2a, _, ns_a, nh, ns_b), (_, ns_fg, ns_fv) in zip(x3s, mids, ffn)]


def kernel(x_prompt, x_sample, mem_prompt, state_conv_a, state_rglru, state_conv_b, state_ffn_conv, cache_mem_k, cache_mem_v, g_mix_pre, g_mix_post, w_in, conv_a_w, conv_a_b, w_r, b_r, w_i, b_i, lru_lambda, conv_b_w, p_a, p_b, w_o, g_x_pre, g_x_post, g_mem, w_q, w_k, w_v, w_xo, g_ffn_pre, g_ffn_post, w_up, ffn_conv_w, ffn_conv_b, w_down):
    depth = w_in.shape[0]
    bsz, _, d = x_prompt.shape
    n_mem = mem_prompt.shape[1]
    yp, ys = x_prompt, x_sample
    outs = [[] for _ in range(10)]
    row = lambda v: v.reshape(1, -1).astype(F32)
    for l in range(depth):
        p = {'g_mix_pre': row(g_mix_pre[l]), 'g_mix_post': row(g_mix_post[l]), 'w_in': w_in[l],
             'conv_a_w': conv_a_w[l], 'conv_a_b': row(conv_a_b[l]),
             'w_ri': jnp.concatenate([w_r[l], w_i[l]], axis=-1).astype(BF16),
             'b_r': row(b_r[l]), 'b_i': row(b_i[l]), 'lru_lambda': row(lru_lambda[l]),
             'conv_b_w': conv_b_w[l], 'p_a': p_a[l], 'p_b': p_b[l],
             'w_o': w_o[l], 'g_x_pre': row(g_x_pre[l]), 'g_x_post': row(g_x_post[l]),
             'w_q': w_q[l], 'w_xo': w_xo[l],
             'g_ffn_pre': row(g_ffn_pre[l]), 'g_ffn_post': row(g_ffn_post[l]), 'w_up': w_up[l],
             'ffn_conv_w': ffn_conv_w[l], 'ffn_conv_b': row(ffn_conv_b[l]), 'w_down': w_down[l]}
        d_rnn, d_conv, d_up = conv_a_w.shape[2], conv_b_w.shape[2], ffn_conv_w.shape[2]
        mem_u = _rmsnorm_call(mem_prompt.reshape(bsz * n_mem, d), row(g_mem[l]), tm=NORM_TM)
        mk = _proj([mem_u], w_k[l], F32, tm=MM_TM, tn=MM_TN)[0].reshape(bsz, n_mem, d)
        mv = _proj([mem_u], w_v[l], F32, tm=MM_TM, tn=MM_TN)[0].reshape(bsz, n_mem, d)
        zeros = lambda *s: jnp.zeros(s, F32)
        prompt = (yp, 0, mk, mv,
                  zeros(bsz, conv_a_w.shape[1] - 1, d_rnn), zeros(bsz, d_rnn),
                  zeros(bsz, conv_b_w.shape[1] - 1, d_conv), zeros(bsz, ffn_conv_w.shape[1] - 1, d_up))
        sample = (ys, PAST_LEN, cache_mem_k[l], cache_mem_v[l], state_conv_a[l], state_rglru[l], state_conv_b[l],
                  state_ffn_conv[l])
        (ys, *s_states), (yp, *p_states) = _layer([sample, prompt], p)
        for o, v in zip(outs, (*p_states, mk.reshape(bsz, n_mem, N_XHEADS, -1),
                               mv.reshape(bsz, n_mem, N_XHEADS, -1), *s_states)):
            o.append(v)
    return (yp, ys) + tuple(jnp.stack(o) for o in outs)
```

```python
import functools

import jax
import jax.numpy as jnp
import numpy as np
from jax import lax
from jax.experimental import pallas as pl
from jax.experimental.pallas import tpu as pltpu

F32 = jnp.float32
BF16 = jnp.bfloat16

EPS = 1e-6
RG_C = 8.0
PAST_LEN = 2048

LANES = 128
SUBLANES = 8
BF16_ROWS = 16
VMEM_LIMIT_BYTES = 60 * 1024 * 1024

ROW_CHUNK = 256
CAST_ROWS = 256
SCAN_LANES = 2048


def _params(*semantics):
    return pltpu.CompilerParams(dimension_semantics=semantics, vmem_limit_bytes=VMEM_LIMIT_BYTES)


def _resident(shape):
    nd = len(shape)
    return pl.BlockSpec(shape, lambda *_: (0,) * nd, pipeline_mode=pl.Buffered(1))


def _rmsnorm(x, g):
    y = x * lax.rsqrt(jnp.mean(x * x, axis=-1, keepdims=True) + EPS)
    return y * g


def _sigmoid(x):
    return 0.5 * jnp.tanh(0.5 * x) + 0.5


def _gelu_tanh(x):
    c1 = (2.0 / jnp.pi) ** 0.5
    return x * (0.5 * jnp.tanh(x * (c1 + (c1 * 0.044715) * (x * x))) + 0.5)


def _segment_major_perm(n_seq, seq):
    n_groups = seq // SUBLANES
    row = np.arange(seq)
    time = (row % SUBLANES) * n_groups + row // SUBLANES
    p = np.zeros((seq, seq), np.float32)
    p[row, time] = 1.0
    return jnp.asarray(np.kron(np.eye(n_seq, dtype=np.float32), p), BF16)


def _seg_coords(n_groups, back):
    t = SUBLANES * n_groups - back
    return t % n_groups, t // n_groups


def _seg_hist_init(hist_ref, state_ref, n_groups):
    n_state = state_ref.shape[1]
    n_hist = hist_ref.shape[1] // SUBLANES
    hist_ref[...] = jnp.zeros(hist_ref.shape, F32)
    for back in range(1, n_state + 1):
        g, s = _seg_coords(n_groups, back)
        row = (g - (n_groups - n_hist)) * SUBLANES + s
        hist_ref[:, row:row + 1, :] = state_ref[:, n_state - back:n_state - back + 1, :]


def _seg_state_out(ns_ref, x_ref, cols=slice(None)):
    n_state = ns_ref.shape[1]
    n_groups = x_ref.shape[1] // SUBLANES
    for back in range(1, n_state + 1):
        g, s = _seg_coords(n_groups, back)
        row = g * SUBLANES + s
        ns_ref[:, n_state - back:n_state - back + 1, cols] = x_ref[:, row:row + 1, :]


def _seg_conv(x_ref, hist_ref, w_ref, bias):
    nb, seq, c = x_ref.shape
    n_groups = seq // SUBLANES
    n_hist = hist_ref.shape[1] // SUBLANES
    width = w_ref.shape[0]
    assert n_hist >= min(n_groups, width - 1)
    sub = lax.broadcasted_iota(jnp.int32, (1, SUBLANES, c), 1)
    y = None
    for k in range(width):
        lag = width - 1 - k
        if lag == 0:
            tap = x_ref[...]
        else:
            parts = []
            for g in range(min(lag, n_groups)):
                src = (g - lag) % n_groups
                crossed = (lag - g + n_groups - 1) // n_groups
                cur = x_ref[:, src * SUBLANES:(src + 1) * SUBLANES, :]
                h0 = (src - (n_groups - n_hist)) * SUBLANES
                prev = hist_ref[:, h0:h0 + SUBLANES, :]
                parts.append(pltpu.roll(jnp.where(sub >= SUBLANES - crossed, prev, cur), crossed, 1))
            if n_groups > lag:
                parts.append(x_ref[:, 0:(n_groups - lag) * SUBLANES, :])
            tap = jnp.concatenate(parts, axis=1)
        y = tap * w_ref[k:k + 1, :] if y is None else y + tap * w_ref[k:k + 1, :]
    return y if bias is None else y + bias


def _softplus(x):
    return jnp.maximum(x, 0.0) + jnp.log1p(jnp.exp(-jnp.abs(x)))


def _permute_rows(ref, perm_ref):
    blk = perm_ref.shape[0]
    for r in range(0, ref.shape[0], blk):
        ref[r:r + blk, :] = jnp.dot(perm_ref[...], ref[r:r + blk, :], preferred_element_type=F32).astype(BF16)


def _rmsnorm_kernel(x_ref, g_ref, *rest):
    u_ref = rest[-1]
    tm = x_ref.shape[0]
    chunk = min(ROW_CHUNK, tm)
    g = g_ref[...]

    def body(c, _):
        rows = pl.ds(pl.multiple_of(c * chunk, chunk), chunk)
        u_ref[rows, :] = _rmsnorm(x_ref[rows, :], g).astype(BF16)
        return None

    lax.fori_loop(0, tm // chunk, body, None)
    if len(rest) == 2:
        _permute_rows(u_ref, rest[0])


def _rmsnorm_call(x, g, *, tm, perm=None):
    m, d = x.shape
    tm = min(tm, m)
    assert m % tm == 0 and tm % min(ROW_CHUNK, tm) == 0 and (perm is None or tm % perm.shape[0] == 0)
    extra = [] if perm is None else [perm]
    return pl.pallas_call(
        _rmsnorm_kernel,
        grid=(m // tm,),
        in_specs=[pl.BlockSpec((tm, d), lambda i: (i, 0)), pl.BlockSpec((1, d), lambda i: (0, 0))]
        + [pl.BlockSpec(a.shape, lambda i: (0, 0)) for a in extra],
        out_specs=pl.BlockSpec((tm, d), lambda i: (i, 0)),
        out_shape=jax.ShapeDtypeStruct((m, d), BF16),
        compiler_params=_params("parallel"),
        name="rmsnorm",
    )(x, g, *extra)


def _cast_rows(src_ref, dst_ref, cols=slice(None)):
    rows_total = src_ref.shape[0]
    chunk = min(CAST_ROWS, rows_total)

    def body(c, _):
        rows = pl.ds(pl.multiple_of(c * chunk, chunk), chunk)
        dst_ref[rows, cols] = src_ref[rows, :].astype(BF16)
        return None

    lax.fori_loop(0, rows_total // chunk, body, None)


def _side_cast_plan(arrays, n_steps, flat_step):
    in_specs, out_specs, out_shape = [], [], []
    for a in arrays:
        rows, cols = a.shape
        chunk = next(c for c in range(BF16_ROWS, rows + 1, BF16_ROWS) if rows % c == 0 and rows // c <= n_steps)
        last = rows // chunk - 1
        spec = pl.BlockSpec((chunk, cols), lambda *idx, last=last: (jnp.minimum(flat_step(*idx), last), 0))
        in_specs.append(spec)
        out_specs.append(spec)
        out_shape.append(jax.ShapeDtypeStruct((rows, cols), BF16))
    return in_specs, out_specs, out_shape


def _side_cast(src_refs, dst_refs):
    for src, dst in zip(src_refs, dst_refs):
        dst[...] = src[...].astype(BF16)


def _group_steps(counts):
    starts = [sum(counts[:g]) for g in range(len(counts))]
    return starts, sum(counts)


def _group_tile(i, start, count):
    return jnp.clip(i - start, 0, count - 1)


def _proj_kernel(*refs, starts, counts):
    n_groups = len(counts)
    u_refs, w_ref, o_refs, w_scr = refs[:n_groups], refs[n_groups], refs[n_groups + 1:-1], refs[-1]
    i = pl.program_id(1)

    @pl.when(i == 0)
    def _():
        _cast_rows(w_ref, w_scr)

    for g in range(n_groups):
        @pl.when((i >= starts[g]) & (i < starts[g] + counts[g]))
        def _(g=g):
            o_refs[g][...] = jnp.dot(u_refs[g][...], w_scr[...],
                                     preferred_element_type=F32).astype(o_refs[g].dtype)


def _proj(us, w, out_dtype, *, tm, tn):
    d, n = w.shape
    tn = min(tn, n)
    tms = [min(tm, u.shape[0]) for u in us]
    counts = [u.shape[0] // t for u, t in zip(us, tms)]
    assert all(u.shape[0] % t == 0 for u, t in zip(us, tms)) and n % tn == 0 and d % min(CAST_ROWS, d) == 0
    starts, steps = _group_steps(counts)
    tile = lambda g: (lambda j, i: (_group_tile(i, starts[g], counts[g]), 0))
    out_tile = lambda g: (lambda j, i: (_group_tile(i, starts[g], counts[g]), j))
    return pl.pallas_call(
        functools.partial(_proj_kernel, starts=starts, counts=counts),
        grid=(n // tn, steps),
        in_specs=([pl.BlockSpec((tms[g], d), tile(g)) for g in range(len(us))]
                  + [pl.BlockSpec((d, tn), lambda j, i: (0, j))]),
        out_specs=[pl.BlockSpec((tms[g], tn), out_tile(g)) for g in range(len(us))],
        out_shape=[jax.ShapeDtypeStruct((u.shape[0], n), out_dtype) for u in us],
        scratch_shapes=[pltpu.VMEM((d, tn), BF16)],
        compiler_params=_params("arbitrary", "arbitrary"),
        name="proj",
    )(*us, w)


def _mixer_pre_kernel(*refs, d_rnn, d_conv, pos0, n_side):
    (z_ref, sa_ref, h0_ref, sb_ref, caw_ref, cab_ref, wri_ref, br_ref, bi_ref, lam_ref, cbw_ref) = refs[:11]
    side_in = refs[11:11 + n_side]
    ya_ref, yb_ref, nsa_ref, nh_ref, nsb_ref = refs[11 + n_side:16 + n_side]
    side_out = refs[16 + n_side:16 + 2 * n_side]
    xa_scr, ha_scr, xc_scr, xcb_scr, ri_scr, cb_scr, hb_scr, h_scr = refs[16 + 2 * n_side:]
    _side_cast(side_in, side_out)
    t = pl.program_id(1)
    nb, seq, _ = ya_ref.shape
    n_groups = seq // SUBLANES
    n_heads, head_dim, _ = wri_ref.shape

    @pl.when(t == 0)
    def _():
        _seg_hist_init(ha_scr, sa_ref, n_groups)
        _seg_hist_init(hb_scr, sb_ref, n_groups)
        h_scr[...] = jnp.broadcast_to(h0_ref[...], h_scr.shape)

    xa_scr[...] = z_ref[:, :, 0:d_rnn].astype(F32)
    xc = _seg_conv(xa_scr, ha_scr, caw_ref, cab_ref[...])
    xc_scr[...] = xc
    xcb_scr[...] = xc.reshape(nb * seq, d_rnn).astype(BF16)
    _seg_state_out(nsa_ref, xa_scr)
    ha_scr[...] = xa_scr[:, seq - ha_scr.shape[1]:, :]

    for h in range(n_heads):
        cols = slice(h * head_dim, (h + 1) * head_dim)
        ri = jnp.dot(xcb_scr[:, cols], wri_ref[h], preferred_element_type=F32)
        ri_scr[:, :, cols] = ri[:, :head_dim].reshape(nb, seq, head_dim)
        ri_scr[:, :, d_rnn + h * head_dim:d_rnn + (h + 1) * head_dim] = ri[:, head_dim:].reshape(nb, seq, head_dim)

    strip = max(LANES, SCAN_LANES // nb)
    pairs = seq // BF16_ROWS
    sub = lax.broadcasted_iota(jnp.int32, (1, SUBLANES, strip), 1)
    for c in range(d_rnn // strip):
        cs = slice(c * strip, (c + 1) * strip)
        cs_i = slice(d_rnn + c * strip, d_rnn + (c + 1) * strip)
        cs_g = slice(d_rnn + c * strip, d_rnn + (c + 1) * strip)
        rate = RG_C * _softplus(-lam_ref[:, cs])
        b_r = br_ref[:, cs]
        b_i = bi_ref[:, cs]

        def pass1(it, carry, first=False, cs=cs, cs_i=cs_i, rate=rate, b_r=b_r, b_i=b_i):
            h_loc, a_run = carry
            for half in range(BF16_ROWS // SUBLANES):
                rows = pl.ds(pl.multiple_of(it * BF16_ROWS + half * SUBLANES, SUBLANES), SUBLANES)
                r = _sigmoid(ri_scr[:, rows, cs] + b_r)
                i = _sigmoid(ri_scr[:, rows, cs_i] + b_i)
                neg_log_a = r * rate
                a = jnp.exp(-neg_log_a)
                mult = jnp.sqrt(jnp.tanh(neg_log_a) * (a * a + 1.0))
                if first and half == 0:
                    mult = jnp.where(jnp.logical_and(t == 0, sub == 0), 1.0, mult)
                b = mult * i * xc_scr[:, rows, cs]
                h_loc = a * h_loc + b
                a_run = a * a_run
                ri_scr[:, rows, cs] = h_loc
                ri_scr[:, rows, cs_i] = a_run
            return h_loc, a_run

        carry = (jnp.zeros((nb, SUBLANES, strip), F32), jnp.ones((nb, SUBLANES, strip), F32))
        start = 0
        if pos0 == 0:
            carry = pass1(0, carry, first=True)
            start = 1
        h_tot, a_tot = lax.fori_loop(start, pairs, pass1, carry)

        for s in (1, 2, 4):
            keep = sub >= s
            a_prev = pltpu.roll(a_tot, s, 1)
            h_prev = pltpu.roll(h_tot, s, 1)
            h_tot = jnp.where(keep, a_tot * h_prev + h_tot, h_tot)
            a_tot = jnp.where(keep, a_tot * a_prev, a_tot)
        h_in = h_scr[:, :, cs]
        seg_end = a_tot * h_in + h_tot
        seg_start = jnp.where(sub == 0, h_in, pltpu.roll(seg_end, 1, 1))
        h_last = jnp.broadcast_to(seg_end[:, SUBLANES - 1:SUBLANES, :], seg_end.shape)
        h_scr[:, :, cs] = h_last
        nh_ref[:, :, cs] = h_last[:, 0:1, :]
        start2 = jnp.concatenate([seg_start] * (BF16_ROWS // SUBLANES), axis=1)

        def pass2(it, _, cs=cs, cs_i=cs_i, cs_g=cs_g, start2=start2):
            rows = pl.ds(pl.multiple_of(it * BF16_ROWS, BF16_ROWS), BF16_ROWS)
            h = ri_scr[:, rows, cs] + ri_scr[:, rows, cs_i] * start2
            gate = _gelu_tanh(z_ref[:, rows, cs_g].astype(F32))
            ya_ref[:, rows, cs] = (h * gate).astype(BF16)
            return None

        lax.fori_loop(0, pairs, pass2, None)

    o_gb, o_gc, o_hb = 2 * d_rnn, 2 * d_rnn + d_conv, 2 * d_rnn + 2 * d_conv
    cb_scr[...] = z_ref[:, :, o_gc:o_gc + d_conv].astype(F32) * z_ref[:, :, o_hb:o_hb + d_conv].astype(F32)
    co = _seg_conv(cb_scr, hb_scr, cbw_ref, None)
    yb_ref[...] = (z_ref[:, :, o_gb:o_gb + d_conv].astype(F32) * co).astype(BF16)
    _seg_state_out(nsb_ref, cb_scr)
    hb_scr[...] = cb_scr[:, seq - hb_scr.shape[1]:, :]


def _mixer_pre(z, state_a, h0, state_b, conv_a_w, conv_a_b, w_ri, b_r, b_i, lam, conv_b_w, *, seq_tile, pos0,
               to_round=()):
    bsz, t, _ = z.shape
    d_rnn = conv_a_w.shape[1]
    d_conv = conv_b_w.shape[1]
    n_pre = 2 * d_rnn + 3 * d_conv
    nb = bsz if t <= seq_tile else 1
    seq = min(seq_tile, t)
    assert t % seq == 0 and seq % BF16_ROWS == 0 and bsz % nb == 0
    wa, wb = conv_a_w.shape[0], conv_b_w.shape[0]
    hist_a, hist_b = min(seq // SUBLANES, wa - 1), min(seq // SUBLANES, wb - 1)
    kern = functools.partial(_mixer_pre_kernel, d_rnn=d_rnn, d_conv=d_conv, pos0=pos0, n_side=len(to_round))
    n_t = t // seq
    side_in, side_out, side_shape = _side_cast_plan(to_round, (bsz // nb) * n_t, lambda b, i: b * n_t + i)
    row = lambda b, i: (b, i, 0)
    per_b = lambda b, i: (b, 0, 0)
    const2 = lambda b, i: (0, 0)
    return pl.pallas_call(
        kern,
        grid=(bsz // nb, t // seq),
        in_specs=[pl.BlockSpec((nb, seq, n_pre), row),
                  pl.BlockSpec((nb, wa - 1, d_rnn), per_b),
                  pl.BlockSpec((nb, 1, d_rnn), per_b),
                  pl.BlockSpec((nb, wb - 1, d_conv), per_b),
                  pl.BlockSpec(conv_a_w.shape, const2),
                  pl.BlockSpec((1, d_rnn), const2),
                  pl.BlockSpec(w_ri.shape, lambda b, i: (0, 0, 0)),
                  pl.BlockSpec((1, d_rnn), const2),
                  pl.BlockSpec((1, d_rnn), const2),
                  pl.BlockSpec((1, d_rnn), const2),
                  pl.BlockSpec(conv_b_w.shape, const2)] + side_in,
        out_specs=[pl.BlockSpec((nb, seq, d_rnn), row),
                   pl.BlockSpec((nb, seq, d_conv), row),
                   pl.BlockSpec((nb, wa - 1, d_rnn), per_b),
                   pl.BlockSpec((nb, 1, d_rnn), per_b),
                   pl.BlockSpec((nb, wb - 1, d_conv), per_b)] + side_out,
        out_shape=[jax.ShapeDtypeStruct((bsz, t, d_rnn), BF16),
                   jax.ShapeDtypeStruct((bsz, t, d_conv), BF16),
                   jax.ShapeDtypeStruct((bsz, wa - 1, d_rnn), F32),
                   jax.ShapeDtypeStruct((bsz, 1, d_rnn), F32),
                   jax.ShapeDtypeStruct((bsz, wb - 1, d_conv), F32)] + side_shape,
        scratch_shapes=[pltpu.VMEM((nb, seq, d_rnn), F32),
                        pltpu.VMEM((nb, hist_a * SUBLANES, d_rnn), F32),
                        pltpu.VMEM((nb, seq, d_rnn), F32),
                        pltpu.VMEM((nb * seq, d_rnn), BF16),
                        pltpu.VMEM((nb, seq, 2 * d_rnn), F32),
                        pltpu.VMEM((nb, seq, d_conv), F32),
                        pltpu.VMEM((nb, hist_b * SUBLANES, d_conv), F32),
                        pltpu.VMEM((nb, SUBLANES, d_rnn), F32)],
        compiler_params=_params("arbitrary", "arbitrary"),
        name="mixer_pre",
    )(z, state_a, h0, state_b, conv_a_w, conv_a_b, w_ri, b_r, b_i, lam, conv_b_w, *to_round)


def _mixer_post_kernel(ya_ref, yb_ref, ga0_ref, ga1_ref, gb0_ref, gb1_ref, x_ref, pa_ref, pb_ref, wo_ref, g_ref,
                       unperm_ref, o_ref, mix_scr):
    half = ga0_ref.shape[1]
    y_a = jnp.dot(ya_ref[...], pa_ref[...], preferred_element_type=F32)
    y_b = jnp.dot(yb_ref[...], pb_ref[...], preferred_element_type=F32)
    for c, (ga_ref, gb_ref) in enumerate(((ga0_ref, gb0_ref), (ga1_ref, gb1_ref))):
        cols = slice(c * half, (c + 1) * half)
        mix = (jax.nn.sigmoid(ga_ref[...].astype(F32)) * y_a[:, cols]
               + jax.nn.sigmoid(gb_ref[...].astype(F32)) * y_b[:, cols])
        mix_scr[:, cols] = mix.astype(BF16)
    _permute_rows(mix_scr, unperm_ref)
    out = jnp.dot(mix_scr[...], wo_ref[...], preferred_element_type=F32)
    o_ref[...] = x_ref[...] + _rmsnorm(out, g_ref[...])


def _mixer_post(ya, yb, z, x, p_a, p_b, w_o, g_post, unperm, *, tm):
    m, d = x.shape
    d_rnn, d_conv = ya.shape[1], yb.shape[1]
    tm = min(tm, m)
    half = d // 2
    gate0 = (2 * d_rnn + 3 * d_conv) // half
    assert m % tm == 0 and (2 * d_rnn + 3 * d_conv) % half == 0 and tm % unperm.shape[0] == 0
    gate_spec = lambda k: pl.BlockSpec((tm, half), lambda i: (i, gate0 + k))
    row = lambda i: (i, 0)
    return pl.pallas_call(
        _mixer_post_kernel,
        grid=(m // tm,),
        in_specs=[pl.BlockSpec((tm, d_rnn), row), pl.BlockSpec((tm, d_conv), row),
                  gate_spec(0), gate_spec(1), gate_spec(2), gate_spec(3),
                  pl.BlockSpec((tm, d), row),
                  _resident(p_a.shape), _resident(p_b.shape), _resident(w_o.shape),
                  pl.BlockSpec((1, d), lambda i: (0, 0)), pl.BlockSpec(unperm.shape, lambda i: (0, 0))],
        out_specs=pl.BlockSpec((tm, d), row),
        out_shape=jax.ShapeDtypeStruct((m, d), F32),
        scratch_shapes=[pltpu.VMEM((tm, d), BF16)],
        compiler_params=_params("parallel"),
        name="mixer_post",
    )(ya, yb, z, z, z, z, x, p_a, p_b, w_o, g_post, unperm)


def _xattn_kernel(x_ref, k_ref, v_ref, wq_ref, wxo_ref, gpre_ref, gpost_ref, gnext_ref, perm_ref, o_ref, u_ref,
                  q_scr, o_scr, *, n_heads):
    nb, seq, d = x_ref.shape
    hd = d // n_heads
    scale = hd ** -0.5
    x = x_ref[...].reshape(nb * seq, d)
    u = _rmsnorm(x, gpre_ref[...]).astype(BF16)
    q_scr[...] = jnp.dot(u, wq_ref[...], preferred_element_type=F32).astype(BF16)
    pairs = [(b, slice(b * seq, (b + 1) * seq), slice(h * hd, (h + 1) * hd))
             for b in range(nb) for h in range(n_heads)]
    stage_major = nb > 1

    def scores(b, rows, cols):
        return lax.dot_general(q_scr[rows, cols], k_ref[b, :, cols], (((1,), (1,)), ((), ())),
                               preferred_element_type=F32) * scale

    def softmax(s):
        p = jnp.exp(s - jnp.max(s, axis=-1, keepdims=True))
        return (p / jnp.sum(p, axis=-1, keepdims=True)).astype(BF16)

    def weighted(att, b, rows, cols):
        o_scr[rows, cols] = jnp.dot(att, v_ref[b, :, cols], preferred_element_type=F32).astype(BF16)

    if stage_major:
        atts = [softmax(s) for s in [scores(*pr) for pr in pairs]]
        for att, pr in zip(atts, pairs):
            weighted(att, *pr)
    else:
        for pr in pairs:
            weighted(softmax(scores(*pr)), *pr)
    out = jnp.dot(o_scr[...], wxo_ref[...], preferred_element_type=F32)
    x_new = x + _rmsnorm(out, gpost_ref[...])
    o_ref[...] = x_new.reshape(nb, seq, d)
    u_next = _rmsnorm(x_new, gnext_ref[...]).astype(BF16)
    blk = perm_ref.shape[0]
    u_next = jnp.concatenate([jnp.dot(perm_ref[...], u_next[r:r + blk, :], preferred_element_type=F32)
                              for r in range(0, nb * seq, blk)], axis=0)
    u_ref[...] = u_next.astype(BF16).reshape(nb, seq, d)


def _xattn(x, mem_k, mem_v, w_q, w_xo, g_pre, g_post, g_next, perm, *, n_heads, seq_tile):
    bsz, t, d = x.shape
    n_mem = mem_k.shape[1]
    nb = bsz if t <= seq_tile else 1
    seq = min(seq_tile, t)
    assert t % seq == 0 and seq % BF16_ROWS == 0 and bsz % nb == 0 and (nb * seq) % perm.shape[0] == 0
    single = bsz // nb == 1
    mem_spec = (_resident((nb, n_mem, d)) if single else pl.BlockSpec((nb, n_mem, d), lambda b, i: (b, 0, 0)))
    row = lambda b, i: (b, i, 0)
    gain = pl.BlockSpec((1, d), lambda b, i: (0, 0))
    return pl.pallas_call(
        functools.partial(_xattn_kernel, n_heads=n_heads),
        grid=(bsz // nb, t // seq),
        in_specs=[pl.BlockSpec((nb, seq, d), row), mem_spec, mem_spec,
                  _resident(w_q.shape), _resident(w_xo.shape), gain, gain, gain,
                  pl.BlockSpec(perm.shape, lambda b, i: (0, 0))],
        out_specs=[pl.BlockSpec((nb, seq, d), row), pl.BlockSpec((nb, seq, d), row)],
        out_shape=[jax.ShapeDtypeStruct((bsz, t, d), F32), jax.ShapeDtypeStruct((bsz, t, d), BF16)],
        scratch_shapes=[pltpu.VMEM((nb * seq, d), BF16), pltpu.VMEM((nb * seq, d), BF16)],
        compiler_params=_params("parallel", "parallel"),
        name="xattn",
    )(x, mem_k, mem_v, w_q, w_xo, g_pre, g_post, g_next, perm)


def _up_geglu_kernel(*refs, starts, counts, blocks, n_side):
    n_groups = len(counts)
    ins, rest = refs[:3 * n_groups], refs[3 * n_groups:]
    wg_ref, wv_ref, cwg_ref, cwv_ref, cbg_ref, cbv_ref = rest[:6]
    side_in, rest = rest[6:6 + n_side], rest[6 + n_side:]
    outs, rest = rest[:3 * n_groups], rest[3 * n_groups:]
    side_out, scr = rest[:n_side], rest[n_side:]
    w_scr, hists = scr[0], scr[1:]
    _side_cast(side_in, side_out)
    i = pl.program_id(1)
    tn = wg_ref.shape[1]

    @pl.when(i == 0)
    def _():
        _cast_rows(wg_ref, w_scr, slice(0, tn))
        _cast_rows(wv_ref, w_scr, slice(tn, 2 * tn))

    for g in range(n_groups):
        u_ref, st_refs = ins[3 * g], ins[3 * g + 1:3 * g + 3]
        hid_ref, ns_refs = outs[3 * g], outs[3 * g + 1:3 * g + 3]
        p_scrs = hists[2 * g:2 * g + 2]

        @pl.when(i == starts[g])
        def _(st_refs=st_refs, p_scrs=p_scrs, blk=blocks[g]):
            for st_ref, p_scr in zip(st_refs, p_scrs):
                _seg_hist_init(p_scr, st_ref, blk // SUBLANES)

        @pl.when((i >= starts[g]) & (i < starts[g] + counts[g]))
        def _(u_ref=u_ref, hid_ref=hid_ref, ns_refs=ns_refs, p_scrs=p_scrs, blk=blocks[g]):
            nb, seq, k_dim = u_ref.shape
            per_seq = seq // blk
            n_hist = p_scrs[0].shape[1]
            both = jnp.dot(u_ref[...].reshape(nb * seq, k_dim), w_scr[...], preferred_element_type=F32)
            ys = []
            for half, (p_scr, cw_ref, cb_ref, ns_ref) in enumerate(zip(p_scrs, (cwg_ref, cwv_ref),
                                                                     (cbg_ref, cbv_ref), ns_refs)):
                a = both[:, half * tn:(half + 1) * tn].reshape(nb * per_seq, blk, tn)
                tails = a[:, blk - n_hist:, :].reshape(nb, per_seq, n_hist, tn)
                hist = p_scr[...][:, None]
                if per_seq > 1:
                    hist = jnp.concatenate([hist, tails[:, :per_seq - 1]], axis=1)
                hist = hist.reshape(nb * per_seq, n_hist, tn)
                ys.append(_seg_conv(a, hist, cw_ref, cb_ref[...]).reshape(nb, seq, tn))
                p_scr[...] = tails[:, per_seq - 1]
                _seg_state_out(ns_ref, a.reshape(nb, per_seq, blk, tn)[:, per_seq - 1])
            hid_ref[...] = (_gelu_tanh(ys[0]) * ys[1]).astype(BF16)


def _up_geglu(us, w_up, states, conv_w, conv_b, blocks, *, tm, tn, to_round=()):
    d = w_up.shape[0]
    d_ff = w_up.shape[1] // 2
    width = conv_w.shape[0]
    n_j = d_ff // tn
    nbs = [u.shape[0] if u.shape[1] <= tm else 1 for u in us]
    seqs = [min(tm, u.shape[1]) for u in us]
    counts = [u.shape[1] // s for u, s in zip(us, seqs)]
    assert all(u.shape[0] == nb and u.shape[1] % s == 0 and s % BF16_ROWS == 0 for u, nb, s in zip(us, nbs, seqs))
    assert d_ff % tn == 0 and tn % LANES == 0
    starts, steps = _group_steps(counts)
    col_g = lambda j, i: (0, j)
    col_v = lambda j, i: (0, n_j + j)
    st_g = lambda j, i: (0, 0, j)
    st_v = lambda j, i: (0, 0, n_j + j)
    in_specs, out_specs, out_shape, hist, args = [], [], [], [], []
    for g, (u, st, nb, seq) in enumerate(zip(us, states, nbs, seqs)):
        row = lambda j, i, g=g: (0, _group_tile(i, starts[g], counts[g]), 0)
        out = lambda j, i, g=g: (0, _group_tile(i, starts[g], counts[g]), j)
        in_specs += [pl.BlockSpec((nb, seq, d), row),
                     pl.BlockSpec((nb, width - 1, tn), st_g), pl.BlockSpec((nb, width - 1, tn), st_v)]
        out_specs += [pl.BlockSpec((nb, seq, tn), out),
                      pl.BlockSpec((nb, width - 1, tn), st_g), pl.BlockSpec((nb, width - 1, tn), st_g)]
        out_shape += [jax.ShapeDtypeStruct((u.shape[0], u.shape[1], d_ff), BF16),
                      jax.ShapeDtypeStruct((u.shape[0], width - 1, d_ff), F32),
                      jax.ShapeDtypeStruct((u.shape[0], width - 1, d_ff), F32)]
        assert seq % blocks[g] == 0 and blocks[g] % SUBLANES == 0
        n_hist = min(blocks[g] // SUBLANES, width - 1) * SUBLANES
        hist += [pltpu.VMEM((nb, n_hist, tn), F32), pltpu.VMEM((nb, n_hist, tn), F32)]
        args += [u, st, st]
    side_in, side_out, side_shape = _side_cast_plan(to_round, n_j * steps, lambda j, i: j * steps + i)
    in_specs += [pl.BlockSpec((d, tn), col_g), pl.BlockSpec((d, tn), col_v),
                 pl.BlockSpec((width, tn), col_g), pl.BlockSpec((width, tn), col_v),
                 pl.BlockSpec((1, tn), col_g), pl.BlockSpec((1, tn), col_v)] + side_in
    res = pl.pallas_call(
        functools.partial(_up_geglu_kernel, starts=starts, counts=counts, blocks=tuple(blocks),
                          n_side=len(to_round)),
        grid=(n_j, steps),
        in_specs=in_specs,
        out_specs=out_specs + side_out,
        out_shape=out_shape + side_shape,
        scratch_shapes=[pltpu.VMEM((d, 2 * tn), BF16)] + hist,
        compiler_params=_params("arbitrary", "arbitrary"),
        name="up_geglu",
    )(*args, w_up, w_up, conv_w, conv_w, conv_b, conv_b, *to_round)
    return [tuple(res[3 * g:3 * g + 3]) for g in range(len(us))], list(res[3 * len(us):])


def _ffn_down_kernel(*refs, starts, counts):
    n_groups = len(counts)
    wd_ref, g_ref = refs[3 * n_groups:3 * n_groups + 2]
    o_refs = refs[3 * n_groups + 2:]
    i = pl.program_id(0)
    for g in range(n_groups):
        hid_ref, x_ref, unperm_ref = refs[3 * g:3 * g + 3]

        @pl.when((i >= starts[g]) & (i < starts[g] + counts[g]))
        def _(hid_ref=hid_ref, x_ref=x_ref, unperm_ref=unperm_ref, o_ref=o_refs[g]):
            y = jnp.dot(hid_ref[...], wd_ref[...], preferred_element_type=F32)
            branch = _rmsnorm(y, g_ref[...]).astype(BF16)
            blk = unperm_ref.shape[0]
            for r in range(0, branch.shape[0], blk):
                o_ref[r:r + blk, :] = x_ref[r:r + blk, :] + jnp.dot(unperm_ref[...], branch[r:r + blk, :],
                                                                    preferred_element_type=F32)


def _ffn_down(hids, xs, w_down, g_post, unperms, *, tm):
    d_ff, d = w_down.shape
    tms = [min(tm, x.shape[0]) for x in xs]
    counts = [x.shape[0] // t for x, t in zip(xs, tms)]
    assert all(x.shape[0] % t == 0 and t % un.shape[0] == 0 for x, t, un in zip(xs, tms, unperms))
    starts, steps = _group_steps(counts)
    in_specs, args = [], []
    for g, (hid, x, un) in enumerate(zip(hids, xs, unperms)):
        row = lambda i, g=g: (_group_tile(i, starts[g], counts[g]), 0)
        in_specs += [pl.BlockSpec((tms[g], d_ff), row), pl.BlockSpec((tms[g], d), row),
                     pl.BlockSpec(un.shape, lambda i: (0, 0))]
        args += [hid, x, un]
    return pl.pallas_call(
        functools.partial(_ffn_down_kernel, starts=starts, counts=counts),
        grid=(steps,),
        in_specs=in_specs + [_resident(w_down.shape), pl.BlockSpec((1, d), lambda i: (0, 0))],
        out_specs=[pl.BlockSpec((tms[g], d), lambda i, g=g: (_group_tile(i, starts[g], counts[g]), 0))
                   for g in range(len(xs))],
        out_shape=[jax.ShapeDtypeStruct(x.shape, F32) for x in xs],
        compiler_params=_params("arbitrary"),
        name="ffn_down",
    )(*args, w_down, g_post)


MM_TM, MM_TN = 2048, 1024
UP_TM, UP_TN = 1024, 512
SEQ_TILE = 256
ROW_TILE = 512
NORM_TM = 2048
N_XHEADS = 4


def _layer(groups, p):
    flat = [x.reshape(-1, x.shape[-1]) for x, *_ in groups]
    perms, blocks = [], []
    for x, *_ in groups:
        bsz, t, _ = x.shape
        perms.append(_segment_major_perm(*((1, SEQ_TILE) if t > SEQ_TILE else (bsz, t))))
        blocks.append(min(t, SEQ_TILE))
    u0s = [_rmsnorm_call(x2, p['g_mix_pre'], tm=NORM_TM, perm=pm) for x2, pm in zip(flat, perms)]
    raw_mems, u0s = lax.optimization_barrier(([m for g in groups for m in g[2:4]], u0s))
    mems = [m.reshape(-1, flat[0].shape[-1]) for m in raw_mems]
    zs = _proj(u0s, p['w_in'], BF16, tm=MM_TM, tn=MM_TN)
    big = max(range(len(groups)), key=lambda g: flat[g].shape[0])
    resident = [p['p_a'], p['p_b'], p['w_o'], p['w_q'], p['w_xo']]
    pre = []
    for g, ((x, pos0, _, _, s_a, h0, s_b, _), z) in enumerate(zip(groups, zs)):
        bsz, t, _ = x.shape
        pre.append(_mixer_pre(z.reshape(bsz, t, -1), s_a, h0[:, None, :], s_b,
                              p['conv_a_w'], p['conv_a_b'], p['w_ri'], p['b_r'], p['b_i'],
                              p['lru_lambda'], p['conv_b_w'], seq_tile=SEQ_TILE, pos0=pos0,
                              to_round=resident + mems if g == big else ()))
    p_a, p_b, w_o, w_q, w_xo, *mems = pre[big][5:]
    mids = []
    for g, ((x, *_), x2, z) in enumerate(zip(groups, flat, zs)):
        bsz, t, d = x.shape
        m = bsz * t
        ya, yb, ns_a, nh, ns_b = pre[g][:5]
        mem_k, mem_v = (mm.reshape(bsz, -1, d) for mm in mems[2 * g:2 * g + 2])
        x1 = _mixer_post(ya.reshape(m, -1), yb.reshape(m, -1), z, x2, p_a, p_b, w_o, p['g_mix_post'],
                         perms[g].T, tm=ROW_TILE)
        x2a, u3 = _xattn(x1.reshape(bsz, t, d), mem_k, mem_v, w_q, w_xo, p['g_x_pre'], p['g_x_post'],
                         p['g_ffn_pre'], perms[g], n_heads=N_XHEADS, seq_tile=ROW_TILE)
        mids.append((x2a, u3, ns_a, nh[:, 0, :], ns_b))
    ffn, (w_down,) = _up_geglu([u3 for _, u3, *_ in mids], p['w_up'], [g[7] for g in groups],
                               p['ffn_conv_w'], p['ffn_conv_b'], blocks, tm=UP_TM, tn=UP_TN,
                               to_round=[p['w_down']])
    x3s = _ffn_down([hid.reshape(-1, hid.shape[-1]) for hid, _, _ in ffn],
                    [x2a.reshape(-1, x2a.shape[-1]) for x2a, *_ in mids], w_down, p['g_ffn_post'],
                    [pm.T for pm in perms], tm=ROW_TILE)
    return [(x3.reshape(x2a.shape), ns_a, nh, ns_b, jnp.concatenate([ns_fg, ns_fv], axis=-1))
            for x3, (x2a, _, ns_a, nh, ns_b), (_, ns_fg, ns_fv) in zip(x3s, mids, ffn)]


def kernel(x_prompt, x_sample, mem_prompt, state_conv_a, state_rglru, state_conv_b, state_ffn_conv, cache_mem_k, cache_mem_v, g_mix_pre, g_mix_post, w_in, conv_a_w, conv_a_b, w_r, b_r, w_i, b_i, lru_lambda, conv_b_w, p_a, p_b, w_o, g_x_pre, g_x_post, g_mem, w_q, w_k, w_v, w_xo, g_ffn_pre, g_ffn_post, w_up, ffn_conv_w, ffn_conv_b, w_down):
    depth = w_in.shape[0]
    bsz, _, d = x_prompt.shape
    n_mem = mem_prompt.shape[1]
    yp, ys = x_prompt, x_sample
    outs = [[] for _ in range(10)]
    row = lambda v: v.reshape(1, -1).astype(F32)
    for l in range(depth):
        p = {'g_mix_pre': row(g_mix_pre[l]), 'g_mix_post': row(g_mix_post[l]), 'w_in': w_in[l],
             'conv_a_w': conv_a_w[l], 'conv_a_b': row(conv_a_b[l]),
             'w_ri': jnp.concatenate([w_r[l], w_i[l]], axis=-1).astype(BF16),
             'b_r': row(b_r[l]), 'b_i': row(b_i[l]), 'lru_lambda': row(lru_lambda[l]),
             'conv_b_w': conv_b_w[l], 'p_a': p_a[l], 'p_b': p_b[l],
             'w_o': w_o[l], 'g_x_pre': row(g_x_pre[l]), 'g_x_post': row(g_x_post[l]),
             'w_q': w_q[l], 'w_xo': w_xo[l],
             'g_ffn_pre': row(g_ffn_pre[l]), 'g_ffn_post': row(g_ffn_post[l]), 'w_up': w_up[l],
             'ffn_conv_w': ffn_conv_w[l], 'ffn_conv_b': row(ffn_conv_b[l]), 'w_down': w_down[l]}
        d_rnn, d_conv, d_up = conv_a_w.shape[2], conv_b_w.shape[2], ffn_conv_w.shape[2]
        mem_u = _rmsnorm_call(mem_prompt.reshape(bsz * n_mem, d), row(g_mem[l]), tm=NORM_TM)
        mk = _proj([mem_u], w_k[l], F32, tm=MM_TM, tn=MM_TN)[0].reshape(bsz, n_mem, d)
        mv = _proj([mem_u], w_v[l], F32, tm=MM_TM, tn=MM_TN)[0].reshape(bsz, n_mem, d)
        zeros = lambda *s: jnp.zeros(s, F32)
        prompt = (yp, 0, mk, mv,
                  zeros(bsz, conv_a_w.shape[1] - 1, d_rnn), zeros(bsz, d_rnn),
                  zeros(bsz, conv_b_w.shape[1] - 1, d_conv), zeros(bsz, ffn_conv_w.shape[1] - 1, d_up))
        sample = (ys, PAST_LEN, cache_mem_k[l], cache_mem_v[l], state_conv_a[l], state_rglru[l], state_conv_b[l],
                  state_ffn_conv[l])
        (ys, *s_states), (yp, *p_states) = _layer([sample, prompt], p)
        for o, v in zip(outs, (*p_states, mk.reshape(bsz, n_mem, N_XHEADS, -1),
                               mv.reshape(bsz, n_mem, N_XHEADS, -1), *s_states)):
            o.append(v)
    return (yp, ys) + tuple(jnp.stack(o) for o in outs)
```

```python
import functools

import jax
import jax.numpy as jnp
import numpy as np
from jax import lax
from jax.experimental import pallas as pl
from jax.experimental.pallas import tpu as pltpu

F32 = jnp.float32
BF16 = jnp.bfloat16

EPS = 1e-6
RG_C = 8.0
PAST_LEN = 2048

LANES = 128
SUBLANES = 8
BF16_ROWS = 16
VMEM_LIMIT_BYTES = 60 * 1024 * 1024

ROW_CHUNK = 256
CAST_ROWS = 256
SCAN_LANES = 2048


def _params(*semantics):
    return pltpu.CompilerParams(dimension_semantics=semantics, vmem_limit_bytes=VMEM_LIMIT_BYTES)


def _resident(shape):
    nd = len(shape)
    return pl.BlockSpec(shape, lambda *_: (0,) * nd, pipeline_mode=pl.Buffered(1))


def _rmsnorm(x, g):
    y = x * lax.rsqrt(jnp.mean(x * x, axis=-1, keepdims=True) + EPS)
    return y * g


def _sigmoid(x):
    return 0.5 * jnp.tanh(0.5 * x) + 0.5


def _gelu_tanh(x):
    c1 = (2.0 / jnp.pi) ** 0.5
    return x * (0.5 * jnp.tanh(x * (c1 + (c1 * 0.044715) * (x * x))) + 0.5)


def _segment_major_perm(n_seq, seq):
    n_groups = seq // SUBLANES
    row = np.arange(seq)
    time = (row % SUBLANES) * n_groups + row // SUBLANES
    p = np.zeros((seq, seq), np.float32)
    p[row, time] = 1.0
    return jnp.asarray(np.kron(np.eye(n_seq, dtype=np.float32), p), BF16)


def _seg_coords(n_groups, back):
    t = SUBLANES * n_groups - back
    return t % n_groups, t // n_groups


def _seg_hist_init(hist_ref, state_ref, n_groups):
    n_state = state_ref.shape[1]
    n_hist = hist_ref.shape[1] // SUBLANES
    hist_ref[...] = jnp.zeros(hist_ref.shape, F32)
    for back in range(1, n_state + 1):
        g, s = _seg_coords(n_groups, back)
        row = (g - (n_groups - n_hist)) * SUBLANES + s
        hist_ref[:, row:row + 1, :] = state_ref[:, n_state - back:n_state - back + 1, :]


def _seg_state_out(ns_ref, x_ref, cols=slice(None)):
    n_state = ns_ref.shape[1]
    n_groups = x_ref.shape[1] // SUBLANES
    for back in range(1, n_state + 1):
        g, s = _seg_coords(n_groups, back)
        row = g * SUBLANES + s
        ns_ref[:, n_state - back:n_state - back + 1, cols] = x_ref[:, row:row + 1, :]


def _seg_conv(x_ref, hist_ref, w_ref, bias):
    nb, seq, c = x_ref.shape
    n_groups = seq // SUBLANES
    n_hist = hist_ref.shape[1] // SUBLANES
    width = w_ref.shape[0]
    assert n_hist >= min(n_groups, width - 1)
    sub = lax.broadcasted_iota(jnp.int32, (1, SUBLANES, c), 1)
    y = None
    for k in range(width):
        lag = width - 1 - k
        if lag == 0:
            tap = x_ref[...]
        else:
            parts = []
            for g in range(min(lag, n_groups)):
                src = (g - lag) % n_groups
                crossed = (lag - g + n_groups - 1) // n_groups
                cur = x_ref[:, src * SUBLANES:(src + 1) * SUBLANES, :]
                h0 = (src - (n_groups - n_hist)) * SUBLANES
                prev = hist_ref[:, h0:h0 + SUBLANES, :]
                parts.append(pltpu.roll(jnp.where(sub >= SUBLANES - crossed, prev, cur), crossed, 1))
            if n_groups > lag:
                parts.append(x_ref[:, 0:(n_groups - lag) * SUBLANES, :])
            tap = jnp.concatenate(parts, axis=1)
        y = tap * w_ref[k:k + 1, :] if y is None else y + tap * w_ref[k:k + 1, :]
    return y if bias is None else y + bias


def _softplus(x):
    return jnp.maximum(x, 0.0) + jnp.log1p(jnp.exp(-jnp.abs(x)))


def _permute_rows(ref, perm_ref):
    blk = perm_ref.shape[0]
    for r in range(0, ref.shape[0], blk):
        ref[r:r + blk, :] = jnp.dot(perm_ref[...], ref[r:r + blk, :], preferred_element_type=F32).astype(BF16)


def _rmsnorm_kernel(x_ref, g_ref, *rest):
    u_ref = rest[-1]
    tm = x_ref.shape[0]
    chunk = min(ROW_CHUNK, tm)
    g = g_ref[...]

    def body(c, _):
        rows = pl.ds(pl.multiple_of(c * chunk, chunk), chunk)
        u_ref[rows, :] = _rmsnorm(x_ref[rows, :], g).astype(BF16)
        return None

    lax.fori_loop(0, tm // chunk, body, None)
    if len(rest) == 2:
        _permute_rows(u_ref, rest[0])


def _rmsnorm_call(x, g, *, tm, perm=None):
    m, d = x.shape
    tm = min(tm, m)
    assert m % tm == 0 and tm % min(ROW_CHUNK, tm) == 0 and (perm is None or tm % perm.shape[0] == 0)
    extra = [] if perm is None else [perm]
    return pl.pallas_call(
        _rmsnorm_kernel,
        grid=(m // tm,),
        in_specs=[pl.BlockSpec((tm, d), lambda i: (i, 0)), pl.BlockSpec((1, d), lambda i: (0, 0))]
        + [pl.BlockSpec(a.shape, lambda i: (0, 0)) for a in extra],
        out_specs=pl.BlockSpec((tm, d), lambda i: (i, 0)),
        out_shape=jax.ShapeDtypeStruct((m, d), BF16),
        compiler_params=_params("parallel"),
        name="rmsnorm",
    )(x, g, *extra)


def _cast_rows(src_ref, dst_ref, cols=slice(None)):
    rows_total = src_ref.shape[0]
    chunk = min(CAST_ROWS, rows_total)

    def body(c, _):
        rows = pl.ds(pl.multiple_of(c * chunk, chunk), chunk)
        dst_ref[rows, cols] = src_ref[rows, :].astype(BF16)
        return None

    lax.fori_loop(0, rows_total // chunk, body, None)


def _side_cast_plan(arrays, n_steps, flat_step):
    in_specs, out_specs, out_shape = [], [], []
    for a in arrays:
        rows, cols = a.shape
        chunk = next(c for c in range(BF16_ROWS, rows + 1, BF16_ROWS) if rows % c == 0 and rows // c <= n_steps)
        last = rows // chunk - 1
        spec = pl.BlockSpec((chunk, cols), lambda *idx, last=last: (jnp.minimum(flat_step(*idx), last), 0))
        in_specs.append(spec)
        out_specs.append(spec)
        out_shape.append(jax.ShapeDtypeStruct((rows, cols), BF16))
    return in_specs, out_specs, out_shape


def _side_cast(src_refs, dst_refs):
    for src, dst in zip(src_refs, dst_refs):
        dst[...] = src[...].astype(BF16)


def _group_steps(counts):
    starts = [sum(counts[:g]) for g in range(len(counts))]
    return starts, sum(counts)


def _group_tile(i, start, count):
    return jnp.clip(i - start, 0, count - 1)


def _proj_kernel(*refs, starts, counts):
    n_groups = len(counts)
    u_refs, w_ref, o_refs, w_scr = refs[:n_groups], refs[n_groups], refs[n_groups + 1:-1], refs[-1]
    i = pl.program_id(1)

    @pl.when(i == 0)
    def _():
        _cast_rows(w_ref, w_scr)

    for g in range(n_groups):
        @pl.when((i >= starts[g]) & (i < starts[g] + counts[g]))
        def _(g=g):
            o_refs[g][...] = jnp.dot(u_refs[g][...], w_scr[...],
                                     preferred_element_type=F32).astype(o_refs[g].dtype)


def _proj(us, w, out_dtype, *, tm, tn):
    d, n = w.shape
    tn = min(tn, n)
    tms = [min(tm, u.shape[0]) for u in us]
    counts = [u.shape[0] // t for u, t in zip(us, tms)]
    assert all(u.shape[0] % t == 0 for u, t in zip(us, tms)) and n % tn == 0 and d % min(CAST_ROWS, d) == 0
    starts, steps = _group_steps(counts)
    tile = lambda g: (lambda j, i: (_group_tile(i, starts[g], counts[g]), 0))
    out_tile = lambda g: (lambda j, i: (_group_tile(i, starts[g], counts[g]), j))
    return pl.pallas_call(
        functools.partial(_proj_kernel, starts=starts, counts=counts),
        grid=(n // tn, steps),
        in_specs=([pl.BlockSpec((tms[g], d), tile(g)) for g in range(len(us))]
                  + [pl.BlockSpec((d, tn), lambda j, i: (0, j))]),
        out_specs=[pl.BlockSpec((tms[g], tn), out_tile(g)) for g in range(len(us))],
        out_shape=[jax.ShapeDtypeStruct((u.shape[0], n), out_dtype) for u in us],
        scratch_shapes=[pltpu.VMEM((d, tn), BF16)],
        compiler_params=_params("arbitrary", "arbitrary"),
        name="proj",
    )(*us, w)


def _mixer_pre_kernel(*refs, d_rnn, d_conv, pos0, n_side):
    (z_ref, sa_ref, h0_ref, sb_ref, caw_ref, cab_ref, wri_ref, br_ref, bi_ref, lam_ref, cbw_ref) = refs[:11]
    side_in = refs[11:11 + n_side]
    ya_ref, yb_ref, nsa_ref, nh_ref, nsb_ref = refs[11 + n_side:16 + n_side]
    side_out = refs[16 + n_side:16 + 2 * n_side]
    xa_scr, ha_scr, xc_scr, xcb_scr, ri_scr, cb_scr, hb_scr, h_scr = refs[16 + 2 * n_side:]
    _side_cast(side_in, side_out)
    t = pl.program_id(1)
    nb, seq, _ = ya_ref.shape
    n_groups = seq // SUBLANES
    n_heads, head_dim, _ = wri_ref.shape

    @pl.when(t == 0)
    def _():
        _seg_hist_init(ha_scr, sa_ref, n_groups)
        _seg_hist_init(hb_scr, sb_ref, n_groups)
        h_scr[...] = jnp.broadcast_to(h0_ref[...], h_scr.shape)

    xa_scr[...] = z_ref[:, :, 0:d_rnn].astype(F32)
    xc = _seg_conv(xa_scr, ha_scr, caw_ref, cab_ref[...])
    xc_scr[...] = xc
    xcb_scr[...] = xc.reshape(nb * seq, d_rnn).astype(BF16)
    _seg_state_out(nsa_ref, xa_scr)
    ha_scr[...] = xa_scr[:, seq - ha_scr.shape[1]:, :]

    for h in range(n_heads):
        cols = slice(h * head_dim, (h + 1) * head_dim)
        ri = jnp.dot(xcb_scr[:, cols], wri_ref[h], preferred_element_type=F32)
        ri_scr[:, :, cols] = ri[:, :head_dim].reshape(nb, seq, head_dim)
        ri_scr[:, :, d_rnn + h * head_dim:d_rnn + (h + 1) * head_dim] = ri[:, head_dim:].reshape(nb, seq, head_dim)

    strip = max(LANES, SCAN_LANES // nb)
    pairs = seq // BF16_ROWS
    sub = lax.broadcasted_iota(jnp.int32, (1, SUBLANES, strip), 1)
    for c in range(d_rnn // strip):
        cs = slice(c * strip, (c + 1) * strip)
        cs_i = slice(d_rnn + c * strip, d_rnn + (c + 1) * strip)
        cs_g = slice(d_rnn + c * strip, d_rnn + (c + 1) * strip)
        rate = RG_C * _softplus(-lam_ref[:, cs])
        b_r = br_ref[:, cs]
        b_i = bi_ref[:, cs]

        def pass1(it, carry, first=False, cs=cs, cs_i=cs_i, rate=rate, b_r=b_r, b_i=b_i):
            h_loc, a_run = carry
            for half in range(BF16_ROWS // SUBLANES):
                rows = pl.ds(pl.multiple_of(it * BF16_ROWS + half * SUBLANES, SUBLANES), SUBLANES)
                r = _sigmoid(ri_scr[:, rows, cs] + b_r)
                i = _sigmoid(ri_scr[:, rows, cs_i] + b_i)
                neg_log_a = r * rate
                a = jnp.exp(-neg_log_a)
                mult = jnp.sqrt(jnp.tanh(neg_log_a) * (a * a + 1.0))
                if first and half == 0:
                    mult = jnp.where(jnp.logical_and(t == 0, sub == 0), 1.0, mult)
                b = mult * i * xc_scr[:, rows, cs]
                h_loc = a * h_loc + b
                a_run = a * a_run
                ri_scr[:, rows, cs] = h_loc
                ri_scr[:, rows, cs_i] = a_run
            return h_loc, a_run

        carry = (jnp.zeros((nb, SUBLANES, strip), F32), jnp.ones((nb, SUBLANES, strip), F32))
        start = 0
        if pos0 == 0:
            carry = pass1(0, carry, first=True)
            start = 1
        h_tot, a_tot = lax.fori_loop(start, pairs, pass1, carry)

        for s in (1, 2, 4):
            keep = sub >= s
            a_prev = pltpu.roll(a_tot, s, 1)
            h_prev = pltpu.roll(h_tot, s, 1)
            h_tot = jnp.where(keep, a_tot * h_prev + h_tot, h_tot)
            a_tot = jnp.where(keep, a_tot * a_prev, a_tot)
        h_in = h_scr[:, :, cs]
        seg_end = a_tot * h_in + h_tot
        seg_start = jnp.where(sub == 0, h_in, pltpu.roll(seg_end, 1, 1))
        h_last = jnp.broadcast_to(seg_end[:, SUBLANES - 1:SUBLANES, :], seg_end.shape)
        h_scr[:, :, cs] = h_last
        nh_ref[:, :, cs] = h_last[:, 0:1, :]
        start2 = jnp.concatenate([seg_start] * (BF16_ROWS // SUBLANES), axis=1)

        def pass2(it, _, cs=cs, cs_i=cs_i, cs_g=cs_g, start2=start2):
            rows = pl.ds(pl.multiple_of(it * BF16_ROWS, BF16_ROWS), BF16_ROWS)
            h = ri_scr[:, rows, cs] + ri_scr[:, rows, cs_i] * start2
            gate = _gelu_tanh(z_ref[:, rows, cs_g].astype(F32))
            ya_ref[:, rows, cs] = (h * gate).astype(BF16)
            return None

        lax.fori_loop(0, pairs, pass2, None)

    o_gb, o_gc, o_hb = 2 * d_rnn, 2 * d_rnn + d_conv, 2 * d_rnn + 2 * d_conv
    cb_scr[...] = z_ref[:, :, o_gc:o_gc + d_conv].astype(F32) * z_ref[:, :, o_hb:o_hb + d_conv].astype(F32)
    co = _seg_conv(cb_scr, hb_scr, cbw_ref, None)
    yb_ref[...] = (z_ref[:, :, o_gb:o_gb + d_conv].astype(F32) * co).astype(BF16)
    _seg_state_out(nsb_ref, cb_scr)
    hb_scr[...] = cb_scr[:, seq - hb_scr.shape[1]:, :]


def _mixer_pre(z, state_a, h0, state_b, conv_a_w, conv_a_b, w_ri, b_r, b_i, lam, conv_b_w, *, seq_tile, pos0,
               to_round=()):
    bsz, t, _ = z.shape
    d_rnn = conv_a_w.shape[1]
    d_conv = conv_b_w.shape[1]
    n_pre = 2 * d_rnn + 3 * d_conv
    nb = bsz if t <= seq_tile else 1
    seq = min(seq_tile, t)
    assert t % seq == 0 and seq % BF16_ROWS == 0 and bsz % nb == 0
    wa, wb = conv_a_w.shape[0], conv_b_w.shape[0]
    hist_a, hist_b = min(seq // SUBLANES, wa - 1), min(seq // SUBLANES, wb - 1)
    kern = functools.partial(_mixer_pre_kernel, d_rnn=d_rnn, d_conv=d_conv, pos0=pos0, n_side=len(to_round))
    n_t = t // seq
    side_in, side_out, side_shape = _side_cast_plan(to_round, (bsz // nb) * n_t, lambda b, i: b * n_t + i)
    row = lambda b, i: (b, i, 0)
    per_b = lambda b, i: (b, 0, 0)
    const2 = lambda b, i: (0, 0)
    return pl.pallas_call(
        kern,
        grid=(bsz // nb, t // seq),
        in_specs=[pl.BlockSpec((nb, seq, n_pre), row),
                  pl.BlockSpec((nb, wa - 1, d_rnn), per_b),
                  pl.BlockSpec((nb, 1, d_rnn), per_b),
                  pl.BlockSpec((nb, wb - 1, d_conv), per_b),
                  pl.BlockSpec(conv_a_w.shape, const2),
                  pl.BlockSpec((1, d_rnn), const2),
                  pl.BlockSpec(w_ri.shape, lambda b, i: (0, 0, 0)),
                  pl.BlockSpec((1, d_rnn), const2),
                  pl.BlockSpec((1, d_rnn), const2),
                  pl.BlockSpec((1, d_rnn), const2),
                  pl.BlockSpec(conv_b_w.shape, const2)] + side_in,
        out_specs=[pl.BlockSpec((nb, seq, d_rnn), row),
                   pl.BlockSpec((nb, seq, d_conv), row),
                   pl.BlockSpec((nb, wa - 1, d_rnn), per_b),
                   pl.BlockSpec((nb, 1, d_rnn), per_b),
                   pl.BlockSpec((nb, wb - 1, d_conv), per_b)] + side_out,
        out_shape=[jax.ShapeDtypeStruct((bsz, t, d_rnn), BF16),
                   jax.ShapeDtypeStruct((bsz, t, d_conv), BF16),
                   jax.ShapeDtypeStruct((bsz, wa - 1, d_rnn), F32),
                   jax.ShapeDtypeStruct((bsz, 1, d_rnn), F32),
                   jax.ShapeDtypeStruct((bsz, wb - 1, d_conv), F32)] + side_shape,
        scratch_shapes=[pltpu.VMEM((nb, seq, d_rnn), F32),
                        pltpu.VMEM((nb, hist_a * SUBLANES, d_rnn), F32),
                        pltpu.VMEM((nb, seq, d_rnn), F32),
                        pltpu.VMEM((nb * seq, d_rnn), BF16),
                        pltpu.VMEM((nb, seq, 2 * d_rnn), F32),
                        pltpu.VMEM((nb, seq, d_conv), F32),
                        pltpu.VMEM((nb, hist_b * SUBLANES, d_conv), F32),
                        pltpu.VMEM((nb, SUBLANES, d_rnn), F32)],
        compiler_params=_params("arbitrary", "arbitrary"),
        name="mixer_pre",
    )(z, state_a, h0, state_b, conv_a_w, conv_a_b, w_ri, b_r, b_i, lam, conv_b_w, *to_round)


def _mixer_post_kernel(ya_ref, yb_ref, ga0_ref, ga1_ref, gb0_ref, gb1_ref, x_ref, pa_ref, pb_ref, wo_ref, g_ref,
                       unperm_ref, o_ref, mix_scr):
    half = ga0_ref.shape[1]
    y_a = jnp.dot(ya_ref[...], pa_ref[...], preferred_element_type=F32)
    y_b = jnp.dot(yb_ref[...], pb_ref[...], preferred_element_type=F32)
    for c, (ga_ref, gb_ref) in enumerate(((ga0_ref, gb0_ref), (ga1_ref, gb1_ref))):
        cols = slice(c * half, (c + 1) * half)
        mix = (jax.nn.sigmoid(ga_ref[...].astype(F32)) * y_a[:, cols]
               + jax.nn.sigmoid(gb_ref[...].astype(F32)) * y_b[:, cols])
        mix_scr[:, cols] = mix.astype(BF16)
    _permute_rows(mix_scr, unperm_ref)
    out = jnp.dot(mix_scr[...], wo_ref[...], preferred_element_type=F32)
    o_ref[...] = x_ref[...] + _rmsnorm(out, g_ref[...])


def _mixer_post(ya, yb, z, x, p_a, p_b, w_o, g_post, unperm, *, tm):
    m, d = x.shape
    d_rnn, d_conv = ya.shape[1], yb.shape[1]
    tm = min(tm, m)
    half = d // 2
    gate0 = (2 * d_rnn + 3 * d_conv) // half
    assert m % tm == 0 and (2 * d_rnn + 3 * d_conv) % half == 0 and tm % unperm.shape[0] == 0
    gate_spec = lambda k: pl.BlockSpec((tm, half), lambda i: (i, gate0 + k))
    row = lambda i: (i, 0)
    return pl.pallas_call(
        _mixer_post_kernel,
        grid=(m // tm,),
        in_specs=[pl.BlockSpec((tm, d_rnn), row), pl.BlockSpec((tm, d_conv), row),
                  gate_spec(0), gate_spec(1), gate_spec(2), gate_spec(3),
                  pl.BlockSpec((tm, d), row),
                  _resident(p_a.shape), _resident(p_b.shape), _resident(w_o.shape),
                  pl.BlockSpec((1, d), lambda i: (0, 0)), pl.BlockSpec(unperm.shape, lambda i: (0, 0))],
        out_specs=pl.BlockSpec((tm, d), row),
        out_shape=jax.ShapeDtypeStruct((m, d), F32),
        scratch_shapes=[pltpu.VMEM((tm, d), BF16)],
        compiler_params=_params("parallel"),
        name="mixer_post",
    )(ya, yb, z, z, z, z, x, p_a, p_b, w_o, g_post, unperm)


def _xattn_kernel(x_ref, k_ref, v_ref, wq_ref, wxo_ref, gpre_ref, gpost_ref, gnext_ref, perm_ref, o_ref, u_ref,
                  q_scr, o_scr, *, n_heads):
    nb, seq, d = x_ref.shape
    hd = d // n_heads
    scale = hd ** -0.5
    x = x_ref[...].reshape(nb * seq, d)
    u = _rmsnorm(x, gpre_ref[...]).astype(BF16)
    q_scr[...] = jnp.dot(u, wq_ref[...], preferred_element_type=F32).astype(BF16)
    pairs = [(b, slice(b * seq, (b + 1) * seq), slice(h * hd, (h + 1) * hd))
             for b in range(nb) for h in range(n_heads)]

    def scores(b, rows, cols):
        return lax.dot_general(q_scr[rows, cols], k_ref[b, :, cols], (((1,), (1,)), ((), ())),
                               preferred_element_type=F32) * scale

    def softmax(s):
        p = jnp.exp(s - jnp.max(s, axis=-1, keepdims=True))
        return (p / jnp.sum(p, axis=-1, keepdims=True)).astype(BF16)

    def weighted(att, b, rows, cols):
        o_scr[rows, cols] = jnp.dot(att, v_ref[b, :, cols], preferred_element_type=F32).astype(BF16)

    atts = [softmax(s) for s in [scores(*pr) for pr in pairs]]
    for att, pr in zip(atts, pairs):
        weighted(att, *pr)
    out = jnp.dot(o_scr[...], wxo_ref[...], preferred_element_type=F32)
    x_new = x + _rmsnorm(out, gpost_ref[...])
    o_ref[...] = x_new.reshape(nb, seq, d)
    u_next = _rmsnorm(x_new, gnext_ref[...]).astype(BF16)
    blk = perm_ref.shape[0]
    u_next = jnp.concatenate([jnp.dot(perm_ref[...], u_next[r:r + blk, :], preferred_element_type=F32)
                              for r in range(0, nb * seq, blk)], axis=0)
    u_ref[...] = u_next.astype(BF16).reshape(nb, seq, d)


def _xattn(x, mem_k, mem_v, w_q, w_xo, g_pre, g_post, g_next, perm, *, n_heads, seq_tile):
    bsz, t, d = x.shape
    n_mem = mem_k.shape[1]
    nb = bsz if t <= seq_tile else 1
    seq = min(seq_tile, t)
    assert t % seq == 0 and seq % BF16_ROWS == 0 and bsz % nb == 0 and (nb * seq) % perm.shape[0] == 0
    single = bsz // nb == 1
    mem_spec = (_resident((nb, n_mem, d)) if single else pl.BlockSpec((nb, n_mem, d), lambda b, i: (b, 0, 0)))
    row = lambda b, i: (b, i, 0)
    gain = pl.BlockSpec((1, d), lambda b, i: (0, 0))
    return pl.pallas_call(
        functools.partial(_xattn_kernel, n_heads=n_heads),
        grid=(bsz // nb, t // seq),
        in_specs=[pl.BlockSpec((nb, seq, d), row), mem_spec, mem_spec,
                  _resident(w_q.shape), _resident(w_xo.shape), gain, gain, gain,
                  pl.BlockSpec(perm.shape, lambda b, i: (0, 0))],
        out_specs=[pl.BlockSpec((nb, seq, d), row), pl.BlockSpec((nb, seq, d), row)],
        out_shape=[jax.ShapeDtypeStruct((bsz, t, d), F32), jax.ShapeDtypeStruct((bsz, t, d), BF16)],
        scratch_shapes=[pltpu.VMEM((nb * seq, d), BF16), pltpu.VMEM((nb * seq, d), BF16)],
        compiler_params=_params("parallel", "parallel"),
        name="xattn",
    )(x, mem_k, mem_v, w_q, w_xo, g_pre, g_post, g_next, perm)


def _up_geglu_kernel(*refs, starts, counts, blocks, n_side):
    n_groups = len(counts)
    ins, rest = refs[:3 * n_groups], refs[3 * n_groups:]
    wg_ref, wv_ref, cwg_ref, cwv_ref, cbg_ref, cbv_ref = rest[:6]
    side_in, rest = rest[6:6 + n_side], rest[6 + n_side:]
    outs, rest = rest[:3 * n_groups], rest[3 * n_groups:]
    side_out, scr = rest[:n_side], rest[n_side:]
    w_scr, hists = scr[0], scr[1:]
    _side_cast(side_in, side_out)
    i = pl.program_id(1)
    tn = wg_ref.shape[1]

    @pl.when(i == 0)
    def _():
        _cast_rows(wg_ref, w_scr, slice(0, tn))
        _cast_rows(wv_ref, w_scr, slice(tn, 2 * tn))

    for g in range(n_groups):
        u_ref, st_refs = ins[3 * g], ins[3 * g + 1:3 * g + 3]
        hid_ref, ns_refs = outs[3 * g], outs[3 * g + 1:3 * g + 3]
        p_scrs = hists[2 * g:2 * g + 2]

        @pl.when(i == starts[g])
        def _(st_refs=st_refs, p_scrs=p_scrs, blk=blocks[g]):
            for st_ref, p_scr in zip(st_refs, p_scrs):
                _seg_hist_init(p_scr, st_ref, blk // SUBLANES)

        @pl.when((i >= starts[g]) & (i < starts[g] + counts[g]))
        def _(u_ref=u_ref, hid_ref=hid_ref, ns_refs=ns_refs, p_scrs=p_scrs, blk=blocks[g]):
            nb, seq, k_dim = u_ref.shape
            per_seq = seq // blk
            n_hist = p_scrs[0].shape[1]
            both = jnp.dot(u_ref[...].reshape(nb * seq, k_dim), w_scr[...], preferred_element_type=F32)
            ys = []
            for half, (p_scr, cw_ref, cb_ref, ns_ref) in enumerate(zip(p_scrs, (cwg_ref, cwv_ref),
                                                                     (cbg_ref, cbv_ref), ns_refs)):
                a = both[:, half * tn:(half + 1) * tn].reshape(nb * per_seq, blk, tn)
                tails = a[:, blk - n_hist:, :].reshape(nb, per_seq, n_hist, tn)
                hist = p_scr[...][:, None]
                if per_seq > 1:
                    hist = jnp.concatenate([hist, tails[:, :per_seq - 1]], axis=1)
                hist = hist.reshape(nb * per_seq, n_hist, tn)
                ys.append(_seg_conv(a, hist, cw_ref, cb_ref[...]).reshape(nb, seq, tn))
                p_scr[...] = tails[:, per_seq - 1]
                _seg_state_out(ns_ref, a.reshape(nb, per_seq, blk, tn)[:, per_seq - 1])
            hid_ref[...] = (_gelu_tanh(ys[0]) * ys[1]).astype(BF16)


def _up_geglu(us, w_up, states, conv_w, conv_b, blocks, *, tm, tn, to_round=()):
    d = w_up.shape[0]
    d_ff = w_up.shape[1] // 2
    width = conv_w.shape[0]
    n_j = d_ff // tn
    nbs = [u.shape[0] if u.shape[1] <= tm else 1 for u in us]
    seqs = [min(tm, u.shape[1]) for u in us]
    counts = [u.shape[1] // s for u, s in zip(us, seqs)]
    assert all(u.shape[0] == nb and u.shape[1] % s == 0 and s % BF16_ROWS == 0 for u, nb, s in zip(us, nbs, seqs))
    assert d_ff % tn == 0 and tn % LANES == 0
    starts, steps = _group_steps(counts)
    col_g = lambda j, i: (0, j)
    col_v = lambda j, i: (0, n_j + j)
    st_g = lambda j, i: (0, 0, j)
    st_v = lambda j, i: (0, 0, n_j + j)
    in_specs, out_specs, out_shape, hist, args = [], [], [], [], []
    for g, (u, st, nb, seq) in enumerate(zip(us, states, nbs, seqs)):
        row = lambda j, i, g=g: (0, _group_tile(i, starts[g], counts[g]), 0)
        out = lambda j, i, g=g: (0, _group_tile(i, starts[g], counts[g]), j)
        in_specs += [pl.BlockSpec((nb, seq, d), row),
                     pl.BlockSpec((nb, width - 1, tn), st_g), pl.BlockSpec((nb, width - 1, tn), st_v)]
        out_specs += [pl.BlockSpec((nb, seq, tn), out),
                      pl.BlockSpec((nb, width - 1, tn), st_g), pl.BlockSpec((nb, width - 1, tn), st_g)]
        out_shape += [jax.ShapeDtypeStruct((u.shape[0], u.shape[1], d_ff), BF16),
                      jax.ShapeDtypeStruct((u.shape[0], width - 1, d_ff), F32),
                      jax.ShapeDtypeStruct((u.shape[0], width - 1, d_ff), F32)]
        assert seq % blocks[g] == 0 and blocks[g] % SUBLANES == 0
        n_hist = min(blocks[g] // SUBLANES, width - 1) * SUBLANES
        hist += [pltpu.VMEM((nb, n_hist, tn), F32), pltpu.VMEM((nb, n_hist, tn), F32)]
        args += [u, st, st]
    side_in, side_out, side_shape = _side_cast_plan(to_round, n_j * steps, lambda j, i: j * steps + i)
    in_specs += [pl.BlockSpec((d, tn), col_g), pl.BlockSpec((d, tn), col_v),
                 pl.BlockSpec((width, tn), col_g), pl.BlockSpec((width, tn), col_v),
                 pl.BlockSpec((1, tn), col_g), pl.BlockSpec((1, tn), col_v)] + side_in
    res = pl.pallas_call(
        functools.partial(_up_geglu_kernel, starts=starts, counts=counts, blocks=tuple(blocks),
                          n_side=len(to_round)),
        grid=(n_j, steps),
        in_specs=in_specs,
        out_specs=out_specs + side_out,
        out_shape=out_shape + side_shape,
        scratch_shapes=[pltpu.VMEM((d, 2 * tn), BF16)] + hist,
        compiler_params=_params("arbitrary", "arbitrary"),
        name="up_geglu",
    )(*args, w_up, w_up, conv_w, conv_w, conv_b, conv_b, *to_round)
    return [tuple(res[3 * g:3 * g + 3]) for g in range(len(us))], list(res[3 * len(us):])


def _ffn_down_kernel(*refs, starts, counts):
    n_groups = len(counts)
    wd_ref, g_ref = refs[3 * n_groups:3 * n_groups + 2]
    o_refs = refs[3 * n_groups + 2:]
    i = pl.program_id(0)
    for g in range(n_groups):
        hid_ref, x_ref, unperm_ref = refs[3 * g:3 * g + 3]

        @pl.when((i >= starts[g]) & (i < starts[g] + counts[g]))
        def _(hid_ref=hid_ref, x_ref=x_ref, unperm_ref=unperm_ref, o_ref=o_refs[g]):
            y = jnp.dot(hid_ref[...], wd_ref[...], preferred_element_type=F32)
            branch = _rmsnorm(y, g_ref[...]).astype(BF16)
            blk = unperm_ref.shape[0]
            for r in range(0, branch.shape[0], blk):
                o_ref[r:r + blk, :] = x_ref[r:r + blk, :] + jnp.dot(unperm_ref[...], branch[r:r + blk, :],
                                                                    preferred_element_type=F32)


def _ffn_down(hids, xs, w_down, g_post, unperms, *, tm):
    d_ff, d = w_down.shape
    tms = [min(tm, x.shape[0]) for x in xs]
    counts = [x.shape[0] // t for x, t in zip(xs, tms)]
    assert all(x.shape[0] % t == 0 and t % un.shape[0] == 0 for x, t, un in zip(xs, tms, unperms))
    starts, steps = _group_steps(counts)
    in_specs, args = [], []
    for g, (hid, x, un) in enumerate(zip(hids, xs, unperms)):
        row = lambda i, g=g: (_group_tile(i, starts[g], counts[g]), 0)
        in_specs += [pl.BlockSpec((tms[g], d_ff), row), pl.BlockSpec((tms[g], d), row),
                     pl.BlockSpec(un.shape, lambda i: (0, 0))]
        args += [hid, x, un]
    return pl.pallas_call(
        functools.partial(_ffn_down_kernel, starts=starts, counts=counts),
        grid=(steps,),
        in_specs=in_specs + [_resident(w_down.shape), pl.BlockSpec((1, d), lambda i: (0, 0))],
        out_specs=[pl.BlockSpec((tms[g], d), lambda i, g=g: (_group_tile(i, starts[g], counts[g]), 0))
                   for g in range(len(xs))],
        out_shape=[jax.ShapeDtypeStruct(x.shape, F32) for x in xs],
        compiler_params=_params("arbitrary"),
        name="ffn_down",
    )(*args, w_down, g_post)


MM_TM, MM_TN = 2048, 1024
UP_TM, UP_TN = 1024, 512
SEQ_TILE = 256
ROW_TILE = 512
NORM_TM = 2048
N_XHEADS = 4


def _layer(groups, p):
    flat = [x.reshape(-1, x.shape[-1]) for x, *_ in groups]
    perms, blocks = [], []
    for x, *_ in groups:
        bsz, t, _ = x.shape
        perms.append(_segment_major_perm(*((1, SEQ_TILE) if t > SEQ_TILE else (bsz, t))))
        blocks.append(min(t, SEQ_TILE))
    u0s = [_rmsnorm_call(x2, p['g_mix_pre'], tm=NORM_TM, perm=pm) for x2, pm in zip(flat, perms)]
    raw_mems, u0s = lax.optimization_barrier(([m for g in groups for m in g[2:4]], u0s))
    mems = [m.reshape(-1, flat[0].shape[-1]) for m in raw_mems]
    zs = _proj(u0s, p['w_in'], BF16, tm=MM_TM, tn=MM_TN)
    big = max(range(len(groups)), key=lambda g: flat[g].shape[0])
    resident = [p['p_a'], p['p_b'], p['w_o'], p['w_q'], p['w_xo']]
    pre = []
    for g, ((x, pos0, _, _, s_a, h0, s_b, _), z) in enumerate(zip(groups, zs)):
        bsz, t, _ = x.shape
        pre.append(_mixer_pre(z.reshape(bsz, t, -1), s_a, h0[:, None, :], s_b,
                              p['conv_a_w'], p['conv_a_b'], p['w_ri'], p['b_r'], p['b_i'],
                              p['lru_lambda'], p['conv_b_w'], seq_tile=SEQ_TILE, pos0=pos0,
                              to_round=resident + mems if g == big else ()))
    p_a, p_b, w_o, w_q, w_xo, *mems = pre[big][5:]
    mids = []
    for g, ((x, *_), x2, z) in enumerate(zip(groups, flat, zs)):
        bsz, t, d = x.shape
        m = bsz * t
        ya, yb, ns_a, nh, ns_b = pre[g][:5]
        mem_k, mem_v = (mm.reshape(bsz, -1, d) for mm in mems[2 * g:2 * g + 2])
        x1 = _mixer_post(ya.reshape(m, -1), yb.reshape(m, -1), z, x2, p_a, p_b, w_o, p['g_mix_post'],
                         perms[g].T, tm=ROW_TILE)
        x2a, u3 = _xattn(x1.reshape(bsz, t, d), mem_k, mem_v, w_q, w_xo, p['g_x_pre'], p['g_x_post'],
                         p['g_ffn_pre'], perms[g], n_heads=N_XHEADS, seq_tile=ROW_TILE)
        mids.append((x2a, u3, ns_a, nh[:, 0, :], ns_b))
    ffn, (w_down,) = _up_geglu([u3 for _, u3, *_ in mids], p['w_up'], [g[7] for g in groups],
                               p['ffn_conv_w'], p['ffn_conv_b'], blocks, tm=UP_TM, tn=UP_TN,
                               to_round=[p['w_down']])
    x3s = _ffn_down([hid.reshape(-1, hid.shape[-1]) for hid, _, _ in ffn],
                    [x2a.reshape(-1, x2a.shape[-1]) for x2a, *_ in mids], w_down, p['g_ffn_post'],
                    [pm.T for pm in perms], tm=ROW_TILE)
    return [(x3.reshape(x2a.shape), ns_a, nh, ns_b, jnp.concatenate([ns_fg, ns_fv], axis=-1))
            for x3, (x2a, _, ns_a, nh, ns_b), (_, ns_fg, ns_fv) in zip(x3s, mids, ffn)]


def kernel(x_prompt, x_sample, mem_prompt, state_conv_a, state_rglru, state_conv_b, state_ffn_conv, cache_mem_k, cache_mem_v, g_mix_pre, g_mix_post, w_in, conv_a_w, conv_a_b, w_r, b_r, w_i, b_i, lru_lambda, conv_b_w, p_a, p_b, w_o, g_x_pre, g_x_post, g_mem, w_q, w_k, w_v, w_xo, g_ffn_pre, g_ffn_post, w_up, ffn_conv_w, ffn_conv_b, w_down):
    depth = w_in.shape[0]
    bsz, _, d = x_prompt.shape
    n_mem = mem_prompt.shape[1]
    yp, ys = x_prompt, x_sample
    outs = [[] for _ in range(10)]
    row = lambda v: v.reshape(1, -1).astype(F32)
    for l in range(depth):
        p = {'g_mix_pre': row(g_mix_pre[l]), 'g_mix_post': row(g_mix_post[l]), 'w_in': w_in[l],
             'conv_a_w': conv_a_w[l], 'conv_a_b': row(conv_a_b[l]),
             'w_ri': jnp.concatenate([w_r[l], w_i[l]], axis=-1).astype(BF16),
             'b_r': row(b_r[l]), 'b_i': row(b_i[l]), 'lru_lambda': row(lru_lambda[l]),
             'conv_b_w': conv_b_w[l], 'p_a': p_a[l], 'p_b': p_b[l],
             'w_o': w_o[l], 'g_x_pre': row(g_x_pre[l]), 'g_x_post': row(g_x_post[l]),
             'w_q': w_q[l], 'w_xo': w_xo[l],
             'g_ffn_pre': row(g_ffn_pre[l]), 'g_ffn_post': row(g_ffn_post[l]), 'w_up': w_up[l],
             'ffn_conv_w': ffn_conv_w[l], 'ffn_conv_b': row(ffn_conv_b[l]), 'w_down': w_down[l]}
        d_rnn, d_conv, d_up = conv_a_w.shape[2], conv_b_w.shape[2], ffn_conv_w.shape[2]
        mem_u = _rmsnorm_call(mem_prompt.reshape(bsz * n_mem, d), row(g_mem[l]), tm=NORM_TM)
        mk = _proj([mem_u], w_k[l], F32, tm=MM_TM, tn=MM_TN)[0].reshape(bsz, n_mem, d)
        mv = _proj([mem_u], w_v[l], F32, tm=MM_TM, tn=MM_TN)[0].reshape(bsz, n_mem, d)
        zeros = lambda *s: jnp.zeros(s, F32)
        prompt = (yp, 0, mk, mv,
                  zeros(bsz, conv_a_w.shape[1] - 1, d_rnn), zeros(bsz, d_rnn),
                  zeros(bsz, conv_b_w.shape[1] - 1, d_conv), zeros(bsz, ffn_conv_w.shape[1] - 1, d_up))
        sample = (ys, PAST_LEN, cache_mem_k[l], cache_mem_v[l], state_conv_a[l], state_rglru[l], state_conv_b[l],
                  state_ffn_conv[l])
        (ys, *s_states), (yp, *p_states) = _layer([sample, prompt], p)
        for o, v in zip(outs, (*p_states, mk.reshape(bsz, n_mem, N_XHEADS, -1),
                               mv.reshape(bsz, n_mem, N_XHEADS, -1), *s_states)):
            o.append(v)
    return (yp, ys) + tuple(jnp.stack(o) for o in outs)
```

```python
import functools

import jax
import jax.numpy as jnp
import numpy as np
from jax import lax
from jax.experimental import pallas as pl
from jax.experimental.pallas import tpu as pltpu

F32 = jnp.float32
BF16 = jnp.bfloat16

EPS = 1e-6
RG_C = 8.0
PAST_LEN = 2048

LANES = 128
SUBLANES = 8
BF16_ROWS = 16
VMEM_LIMIT_BYTES = 60 * 1024 * 1024

ROW_CHUNK = 256
CAST_ROWS = 256
SCAN_LANES = 2048


def _params(*semantics):
    return pltpu.CompilerParams(dimension_semantics=semantics, vmem_limit_bytes=VMEM_LIMIT_BYTES)


def _resident(shape):
    nd = len(shape)
    return pl.BlockSpec(shape, lambda *_: (0,) * nd, pipeline_mode=pl.Buffered(1))


def _rmsnorm(x, g):
    y = x * lax.rsqrt(jnp.mean(x * x, axis=-1, keepdims=True) + EPS)
    return y * g


def _sigmoid(x):
    return 0.5 * jnp.tanh(0.5 * x) + 0.5


def _gelu_tanh(x):
    c1 = (2.0 / jnp.pi) ** 0.5
    return x * (0.5 * jnp.tanh(x * (c1 + (c1 * 0.044715) * (x * x))) + 0.5)


def _segment_major_perm(n_seq, seq):
    n_groups = seq // SUBLANES
    row = np.arange(seq)
    time = (row % SUBLANES) * n_groups + row // SUBLANES
    p = np.zeros((seq, seq), np.float32)
    p[row, time] = 1.0
    return jnp.asarray(np.kron(np.eye(n_seq, dtype=np.float32), p), BF16)


def _seg_coords(n_groups, back):
    t = SUBLANES * n_groups - back
    return t % n_groups, t // n_groups


def _seg_hist_init(hist_ref, state_ref, n_groups):
    n_state = state_ref.shape[1]
    n_hist = hist_ref.shape[1] // SUBLANES
    hist_ref[...] = jnp.zeros(hist_ref.shape, F32)
    for back in range(1, n_state + 1):
        g, s = _seg_coords(n_groups, back)
        row = (g - (n_groups - n_hist)) * SUBLANES + s
        hist_ref[:, row:row + 1, :] = state_ref[:, n_state - back:n_state - back + 1, :]


def _seg_state_out(ns_ref, x_ref, cols=slice(None)):
    n_state = ns_ref.shape[1]
    n_groups = x_ref.shape[1] // SUBLANES
    for back in range(1, n_state + 1):
        g, s = _seg_coords(n_groups, back)
        row = g * SUBLANES + s
        ns_ref[:, n_state - back:n_state - back + 1, cols] = x_ref[:, row:row + 1, :]


def _seg_conv(x_ref, hist_ref, w_ref, bias):
    nb, seq, c = x_ref.shape
    n_groups = seq // SUBLANES
    n_hist = hist_ref.shape[1] // SUBLANES
    width = w_ref.shape[0]
    assert n_hist >= min(n_groups, width - 1)
    sub = lax.broadcasted_iota(jnp.int32, (1, SUBLANES, c), 1)
    y = None
    for k in range(width):
        lag = width - 1 - k
        if lag == 0:
            tap = x_ref[...]
        else:
            parts = []
            for g in range(min(lag, n_groups)):
                src = (g - lag) % n_groups
                crossed = (lag - g + n_groups - 1) // n_groups
                cur = x_ref[:, src * SUBLANES:(src + 1) * SUBLANES, :]
                h0 = (src - (n_groups - n_hist)) * SUBLANES
                prev = hist_ref[:, h0:h0 + SUBLANES, :]
                parts.append(pltpu.roll(jnp.where(sub >= SUBLANES - crossed, prev, cur), crossed, 1))
            if n_groups > lag:
                parts.append(x_ref[:, 0:(n_groups - lag) * SUBLANES, :])
            tap = jnp.concatenate(parts, axis=1)
        y = tap * w_ref[k:k + 1, :] if y is None else y + tap * w_ref[k:k + 1, :]
    return y if bias is None else y + bias


def _softplus(x):
    return jnp.maximum(x, 0.0) + jnp.log1p(jnp.exp(-jnp.abs(x)))


def _permute_rows(ref, perm_ref):
    blk = perm_ref.shape[0]
    for r in range(0, ref.shape[0], blk):
        ref[r:r + blk, :] = jnp.dot(perm_ref[...], ref[r:r + blk, :], preferred_element_type=F32).astype(BF16)


def _rmsnorm_kernel(x_ref, g_ref, *rest):
    u_ref = rest[-1]
    tm = x_ref.shape[0]
    chunk = min(ROW_CHUNK, tm)
    g = g_ref[...]

    def body(c, _):
        rows = pl.ds(pl.multiple_of(c * chunk, chunk), chunk)
        u_ref[rows, :] = _rmsnorm(x_ref[rows, :], g).astype(BF16)
        return None

    lax.fori_loop(0, tm // chunk, body, None)
    if len(rest) == 2:
        _permute_rows(u_ref, rest[0])


def _rmsnorm_call(x, g, *, tm, perm=None):
    m, d = x.shape
    tm = min(tm, m)
    assert m % tm == 0 and tm % min(ROW_CHUNK, tm) == 0 and (perm is None or tm % perm.shape[0] == 0)
    extra = [] if perm is None else [perm]
    return pl.pallas_call(
        _rmsnorm_kernel,
        grid=(m // tm,),
        in_specs=[pl.BlockSpec((tm, d), lambda i: (i, 0)), pl.BlockSpec((1, d), lambda i: (0, 0))]
        + [pl.BlockSpec(a.shape, lambda i: (0, 0)) for a in extra],
        out_specs=pl.BlockSpec((tm, d), lambda i: (i, 0)),
        out_shape=jax.ShapeDtypeStruct((m, d), BF16),
        compiler_params=_params("parallel"),
        name="rmsnorm",
    )(x, g, *extra)


def _cast_rows(src_ref, dst_ref, cols=slice(None)):
    rows_total = src_ref.shape[0]
    chunk = min(CAST_ROWS, rows_total)

    def body(c, _):
        rows = pl.ds(pl.multiple_of(c * chunk, chunk), chunk)
        dst_ref[rows, cols] = src_ref[rows, :].astype(BF16)
        return None

    lax.fori_loop(0, rows_total // chunk, body, None)


def _side_cast_plan(arrays, n_steps, flat_step):
    in_specs, out_specs, out_shape = [], [], []
    for a in arrays:
        rows, cols = a.shape
        chunk = next(c for c in range(BF16_ROWS, rows + 1, BF16_ROWS) if rows % c == 0 and rows // c <= n_steps)
        last = rows // chunk - 1
        spec = pl.BlockSpec((chunk, cols), lambda *idx, last=last: (jnp.minimum(flat_step(*idx), last), 0))
        in_specs.append(spec)
        out_specs.append(spec)
        out_shape.append(jax.ShapeDtypeStruct((rows, cols), BF16))
    return in_specs, out_specs, out_shape


def _side_cast(src_refs, dst_refs):
    for src, dst in zip(src_refs, dst_refs):
        dst[...] = src[...].astype(BF16)


def _group_steps(counts):
    starts = [sum(counts[:g]) for g in range(len(counts))]
    return starts, sum(counts)


def _group_tile(i, start, count):
    return jnp.clip(i - start, 0, count - 1)


def _proj_kernel(*refs, starts, counts):
    n_groups = len(counts)
    u_refs, w_ref, o_refs, w_scr = refs[:n_groups], refs[n_groups], refs[n_groups + 1:-1], refs[-1]
    i = pl.program_id(1)

    @pl.when(i == 0)
    def _():
        _cast_rows(w_ref, w_scr)

    for g in range(n_groups):
        @pl.when((i >= starts[g]) & (i < starts[g] + counts[g]))
        def _(g=g):
            o_refs[g][...] = jnp.dot(u_refs[g][...], w_scr[...],
                                     preferred_element_type=F32).astype(o_refs[g].dtype)


def _proj(us, w, out_dtype, *, tm, tn):
    d, n = w.shape
    tn = min(tn, n)
    tms = [min(tm, u.shape[0]) for u in us]
    counts = [u.shape[0] // t for u, t in zip(us, tms)]
    assert all(u.shape[0] % t == 0 for u, t in zip(us, tms)) and n % tn == 0 and d % min(CAST_ROWS, d) == 0
    starts, steps = _group_steps(counts)
    tile = lambda g: (lambda j, i: (_group_tile(i, starts[g], counts[g]), 0))
    out_tile = lambda g: (lambda j, i: (_group_tile(i, starts[g], counts[g]), j))
    return pl.pallas_call(
        functools.partial(_proj_kernel, starts=starts, counts=counts),
        grid=(n // tn, steps),
        in_specs=([pl.BlockSpec((tms[g], d), tile(g)) for g in range(len(us))]
                  + [pl.BlockSpec((d, tn), lambda j, i: (0, j))]),
        out_specs=[pl.BlockSpec((tms[g], tn), out_tile(g)) for g in range(len(us))],
        out_shape=[jax.ShapeDtypeStruct((u.shape[0], n), out_dtype) for u in us],
        scratch_shapes=[pltpu.VMEM((d, tn), BF16)],
        compiler_params=_params("arbitrary", "arbitrary"),
        name="proj",
    )(*us, w)


def _memory_kv_kernel(mem_ref, g_ref, wk_ref, wv_ref, k_ref, v_ref, u_scr, w_scr, *, n_j):
    j = pl.program_id(0)

    @pl.when(j == 0)
    def _():
        u_scr[...] = _rmsnorm(mem_ref[...], g_ref[...]).astype(BF16)

    for first, w_ref, o_ref in ((0, wk_ref, k_ref), (n_j, wv_ref, v_ref)):
        @pl.when((j >= first) & (j < first + n_j))
        def _(w_ref=w_ref, o_ref=o_ref):
            _cast_rows(w_ref, w_scr)
            o_ref[...] = jnp.dot(u_scr[...], w_scr[...], preferred_element_type=F32)


def _memory_kv(mem, g, w_k, w_v, *, tn):
    m, d = mem.shape
    n = w_k.shape[1]
    tn = min(tn, n)
    n_j = n // tn
    assert n % tn == 0 and w_v.shape == w_k.shape and d % min(CAST_ROWS, d) == 0
    k_tile = lambda j: (0, jnp.minimum(j, n_j - 1))
    v_tile = lambda j: (0, jnp.maximum(j - n_j, 0))
    return pl.pallas_call(
        functools.partial(_memory_kv_kernel, n_j=n_j),
        grid=(2 * n_j,),
        in_specs=[pl.BlockSpec((m, d), lambda j: (0, 0)), pl.BlockSpec((1, d), lambda j: (0, 0)),
                  pl.BlockSpec((d, tn), k_tile), pl.BlockSpec((d, tn), v_tile)],
        out_specs=[pl.BlockSpec((m, tn), k_tile), pl.BlockSpec((m, tn), v_tile)],
        out_shape=[jax.ShapeDtypeStruct((m, n), F32), jax.ShapeDtypeStruct((m, n), F32)],
        scratch_shapes=[pltpu.VMEM((m, d), BF16), pltpu.VMEM((d, tn), BF16)],
        compiler_params=_params("arbitrary"),
        name="memory_kv",
    )(mem, g, w_k, w_v)


def _mixer_pre_kernel(*refs, d_rnn, d_conv, pos0, n_side):
    (z_ref, sa_ref, h0_ref, sb_ref, caw_ref, cab_ref, wri_ref, br_ref, bi_ref, lam_ref, cbw_ref) = refs[:11]
    side_in = refs[11:11 + n_side]
    ya_ref, yb_ref, nsa_ref, nh_ref, nsb_ref = refs[11 + n_side:16 + n_side]
    side_out = refs[16 + n_side:16 + 2 * n_side]
    xa_scr, ha_scr, xc_scr, xcb_scr, ri_scr, cb_scr, hb_scr, h_scr = refs[16 + 2 * n_side:]
    _side_cast(side_in, side_out)
    t = pl.program_id(1)
    nb, seq, _ = ya_ref.shape
    n_groups = seq // SUBLANES
    n_heads, head_dim, _ = wri_ref.shape

    @pl.when(t == 0)
    def _():
        _seg_hist_init(ha_scr, sa_ref, n_groups)
        _seg_hist_init(hb_scr, sb_ref, n_groups)
        h_scr[...] = jnp.broadcast_to(h0_ref[...], h_scr.shape)

    xa_scr[...] = z_ref[:, :, 0:d_rnn].astype(F32)
    xc = _seg_conv(xa_scr, ha_scr, caw_ref, cab_ref[...])
    xc_scr[...] = xc
    xcb_scr[...] = xc.reshape(nb * seq, d_rnn).astype(BF16)
    _seg_state_out(nsa_ref, xa_scr)
    ha_scr[...] = xa_scr[:, seq - ha_scr.shape[1]:, :]

    for h in range(n_heads):
        cols = slice(h * head_dim, (h + 1) * head_dim)
        ri = jnp.dot(xcb_scr[:, cols], wri_ref[h], preferred_element_type=F32)
        ri_scr[:, :, cols] = ri[:, :head_dim].reshape(nb, seq, head_dim)
        ri_scr[:, :, d_rnn + h * head_dim:d_rnn + (h + 1) * head_dim] = ri[:, head_dim:].reshape(nb, seq, head_dim)

    strip = max(LANES, SCAN_LANES // nb)
    pairs = seq // BF16_ROWS
    sub = lax.broadcasted_iota(jnp.int32, (1, SUBLANES, strip), 1)
    for c in range(d_rnn // strip):
        cs = slice(c * strip, (c + 1) * strip)
        cs_i = slice(d_rnn + c * strip, d_rnn + (c + 1) * strip)
        cs_g = slice(d_rnn + c * strip, d_rnn + (c + 1) * strip)
        rate = RG_C * _softplus(-lam_ref[:, cs])
        b_r = br_ref[:, cs]
        b_i = bi_ref[:, cs]

        def pass1(it, carry, first=False, cs=cs, cs_i=cs_i, rate=rate, b_r=b_r, b_i=b_i):
            h_loc, a_run = carry
            for half in range(BF16_ROWS // SUBLANES):
                rows = pl.ds(pl.multiple_of(it * BF16_ROWS + half * SUBLANES, SUBLANES), SUBLANES)
                r = _sigmoid(ri_scr[:, rows, cs] + b_r)
                i = _sigmoid(ri_scr[:, rows, cs_i] + b_i)
                neg_log_a = r * rate
                a = jnp.exp(-neg_log_a)
                mult = jnp.sqrt(jnp.tanh(neg_log_a) * (a * a + 1.0))
                if first and half == 0:
                    mult = jnp.where(jnp.logical_and(t == 0, sub == 0), 1.0, mult)
                b = mult * i * xc_scr[:, rows, cs]
                h_loc = a * h_loc + b
                a_run = a * a_run
                ri_scr[:, rows, cs] = h_loc
                ri_scr[:, rows, cs_i] = a_run
            return h_loc, a_run

        carry = (jnp.zeros((nb, SUBLANES, strip), F32), jnp.ones((nb, SUBLANES, strip), F32))
        start = 0
        if pos0 == 0:
            carry = pass1(0, carry, first=True)
            start = 1
        h_tot, a_tot = lax.fori_loop(start, pairs, pass1, carry)

        for s in (1, 2, 4):
            keep = sub >= s
            a_prev = pltpu.roll(a_tot, s, 1)
            h_prev = pltpu.roll(h_tot, s, 1)
            h_tot = jnp.where(keep, a_tot * h_prev + h_tot, h_tot)
            a_tot = jnp.where(keep, a_tot * a_prev, a_tot)
        h_in = h_scr[:, :, cs]
        seg_end = a_tot * h_in + h_tot
        seg_start = jnp.where(sub == 0, h_in, pltpu.roll(seg_end, 1, 1))
        h_last = jnp.broadcast_to(seg_end[:, SUBLANES - 1:SUBLANES, :], seg_end.shape)
        h_scr[:, :, cs] = h_last
        nh_ref[:, :, cs] = h_last[:, 0:1, :]
        start2 = jnp.concatenate([seg_start] * (BF16_ROWS // SUBLANES), axis=1)

        def pass2(it, _, cs=cs, cs_i=cs_i, cs_g=cs_g, start2=start2):
            rows = pl.ds(pl.multiple_of(it * BF16_ROWS, BF16_ROWS), BF16_ROWS)
            h = ri_scr[:, rows, cs] + ri_scr[:, rows, cs_i] * start2
            gate = _gelu_tanh(z_ref[:, rows, cs_g].astype(F32))
            ya_ref[:, rows, cs] = (h * gate).astype(BF16)
            return None

        lax.fori_loop(0, pairs, pass2, None)

    o_gb, o_gc, o_hb = 2 * d_rnn, 2 * d_rnn + d_conv, 2 * d_rnn + 2 * d_conv
    cb_scr[...] = z_ref[:, :, o_gc:o_gc + d_conv].astype(F32) * z_ref[:, :, o_hb:o_hb + d_conv].astype(F32)
    co = _seg_conv(cb_scr, hb_scr, cbw_ref, None)
    yb_ref[...] = (z_ref[:, :, o_gb:o_gb + d_conv].astype(F32) * co).astype(BF16)
    _seg_state_out(nsb_ref, cb_scr)
    hb_scr[...] = cb_scr[:, seq - hb_scr.shape[1]:, :]


def _mixer_pre(z, state_a, h0, state_b, conv_a_w, conv_a_b, w_ri, b_r, b_i, lam, conv_b_w, *, seq_tile, pos0,
               to_round=()):
    bsz, t, _ = z.shape
    d_rnn = conv_a_w.shape[1]
    d_conv = conv_b_w.shape[1]
    n_pre = 2 * d_rnn + 3 * d_conv
    nb = bsz if t <= seq_tile else 1
    seq = min(seq_tile, t)
    assert t % seq == 0 and seq % BF16_ROWS == 0 and bsz % nb == 0
    wa, wb = conv_a_w.shape[0], conv_b_w.shape[0]
    hist_a, hist_b = min(seq // SUBLANES, wa - 1), min(seq // SUBLANES, wb - 1)
    kern = functools.partial(_mixer_pre_kernel, d_rnn=d_rnn, d_conv=d_conv, pos0=pos0, n_side=len(to_round))
    n_t = t // seq
    side_in, side_out, side_shape = _side_cast_plan(to_round, (bsz // nb) * n_t, lambda b, i: b * n_t + i)
    row = lambda b, i: (b, i, 0)
    per_b = lambda b, i: (b, 0, 0)
    const2 = lambda b, i: (0, 0)
    return pl.pallas_call(
        kern,
        grid=(bsz // nb, t // seq),
        in_specs=[pl.BlockSpec((nb, seq, n_pre), row),
                  pl.BlockSpec((nb, wa - 1, d_rnn), per_b),
                  pl.BlockSpec((nb, 1, d_rnn), per_b),
                  pl.BlockSpec((nb, wb - 1, d_conv), per_b),
                  pl.BlockSpec(conv_a_w.shape, const2),
                  pl.BlockSpec((1, d_rnn), const2),
                  pl.BlockSpec(w_ri.shape, lambda b, i: (0, 0, 0)),
                  pl.BlockSpec((1, d_rnn), const2),
                  pl.BlockSpec((1, d_rnn), const2),
                  pl.BlockSpec((1, d_rnn), const2),
                  pl.BlockSpec(conv_b_w.shape, const2)] + side_in,
        out_specs=[pl.BlockSpec((nb, seq, d_rnn), row),
                   pl.BlockSpec((nb, seq, d_conv), row),
                   pl.BlockSpec((nb, wa - 1, d_rnn), per_b),
                   pl.BlockSpec((nb, 1, d_rnn), per_b),
                   pl.BlockSpec((nb, wb - 1, d_conv), per_b)] + side_out,
        out_shape=[jax.ShapeDtypeStruct((bsz, t, d_rnn), BF16),
                   jax.ShapeDtypeStruct((bsz, t, d_conv), BF16),
                   jax.ShapeDtypeStruct((bsz, wa - 1, d_rnn), F32),
                   jax.ShapeDtypeStruct((bsz, 1, d_rnn), F32),
                   jax.ShapeDtypeStruct((bsz, wb - 1, d_conv), F32)] + side_shape,
        scratch_shapes=[pltpu.VMEM((nb, seq, d_rnn), F32),
                        pltpu.VMEM((nb, hist_a * SUBLANES, d_rnn), F32),
                        pltpu.VMEM((nb, seq, d_rnn), F32),
                        pltpu.VMEM((nb * seq, d_rnn), BF16),
                        pltpu.VMEM((nb, seq, 2 * d_rnn), F32),
                        pltpu.VMEM((nb, seq, d_conv), F32),
                        pltpu.VMEM((nb, hist_b * SUBLANES, d_conv), F32),
                        pltpu.VMEM((nb, SUBLANES, d_rnn), F32)],
        compiler_params=_params("arbitrary", "arbitrary"),
        name="mixer_pre",
    )(z, state_a, h0, state_b, conv_a_w, conv_a_b, w_ri, b_r, b_i, lam, conv_b_w, *to_round)


def _mixer_post_kernel(ya_ref, yb_ref, ga0_ref, ga1_ref, gb0_ref, gb1_ref, x_ref, pa_ref, pb_ref, wo_ref, g_ref,
                       unperm_ref, o_ref, mix_scr):
    half = ga0_ref.shape[1]
    y_a = jnp.dot(ya_ref[...], pa_ref[...], preferred_element_type=F32)
    y_b = jnp.dot(yb_ref[...], pb_ref[...], preferred_element_type=F32)
    for c, (ga_ref, gb_ref) in enumerate(((ga0_ref, gb0_ref), (ga1_ref, gb1_ref))):
        cols = slice(c * half, (c + 1) * half)
        mix = (jax.nn.sigmoid(ga_ref[...].astype(F32)) * y_a[:, cols]
               + jax.nn.sigmoid(gb_ref[...].astype(F32)) * y_b[:, cols])
        mix_scr[:, cols] = mix.astype(BF16)
    _permute_rows(mix_scr, unperm_ref)
    out = jnp.dot(mix_scr[...], wo_ref[...], preferred_element_type=F32)
    o_ref[...] = x_ref[...] + _rmsnorm(out, g_ref[...])


def _mixer_post(ya, yb, z, x, p_a, p_b, w_o, g_post, unperm, *, tm):
    m, d = x.shape
    d_rnn, d_conv = ya.shape[1], yb.shape[1]
    tm = min(tm, m)
    half = d // 2
    gate0 = (2 * d_rnn + 3 * d_conv) // half
    assert m % tm == 0 and (2 * d_rnn + 3 * d_conv) % half == 0 and tm % unperm.shape[0] == 0
    gate_spec = lambda k: pl.BlockSpec((tm, half), lambda i: (i, gate0 + k))
    row = lambda i: (i, 0)
    return pl.pallas_call(
        _mixer_post_kernel,
        grid=(m // tm,),
        in_specs=[pl.BlockSpec((tm, d_rnn), row), pl.BlockSpec((tm, d_conv), row),
                  gate_spec(0), gate_spec(1), gate_spec(2), gate_spec(3),
                  pl.BlockSpec((tm, d), row),
                  _resident(p_a.shape), _resident(p_b.shape), _resident(w_o.shape),
                  pl.BlockSpec((1, d), lambda i: (0, 0)), pl.BlockSpec(unperm.shape, lambda i: (0, 0))],
        out_specs=pl.BlockSpec((tm, d), row),
        out_shape=jax.ShapeDtypeStruct((m, d), F32),
        scratch_shapes=[pltpu.VMEM((tm, d), BF16)],
        compiler_params=_params("parallel"),
        name="mixer_post",
    )(ya, yb, z, z, z, z, x, p_a, p_b, w_o, g_post, unperm)


def _xattn_kernel(x_ref, k_ref, v_ref, wq_ref, wxo_ref, gpre_ref, gpost_ref, gnext_ref, perm_ref, o_ref, u_ref,
                  q_scr, o_scr, *, n_heads):
    nb, seq, d = x_ref.shape
    hd = d // n_heads
    scale = hd ** -0.5
    x = x_ref[...].reshape(nb * seq, d)
    u = _rmsnorm(x, gpre_ref[...]).astype(BF16)
    q_scr[...] = jnp.dot(u, wq_ref[...], preferred_element_type=F32).astype(BF16)
    pairs = [(b, slice(b * seq, (b + 1) * seq), slice(h * hd, (h + 1) * hd))
             for b in range(nb) for h in range(n_heads)]

    def scores(b, rows, cols):
        return lax.dot_general(q_scr[rows, cols], k_ref[b, :, cols], (((1,), (1,)), ((), ())),
                               preferred_element_type=F32) * scale

    def softmax(s):
        p = jnp.exp(s - jnp.max(s, axis=-1, keepdims=True))
        return (p / jnp.sum(p, axis=-1, keepdims=True)).astype(BF16)

    def weighted(att, b, rows, cols):
        o_scr[rows, cols] = jnp.dot(att, v_ref[b, :, cols], preferred_element_type=F32).astype(BF16)

    atts = [softmax(s) for s in [scores(*pr) for pr in pairs]]
    for att, pr in zip(atts, pairs):
        weighted(att, *pr)
    out = jnp.dot(o_scr[...], wxo_ref[...], preferred_element_type=F32)
    x_new = x + _rmsnorm(out, gpost_ref[...])
    o_ref[...] = x_new.reshape(nb, seq, d)
    u_next = _rmsnorm(x_new, gnext_ref[...]).astype(BF16)
    blk = perm_ref.shape[0]
    u_next = jnp.concatenate([jnp.dot(perm_ref[...], u_next[r:r + blk, :], preferred_element_type=F32)
                              for r in range(0, nb * seq, blk)], axis=0)
    u_ref[...] = u_next.astype(BF16).reshape(nb, seq, d)


def _xattn(x, mem_k, mem_v, w_q, w_xo, g_pre, g_post, g_next, perm, *, n_heads, seq_tile):
    bsz, t, d = x.shape
    n_mem = mem_k.shape[1]
    nb = bsz if t <= seq_tile else 1
    seq = min(seq_tile, t)
    assert t % seq == 0 and seq % BF16_ROWS == 0 and bsz % nb == 0 and (nb * seq) % perm.shape[0] == 0
    single = bsz // nb == 1
    mem_spec = (_resident((nb, n_mem, d)) if single else pl.BlockSpec((nb, n_mem, d), lambda b, i: (b, 0, 0)))
    row = lambda b, i: (b, i, 0)
    gain = pl.BlockSpec((1, d), lambda b, i: (0, 0))
    return pl.pallas_call(
        functools.partial(_xattn_kernel, n_heads=n_heads),
        grid=(bsz // nb, t // seq),
        in_specs=[pl.BlockSpec((nb, seq, d), row), mem_spec, mem_spec,
                  _resident(w_q.shape), _resident(w_xo.shape), gain, gain, gain,
                  pl.BlockSpec(perm.shape, lambda b, i: (0, 0))],
        out_specs=[pl.BlockSpec((nb, seq, d), row), pl.BlockSpec((nb, seq, d), row)],
        out_shape=[jax.ShapeDtypeStruct((bsz, t, d), F32), jax.ShapeDtypeStruct((bsz, t, d), BF16)],
        scratch_shapes=[pltpu.VMEM((nb * seq, d), BF16), pltpu.VMEM((nb * seq, d), BF16)],
        compiler_params=_params("parallel", "parallel"),
        name="xattn",
    )(x, mem_k, mem_v, w_q, w_xo, g_pre, g_post, g_next, perm)


def _up_geglu_kernel(*refs, starts, counts, blocks, n_side):
    n_groups = len(counts)
    ins, rest = refs[:3 * n_groups], refs[3 * n_groups:]
    wg_ref, wv_ref, cwg_ref, cwv_ref, cbg_ref, cbv_ref = rest[:6]
    side_in, rest = rest[6:6 + n_side], rest[6 + n_side:]
    outs, rest = rest[:3 * n_groups], rest[3 * n_groups:]
    side_out, scr = rest[:n_side], rest[n_side:]
    w_scr, hists = scr[0], scr[1:]
    _side_cast(side_in, side_out)
    i = pl.program_id(1)
    tn = wg_ref.shape[1]

    @pl.when(i == 0)
    def _():
        _cast_rows(wg_ref, w_scr, slice(0, tn))
        _cast_rows(wv_ref, w_scr, slice(tn, 2 * tn))

    for g in range(n_groups):
        u_ref, st_refs = ins[3 * g], ins[3 * g + 1:3 * g + 3]
        hid_ref, ns_refs = outs[3 * g], outs[3 * g + 1:3 * g + 3]
        p_scrs = hists[2 * g:2 * g + 2]

        @pl.when(i == starts[g])
        def _(st_refs=st_refs, p_scrs=p_scrs, blk=blocks[g]):
            for st_ref, p_scr in zip(st_refs, p_scrs):
                _seg_hist_init(p_scr, st_ref, blk // SUBLANES)

        @pl.when((i >= starts[g]) & (i < starts[g] + counts[g]))
        def _(u_ref=u_ref, hid_ref=hid_ref, ns_refs=ns_refs, p_scrs=p_scrs, blk=blocks[g]):
            nb, seq, k_dim = u_ref.shape
            per_seq = seq // blk
            n_hist = p_scrs[0].shape[1]
            both = jnp.dot(u_ref[...].reshape(nb * seq, k_dim), w_scr[...], preferred_element_type=F32)
            ys = []
            for half, (p_scr, cw_ref, cb_ref, ns_ref) in enumerate(zip(p_scrs, (cwg_ref, cwv_ref),
                                                                     (cbg_ref, cbv_ref), ns_refs)):
                a = both[:, half * tn:(half + 1) * tn].reshape(nb * per_seq, blk, tn)
                tails = a[:, blk - n_hist:, :].reshape(nb, per_seq, n_hist, tn)
                hist = p_scr[...][:, None]
                if per_seq > 1:
                    hist = jnp.concatenate([hist, tails[:, :per_seq - 1]], axis=1)
                hist = hist.reshape(nb * per_seq, n_hist, tn)
                ys.append(_seg_conv(a, hist, cw_ref, cb_ref[...]).reshape(nb, seq, tn))
                p_scr[...] = tails[:, per_seq - 1]
                _seg_state_out(ns_ref, a.reshape(nb, per_seq, blk, tn)[:, per_seq - 1])
            hid_ref[...] = (_gelu_tanh(ys[0]) * ys[1]).astype(BF16)


def _up_geglu(us, w_up, states, conv_w, conv_b, blocks, *, tm, tn, to_round=()):
    d = w_up.shape[0]
    d_ff = w_up.shape[1] // 2
    width = conv_w.shape[0]
    n_j = d_ff // tn
    nbs = [u.shape[0] if u.shape[1] <= tm else 1 for u in us]
    seqs = [min(tm, u.shape[1]) for u in us]
    counts = [u.shape[1] // s for u, s in zip(us, seqs)]
    assert all(u.shape[0] == nb and u.shape[1] % s == 0 and s % BF16_ROWS == 0 for u, nb, s in zip(us, nbs, seqs))
    assert d_ff % tn == 0 and tn % LANES == 0
    starts, steps = _group_steps(counts)
    col_g = lambda j, i: (0, j)
    col_v = lambda j, i: (0, n_j + j)
    st_g = lambda j, i: (0, 0, j)
    st_v = lambda j, i: (0, 0, n_j + j)
    in_specs, out_specs, out_shape, hist, args = [], [], [], [], []
    for g, (u, st, nb, seq) in enumerate(zip(us, states, nbs, seqs)):
        row = lambda j, i, g=g: (0, _group_tile(i, starts[g], counts[g]), 0)
        out = lambda j, i, g=g: (0, _group_tile(i, starts[g], counts[g]), j)
        in_specs += [pl.BlockSpec((nb, seq, d), row),
                     pl.BlockSpec((nb, width - 1, tn), st_g), pl.BlockSpec((nb, width - 1, tn), st_v)]
        out_specs += [pl.BlockSpec((nb, seq, tn), out),
                      pl.BlockSpec((nb, width - 1, tn), st_g), pl.BlockSpec((nb, width - 1, tn), st_g)]
        out_shape += [jax.ShapeDtypeStruct((u.shape[0], u.shape[1], d_ff), BF16),
                      jax.ShapeDtypeStruct((u.shape[0], width - 1, d_ff), F32),
                      jax.ShapeDtypeStruct((u.shape[0], width - 1, d_ff), F32)]
        assert seq % blocks[g] == 0 and blocks[g] % SUBLANES == 0
        n_hist = min(blocks[g] // SUBLANES, width - 1) * SUBLANES
        hist += [pltpu.VMEM((nb, n_hist, tn), F32), pltpu.VMEM((nb, n_hist, tn), F32)]
        args += [u, st, st]
    side_in, side_out, side_shape = _side_cast_plan(to_round, n_j * steps, lambda j, i: j * steps + i)
    in_specs += [pl.BlockSpec((d, tn), col_g), pl.BlockSpec((d, tn), col_v),
                 pl.BlockSpec((width, tn), col_g), pl.BlockSpec((width, tn), col_v),
                 pl.BlockSpec((1, tn), col_g), pl.BlockSpec((1, tn), col_v)] + side_in
    res = pl.pallas_call(
        functools.partial(_up_geglu_kernel, starts=starts, counts=counts, blocks=tuple(blocks),
                          n_side=len(to_round)),
        grid=(n_j, steps),
        in_specs=in_specs,
        out_specs=out_specs + side_out,
        out_shape=out_shape + side_shape,
        scratch_shapes=[pltpu.VMEM((d, 2 * tn), BF16)] + hist,
        compiler_params=_params("arbitrary", "arbitrary"),
        name="up_geglu",
    )(*args, w_up, w_up, conv_w, conv_w, conv_b, conv_b, *to_round)
    return [tuple(res[3 * g:3 * g + 3]) for g in range(len(us))], list(res[3 * len(us):])


def _ffn_down_kernel(*refs, starts, counts):
    n_groups = len(counts)
    wd_ref, g_ref = refs[3 * n_groups:3 * n_groups + 2]
    o_refs = refs[3 * n_groups + 2:]
    i = pl.program_id(0)
    for g in range(n_groups):
        hid_ref, x_ref, unperm_ref = refs[3 * g:3 * g + 3]

        @pl.when((i >= starts[g]) & (i < starts[g] + counts[g]))
        def _(hid_ref=hid_ref, x_ref=x_ref, unperm_ref=unperm_ref, o_ref=o_refs[g]):
            y = jnp.dot(hid_ref[...], wd_ref[...], preferred_element_type=F32)
            branch = _rmsnorm(y, g_ref[...]).astype(BF16)
            blk = unperm_ref.shape[0]
            for r in range(0, branch.shape[0], blk):
                o_ref[r:r + blk, :] = x_ref[r:r + blk, :] + jnp.dot(unperm_ref[...], branch[r:r + blk, :],
                                                                    preferred_element_type=F32)


def _ffn_down(hids, xs, w_down, g_post, unperms, *, tm):
    d_ff, d = w_down.shape
    tms = [min(tm, x.shape[0]) for x in xs]
    counts = [x.shape[0] // t for x, t in zip(xs, tms)]
    assert all(x.shape[0] % t == 0 and t % un.shape[0] == 0 for x, t, un in zip(xs, tms, unperms))
    starts, steps = _group_steps(counts)
    in_specs, args = [], []
    for g, (hid, x, un) in enumerate(zip(hids, xs, unperms)):
        row = lambda i, g=g: (_group_tile(i, starts[g], counts[g]), 0)
        in_specs += [pl.BlockSpec((tms[g], d_ff), row), pl.BlockSpec((tms[g], d), row),
                     pl.BlockSpec(un.shape, lambda i: (0, 0))]
        args += [hid, x, un]
    return pl.pallas_call(
        functools.partial(_ffn_down_kernel, starts=starts, counts=counts),
        grid=(steps,),
        in_specs=in_specs + [_resident(w_down.shape), pl.BlockSpec((1, d), lambda i: (0, 0))],
        out_specs=[pl.BlockSpec((tms[g], d), lambda i, g=g: (_group_tile(i, starts[g], counts[g]), 0))
                   for g in range(len(xs))],
        out_shape=[jax.ShapeDtypeStruct(x.shape, F32) for x in xs],
        compiler_params=_params("arbitrary"),
        name="ffn_down",
    )(*args, w_down, g_post)


MM_TM, MM_TN = 2048, 1024
UP_TM, UP_TN = 1024, 512
SEQ_TILE = 256
ROW_TILE = 512
NORM_TM = 2048
N_XHEADS = 4


def _layer(groups, p):
    flat = [x.reshape(-1, x.shape[-1]) for x, *_ in groups]
    perms, blocks = [], []
    for x, *_ in groups:
        bsz, t, _ = x.shape
        perms.append(_segment_major_perm(*((1, SEQ_TILE) if t > SEQ_TILE else (bsz, t))))
        blocks.append(min(t, SEQ_TILE))
    u0s = [_rmsnorm_call(x2, p['g_mix_pre'], tm=NORM_TM, perm=pm) for x2, pm in zip(flat, perms)]
    raw_mems, u0s = lax.optimization_barrier(([m for g in groups for m in g[2:4]], u0s))
    mems = [m.reshape(-1, flat[0].shape[-1]) for m in raw_mems]
    zs = _proj(u0s, p['w_in'], BF16, tm=MM_TM, tn=MM_TN)
    big = max(range(len(groups)), key=lambda g: flat[g].shape[0])
    resident = [p['p_a'], p['p_b'], p['w_o'], p['w_q'], p['w_xo']]
    pre = []
    for g, ((x, pos0, _, _, s_a, h0, s_b, _), z) in enumerate(zip(groups, zs)):
        bsz, t, _ = x.shape
        pre.append(_mixer_pre(z.reshape(bsz, t, -1), s_a, h0[:, None, :], s_b,
                              p['conv_a_w'], p['conv_a_b'], p['w_ri'], p['b_r'], p['b_i'],
                              p['lru_lambda'], p['conv_b_w'], seq_tile=SEQ_TILE, pos0=pos0,
                              to_round=resident + mems if g == big else ()))
    p_a, p_b, w_o, w_q, w_xo, *mems = pre[big][5:]
    mids = []
    for g, ((x, *_), x2, z) in enumerate(zip(groups, flat, zs)):
        bsz, t, d = x.shape
        m = bsz * t
        ya, yb, ns_a, nh, ns_b = pre[g][:5]
        mem_k, mem_v = (mm.reshape(bsz, -1, d) for mm in mems[2 * g:2 * g + 2])
        x1 = _mixer_post(ya.reshape(m, -1), yb.reshape(m, -1), z, x2, p_a, p_b, w_o, p['g_mix_post'],
                         perms[g].T, tm=ROW_TILE)
        x2a, u3 = _xattn(x1.reshape(bsz, t, d), mem_k, mem_v, w_q, w_xo, p['g_x_pre'], p['g_x_post'],
                         p['g_ffn_pre'], perms[g], n_heads=N_XHEADS, seq_tile=ROW_TILE)
        mids.append((x2a, u3, ns_a, nh[:, 0, :], ns_b))
    ffn, (w_down,) = _up_geglu([u3 for _, u3, *_ in mids], p['w_up'], [g[7] for g in groups],
                               p['ffn_conv_w'], p['ffn_conv_b'], blocks, tm=UP_TM, tn=UP_TN,
                               to_round=[p['w_down']])
    x3s = _ffn_down([hid.reshape(-1, hid.shape[-1]) for hid, _, _ in ffn],
                    [x2a.reshape(-1, x2a.shape[-1]) for x2a, *_ in mids], w_down, p['g_ffn_post'],
                    [pm.T for pm in perms], tm=ROW_TILE)
    return [(x3.reshape(x2a.shape), ns_a, nh, ns_b, jnp.concatenate([ns_fg, ns_fv], axis=-1))
            for x3, (x2a, _, ns_a, nh, ns_b), (_, ns_fg, ns_fv) in zip(x3s, mids, ffn)]


def kernel(x_prompt, x_sample, mem_prompt, state_conv_a, state_rglru, state_conv_b, state_ffn_conv, cache_mem_k, cache_mem_v, g_mix_pre, g_mix_post, w_in, conv_a_w, conv_a_b, w_r, b_r, w_i, b_i, lru_lambda, conv_b_w, p_a, p_b, w_o, g_x_pre, g_x_post, g_mem, w_q, w_k, w_v, w_xo, g_ffn_pre, g_ffn_post, w_up, ffn_conv_w, ffn_conv_b, w_down):
    depth = w_in.shape[0]
    bsz, _, d = x_prompt.shape
    n_mem = mem_prompt.shape[1]
    yp, ys = x_prompt, x_sample
    outs = [[] for _ in range(10)]
    row = lambda v: v.reshape(1, -1).astype(F32)
    for l in range(depth):
        p = {'g_mix_pre': row(g_mix_pre[l]), 'g_mix_post': row(g_mix_post[l]), 'w_in': w_in[l],
             'conv_a_w': conv_a_w[l], 'conv_a_b': row(conv_a_b[l]),
             'w_ri': jnp.concatenate([w_r[l], w_i[l]], axis=-1).astype(BF16),
             'b_r': row(b_r[l]), 'b_i': row(b_i[l]), 'lru_lambda': row(lru_lambda[l]),
             'conv_b_w': conv_b_w[l], 'p_a': p_a[l], 'p_b': p_b[l],
             'w_o': w_o[l], 'g_x_pre': row(g_x_pre[l]), 'g_x_post': row(g_x_post[l]),
             'w_q': w_q[l], 'w_xo': w_xo[l],
             'g_ffn_pre': row(g_ffn_pre[l]), 'g_ffn_post': row(g_ffn_post[l]), 'w_up': w_up[l],
             'ffn_conv_w': ffn_conv_w[l], 'ffn_conv_b': row(ffn_conv_b[l]), 'w_down': w_down[l]}
        d_rnn, d_conv, d_up = conv_a_w.shape[2], conv_b_w.shape[2], ffn_conv_w.shape[2]
        mk, mv = (m.reshape(bsz, n_mem, d) for m in
                  _memory_kv(mem_prompt.reshape(bsz * n_mem, d), row(g_mem[l]), w_k[l], w_v[l], tn=MM_TN))
        zeros = lambda *s: jnp.zeros(s, F32)
        prompt = (yp, 0, mk, mv,
                  zeros(bsz, conv_a_w.shape[1] - 1, d_rnn), zeros(bsz, d_rnn),
                  zeros(bsz, conv_b_w.shape[1] - 1, d_conv), zeros(bsz, ffn_conv_w.shape[1] - 1, d_up))
        sample = (ys, PAST_LEN, cache_mem_k[l], cache_mem_v[l], state_conv_a[l], state_rglru[l], state_conv_b[l],
                  state_ffn_conv[l])
        (ys, *s_states), (yp, *p_states) = _layer([sample, prompt], p)
        for o, v in zip(outs, (*p_states, mk.reshape(bsz, n_mem, N_XHEADS, -1),
                               mv.reshape(bsz, n_mem, N_XHEADS, -1), *s_states)):
            o.append(v)
    return (yp, ys) + tuple(jnp.stack(o) for o in outs)
```
